```python
import math
import jax, jax.numpy as jnp
from jax import lax
import numpy as np

D_MODEL = 2048
BATCH = 16
SEQ = 256
DEPTH = 2
DEC_BATCH = 8
DEC_SEQ = 2048
PAST_LEN = 512

GRID_W = 64
N_BRANCH = 4
MLA_HEADS = 8
Q_LORA = 512
KV_LORA = 256
QK_NOPE = 64
ROPE_DIM = 32
QK_HEAD = QK_NOPE + ROPE_DIM
V_HEAD = 64
ROPE_THETA = 10000.0
CONV_W = 512
SSM_W = 512
SSM_GROUP_CH = 16
SSM_GROUPS = SSM_W // SSM_GROUP_CH
SSM_STATE = 64
POOL_W = 512
POOL_WINDOWS = (2, 4, 8, 16)
POOL_GROUP_CH = POOL_W // 4
MLP_HIDDEN = 4 * D_MODEL
Q_BLOCK = 128
EPS = 1e-6
IN_SIZES = (Q_LORA, KV_LORA, ROPE_DIM, CONV_W, CONV_W, CONV_W, SSM_W, POOL_W, N_BRANCH * D_MODEL)
IN_SPLITS = tuple(int(s) for s in np.cumsum(IN_SIZES)[:-1])
IN_COLS = int(sum(IN_SIZES))

kernel_name = 'hybrid_prefix_diffusion_step'


def rms_norm(x, g):
    xf = x.astype(jnp.float32)
    y = xf * lax.rsqrt(jnp.mean(xf * xf, axis=-1, keepdims=True) + EPS)
    return (y * g.astype(jnp.float32)).astype(x.dtype)


def axial_rope_tables(rows):
    half = ROPE_DIM // 2
    inv_freq = ROPE_THETA ** (-jnp.arange(0, half, 2, dtype=jnp.float32) / half)
    row_pos = jnp.repeat(jnp.arange(rows, dtype=jnp.float32), GRID_W)
    col_pos = jnp.tile(jnp.arange(GRID_W, dtype=jnp.float32), rows)
    ang_r = row_pos[:, None] * inv_freq
    ang_c = col_pos[:, None] * inv_freq
    return (jnp.cos(ang_r)[:, None, :], jnp.sin(ang_r)[:, None, :],
            jnp.cos(ang_c)[:, None, :], jnp.sin(ang_c)[:, None, :])


def rotate_pairs(x, cos, sin):
    x1, x2 = jnp.split(x, 2, axis=-1)
    return jnp.concatenate([x1 * cos - x2 * sin, x1 * sin + x2 * cos], axis=-1)


def apply_axial_rope(x, rope):
    cos_r, sin_r, cos_c, sin_c = rope
    half = ROPE_DIM // 2
    x_nope = x[..., :QK_NOPE]
    x_row = x[..., QK_NOPE:QK_NOPE + half]
    x_col = x[..., QK_NOPE + half:]
    return jnp.concatenate([x_nope, rotate_pairs(x_row, cos_r, sin_r), rotate_pairs(x_col, cos_c, sin_c)], axis=-1)


def mla_keys_values(ckv, k_rope, lp, rope):
    b, n, _ = ckv.shape
    kv = (ckv @ lp['w_ukv']).reshape(b, n, MLA_HEADS, QK_NOPE + V_HEAD)
    k_nope, v = kv[..., :QK_NOPE], kv[..., QK_NOPE:]
    k_pe = jnp.broadcast_to(k_rope[:, :, None, :], (b, n, MLA_HEADS, ROPE_DIM))
    k = rms_norm(jnp.concatenate([k_nope, k_pe.astype(k_nope.dtype)], axis=-1), lp['k_norm_g'])
    if rope is not None:
        k = apply_axial_rope(k, rope)
    return k, v


def block_attention(q, k, v):
    b, lq, h, dh = q.shape
    nb = lq // Q_BLOCK
    qb = jnp.moveaxis(q.reshape(b, nb, Q_BLOCK, h, dh), 1, 0)
    scale = dh ** -0.5

    def one_block(qi):
        s = jnp.einsum('bqhd,bkhd->bhqk', qi, k).astype(jnp.float32) * scale
        p = jax.nn.softmax(s, axis=-1).astype(v.dtype)
        return jnp.einsum('bhqk,bkhe->bqhe', p, v)

    o = lax.map(one_block, qb)
    return jnp.moveaxis(o, 0, 1).reshape(b, lq, h * v.shape[-1])


def short_conv(u, w, bias):
    n = u.shape[1]
    up = jnp.pad(u, ((0, 0), (1, 1), (0, 0)))
    return up[:, 0:n] * w[0] + up[:, 1:n + 1] * w[1] + up[:, 2:n + 2] * w[2] + bias


def _linear_combine(e1, e2):
    a1, b1 = e1
    a2, b2 = e2
    return a2 * a1, a2 * b1 + b2


def s5_direction(u, lam_re, lam_im, log_step, b_re, b_im, c_re, c_im, h0, reverse):
    f32 = jnp.float32
    lam = lax.complex(lam_re.astype(f32), lam_im.astype(f32))
    step = jnp.exp(log_step.astype(f32))[:, None]
    lam_bar = jnp.exp(lam * step)
    bmat = lax.complex(b_re.astype(f32), b_im.astype(f32))
    b_bar = ((lam_bar - 1.0) / lam)[..., None] * bmat
    bu = jnp.einsum('gnc,blgc->blgn', b_bar, u.astype(f32).astype(jnp.complex64))
    if reverse:
        bu = jnp.flip(bu, axis=1)
    bu = bu.at[:, 0].add(lam_bar * h0)
    a = jnp.broadcast_to(lam_bar, bu.shape)
    _, xs = lax.associative_scan(_linear_combine, (a, bu), axis=1)
    h_final = xs[:, -1]
    if reverse:
        xs = jnp.flip(xs, axis=1)
    cmat = lax.complex(c_re.astype(f32), c_im.astype(f32))
    y = jnp.real(jnp.einsum('gcn,blgn->blgc', cmat, xs))
    return y, h_final


def multiscale_pool(u, pool_w, pool_scale):
    b, n, _ = u.shape
    uf = u.astype(jnp.float32)
    cs = jnp.concatenate([jnp.zeros((b, 1, POOL_W), jnp.float32), jnp.cumsum(uf, axis=1)], axis=1)
    t = jnp.arange(n)
    outs = []
    for gi, w in enumerate(POOL_WINDOWS):
        lo = jnp.clip(t - w // 2, 0, n)
        hi = jnp.clip(t + w // 2, 0, n)
        sl = slice(gi * POOL_GROUP_CH, (gi + 1) * POOL_GROUP_CH)
        cs_g = cs[..., sl]
        cnt = (hi - lo).astype(jnp.float32)[None, :, None]
        mean = (cs_g[:, hi] - cs_g[:, lo]) / cnt - uf[..., sl]
        outs.append(jnp.einsum('bnc,cd->bnd', mean, pool_w[gi]))
    return jnp.concatenate(outs, axis=-1) * pool_scale


def mixers(h, lp, rope, ctx):
    b, n, _ = h.shape
    proj = h @ lp['w_in']
    q_a, kv_a, k_rope, conv_u, conv_bg, conv_cg, ssm_u, pool_u, gate_in = jnp.split(proj, IN_SPLITS, axis=-1)

    q = (rms_norm(q_a, lp['q_a_norm_g']) @ lp['w_uq']).reshape(b, n, MLA_HEADS, QK_HEAD)
    q = rms_norm(q, lp['q_norm_g'])
    if rope is not None:
        q = apply_axial_rope(q, rope)
    ckv = rms_norm(kv_a, lp['kv_a_norm_g'])
    k, v = mla_keys_values(ckv, k_rope, lp, rope)
    if ctx is not None:
        ctx_ckv, ctx_krope, ctx_state = ctx
        k_c, v_c = mla_keys_values(ctx_ckv, ctx_krope, lp, None)
        k = jnp.concatenate([k, k_c.astype(k.dtype)], axis=1)
        v = jnp.concatenate([v, v_c.astype(v.dtype)], axis=1)
    branch_a = block_attention(q, k, v) @ lp['w_mla_o']

    branch_b = (conv_bg * short_conv(conv_cg * conv_u, lp['conv_w'], lp['conv_b'])) @ lp['w_conv_o']

    u = ssm_u.reshape(b, n, SSM_GROUPS, SSM_GROUP_CH)
    if ctx is None:
        h0f = jnp.zeros((b, SSM_GROUPS, SSM_STATE), jnp.complex64)
        h0b = h0f
    else:
        st = ctx_state.astype(jnp.float32)
        h0f = lax.complex(st[:, 0, 0], st[:, 0, 1])
        h0b = lax.complex(st[:, 1, 0], st[:, 1, 1])
    y_f, hf = s5_direction(u, lp['ssm_lam_re'][0], lp['ssm_lam_im'][0], lp['ssm_log_step'][0],
                           lp['ssm_b_re'][0], lp['ssm_b_im'][0], lp['ssm_c_re'][0], lp['ssm_c_im'][0], h0f, False)
    y_b, hb = s5_direction(u, lp['ssm_lam_re'][1], lp['ssm_lam_im'][1], lp['ssm_log_step'][1],
                           lp['ssm_b_re'][1], lp['ssm_b_im'][1], lp['ssm_c_re'][1], lp['ssm_c_im'][1], h0b, True)
    y_ssm = (y_f + y_b).reshape(b, n, SSM_W) + lp['ssm_d'] * ssm_u
    glu_a, glu_g = jnp.split(y_ssm @ lp['w_glu'], 2, axis=-1)
    branch_c = glu_a * jax.nn.sigmoid(glu_g)

    branch_d = multiscale_pool(pool_u, lp['pool_w'], lp['pool_scale']) @ lp['w_pool_o']

    gates = jax.nn.sigmoid(gate_in.reshape(b, n, N_BRANCH, D_MODEL).astype(jnp.float32))
    merged = (gates[:, :, 0] * branch_a + gates[:, :, 1] * branch_b
              + gates[:, :, 2] * branch_c + gates[:, :, 3] * branch_d)
    out = merged @ lp['w_o']
    if ctx is None:
        state = jnp.stack([jnp.stack([jnp.real(hf), jnp.imag(hf)], axis=1),
                           jnp.stack([jnp.real(hb), jnp.imag(hb)], axis=1)], axis=1)
        return out, (ckv, k_rope, state)
    return out, None


def trunk_layer(x, mod, lp, rope, ctx):
    shift1, scale1, gate1, shift2, scale2, gate2 = jnp.split(mod, 6, axis=-1)
    h = rms_norm(x, lp['norm1_g']) * (1.0 + scale1) + shift1
    mix, cache = mixers(h, lp, rope, ctx)
    x = x + gate1 * mix
    h = rms_norm(x, lp['norm2_g']) * (1.0 + scale2) + shift2
    x = x + gate2 * (jnp.square(jax.nn.relu(h @ lp['w_mlp1'])) @ lp['w_mlp2'])
    return x, cache


def setup_inputs(seed: int = 0) -> dict:
    key = jax.random.key(seed)
    ks = iter(list(jax.random.split(key, 48)))
    f32 = jnp.float32

    def nrm(shape, scale=1.0):
        return jax.random.normal(next(ks), shape, f32) * scale

    L, G, N, CG = DEPTH, SSM_GROUPS, SSM_STATE, SSM_GROUP_CH
    n_idx = jnp.arange(N, dtype=f32)
    return {
        'x_prompt': nrm((BATCH, SEQ, D_MODEL)),
        'x_sample': nrm((DEC_BATCH, DEC_SEQ, D_MODEL)),
        'c': nrm((DEC_BATCH, D_MODEL)),
        'cache_ckv': nrm((DEC_BATCH, DEPTH, PAST_LEN, KV_LORA)),
        'cache_krope': nrm((DEC_BATCH, DEPTH, PAST_LEN, ROPE_DIM)),
        'state_ssm': nrm((DEC_BATCH, DEPTH, 2, 2, G, N), 0.1),
        'c_ctx': nrm((D_MODEL,)),
        'w_ada': nrm((L, D_MODEL, 6 * D_MODEL), 0.5 * D_MODEL ** -0.5),
        'b_ada': nrm((L, 6 * D_MODEL), 0.02),
        'norm1_g': 1.0 + nrm((L, D_MODEL), 0.02),
        'norm2_g': 1.0 + nrm((L, D_MODEL), 0.02),
        'w_in': nrm((L, D_MODEL, IN_COLS), D_MODEL ** -0.5),
        'q_a_norm_g': 1.0 + nrm((L, Q_LORA), 0.02),
        'kv_a_norm_g': 1.0 + nrm((L, KV_LORA), 0.02),
        'w_uq': nrm((L, Q_LORA, MLA_HEADS * QK_HEAD), Q_LORA ** -0.5),
        'w_ukv': nrm((L, KV_LORA, MLA_HEADS * (QK_NOPE + V_HEAD)), KV_LORA ** -0.5),
        'q_norm_g': 1.0 + nrm((L, QK_HEAD), 0.02),
        'k_norm_g': 1.0 + nrm((L, QK_HEAD), 0.02),
        'w_mla_o': nrm((L, MLA_HEADS * V_HEAD, D_MODEL), (MLA_HEADS * V_HEAD) ** -0.5),
        'conv_w': nrm((L, 3, CONV_W), 3 ** -0.5),
        'conv_b': nrm((L, CONV_W), 0.02),
        'w_conv_o': nrm((L, CONV_W, D_MODEL), CONV_W ** -0.5),
        'ssm_lam_re': -0.5 + nrm((L, 2, G, N), 0.01),
        'ssm_lam_im': math.pi * n_idx + nrm((L, 2, G, N), 0.01),
        'ssm_log_step': jax.random.uniform(next(ks), (L, 2, G), f32, math.log(1e-3), math.log(1e-1)),
        'ssm_b_re': nrm((L, 2, G, N, CG), (2 * CG) ** -0.5),
        'ssm_b_im': nrm((L, 2, G, N, CG), (2 * CG) ** -0.5),
        'ssm_c_re': nrm((L, 2, G, CG, N), (2 * N) ** -0.5),
        'ssm_c_im': nrm((L, 2, G, CG, N), (2 * N) ** -0.5),
        'ssm_d': nrm((L, SSM_W)),
        'w_glu': nrm((L, SSM_W, 2 * D_MODEL), SSM_W ** -0.5),
        'pool_w': nrm((L, 4, POOL_GROUP_CH, POOL_GROUP_CH), POOL_GROUP_CH ** -0.5),
        'pool_scale': 1.0 + nrm((L, POOL_W), 0.1),
        'w_pool_o': nrm((L, POOL_W, D_MODEL), POOL_W ** -0.5),
        'w_o': nrm((L, D_MODEL, D_MODEL), D_MODEL ** -0.5),
        'w_mlp1': nrm((L, D_MODEL, MLP_HIDDEN), D_MODEL ** -0.5),
        'w_mlp2': nrm((L, MLP_HIDDEN, D_MODEL), MLP_HIDDEN ** -0.5),
    }


def reference(x_prompt, x_sample, c, cache_ckv, cache_krope, state_ssm, c_ctx, w_ada, b_ada,
              norm1_g, norm2_g, w_in, q_a_norm_g, kv_a_norm_g, w_uq, w_ukv, q_norm_g, k_norm_g,
              w_mla_o, conv_w, conv_b, w_conv_o, ssm_lam_re, ssm_lam_im, ssm_log_step,
              ssm_b_re, ssm_b_im, ssm_c_re, ssm_c_im, ssm_d, w_glu, pool_w, pool_scale, w_pool_o,
              w_o, w_mlp1, w_mlp2):
    rows = x_sample.shape[1] // GRID_W
    rope = axial_rope_tables(rows)
    silu_ctx = jax.nn.silu(c_ctx)
    silu_c = jax.nn.silu(c)
    y_prompt = x_prompt
    y_sample = x_sample
    ckv_list, krope_list, ssm_list = [], [], []
    for l in range(DEPTH):
        lp = dict(norm1_g=norm1_g[l], norm2_g=norm2_g[l], w_in=w_in[l],
                  q_a_norm_g=q_a_norm_g[l], kv_a_norm_g=kv_a_norm_g[l], w_uq=w_uq[l], w_ukv=w_ukv[l],
                  q_norm_g=q_norm_g[l], k_norm_g=k_norm_g[l], w_mla_o=w_mla_o[l],
                  conv_w=conv_w[l], conv_b=conv_b[l], w_conv_o=w_conv_o[l],
                  ssm_lam_re=ssm_lam_re[l], ssm_lam_im=ssm_lam_im[l], ssm_log_step=ssm_log_step[l],
                  ssm_b_re=ssm_b_re[l], ssm_b_im=ssm_b_im[l], ssm_c_re=ssm_c_re[l], ssm_c_im=ssm_c_im[l],
                  ssm_d=ssm_d[l], w_glu=w_glu[l], pool_w=pool_w[l], pool_scale=pool_scale[l],
                  w_pool_o=w_pool_o[l], w_o=w_o[l], w_mlp1=w_mlp1[l], w_mlp2=w_mlp2[l])
        mod_ctx = (silu_ctx @ w_ada[l] + b_ada[l])[None, None, :]
        y_prompt, (ckv_l, krope_l, ssm_l) = trunk_layer(y_prompt, mod_ctx, lp, None, None)
        ckv_list.append(ckv_l)
        krope_list.append(krope_l)
        ssm_list.append(ssm_l)
        mod_lat = (silu_c @ w_ada[l] + b_ada[l])[:, None, :]
        y_sample, _ = trunk_layer(y_sample, mod_lat, lp, rope,
                                  (cache_ckv[:, l], cache_krope[:, l], state_ssm[:, l]))
    new_ckv = jnp.stack(ckv_list, axis=1)
    new_krope = jnp.stack(krope_list, axis=1)
    new_ssm = jnp.stack(ssm_list, axis=1)
    return (y_prompt, y_sample, new_ckv, new_krope, new_ssm)
```

```python
import functools
import math

import jax
import jax.numpy as jnp
from jax import lax
from jax.experimental import pallas as pl
from jax.experimental.pallas import tpu as pltpu

F32 = jnp.float32
BF16 = jnp.bfloat16

GRID_W = 64
N_BRANCH = 4
MLA_HEADS = 8
QK_NOPE = 64
ROPE_DIM = 32
QK_HEAD = QK_NOPE + ROPE_DIM
V_HEAD = 64
ROPE_THETA = 10000.0
SSM_GROUP_CH = 16
POOL_WINDOWS = (2, 4, 8, 16)
EPS = 1e-6
LOG2E = 1.4426950408889634

LANES = 128
SUBLANES = 8
HEAD_PAD = LANES
MIB = 1024 * 1024


def _cparams(sem, vmem_mib):
    return pltpu.CompilerParams(dimension_semantics=sem, vmem_limit_bytes=vmem_mib * MIB)


def _dot(a, b):
    return jnp.dot(a, b, preferred_element_type=F32)


def _mod_spec(chunk, d, tm, seq_len, per_batch):
    def imap(i, *_):
        row = (i * tm) // seq_len if per_batch else 0
        return (row * 6 + chunk, 0, 0)
    return pl.BlockSpec((1, 1, d), imap)


def _ada_kernel(c_ref, w_ref, b_ref, o_ref):
    cv = c_ref[...]
    s = cv * jax.nn.sigmoid(cv)
    o_ref[0] = _dot(s.astype(BF16), w_ref[0].astype(BF16)) + b_ref[0]


def _ada_call(cvec, w_ada, b_ada):
    depth, d, n = w_ada.shape
    rows = cvec.shape[0]
    tn = 1024
    return pl.pallas_call(
        _ada_kernel,
        out_shape=jax.ShapeDtypeStruct((depth, rows, n), F32),
        grid=(depth, n // tn),
        in_specs=[
            pl.BlockSpec((rows, d), lambda l, j: (0, 0)),
            pl.BlockSpec((1, d, tn), lambda l, j: (l, 0, j)),
            pl.BlockSpec((1, 1, tn), lambda l, j: (l, 0, j)),
        ],
        out_specs=pl.BlockSpec((1, rows, tn), lambda l, j: (l, 0, j)),
        compiler_params=_cparams(("parallel", "parallel"), 40),
        name="ada_mod",
    )(cvec, w_ada, b_ada.reshape(depth, 1, n))


def _modnorm(x, g, scale, shift):
    ms = jnp.mean(x * x, axis=-1, keepdims=True)
    return (x * lax.rsqrt(ms + EPS) * g) * (1.0 + scale) + shift


def _inproj_kernel(x_ref, sh_ref, sc_ref, g_ref, w_ref, o_ref, h_scr):
    @pl.when(pl.program_id(1) == 0)
    def _():
        h_scr[...] = _modnorm(x_ref[...], g_ref[...], sc_ref[0], sh_ref[0]).astype(BF16)

    o_ref[...] = _dot(h_scr[...], w_ref[...])


def _inproj_call(x, mod, g, w, seq_len, per_batch):
    t, d = x.shape
    n = w.shape[1]
    tm, tn = 1024, 896
    mspec = functools.partial(_mod_spec, d=d, tm=tm, seq_len=seq_len, per_batch=per_batch)
    return pl.pallas_call(
        _inproj_kernel,
        out_shape=jax.ShapeDtypeStruct((t, n), F32),
        grid=(t // tm, n // tn),
        in_specs=[
            pl.BlockSpec((tm, d), lambda i, j: (i, 0)),
            mspec(0), mspec(1),
            pl.BlockSpec((1, d), lambda i, j: (0, 0)),
            pl.BlockSpec((d, tn), lambda i, j: (0, j)),
        ],
        out_specs=pl.BlockSpec((tm, tn), lambda i, j: (i, j)),
        scratch_shapes=[pltpu.VMEM((tm, d), BF16)],
        compiler_params=_cparams(("parallel", "arbitrary"), 48),
        name="in_proj",
    )(x, mod, mod, g, w)


def _rms(x, g, width):
    ms = jnp.sum(x * x, axis=-1, keepdims=True) * (1.0 / width)
    return x * lax.rsqrt(ms + EPS) * g


def _heads_norm(x, g, rope, mult):
    outs = []
    for h in range(MLA_HEADS):
        sl = slice(h * HEAD_PAD, (h + 1) * HEAD_PAD)
        y = _rms(x[:, sl], g[:, sl], QK_HEAD)
        if rope is not None:
            c, s1, s2 = rope
            y = (y * c + pltpu.roll(y, HEAD_PAD - 8, axis=1) * s1
                 + pltpu.roll(y, 8, axis=1) * s2)
        if mult != 1.0:
            y = y * mult
        outs.append(y)
    return jnp.concatenate(outs, axis=1)


def _qkprep_kernel(*refs, use_rope, q_mult):
    if use_rope:
        (qa_ref, kva_ref, kpe_ref, gqa_ref, gkva_ref, gq_ref, gk_ref, wuq_ref, wuk_ref, wuv_ref,
         c_ref, s1_ref, s2_ref, q_out, k_out, v_out, ckv_out) = refs
        rope = (c_ref[...], s1_ref[...], s2_ref[...])
    else:
        (qa_ref, kva_ref, kpe_ref, gqa_ref, gkva_ref, gq_ref, gk_ref, wuq_ref, wuk_ref, wuv_ref,
         q_out, k_out, v_out, ckv_out) = refs
        rope = None
    qa = qa_ref[...]
    qa_n = _rms(qa, gqa_ref[...], qa.shape[-1]).astype(BF16)
    q = _dot(qa_n, wuq_ref[...])
    q_out[...] = _heads_norm(q, gq_ref[...], rope, q_mult).astype(BF16)

    kva = kva_ref[...]
    ckv = _rms(kva, gkva_ref[...], kva.shape[-1])
    ckv_out[...] = ckv
    ckv_b = ckv.astype(BF16)
    k = _dot(ckv_b, wuk_ref[...]) + kpe_ref[...]
    k_out[...] = _heads_norm(k, gk_ref[...], rope, 1.0).astype(BF16)
    v_out[...] = _dot(ckv_b, wuv_ref[...]).astype(BF16)


def _qkprep_call(p, lw, rope, seq_len, cols):
    t = p.shape[0]
    tm = 512
    hq = MLA_HEADS * HEAD_PAD
    hv = MLA_HEADS * V_HEAD
    qlora = lw["w_uq"].shape[0]
    kvlora = lw["w_uk"].shape[0]
    const = lambda i: (0, 0)
    in_specs = [
        pl.BlockSpec((tm, qlora), lambda i: (i, cols["q_a"] // qlora)),
        pl.BlockSpec((tm, kvlora), lambda i: (i, cols["kv_a"] // kvlora)),
        pl.BlockSpec((tm, hq), lambda i: (i, cols["kpe"] // hq)),
        pl.BlockSpec((1, qlora), const),
        pl.BlockSpec((1, kvlora), const),
        pl.BlockSpec((1, hq), const),
        pl.BlockSpec((1, hq), const),
        pl.BlockSpec((qlora, hq), const),
        pl.BlockSpec((kvlora, hq), const),
        pl.BlockSpec((kvlora, hv), const),
    ]
    args = [p, p, p, lw["q_a_norm_g"], lw["kv_a_norm_g"], lw["q_norm_g"], lw["k_norm_g"],
            lw["w_uq"], lw["w_uk"], lw["w_uv"]]
    if rope is not None:
        nblk = seq_len // tm
        in_specs += [pl.BlockSpec((tm, HEAD_PAD), lambda i: (i % nblk, 0))] * 3
        args += list(rope)
    q_mult = LOG2E * QK_HEAD ** -0.5
    return pl.pallas_call(
        functools.partial(_qkprep_kernel, use_rope=rope is not None, q_mult=q_mult),
        out_shape=(jax.ShapeDtypeStruct((t, hq), BF16), jax.ShapeDtypeStruct((t, hq), BF16),
                   jax.ShapeDtypeStruct((t, hv), BF16), jax.ShapeDtypeStruct((t, kvlora), F32)),
        grid=(t // tm,),
        in_specs=in_specs,
        out_specs=(pl.BlockSpec((tm, hq), lambda i: (i, 0)), pl.BlockSpec((tm, hq), lambda i: (i, 0)),
                   pl.BlockSpec((tm, hv), lambda i: (i, 0)), pl.BlockSpec((tm, kvlora), lambda i: (i, 0))),
        compiler_params=_cparams(("parallel",), 40),
        name="qk_prep",
    )(*args)


def _ctxprep_kernel(ckv_ref, kpe_ref, gk_ref, wuk_ref, wuv_ref, k_out, v_out):
    ckv_b = ckv_ref[...].astype(BF16)
    k = _dot(ckv_b, wuk_ref[...]) + kpe_ref[...]
    k_out[...] = _heads_norm(k, gk_ref[...], None, 1.0).astype(BF16)
    v_out[...] = _dot(ckv_b, wuv_ref[...]).astype(BF16)


def _ctxprep_call(ckv, kpe, lw):
    t, kvlora = ckv.shape
    tm = 512
    hq = MLA_HEADS * HEAD_PAD
    hv = MLA_HEADS * V_HEAD
    const = lambda i: (0, 0)
    return pl.pallas_call(
        _ctxprep_kernel,
        out_shape=(jax.ShapeDtypeStruct((t, hq), BF16), jax.ShapeDtypeStruct((t, hv), BF16)),
        grid=(t // tm,),
        in_specs=[
            pl.BlockSpec((tm, kvlora), lambda i: (i, 0)),
            pl.BlockSpec((tm, hq), lambda i: (i, 0)),
            pl.BlockSpec((1, hq), const),
            pl.BlockSpec((kvlora, hq), const),
            pl.BlockSpec((kvlora, hv), const),
        ],
        out_specs=(pl.BlockSpec((tm, hq), lambda i: (i, 0)), pl.BlockSpec((tm, hv), lambda i: (i, 0))),
        compiler_params=_cparams(("parallel",), 40),
        name="ctx_prep",
    )(ckv, kpe, lw["k_norm_g"], lw["w_uk"], lw["w_uv"])


def _attn_kernel(*refs, has_ctx):
    if has_ctx:
        q_ref, ko_ref, vo_ref, kc_ref, vc_ref, o_ref = refs
    else:
        q_ref, ko_ref, vo_ref, o_ref = refs
    contract_last = (((1,), (1,)), ((), ()))
    outs = []
    for hh in range(2):
        sl = slice(hh * HEAD_PAD, (hh + 1) * HEAD_PAD)
        q = q_ref[:, sl]
        s_o = lax.dot_general(q, ko_ref[:, sl], contract_last, preferred_element_type=F32)
        m = jnp.max(s_o, axis=-1, keepdims=True)
        if has_ctx:
            s_c = lax.dot_general(q, kc_ref[:, sl], contract_last, preferred_element_type=F32)
            m = jnp.maximum(m, jnp.max(s_c, axis=-1, keepdims=True))
        p_o = jnp.exp2(s_o - m)
        l = jnp.sum(p_o, axis=-1, keepdims=True)
        acc = _dot(p_o.astype(BF16), vo_ref[...])
        if has_ctx:
            p_c = jnp.exp2(s_c - m)
            l = l + jnp.sum(p_c, axis=-1, keepdims=True)
            acc = acc + _dot(p_c.astype(BF16), vc_ref[...])
        outs.append(acc / l)
    lane = lax.broadcasted_iota(jnp.int32, outs[0].shape, 1)
    o_ref[...] = jnp.where(lane < V_HEAD, outs[0], outs[1]).astype(BF16)


def _attn_call(q, k, v, kc, vc, batch, seq_len, ctx_len):
    t = q.shape[0]
    tq = 256
    nq = seq_len // tq
    npair = MLA_HEADS // 2
    qw = 2 * HEAD_PAD
    vw = 2 * V_HEAD
    in_specs = [
        pl.BlockSpec((tq, qw), lambda b, h, i: (b * nq + i, h)),
        pl.BlockSpec((seq_len, qw), lambda b, h, i: (b, h)),
        pl.BlockSpec((seq_len, vw), lambda b, h, i: (b, h)),
    ]
    args = [q, k, v]
    if kc is not None:
        in_specs += [pl.BlockSpec((ctx_len, qw), lambda b, h, i: (b, h)),
                     pl.BlockSpec((ctx_len, vw), lambda b, h, i: (b, h))]
        args += [kc, vc]
    return pl.pallas_call(
        functools.partial(_attn_kernel, has_ctx=kc is not None),
        out_shape=jax.ShapeDtypeStruct((t, MLA_HEADS * V_HEAD), BF16),
        grid=(batch, npair, nq),
        in_specs=in_specs,
        out_specs=pl.BlockSpec((tq, vw), lambda b, h, i: (b * nq + i, h)),
        compiler_params=_cparams(("parallel", "parallel", "arbitrary"), 48),
        name="attention",
    )(*args)


def _convpool_kernel(cu_ref, bg_ref, cg_ref, pu_ref, cw_ref, cb_ref, pw_ref, ps_ref, zb_ref, zd_ref):
    n = cu_ref.shape[0]
    row = lax.broadcasted_iota(jnp.int32, cu_ref.shape, 0)

    def down(x, k):
        return jnp.where(row >= k, pltpu.roll(x, k, axis=0), 0.0)

    def up(x, k):
        return jnp.where(row < n - k, pltpu.roll(x, n - k, axis=0), 0.0)

    v = cg_ref[...] * cu_ref[...]
    conv = (down(v, 1) * cw_ref[0:1, :] + v * cw_ref[1:2, :] + up(v, 1) * cw_ref[2:3, :]
            + cb_ref[...])
    zb_ref[...] = (bg_ref[...] * conv).astype(BF16)

    u = pu_ref[...]
    rowf = row.astype(F32)
    grp = pl.program_id(1)
    for gi, win in enumerate(POOL_WINDOWS):
        @pl.when(grp == gi)
        def _(win=win):
            half = win // 2
            fwd = u
            bwd = u
            k = 1
            while k < half:
                fwd = fwd + up(fwd, k)
                bwd = bwd + down(bwd, k)
                k *= 2
            total = fwd + down(bwd, 1)
            cnt = jnp.minimum(rowf + half, float(n)) - jnp.maximum(rowf - half, 0.0)
            mean = total / cnt - u
            zd_ref[...] = (_dot(mean.astype(BF16), pw_ref[0]) * ps_ref[...]).astype(BF16)


def _convpool_call(p, lw, batch, seq_len, cols):
    t = p.shape[0]
    cw = LANES
    nblk = lw["conv_w"].shape[1] // cw

    def pspec(name):
        base = cols[name] // cw
        return pl.BlockSpec((seq_len, cw), lambda b, g: (b, base + g))

    vec = lambda b, g: (0, g)
    return pl.pallas_call(
        _convpool_kernel,
        out_shape=(jax.ShapeDtypeStruct((t, nblk * cw), BF16), jax.ShapeDtypeStruct((t, nblk * cw), BF16)),
        grid=(batch, nblk),
        in_specs=[pspec("conv_u"), pspec("conv_bg"), pspec("conv_cg"), pspec("pool_u"),
                  pl.BlockSpec((3, cw), vec), pl.BlockSpec((1, cw), vec),
                  pl.BlockSpec((1, cw, cw), lambda b, g: (g, 0, 0)), pl.BlockSpec((1, cw), vec)],
        out_specs=(pl.BlockSpec((seq_len, cw), lambda b, g: (b, g)),
                   pl.BlockSpec((seq_len, cw), lambda b, g: (b, g))),
        compiler_params=_cparams(("parallel", "parallel"), 40),
        name="conv_pool",
    )(p, p, p, p, lw["conv_w"], lw["conv_b"], lw["pool_w"], lw["pool_scale"])


def _ssm_kernel(u_ref, h0_ref, a_ref, bm_ref, cm_ref, y_ref, hf_ref, xs_scr, st_scr, *, chunk, nchunks):
    d = pl.program_id(0)
    c = pl.program_id(2)
    half_w = st_scr.shape[1] // 2
    part_w = half_w // 2
    uw = u_ref.shape[-1] // 2

    @pl.when(c == 0)
    def _():
        st_scr[...] = h0_ref[0]

    u = u_ref[...].reshape(chunk * SUBLANES, 2 * uw).astype(BF16)
    for half in range(2):
        xs_scr[:, half * half_w:(half + 1) * half_w] = _dot(u[:, half * uw:(half + 1) * uw], bm_ref[0, half])

    gw = 512
    for half in range(2):
        for part in range(part_w // gw):
            re0 = half * half_w + part * gw
            im0 = re0 + part_w
            a_re = a_ref[0, :, re0:re0 + gw]
            a_im = a_ref[0, :, im0:im0 + gw]

            def body(t, carry, re0=re0, im0=im0, a_re=a_re, a_im=a_im):
                x_re, x_im = carry
                tt = t + d * (chunk - 1 - 2 * t)
                r0 = pl.multiple_of(tt * SUBLANES, SUBLANES)
                n_re = (a_re * x_re - a_im * x_im) + xs_scr[pl.ds(r0, SUBLANES), re0:re0 + gw]
                n_im = (a_re * x_im + a_im * x_re) + xs_scr[pl.ds(r0, SUBLANES), im0:im0 + gw]
                xs_scr[pl.ds(r0, SUBLANES), re0:re0 + gw] = n_re
                xs_scr[pl.ds(r0, SUBLANES), im0:im0 + gw] = n_im
                return n_re, n_im

            x_re, x_im = lax.fori_loop(
                0, chunk, body, (st_scr[:, re0:re0 + gw], st_scr[:, im0:im0 + gw]), unroll=2)
            st_scr[:, re0:re0 + gw] = x_re
            st_scr[:, im0:im0 + gw] = x_im

    for half in range(2):
        xb = xs_scr[:, half * half_w:(half + 1) * half_w].astype(BF16)
        y_ref[0, :, :, half * uw:(half + 1) * uw] = _dot(xb, cm_ref[0, half]).reshape(chunk, SUBLANES, uw)

    @pl.when(c == nchunks - 1)
    def _():
        hf_ref[0] = st_scr[...]


def _ssm_call(u_tm, h0, lw):
    seq_len, batch, w = u_tm.shape
    s = h0.shape[-1]
    chunk = 128
    nchunks = seq_len // chunk
    nbg = batch // SUBLANES

    def cidx(d, c):
        return c + d * (nchunks - 1 - 2 * c)

    return pl.pallas_call(
        functools.partial(_ssm_kernel, chunk=chunk, nchunks=nchunks),
        out_shape=(jax.ShapeDtypeStruct((2, seq_len, batch, w), F32),
                   jax.ShapeDtypeStruct((2, batch, s), F32)),
        grid=(2, nbg, nchunks),
        in_specs=[
            pl.BlockSpec((chunk, SUBLANES, w), lambda d, b, c: (cidx(d, c), b, 0)),
            pl.BlockSpec((1, SUBLANES, s), lambda d, b, c: (d, b, 0)),
            pl.BlockSpec((1, SUBLANES, s), lambda d, b, c: (d, 0, 0)),
            pl.BlockSpec((1, 2, w // 2, s // 2), lambda d, b, c: (d, 0, 0, 0)),
            pl.BlockSpec((1, 2, s // 2, w // 2), lambda d, b, c: (d, 0, 0, 0)),
        ],
        out_specs=(pl.BlockSpec((1, chunk, SUBLANES, w), lambda d, b, c: (d, cidx(d, c), b, 0)),
                   pl.BlockSpec((1, SUBLANES, s), lambda d, b, c: (d, b, 0))),
        scratch_shapes=[pltpu.VMEM((chunk * SUBLANES, s), F32), pltpu.VMEM((SUBLANES, s), F32)],
        compiler_params=_cparams(("arbitrary", "arbitrary", "arbitrary"), 48),
        name="s5_scan",
    )(u_tm, h0, lw["ssm_a"], lw["ssm_bm"], lw["ssm_cm"])


def _merge_kernel(x_ref, sh_ref, sc_ref, gt_ref, g_ref, za_ref, zb_ref, yf_ref, yb_ref, su_ref, sd_ref,
                  zd_ref, wg0, wg1, wg2, wg3, wa, wb, wca, wcg, wd, wo, o_ref, h_scr, y_scr, acc_scr):
    j = pl.program_id(1)

    @pl.when(j == 0)
    def _():
        h_scr[...] = _modnorm(x_ref[...], g_ref[...], sc_ref[0], sh_ref[0]).astype(BF16)
        y_scr[...] = ((yf_ref[0] + yb_ref[0]) + sd_ref[...] * su_ref[...]).astype(BF16)

    h = h_scr[...]
    y = y_scr[...]

    def gate(w_ref):
        return jax.nn.sigmoid(_dot(h, w_ref[...]))

    merged = gate(wg0) * _dot(za_ref[...], wa[...])
    merged = merged + gate(wg1) * _dot(zb_ref[...], wb[...])
    merged = merged + gate(wg2) * (_dot(y, wca[...]) * jax.nn.sigmoid(_dot(y, wcg[...])))
    merged = merged + gate(wg3) * _dot(zd_ref[...], wd[...])
    contrib = _dot(merged.astype(BF16), wo[...])

    @pl.when(j == 0)
    def _():
        acc_scr[...] = contrib

    @pl.when(j > 0)
    def _():
        acc_scr[...] += contrib

    @pl.when(j == pl.num_programs(1) - 1)
    def _():
        o_ref[...] = x_ref[...] + gt_ref[0] * acc_scr[...]


def _merge_call(x, mod, p, za, zb, y2, zd, lw, seq_len, per_batch, cols):
    t, d = x.shape
    bw = za.shape[1]
    tm, tn = 512, 256
    nj = d // tn
    mspec = functools.partial(_mod_spec, d=d, tm=tm, seq_len=seq_len, per_batch=per_batch)
    row = lambda i, j: (i, 0)
    colj = lambda i, j: (0, j)
    su_blk = cols["ssm_u"] // bw

    def gspec(k):
        return pl.BlockSpec((d, tn), lambda i, j: (0, k * nj + j))

    in_specs = [
        pl.BlockSpec((tm, d), row), mspec(0), mspec(1), mspec(2),
        pl.BlockSpec((1, d), lambda i, j: (0, 0)),
        pl.BlockSpec((tm, bw), row), pl.BlockSpec((tm, bw), row),
        pl.BlockSpec((1, tm, bw), lambda i, j: (0, i, 0)), pl.BlockSpec((1, tm, bw), lambda i, j: (1, i, 0)),
        pl.BlockSpec((tm, bw), lambda i, j: (i, su_blk)),
        pl.BlockSpec((1, bw), lambda i, j: (0, 0)),
        pl.BlockSpec((tm, bw), row),
        gspec(0), gspec(1), gspec(2), gspec(3),
        pl.BlockSpec((bw, tn), colj), pl.BlockSpec((bw, tn), colj),
        pl.BlockSpec((bw, tn), colj), pl.BlockSpec((bw, tn), lambda i, j: (0, nj + j)),
        pl.BlockSpec((bw, tn), colj),
        pl.BlockSpec((tn, d), lambda i, j: (j, 0)),
    ]
    wg = lw["w_gate"]
    return pl.pallas_call(
        _merge_kernel,
        out_shape=jax.ShapeDtypeStruct((t, d), F32),
        grid=(t // tm, nj),
        in_specs=in_specs,
        out_specs=pl.BlockSpec((tm, d), row),
        scratch_shapes=[pltpu.VMEM((tm, d), BF16), pltpu.VMEM((tm, bw), BF16), pltpu.VMEM((tm, d), F32)],
        compiler_params=_cparams(("parallel", "arbitrary"), 56),
        name="merge_out",
    )(x, mod, mod, mod, lw["norm1_g"], za, zb, y2, y2, p, lw["ssm_d"], zd,
      wg, wg, wg, wg, lw["w_mla_o"], lw["w_conv_o"], lw["w_glu"], lw["w_glu"], lw["w_pool_o"], lw["w_o"])


def _mlp_kernel(x_ref, sh_ref, sc_ref, gt_ref, g_ref, w1_ref, w2_ref, o_ref, h_scr, acc_scr):
    k = pl.program_id(1)

    @pl.when(k == 0)
    def _():
        h_scr[...] = _modnorm(x_ref[...], g_ref[...], sc_ref[0], sh_ref[0]).astype(BF16)

    hid = jnp.square(jnp.maximum(_dot(h_scr[...], w1_ref[...]), 0.0))
    contrib = _dot(hid.astype(BF16), w2_ref[...])

    @pl.when(k == 0)
    def _():
        acc_scr[...] = contrib

    @pl.when(k > 0)
    def _():
        acc_scr[...] += contrib

    @pl.when(k == pl.num_programs(1) - 1)
    def _():
        o_ref[...] = x_ref[...] + gt_ref[0] * acc_scr[...]


def _mlp_call(x, mod, lw, seq_len, per_batch):
    t, d = x.shape
    hidden = lw["w_mlp1"].shape[1]
    tm, tk = 512, 512
    mspec = functools.partial(_mod_spec, d=d, tm=tm, seq_len=seq_len, per_batch=per_batch)
    row = lambda i, k: (i, 0)
    return pl.pallas_call(
        _mlp_kernel,
        out_shape=jax.ShapeDtypeStruct((t, d), F32),
        grid=(t // tm, hidden // tk),
        in_specs=[
            pl.BlockSpec((tm, d), row), mspec(3), mspec(4), mspec(5),
            pl.BlockSpec((1, d), lambda i, k: (0, 0)),
            pl.BlockSpec((d, tk), lambda i, k: (0, k)),
            pl.BlockSpec((tk, d), lambda i, k: (k, 0)),
        ],
        out_specs=pl.BlockSpec((tm, d), row),
        scratch_shapes=[pltpu.VMEM((tm, d), BF16), pltpu.VMEM((tm, d), F32)],
        compiler_params=_cparams(("parallel", "arbitrary"), 48),
        name="mlp",
    )(x, mod, mod, mod, lw["norm2_g"], lw["w_mlp1"], lw["w_mlp2"])


def _pad_heads(w, per_head, lo):
    lead = w.shape[:-1]
    w = w.reshape(lead + (MLA_HEADS, per_head))
    w = jnp.pad(w, [(0, 0)] * len(lead) + [(0, 0), (lo, HEAD_PAD - lo - per_head)])
    return w.reshape(lead + (MLA_HEADS * HEAD_PAD,))


def _block_diag(blocks):
    g, r, c = blocks.shape[-3:]
    eye = jnp.eye(g, dtype=blocks.dtype)
    out = jnp.einsum("...grc,gk->...grkc", blocks, eye)
    return out.reshape(blocks.shape[:-3] + (g * r, g * c))


def _prep_layer(l, prm, sizes):
    q_lora, kv_lora, conv_w, ssm_w, pool_w_ = sizes
    w_in = prm["w_in"][l]
    d = w_in.shape[0]
    offs = {}
    o = 0
    for name, width in (("q_a", q_lora), ("kv_a", kv_lora), ("k_rope", ROPE_DIM), ("conv_u", conv_w),
                        ("conv_bg", conv_w), ("conv_cg", conv_w), ("ssm_u", ssm_w), ("pool_u", pool_w_)):
        offs[name] = (o, o + width)
        o += width
    gate0 = o
    take = lambda name: w_in[:, offs[name][0]:offs[name][1]]
    w_kr = take("k_rope")
    kpe_tiled = _pad_heads(jnp.tile(w_kr, (1, MLA_HEADS)), ROPE_DIM, QK_NOPE)
    kr_pad = jnp.pad(w_kr, ((0, 0), (0, LANES - ROPE_DIM)))
    pieces = [("q_a", take("q_a")), ("conv_u", take("conv_u")), ("conv_bg", take("conv_bg")),
              ("conv_cg", take("conv_cg")), ("ssm_u", take("ssm_u")), ("pool_u", take("pool_u")),
              ("kpe", kpe_tiled), ("kv_a", take("kv_a")), ("k_rope", kr_pad)]
    cols = {}
    o = 0
    for name, w in pieces:
        cols[name] = o
        o += w.shape[1]
    w_small = jnp.concatenate([w for _, w in pieces], axis=1).astype(BF16)

    w_ukv = prm["w_ukv"][l].reshape(kv_lora, MLA_HEADS, QK_NOPE + V_HEAD)
    w_uk = _pad_heads(w_ukv[:, :, :QK_NOPE].reshape(kv_lora, -1), QK_NOPE, 0)
    w_uv = w_ukv[:, :, QK_NOPE:].reshape(kv_lora, -1)

    lam = lax.complex(prm["ssm_lam_re"][l], prm["ssm_lam_im"][l])
    step = jnp.exp(prm["ssm_log_step"][l])[..., None]
    lam_bar = jnp.exp(lam * step)
    b_bar = ((lam_bar - 1.0) / lam)[..., None] * lax.complex(prm["ssm_b_re"][l], prm["ssm_b_im"][l])
    ndir, g, n, cg = b_bar.shape
    gh = g // 2
    bt = jnp.swapaxes(b_bar, -1, -2).reshape(ndir, 2, gh, cg, n)
    bm = jnp.concatenate([_block_diag(jnp.real(bt)), _block_diag(jnp.imag(bt))], axis=-1)
    c_re = jnp.swapaxes(prm["ssm_c_re"][l], -1, -2).reshape(ndir, 2, gh, n, cg)
    c_im = jnp.swapaxes(prm["ssm_c_im"][l], -1, -2).reshape(ndir, 2, gh, n, cg)
    cm = jnp.concatenate([_block_diag(c_re), _block_diag(-c_im)], axis=-2)
    lb = lam_bar.reshape(ndir, 2, gh * n)
    a = jnp.stack([jnp.real(lb), jnp.imag(lb)], axis=2).reshape(ndir, 1, 2 * 2 * gh * n)
    a = jnp.broadcast_to(a, (ndir, SUBLANES, a.shape[-1]))

    row = lambda v: v.reshape(1, -1)
    lw = dict(
        w_small=w_small, w_gate=w_in[:, gate0:].astype(BF16),
        norm1_g=row(prm["norm1_g"][l]), norm2_g=row(prm["norm2_g"][l]),
        q_a_norm_g=row(prm["q_a_norm_g"][l]), kv_a_norm_g=row(prm["kv_a_norm_g"][l]),
        q_norm_g=row(_pad_heads(jnp.tile(prm["q_norm_g"][l], MLA_HEADS), QK_HEAD, 0)),
        k_norm_g=row(_pad_heads(jnp.tile(prm["k_norm_g"][l], MLA_HEADS), QK_HEAD, 0)),
        w_uq=_pad_heads(prm["w_uq"][l], QK_HEAD, 0).astype(BF16),
        w_uk=w_uk.astype(BF16), w_uv=w_uv.astype(BF16),
        w_mla_o=prm["w_mla_o"][l].astype(BF16),
        conv_w=prm["conv_w"][l], conv_b=row(prm["conv_b"][l]),
        w_conv_o=prm["w_conv_o"][l].astype(BF16),
        ssm_a=a.astype(F32), ssm_bm=bm.astype(BF16), ssm_cm=cm.astype(BF16),
        ssm_d=row(prm["ssm_d"][l]), w_glu=prm["w_glu"][l].astype(BF16),
        pool_w=prm["pool_w"][l].astype(BF16), pool_scale=row(prm["pool_scale"][l]),
        w_pool_o=prm["w_pool_o"][l].astype(BF16), w_o=prm["w_o"][l].astype(BF16),
        w_mlp1=prm["w_mlp1"][l].astype(BF16), w_mlp2=prm["w_mlp2"][l].astype(BF16),
    )
    return lw, cols


def _rope_tables(seq_len):
    half = ROPE_DIM // 2
    nf = half // 2
    inv_freq = ROPE_THETA ** (-jnp.arange(0, half, 2, dtype=F32) / half)
    t = jnp.arange(seq_len)
    ang_r = (t // GRID_W).astype(F32)[:, None] * inv_freq
    ang_c = (t % GRID_W).astype(F32)[:, None] * inv_freq
    zeros = jnp.zeros((seq_len, nf), F32)
    ones_lo = jnp.ones((seq_len, QK_NOPE), F32)
    zeros_lo = jnp.zeros((seq_len, QK_NOPE), F32)
    tail = jnp.zeros((seq_len, HEAD_PAD - QK_HEAD), F32)
    c = jnp.concatenate([ones_lo, jnp.cos(ang_r), jnp.cos(ang_r), jnp.cos(ang_c), jnp.cos(ang_c), tail], axis=1)
    s1 = jnp.concatenate([zeros_lo, -jnp.sin(ang_r), zeros, -jnp.sin(ang_c), zeros, tail], axis=1)
    s2 = jnp.concatenate([zeros_lo, zeros, jnp.sin(ang_r), zeros, jnp.sin(ang_c), tail], axis=1)
    return c, s1, s2


def _states_to_cols(st):
    b, ndir, _, g, n = st.shape
    st = st.reshape(b, ndir, 2, 2, (g // 2) * n)
    return jnp.transpose(st, (1, 0, 3, 2, 4)).reshape(ndir, b, 2 * g * n)


def _cols_to_states(hf, g, n):
    ndir, b, _ = hf.shape
    hf = hf.reshape(ndir, b, 2, 2, (g // 2) * n)
    return jnp.transpose(hf, (1, 0, 3, 2, 4)).reshape(b, ndir, 2, g, n)


def _mixer_layer(x, mod, lw, cols, batch, seq_len, per_batch, rope, ctx):
    p = _inproj_call(x, mod, lw["norm1_g"], lw["w_small"], seq_len, per_batch)
    q, k, v, ckv = _qkprep_call(p, lw, rope, seq_len, cols)
    if ctx is not None:
        ctx_ckv, ctx_kpe, h0 = ctx
        kc, vc = _ctxprep_call(ctx_ckv, ctx_kpe, lw)
        ctx_len = ctx_ckv.shape[0] // batch
    else:
        kc = vc = None
        ctx_len = 0
        h0 = jnp.zeros((2, batch, lw["ssm_a"].shape[-1]), F32)
    za = _attn_call(q, k, v, kc, vc, batch, seq_len, ctx_len)
    zb, zd = _convpool_call(p, lw, batch, seq_len, cols)

    w = zb.shape[1]
    su = cols["ssm_u"]
    u_tm = jnp.transpose(p[:, su:su + w].reshape(batch, seq_len, w), (1, 0, 2))
    y_tm, hf = _ssm_call(u_tm, h0, lw)
    y2 = jnp.transpose(y_tm, (0, 2, 1, 3)).reshape(2, batch * seq_len, w)

    x1 = _merge_call(x, mod, p, za, zb, y2, zd, lw, seq_len, per_batch, cols)
    x2 = _mlp_call(x1, mod, lw, seq_len, per_batch)
    kr = cols["k_rope"]
    return x2, ckv, p[:, kr:kr + ROPE_DIM], hf


def kernel(x_prompt, x_sample, c, cache_ckv, cache_krope, state_ssm, c_ctx, w_ada, b_ada, norm1_g, norm2_g, w_in, q_a_norm_g, kv_a_norm_g, w_uq, w_ukv, q_norm_g, k_norm_g, w_mla_o, conv_w, conv_b, w_conv_o, ssm_lam_re, ssm_lam_im, ssm_log_step, ssm_b_re, ssm_b_im, ssm_c_re, ssm_c_im, ssm_d, w_glu, pool_w, pool_scale, w_pool_o, w_o, w_mlp1, w_mlp2):
    prm = dict(norm1_g=norm1_g, norm2_g=norm2_g, w_in=w_in, q_a_norm_g=q_a_norm_g, kv_a_norm_g=kv_a_norm_g,
               w_uq=w_uq, w_ukv=w_ukv, q_norm_g=q_norm_g, k_norm_g=k_norm_g, w_mla_o=w_mla_o,
               conv_w=conv_w, conv_b=conv_b, w_conv_o=w_conv_o, ssm_lam_re=ssm_lam_re, ssm_lam_im=ssm_lam_im,
               ssm_log_step=ssm_log_step, ssm_b_re=ssm_b_re, ssm_b_im=ssm_b_im, ssm_c_re=ssm_c_re,
               ssm_c_im=ssm_c_im, ssm_d=ssm_d, w_glu=w_glu, pool_w=pool_w, pool_scale=pool_scale,
               w_pool_o=w_pool_o, w_o=w_o, w_mlp1=w_mlp1, w_mlp2=w_mlp2)
    depth = w_in.shape[0]
    bp, lp, d = x_prompt.shape
    bs, ls, _ = x_sample.shape
    past = cache_ckv.shape[2]
    g, n = state_ssm.shape[-2:]
    sizes = (w_uq.shape[1], w_ukv.shape[1], conv_w.shape[-1], ssm_d.shape[-1], pool_scale.shape[-1])

    rows = -(-(1 + bs) // SUBLANES) * SUBLANES
    cvec = jnp.zeros((rows, d), F32).at[0].set(c_ctx).at[1:1 + bs].set(c)
    mods = _ada_call(cvec, w_ada, b_ada)
    rope = _rope_tables(ls)

    yp = x_prompt.reshape(bp * lp, d)
    ys = x_sample.reshape(bs * ls, d)
    ckv_list, krope_list, ssm_list = [], [], []
    for l in range(depth):
        lw, cols = _prep_layer(l, prm, sizes)
        mod_ctx = mods[l, 0:1].reshape(6, 1, d)
        mod_lat = mods[l, 1:1 + bs].reshape(bs * 6, 1, d)
        yp, ckv_l, krope_l, hf_l = _mixer_layer(yp, mod_ctx, lw, cols, bp, lp, False, None, None)
        ckv_list.append(ckv_l.reshape(bp, lp, -1))
        krope_list.append(krope_l.reshape(bp, lp, -1))
        ssm_list.append(_cols_to_states(hf_l, g, n))
        ctx_kpe = _pad_heads(jnp.tile(cache_krope[:, l].reshape(bs * past, ROPE_DIM), (1, MLA_HEADS)),
                             ROPE_DIM, QK_NOPE)
        ctx = (cache_ckv[:, l].reshape(bs * past, -1), ctx_kpe, _states_to_cols(state_ssm[:, l]))
        ys, _, _, _ = _mixer_layer(ys, mod_lat, lw, cols, bs, ls, True, rope, ctx)
    return (yp.reshape(bp, lp, d), ys.reshape(bs, ls, d), jnp.stack(ckv_list, axis=1),
            jnp.stack(krope_list, axis=1), jnp.stack(ssm_list, axis=1))
```

```python
import functools
import math

import jax
import jax.numpy as jnp
from jax import lax
from jax.experimental import pallas as pl
from jax.experimental.pallas import tpu as pltpu

F32 = jnp.float32
BF16 = jnp.bfloat16

GRID_W = 64
N_BRANCH = 4
MLA_HEADS = 8
QK_NOPE = 64
ROPE_DIM = 32
QK_HEAD = QK_NOPE + ROPE_DIM
V_HEAD = 64
ROPE_THETA = 10000.0
SSM_GROUP_CH = 16
POOL_WINDOWS = (2, 4, 8, 16)
EPS = 1e-6
LOG2E = 1.4426950408889634

LANES = 128
SUBLANES = 8
HEAD_PAD = LANES
MIB = 1024 * 1024


def _cparams(sem, vmem_mib):
    return pltpu.CompilerParams(dimension_semantics=sem, vmem_limit_bytes=vmem_mib * MIB)


def _dot(a, b):
    return jnp.dot(a, b, preferred_element_type=F32)


def _mod_spec(chunk, d, tm, seq_len, per_batch):
    def imap(i, *_):
        row = (i * tm) // seq_len if per_batch else 0
        return (row * 6 + chunk, 0, 0)
    return pl.BlockSpec((1, 1, d), imap)


def _ada_kernel(c_ref, w_ref, b_ref, o_ref):
    cv = c_ref[...]
    s = cv * jax.nn.sigmoid(cv)
    o_ref[0] = _dot(s.astype(BF16), w_ref[0].astype(BF16)) + b_ref[0]


def _ada_call(cvec, w_ada, b_ada):
    depth, d, n = w_ada.shape
    rows = cvec.shape[0]
    tn = 1024
    return pl.pallas_call(
        _ada_kernel,
        out_shape=jax.ShapeDtypeStruct((depth, rows, n), F32),
        grid=(depth, n // tn),
        in_specs=[
            pl.BlockSpec((rows, d), lambda l, j: (0, 0)),
            pl.BlockSpec((1, d, tn), lambda l, j: (l, 0, j)),
            pl.BlockSpec((1, 1, tn), lambda l, j: (l, 0, j)),
        ],
        out_specs=pl.BlockSpec((1, rows, tn), lambda l, j: (l, 0, j)),
        compiler_params=_cparams(("parallel", "parallel"), 40),
        name="ada_mod",
    )(cvec, w_ada, b_ada.reshape(depth, 1, n))


def _modnorm(x, g, scale, shift):
    ms = jnp.mean(x * x, axis=-1, keepdims=True)
    return (x * lax.rsqrt(ms + EPS) * g) * (1.0 + scale) + shift


def _inproj_kernel(x_ref, sh_ref, sc_ref, g_ref, w_ref, o_ref, h_ref):
    h = _modnorm(x_ref[...], g_ref[...], sc_ref[0], sh_ref[0]).astype(BF16)
    h_ref[...] = h
    o_ref[...] = _dot(h, w_ref[...])


def _inproj_call(x, mod, g, w, seq_len, per_batch):
    t, d = x.shape
    n = w.shape[1]
    tm = 512
    mspec = functools.partial(_mod_spec, d=d, tm=tm, seq_len=seq_len, per_batch=per_batch)
    return pl.pallas_call(
        _inproj_kernel,
        out_shape=(jax.ShapeDtypeStruct((t, n), F32), jax.ShapeDtypeStruct((t, d), BF16)),
        grid=(t // tm,),
        in_specs=[
            pl.BlockSpec((tm, d), lambda i: (i, 0)),
            mspec(0), mspec(1),
            pl.BlockSpec((1, d), lambda i: (0, 0)),
            pl.BlockSpec((d, n), lambda i: (0, 0), pipeline_mode=pl.Buffered(1)),
        ],
        out_specs=(pl.BlockSpec((tm, n), lambda i: (i, 0)), pl.BlockSpec((tm, d), lambda i: (i, 0))),
        compiler_params=_cparams(("parallel",), 52),
        name="in_proj",
    )(x, mod, mod, g, w)


def _rms(x, g, width):
    ms = jnp.sum(x * x, axis=-1, keepdims=True) * (1.0 / width)
    return x * lax.rsqrt(ms + EPS) * g


def _heads_norm(x, g, rope, mult, shared=None):
    outs = []
    for h in range(MLA_HEADS):
        sl = slice(h * HEAD_PAD, (h + 1) * HEAD_PAD)
        xh = x[:, sl] if shared is None else x[:, sl] + shared
        y = _rms(xh, g[:, sl], QK_HEAD)
        if rope is not None:
            c, s1, s2 = rope
            y = (y * c + pltpu.roll(y, HEAD_PAD - 8, axis=1) * s1
                 + pltpu.roll(y, 8, axis=1) * s2)
        if mult != 1.0:
            y = y * mult
        outs.append(y)
    return jnp.concatenate(outs, axis=1)


def _qkprep_kernel(*refs, use_rope, q_mult):
    if use_rope:
        (qa_ref, kva_ref, kpe_ref, gqa_ref, gkva_ref, gq_ref, gk_ref, wuq_ref, wuk_ref, wuv_ref,
         c_ref, s1_ref, s2_ref, q_out, k_out, v_out, ckv_out) = refs
        rope = (c_ref[...], s1_ref[...], s2_ref[...])
    else:
        (qa_ref, kva_ref, kpe_ref, gqa_ref, gkva_ref, gq_ref, gk_ref, wuq_ref, wuk_ref, wuv_ref,
         q_out, k_out, v_out, ckv_out) = refs
        rope = None
    qa = qa_ref[...]
    qa_n = _rms(qa, gqa_ref[...], qa.shape[-1]).astype(BF16)
    q = _dot(qa_n, wuq_ref[...])
    q_out[...] = _heads_norm(q, gq_ref[...], rope, q_mult).astype(BF16)

    kva = kva_ref[...]
    ckv = _rms(kva, gkva_ref[...], kva.shape[-1])
    ckv_out[...] = ckv
    ckv_b = ckv.astype(BF16)
    k = _dot(ckv_b, wuk_ref[...])
    k_out[...] = _heads_norm(k, gk_ref[...], rope, 1.0, shared=kpe_ref[...]).astype(BF16)
    v_out[...] = _dot(ckv_b, wuv_ref[...]).astype(BF16)


def _qkprep_call(p, lw, rope, seq_len, cols):
    t = p.shape[0]
    tm = 512
    hq = MLA_HEADS * HEAD_PAD
    hv = MLA_HEADS * V_HEAD
    qlora = lw["w_uq"].shape[0]
    kvlora = lw["w_uk"].shape[0]
    const = lambda i: (0, 0)
    in_specs = [
        pl.BlockSpec((tm, qlora), lambda i: (i, cols["q_a"] // qlora)),
        pl.BlockSpec((tm, kvlora), lambda i: (i, cols["kv_a"] // kvlora)),
        pl.BlockSpec((tm, HEAD_PAD), lambda i: (i, cols["k_rope"] // HEAD_PAD)),
        pl.BlockSpec((1, qlora), const),
        pl.BlockSpec((1, kvlora), const),
        pl.BlockSpec((1, hq), const),
        pl.BlockSpec((1, hq), const),
        pl.BlockSpec((qlora, hq), const),
        pl.BlockSpec((kvlora, hq), const),
        pl.BlockSpec((kvlora, hv), const),
    ]
    args = [p, p, p, lw["q_a_norm_g"], lw["kv_a_norm_g"], lw["q_norm_g"], lw["k_norm_g"],
            lw["w_uq"], lw["w_uk"], lw["w_uv"]]
    if rope is not None:
        nblk = seq_len // tm
        in_specs += [pl.BlockSpec((tm, HEAD_PAD), lambda i: (i % nblk, 0))] * 3
        args += list(rope)
    q_mult = LOG2E * QK_HEAD ** -0.5
    return pl.pallas_call(
        functools.partial(_qkprep_kernel, use_rope=rope is not None, q_mult=q_mult),
        out_shape=(jax.ShapeDtypeStruct((t, hq), BF16), jax.ShapeDtypeStruct((t, hq), BF16),
                   jax.ShapeDtypeStruct((t, hv), BF16), jax.ShapeDtypeStruct((t, kvlora), F32)),
        grid=(t // tm,),
        in_specs=in_specs,
        out_specs=(pl.BlockSpec((tm, hq), lambda i: (i, 0)), pl.BlockSpec((tm, hq), lambda i: (i, 0)),
                   pl.BlockSpec((tm, hv), lambda i: (i, 0)), pl.BlockSpec((tm, kvlora), lambda i: (i, 0))),
        compiler_params=_cparams(("parallel",), 40),
        name="qk_prep",
    )(*args)


def _ctxprep_kernel(ckv_ref, kpe_ref, gk_ref, wuk_ref, wuv_ref, k_out, v_out):
    ckv_b = ckv_ref[...].astype(BF16)
    k = _dot(ckv_b, wuk_ref[...])
    k_out[...] = _heads_norm(k, gk_ref[...], None, 1.0, shared=kpe_ref[...]).astype(BF16)
    v_out[...] = _dot(ckv_b, wuv_ref[...]).astype(BF16)


def _ctxprep_call(ckv, kpe, lw):
    t, kvlora = ckv.shape
    tm = 512
    hq = MLA_HEADS * HEAD_PAD
    hv = MLA_HEADS * V_HEAD
    const = lambda i: (0, 0)
    return pl.pallas_call(
        _ctxprep_kernel,
        out_shape=(jax.ShapeDtypeStruct((t, hq), BF16), jax.ShapeDtypeStruct((t, hv), BF16)),
        grid=(t // tm,),
        in_specs=[
            pl.BlockSpec((tm, kvlora), lambda i: (i, 0)),
            pl.BlockSpec((tm, HEAD_PAD), lambda i: (i, 0)),
            pl.BlockSpec((1, hq), const),
            pl.BlockSpec((kvlora, hq), const),
            pl.BlockSpec((kvlora, hv), const),
        ],
        out_specs=(pl.BlockSpec((tm, hq), lambda i: (i, 0)), pl.BlockSpec((tm, hv), lambda i: (i, 0))),
        compiler_params=_cparams(("parallel",), 40),
        name="ctx_prep",
    )(ckv, kpe, lw["k_norm_g"], lw["w_uk"], lw["w_uv"])


def _attn_kernel(*refs, has_ctx):
    if has_ctx:
        q_ref, ko_ref, vo_ref, kc_ref, vc_ref, o_ref = refs
    else:
        q_ref, ko_ref, vo_ref, o_ref = refs
    contract_last = (((1,), (1,)), ((), ()))
    outs = []
    for hh in range(2):
        sl = slice(hh * HEAD_PAD, (hh + 1) * HEAD_PAD)
        q = q_ref[:, sl]
        s_o = lax.dot_general(q, ko_ref[:, sl], contract_last, preferred_element_type=F32)
        m = jnp.max(s_o, axis=-1, keepdims=True)
        if has_ctx:
            s_c = lax.dot_general(q, kc_ref[:, sl], contract_last, preferred_element_type=F32)
            m = jnp.maximum(m, jnp.max(s_c, axis=-1, keepdims=True))
        p_o = jnp.exp2(s_o - m)
        l = jnp.sum(p_o, axis=-1, keepdims=True)
        acc = _dot(p_o.astype(BF16), vo_ref[...])
        if has_ctx:
            p_c = jnp.exp2(s_c - m)
            l = l + jnp.sum(p_c, axis=-1, keepdims=True)
            acc = acc + _dot(p_c.astype(BF16), vc_ref[...])
        outs.append(acc / l)
    lane = lax.broadcasted_iota(jnp.int32, outs[0].shape, 1)
    o_ref[...] = jnp.where(lane < V_HEAD, outs[0], outs[1]).astype(BF16)


def _attn_call(q, k, v, kc, vc, batch, seq_len, ctx_len):
    t = q.shape[0]
    tq = 256
    nq = seq_len // tq
    npair = MLA_HEADS // 2
    qw = 2 * HEAD_PAD
    vw = 2 * V_HEAD
    in_specs = [
        pl.BlockSpec((tq, qw), lambda b, h, i: (b * nq + i, h)),
        pl.BlockSpec((seq_len, qw), lambda b, h, i: (b, h)),
        pl.BlockSpec((seq_len, vw), lambda b, h, i: (b, h)),
    ]
    args = [q, k, v]
    if kc is not None:
        in_specs += [pl.BlockSpec((ctx_len, qw), lambda b, h, i: (b, h)),
                     pl.BlockSpec((ctx_len, vw), lambda b, h, i: (b, h))]
        args += [kc, vc]
    return pl.pallas_call(
        functools.partial(_attn_kernel, has_ctx=kc is not None),
        out_shape=jax.ShapeDtypeStruct((t, MLA_HEADS * V_HEAD), BF16),
        grid=(batch, npair, nq),
        in_specs=in_specs,
        out_specs=pl.BlockSpec((tq, vw), lambda b, h, i: (b * nq + i, h)),
        compiler_params=_cparams(("parallel", "parallel", "arbitrary"), 48),
        name="attention",
    )(*args)


def _convpool_kernel(cu_ref, bg_ref, cg_ref, pu_ref, cw_ref, cb_ref, pw_ref, ps_ref, zb_ref, zd_ref):
    n = cu_ref.shape[0]
    row = lax.broadcasted_iota(jnp.int32, cu_ref.shape, 0)

    def down(x, k):
        return jnp.where(row >= k, pltpu.roll(x, k, axis=0), 0.0)

    def up(x, k):
        return jnp.where(row < n - k, pltpu.roll(x, n - k, axis=0), 0.0)

    v = cg_ref[...] * cu_ref[...]
    conv = (down(v, 1) * cw_ref[0:1, :] + v * cw_ref[1:2, :] + up(v, 1) * cw_ref[2:3, :]
            + cb_ref[...])
    zb_ref[...] = (bg_ref[...] * conv).astype(BF16)

    u = pu_ref[...]
    rowf = row.astype(F32)
    grp = pl.program_id(1)
    for gi, win in enumerate(POOL_WINDOWS):
        @pl.when(grp == gi)
        def _(win=win):
            half = win // 2
            fwd = u
            bwd = u
            k = 1
            while k < half:
                fwd = fwd + up(fwd, k)
                bwd = bwd + down(bwd, k)
                k *= 2
            total = fwd + down(bwd, 1)
            cnt = jnp.minimum(rowf + half, float(n)) - jnp.maximum(rowf - half, 0.0)
            mean = total / cnt - u
            zd_ref[...] = (_dot(mean.astype(BF16), pw_ref[0]) * ps_ref[...]).astype(BF16)


def _convpool_call(p, lw, batch, seq_len, cols):
    t = p.shape[0]
    cw = LANES
    nblk = lw["conv_w"].shape[1] // cw

    def pspec(name):
        base = cols[name] // cw
        return pl.BlockSpec((seq_len, cw), lambda b, g: (b, base + g))

    vec = lambda b, g: (0, g)
    return pl.pallas_call(
        _convpool_kernel,
        out_shape=(jax.ShapeDtypeStruct((t, nblk * cw), BF16), jax.ShapeDtypeStruct((t, nblk * cw), BF16)),
        grid=(batch, nblk),
        in_specs=[pspec("conv_u"), pspec("conv_bg"), pspec("conv_cg"), pspec("pool_u"),
                  pl.BlockSpec((3, cw), vec), pl.BlockSpec((1, cw), vec),
                  pl.BlockSpec((1, cw, cw), lambda b, g: (g, 0, 0)), pl.BlockSpec((1, cw), vec)],
        out_specs=(pl.BlockSpec((seq_len, cw), lambda b, g: (b, g)),
                   pl.BlockSpec((seq_len, cw), lambda b, g: (b, g))),
        compiler_params=_cparams(("parallel", "parallel"), 40),
        name="conv_pool",
    )(p, p, p, p, lw["conv_w"], lw["conv_b"], lw["pool_w"], lw["pool_scale"])


def _ssm_kernel(u_ref, h0_ref, a_ref, bm_ref, cm_ref, y_ref, hf_ref, xs_scr, st_scr, *, chunk, nchunks):
    d = pl.program_id(0)
    c = pl.program_id(2)
    half_w = st_scr.shape[1] // 2
    part_w = half_w // 2
    uw = u_ref.shape[-1] // 2

    @pl.when(c == 0)
    def _():
        st_scr[...] = h0_ref[0]

    u = u_ref[...].reshape(chunk * SUBLANES, 2 * uw).astype(BF16)
    for half in range(2):
        xs_scr[:, half * half_w:(half + 1) * half_w] = _dot(u[:, half * uw:(half + 1) * uw], bm_ref[0, half])

    gw = 512
    for half in range(2):
        for part in range(part_w // gw):
            re0 = half * half_w + part * gw
            im0 = re0 + part_w
            a_re = a_ref[0, :, re0:re0 + gw]
            a_im = a_ref[0, :, im0:im0 + gw]

            def body(t, carry, re0=re0, im0=im0, a_re=a_re, a_im=a_im):
                x_re, x_im = carry
                tt = t + d * (chunk - 1 - 2 * t)
                r0 = pl.multiple_of(tt * SUBLANES, SUBLANES)
                n_re = (a_re * x_re - a_im * x_im) + xs_scr[pl.ds(r0, SUBLANES), re0:re0 + gw]
                n_im = (a_re * x_im + a_im * x_re) + xs_scr[pl.ds(r0, SUBLANES), im0:im0 + gw]
                xs_scr[pl.ds(r0, SUBLANES), re0:re0 + gw] = n_re
                xs_scr[pl.ds(r0, SUBLANES), im0:im0 + gw] = n_im
                return n_re, n_im

            x_re, x_im = lax.fori_loop(
                0, chunk, body, (st_scr[:, re0:re0 + gw], st_scr[:, im0:im0 + gw]), unroll=2)
            st_scr[:, re0:re0 + gw] = x_re
            st_scr[:, im0:im0 + gw] = x_im

    for half in range(2):
        xb = xs_scr[:, half * half_w:(half + 1) * half_w].astype(BF16)
        y_ref[0, :, :, half * uw:(half + 1) * uw] = _dot(xb, cm_ref[0, half]).reshape(chunk, SUBLANES, uw)

    @pl.when(c == nchunks - 1)
    def _():
        hf_ref[0] = st_scr[...]


def _ssm_call(u_tm, h0, lw):
    seq_len, batch, w = u_tm.shape
    s = h0.shape[-1]
    chunk = 128
    nchunks = seq_len // chunk
    nbg = batch // SUBLANES

    def cidx(d, c):
        return c + d * (nchunks - 1 - 2 * c)

    return pl.pallas_call(
        functools.partial(_ssm_kernel, chunk=chunk, nchunks=nchunks),
        out_shape=(jax.ShapeDtypeStruct((2, seq_len, batch, w), F32),
                   jax.ShapeDtypeStruct((2, batch, s), F32)),
        grid=(2, nbg, nchunks),
        in_specs=[
            pl.BlockSpec((chunk, SUBLANES, w), lambda d, b, c: (cidx(d, c), b, 0)),
            pl.BlockSpec((1, SUBLANES, s), lambda d, b, c: (d, b, 0)),
            pl.BlockSpec((1, SUBLANES, s), lambda d, b, c: (d, 0, 0)),
            pl.BlockSpec((1, 2, w // 2, s // 2), lambda d, b, c: (d, 0, 0, 0)),
            pl.BlockSpec((1, 2, s // 2, w // 2), lambda d, b, c: (d, 0, 0, 0)),
        ],
        out_specs=(pl.BlockSpec((1, chunk, SUBLANES, w), lambda d, b, c: (d, cidx(d, c), b, 0)),
                   pl.BlockSpec((1, SUBLANES, s), lambda d, b, c: (d, b, 0))),
        scratch_shapes=[pltpu.VMEM((chunk * SUBLANES, s), F32), pltpu.VMEM((SUBLANES, s), F32)],
        compiler_params=_cparams(("arbitrary", "arbitrary", "arbitrary"), 48),
        name="s5_scan",
    )(u_tm, h0, lw["ssm_a"], lw["ssm_bm"], lw["ssm_cm"])


def _merge_kernel(h_ref, za_ref, zb_ref, yf_ref, yb_ref, su_ref, sd_ref, zd_ref,
                  wg0, wg1, wg2, wg3, wa, wb, wca, wcg, wd, o_ref, y_scr):
    @pl.when(pl.program_id(1) == 0)
    def _():
        y_scr[...] = ((yf_ref[0] + yb_ref[0]) + sd_ref[...] * su_ref[...]).astype(BF16)

    h = h_ref[...]
    y = y_scr[...]

    def gate(w_ref):
        return jax.nn.sigmoid(_dot(h, w_ref[...]))

    merged = gate(wg0) * _dot(za_ref[...], wa[...])
    merged = merged + gate(wg1) * _dot(zb_ref[...], wb[...])
    merged = merged + gate(wg2) * (_dot(y, wca[...]) * jax.nn.sigmoid(_dot(y, wcg[...])))
    merged = merged + gate(wg3) * _dot(zd_ref[...], wd[...])
    o_ref[...] = merged.astype(BF16)


def _merge_call(h, p, za, zb, y2, zd, lw, cols):
    t, d = h.shape
    bw = za.shape[1]
    tm, tn = 512, 512
    nj = d // tn
    row = lambda i, j: (i, 0)
    colj = lambda i, j: (0, j)
    su_blk = cols["ssm_u"] // bw

    def gspec(k):
        return pl.BlockSpec((d, tn), lambda i, j: (0, k * nj + j))

    in_specs = [
        pl.BlockSpec((tm, d), row),
        pl.BlockSpec((tm, bw), row), pl.BlockSpec((tm, bw), row),
        pl.BlockSpec((1, tm, bw), lambda i, j: (0, i, 0)), pl.BlockSpec((1, tm, bw), lambda i, j: (1, i, 0)),
        pl.BlockSpec((tm, bw), lambda i, j: (i, su_blk)),
        pl.BlockSpec((1, bw), lambda i, j: (0, 0)),
        pl.BlockSpec((tm, bw), row),
        gspec(0), gspec(1), gspec(2), gspec(3),
        pl.BlockSpec((bw, tn), colj), pl.BlockSpec((bw, tn), colj),
        pl.BlockSpec((bw, tn), colj), pl.BlockSpec((bw, tn), lambda i, j: (0, nj + j)),
        pl.BlockSpec((bw, tn), colj),
    ]
    wg = lw["w_gate"]
    return pl.pallas_call(
        _merge_kernel,
        out_shape=jax.ShapeDtypeStruct((t, d), BF16),
        grid=(t // tm, nj),
        in_specs=in_specs,
        out_specs=pl.BlockSpec((tm, tn), lambda i, j: (i, j)),
        scratch_shapes=[pltpu.VMEM((tm, bw), BF16)],
        compiler_params=_cparams(("parallel", "arbitrary"), 48),
        name="branch_merge",
    )(h, za, zb, y2, y2, p, lw["ssm_d"], zd,
      wg, wg, wg, wg, lw["w_mla_o"], lw["w_conv_o"], lw["w_glu"], lw["w_glu"], lw["w_pool_o"])


def _outproj_kernel(x_ref, m_ref, gt_ref, sh_ref, sc_ref, g_ref, wo_ref, x1_ref, h2_ref):
    x1 = x_ref[...] + gt_ref[0] * _dot(m_ref[...], wo_ref[...])
    x1_ref[...] = x1
    h2_ref[...] = _modnorm(x1, g_ref[...], sc_ref[0], sh_ref[0]).astype(BF16)


def _outproj_call(x, merged, mod, lw, seq_len, per_batch):
    t, d = x.shape
    tm = 512
    mspec = functools.partial(_mod_spec, d=d, tm=tm, seq_len=seq_len, per_batch=per_batch)
    row = lambda i: (i, 0)
    return pl.pallas_call(
        _outproj_kernel,
        out_shape=(jax.ShapeDtypeStruct((t, d), F32), jax.ShapeDtypeStruct((t, d), BF16)),
        grid=(t // tm,),
        in_specs=[
            pl.BlockSpec((tm, d), row), pl.BlockSpec((tm, d), row),
            mspec(2), mspec(3), mspec(4),
            pl.BlockSpec((1, d), lambda i: (0, 0)),
            pl.BlockSpec((d, d), lambda i: (0, 0), pipeline_mode=pl.Buffered(1)),
        ],
        out_specs=(pl.BlockSpec((tm, d), row), pl.BlockSpec((tm, d), row)),
        compiler_params=_cparams(("parallel",), 48),
        name="out_proj",
    )(x, merged, mod, mod, mod, lw["norm2_g"], lw["w_o"])


def _mlp_kernel(x_ref, h_ref, gt_ref, w1_ref, w2_ref, o_ref):
    @pl.when(pl.program_id(1) == 0)
    def _():
        o_ref[...] = x_ref[...]

    hid = jnp.square(jnp.maximum(_dot(h_ref[...], w1_ref[...]), 0.0))
    o_ref[...] += gt_ref[0] * _dot(hid.astype(BF16), w2_ref[...])


def _mlp_call(x, h, mod, lw, seq_len, per_batch):
    t, d = x.shape
    hidden = lw["w_mlp1"].shape[1]
    tm, tk = 512, 1024
    mspec = functools.partial(_mod_spec, d=d, tm=tm, seq_len=seq_len, per_batch=per_batch)
    row = lambda i, k: (i, 0)
    return pl.pallas_call(
        _mlp_kernel,
        out_shape=jax.ShapeDtypeStruct((t, d), F32),
        grid=(t // tm, hidden // tk),
        in_specs=[
            pl.BlockSpec((tm, d), row), pl.BlockSpec((tm, d), row), mspec(5),
            pl.BlockSpec((d, tk), lambda i, k: (0, k)),
            pl.BlockSpec((tk, d), lambda i, k: (k, 0)),
        ],
        out_specs=pl.BlockSpec((tm, d), row),
        compiler_params=_cparams(("parallel", "arbitrary"), 48),
        name="mlp",
    )(x, h, mod, lw["w_mlp1"], lw["w_mlp2"])


def _pad_heads(w, per_head, lo):
    lead = w.shape[:-1]
    w = w.reshape(lead + (MLA_HEADS, per_head))
    w = jnp.pad(w, [(0, 0)] * len(lead) + [(0, 0), (lo, HEAD_PAD - lo - per_head)])
    return w.reshape(lead + (MLA_HEADS * HEAD_PAD,))


def _block_diag(blocks):
    g, r, c = blocks.shape[-3:]
    eye = jnp.eye(g, dtype=blocks.dtype)
    out = jnp.einsum("...grc,gk->...grkc", blocks, eye)
    return out.reshape(blocks.shape[:-3] + (g * r, g * c))


def _prep_layer(l, prm, sizes):
    q_lora, kv_lora, conv_w, ssm_w, pool_w_ = sizes
    w_in = prm["w_in"][l]
    d = w_in.shape[0]
    offs = {}
    o = 0
    for name, width in (("q_a", q_lora), ("kv_a", kv_lora), ("k_rope", ROPE_DIM), ("conv_u", conv_w),
                        ("conv_bg", conv_w), ("conv_cg", conv_w), ("ssm_u", ssm_w), ("pool_u", pool_w_)):
        offs[name] = (o, o + width)
        o += width
    gate0 = o
    take = lambda name: w_in[:, offs[name][0]:offs[name][1]]
    kr_pad = jnp.pad(take("k_rope"), ((0, 0), (QK_NOPE, HEAD_PAD - QK_HEAD)))
    pieces = [("q_a", take("q_a")), ("conv_u", take("conv_u")), ("conv_bg", take("conv_bg")),
              ("conv_cg", take("conv_cg")), ("ssm_u", take("ssm_u")), ("pool_u", take("pool_u")),
              ("kv_a", take("kv_a")), ("k_rope", kr_pad)]
    cols = {}
    o = 0
    for name, w in pieces:
        cols[name] = o
        o += w.shape[1]
    w_small = jnp.concatenate([w for _, w in pieces], axis=1).astype(BF16)

    w_ukv = prm["w_ukv"][l].reshape(kv_lora, MLA_HEADS, QK_NOPE + V_HEAD)
    w_uk = _pad_heads(w_ukv[:, :, :QK_NOPE].reshape(kv_lora, -1), QK_NOPE, 0)
    w_uv = w_ukv[:, :, QK_NOPE:].reshape(kv_lora, -1)

    lr, li = prm["ssm_lam_re"][l], prm["ssm_lam_im"][l]
    step = jnp.exp(prm["ssm_log_step"][l])[..., None]
    mag = jnp.exp(lr * step)
    ar, ai = mag * jnp.cos(li * step), mag * jnp.sin(li * step)
    den = lr * lr + li * li
    qr = ((ar - 1.0) * lr + ai * li) / den
    qi = (ai * lr - (ar - 1.0) * li) / den
    b_re, b_im = prm["ssm_b_re"][l], prm["ssm_b_im"][l]
    bb_re = qr[..., None] * b_re - qi[..., None] * b_im
    bb_im = qr[..., None] * b_im + qi[..., None] * b_re
    ndir, g, n, cg = bb_re.shape
    gh = g // 2
    to_cn = lambda v: jnp.swapaxes(v, -1, -2).reshape(ndir, 2, gh, cg, n)
    bm = jnp.concatenate([_block_diag(to_cn(bb_re)), _block_diag(to_cn(bb_im))], axis=-1)
    c_re = jnp.swapaxes(prm["ssm_c_re"][l], -1, -2).reshape(ndir, 2, gh, n, cg)
    c_im = jnp.swapaxes(prm["ssm_c_im"][l], -1, -2).reshape(ndir, 2, gh, n, cg)
    cm = jnp.concatenate([_block_diag(c_re), _block_diag(-c_im)], axis=-2)
    a = jnp.stack([ar.reshape(ndir, 2, gh * n), ai.reshape(ndir, 2, gh * n)], axis=2)
    a = jnp.broadcast_to(a.reshape(ndir, 1, 2 * 2 * gh * n), (ndir, SUBLANES, 2 * 2 * gh * n))

    row = lambda v: v.reshape(1, -1)
    lw = dict(
        w_small=w_small, w_gate=w_in[:, gate0:].astype(BF16),
        norm1_g=row(prm["norm1_g"][l]), norm2_g=row(prm["norm2_g"][l]),
        q_a_norm_g=row(prm["q_a_norm_g"][l]), kv_a_norm_g=row(prm["kv_a_norm_g"][l]),
        q_norm_g=row(_pad_heads(jnp.tile(prm["q_norm_g"][l], MLA_HEADS), QK_HEAD, 0)),
        k_norm_g=row(_pad_heads(jnp.tile(prm["k_norm_g"][l], MLA_HEADS), QK_HEAD, 0)),
        w_uq=_pad_heads(prm["w_uq"][l], QK_HEAD, 0).astype(BF16),
        w_uk=w_uk.astype(BF16), w_uv=w_uv.astype(BF16),
        w_mla_o=prm["w_mla_o"][l].astype(BF16),
        conv_w=prm["conv_w"][l], conv_b=row(prm["conv_b"][l]),
        w_conv_o=prm["w_conv_o"][l].astype(BF16),
        ssm_a=a.astype(F32), ssm_bm=bm.astype(BF16), ssm_cm=cm.astype(BF16),
        ssm_d=row(prm["ssm_d"][l]), w_glu=prm["w_glu"][l].astype(BF16),
        pool_w=prm["pool_w"][l].astype(BF16), pool_scale=row(prm["pool_scale"][l]),
        w_pool_o=prm["w_pool_o"][l].astype(BF16), w_o=prm["w_o"][l].astype(BF16),
        w_mlp1=prm["w_mlp1"][l].astype(BF16), w_mlp2=prm["w_mlp2"][l].astype(BF16),
    )
    return lw, cols


def _rope_tables(seq_len):
    half = ROPE_DIM // 2
    nf = half // 2
    inv_freq = ROPE_THETA ** (-jnp.arange(0, half, 2, dtype=F32) / half)
    t = jnp.arange(seq_len)
    ang_r = (t // GRID_W).astype(F32)[:, None] * inv_freq
    ang_c = (t % GRID_W).astype(F32)[:, None] * inv_freq
    zeros = jnp.zeros((seq_len, nf), F32)
    ones_lo = jnp.ones((seq_len, QK_NOPE), F32)
    zeros_lo = jnp.zeros((seq_len, QK_NOPE), F32)
    tail = jnp.zeros((seq_len, HEAD_PAD - QK_HEAD), F32)
    c = jnp.concatenate([ones_lo, jnp.cos(ang_r), jnp.cos(ang_r), jnp.cos(ang_c), jnp.cos(ang_c), tail], axis=1)
    s1 = jnp.concatenate([zeros_lo, -jnp.sin(ang_r), zeros, -jnp.sin(ang_c), zeros, tail], axis=1)
    s2 = jnp.concatenate([zeros_lo, zeros, jnp.sin(ang_r), zeros, jnp.sin(ang_c), tail], axis=1)
    return c, s1, s2


def _states_to_cols(st):
    b, ndir, _, g, n = st.shape
    st = st.reshape(b, ndir, 2, 2, (g // 2) * n)
    return jnp.transpose(st, (1, 0, 3, 2, 4)).reshape(ndir, b, 2 * g * n)


def _cols_to_states(hf, g, n):
    ndir, b, _ = hf.shape
    hf = hf.reshape(ndir, b, 2, 2, (g // 2) * n)
    return jnp.transpose(hf, (1, 0, 3, 2, 4)).reshape(b, ndir, 2, g, n)


def _mixer_layer(x, mod, lw, cols, batch, seq_len, per_batch, rope, ctx):
    p, h1 = _inproj_call(x, mod, lw["norm1_g"], lw["w_small"], seq_len, per_batch)
    q, k, v, ckv = _qkprep_call(p, lw, rope, seq_len, cols)
    if ctx is not None:
        ctx_ckv, ctx_kpe, h0 = ctx
        kc, vc = _ctxprep_call(ctx_ckv, ctx_kpe, lw)
        ctx_len = ctx_ckv.shape[0] // batch
    else:
        kc = vc = None
        ctx_len = 0
        h0 = jnp.zeros((2, batch, lw["ssm_a"].shape[-1]), F32)
    za = _attn_call(q, k, v, kc, vc, batch, seq_len, ctx_len)
    zb, zd = _convpool_call(p, lw, batch, seq_len, cols)

    w = zb.shape[1]
    su = cols["ssm_u"]
    u_tm = jnp.transpose(p[:, su:su + w].reshape(batch, seq_len, w), (1, 0, 2))
    y_tm, hf = _ssm_call(u_tm, h0, lw)
    y2 = jnp.transpose(y_tm, (0, 2, 1, 3)).reshape(2, batch * seq_len, w)

    merged = _merge_call(h1, p, za, zb, y2, zd, lw, cols)
    x1, h2 = _outproj_call(x, merged, mod, lw, seq_len, per_batch)
    x2 = _mlp_call(x1, h2, mod, lw, seq_len, per_batch)
    kr = cols["k_rope"] + QK_NOPE
    return x2, ckv, p[:, kr:kr + ROPE_DIM], hf


def kernel(x_prompt, x_sample, c, cache_ckv, cache_krope, state_ssm, c_ctx, w_ada, b_ada, norm1_g, norm2_g, w_in, q_a_norm_g, kv_a_norm_g, w_uq, w_ukv, q_norm_g, k_norm_g, w_mla_o, conv_w, conv_b, w_conv_o, ssm_lam_re, ssm_lam_im, ssm_log_step, ssm_b_re, ssm_b_im, ssm_c_re, ssm_c_im, ssm_d, w_glu, pool_w, pool_scale, w_pool_o, w_o, w_mlp1, w_mlp2):
    prm = dict(norm1_g=norm1_g, norm2_g=norm2_g, w_in=w_in, q_a_norm_g=q_a_norm_g, kv_a_norm_g=kv_a_norm_g,
               w_uq=w_uq, w_ukv=w_ukv, q_norm_g=q_norm_g, k_norm_g=k_norm_g, w_mla_o=w_mla_o,
               conv_w=conv_w, conv_b=conv_b, w_conv_o=w_conv_o, ssm_lam_re=ssm_lam_re, ssm_lam_im=ssm_lam_im,
               ssm_log_step=ssm_log_step, ssm_b_re=ssm_b_re, ssm_b_im=ssm_b_im, ssm_c_re=ssm_c_re,
               ssm_c_im=ssm_c_im, ssm_d=ssm_d, w_glu=w_glu, pool_w=pool_w, pool_scale=pool_scale,
               w_pool_o=w_pool_o, w_o=w_o, w_mlp1=w_mlp1, w_mlp2=w_mlp2)
    depth = w_in.shape[0]
    bp, lp, d = x_prompt.shape
    bs, ls, _ = x_sample.shape
    past = cache_ckv.shape[2]
    g, n = state_ssm.shape[-2:]
    sizes = (w_uq.shape[1], w_ukv.shape[1], conv_w.shape[-1], ssm_d.shape[-1], pool_scale.shape[-1])

    rows = -(-(1 + bs) // SUBLANES) * SUBLANES
    cvec = jnp.zeros((rows, d), F32).at[0].set(c_ctx).at[1:1 + bs].set(c)
    mods = _ada_call(cvec, w_ada, b_ada)
    rope = _rope_tables(ls)

    yp = x_prompt.reshape(bp * lp, d)
    ys = x_sample.reshape(bs * ls, d)
    ckv_list, krope_list, ssm_list = [], [], []
    for l in range(depth):
        lw, cols = _prep_layer(l, prm, sizes)
        mod_ctx = mods[l, 0:1].reshape(6, 1, d)
        mod_lat = mods[l, 1:1 + bs].reshape(bs * 6, 1, d)
        yp, ckv_l, krope_l, hf_l = _mixer_layer(yp, mod_ctx, lw, cols, bp, lp, False, None, None)
        ckv_list.append(ckv_l.reshape(bp, lp, -1))
        krope_list.append(krope_l.reshape(bp, lp, -1))
        ssm_list.append(_cols_to_states(hf_l, g, n))
        ctx_kpe = jnp.pad(cache_krope[:, l].reshape(bs * past, ROPE_DIM),
                          ((0, 0), (QK_NOPE, HEAD_PAD - QK_HEAD)))
        ctx = (cache_ckv[:, l].reshape(bs * past, -1), ctx_kpe, _states_to_cols(state_ssm[:, l]))
        ys, _, _, _ = _mixer_layer(ys, mod_lat, lw, cols, bs, ls, True, rope, ctx)
    return (yp.reshape(bp, lp, d), ys.reshape(bs, ls, d), jnp.stack(ckv_list, axis=1),
            jnp.stack(krope_list, axis=1), jnp.stack(ssm_list, axis=1))
```

```python
import functools
import math

import jax
import jax.numpy as jnp
from jax import lax
from jax.experimental import pallas as pl
from jax.experimental.pallas import tpu as pltpu

F32 = jnp.float32
BF16 = jnp.bfloat16

GRID_W = 64
N_BRANCH = 4
MLA_HEADS = 8
QK_NOPE = 64
ROPE_DIM = 32
QK_HEAD = QK_NOPE + ROPE_DIM
V_HEAD = 64
ROPE_THETA = 10000.0
SSM_GROUP_CH = 16
POOL_WINDOWS = (2, 4, 8, 16)
EPS = 1e-6
LOG2E = 1.4426950408889634

LANES = 128
SUBLANES = 8
HEAD_PAD = LANES
MIB = 1024 * 1024


def _cparams(sem, vmem_mib):
    return pltpu.CompilerParams(dimension_semantics=sem, vmem_limit_bytes=vmem_mib * MIB)


def _dot(a, b):
    return jnp.dot(a, b, preferred_element_type=F32)


def _mod_spec(chunk, d, tm, seq_len, per_batch):
    def imap(i, *_):
        row = (i * tm) // seq_len if per_batch else 0
        return (row * 6 + chunk, 0, 0)
    return pl.BlockSpec((1, 1, d), imap)


def _ada_kernel(c_ref, w_ref, b_ref, o_ref):
    cv = c_ref[...]
    s = cv * jax.nn.sigmoid(cv)
    o_ref[0] = _dot(s.astype(BF16), w_ref[0].astype(BF16)) + b_ref[0]


def _ada_call(cvec, w_ada, b_ada):
    depth, d, n = w_ada.shape
    rows = cvec.shape[0]
    tn = 1024
    return pl.pallas_call(
        _ada_kernel,
        out_shape=jax.ShapeDtypeStruct((depth, rows, n), F32),
        grid=(depth, n // tn),
        in_specs=[
            pl.BlockSpec((rows, d), lambda l, j: (0, 0)),
            pl.BlockSpec((1, d, tn), lambda l, j: (l, 0, j)),
            pl.BlockSpec((1, 1, tn), lambda l, j: (l, 0, j)),
        ],
        out_specs=pl.BlockSpec((1, rows, tn), lambda l, j: (l, 0, j)),
        compiler_params=_cparams(("parallel", "parallel"), 40),
        name="ada_mod",
    )(cvec, w_ada, b_ada.reshape(depth, 1, n))


def _modnorm(x, g, scale, shift):
    ms = jnp.mean(x * x, axis=-1, keepdims=True)
    return (x * lax.rsqrt(ms + EPS) * g) * (1.0 + scale) + shift


def _inproj_kernel(x_ref, sh_ref, sc_ref, g_ref, w_ref, o_ref, h_ref):
    h = _modnorm(x_ref[...], g_ref[...], sc_ref[0], sh_ref[0]).astype(BF16)
    h_ref[...] = h
    o_ref[...] = _dot(h, w_ref[...])


def _inproj_call(x, mod, g, w, seq_len, per_batch):
    t, d = x.shape
    n = w.shape[1]
    tm = 512
    mspec = functools.partial(_mod_spec, d=d, tm=tm, seq_len=seq_len, per_batch=per_batch)
    return pl.pallas_call(
        _inproj_kernel,
        out_shape=(jax.ShapeDtypeStruct((t, n), F32), jax.ShapeDtypeStruct((t, d), BF16)),
        grid=(t // tm,),
        in_specs=[
            pl.BlockSpec((tm, d), lambda i: (i, 0)),
            mspec(0), mspec(1),
            pl.BlockSpec((1, d), lambda i: (0, 0)),
            pl.BlockSpec((d, n), lambda i: (0, 0), pipeline_mode=pl.Buffered(1)),
        ],
        out_specs=(pl.BlockSpec((tm, n), lambda i: (i, 0)), pl.BlockSpec((tm, d), lambda i: (i, 0))),
        compiler_params=_cparams(("parallel",), 52),
        name="in_proj",
    )(x, mod, mod, g, w)


def _rms(x, g, width):
    ms = jnp.sum(x * x, axis=-1, keepdims=True) * (1.0 / width)
    return x * lax.rsqrt(ms + EPS) * g


def _heads_norm(x, g, mult, shared=None, rope=None):
    outs = []
    for h in range(MLA_HEADS):
        sl = slice(h * HEAD_PAD, (h + 1) * HEAD_PAD)
        xh = x[:, sl] if shared is None else x[:, sl] + shared
        ms = jnp.sum(xh * xh, axis=-1, keepdims=True) * (1.0 / QK_HEAD)
        r = lax.rsqrt(ms + EPS)
        y = xh * r * g[:, sl]
        if rope is not None:
            xs, shared_s, gs, cos, sin = rope
            part = shared_s if xs is None else xs[:, sl]
            y = y * cos + (part * r * gs[:, sl]) * sin
        if mult != 1.0:
            y = y * mult
        outs.append(y)
    return jnp.concatenate(outs, axis=1)


def _qkprep_kernel(*refs, use_rope, q_mult):
    if use_rope:
        (qa_ref, kva_ref, kpe_ref, gqa_ref, gkva_ref, gq_ref, gk_ref, wuq_ref, wuk_ref, wuv_ref,
         kpes_ref, gqs_ref, gks_ref, wuqs_ref, c_ref, s_ref, q_out, k_out, v_out, ckv_out) = refs
    else:
        (qa_ref, kva_ref, kpe_ref, gqa_ref, gkva_ref, gq_ref, gk_ref, wuq_ref, wuk_ref, wuv_ref,
         q_out, k_out, v_out, ckv_out) = refs
    qa = qa_ref[...]
    qa_n = _rms(qa, gqa_ref[...], qa.shape[-1]).astype(BF16)
    q = _dot(qa_n, wuq_ref[...])
    q_rope = (_dot(qa_n, wuqs_ref[...]), None, gqs_ref[...], c_ref[...], s_ref[...]) if use_rope else None
    q_out[...] = _heads_norm(q, gq_ref[...], q_mult, rope=q_rope).astype(BF16)

    kva = kva_ref[...]
    ckv = _rms(kva, gkva_ref[...], kva.shape[-1])
    ckv_out[...] = ckv
    ckv_b = ckv.astype(BF16)
    k = _dot(ckv_b, wuk_ref[...])
    k_rope = (None, kpes_ref[...], gks_ref[...], c_ref[...], s_ref[...]) if use_rope else None
    k_out[...] = _heads_norm(k, gk_ref[...], 1.0, shared=kpe_ref[...], rope=k_rope).astype(BF16)
    v_out[...] = _dot(ckv_b, wuv_ref[...]).astype(BF16)


def _qkprep_call(p, lw, rope, seq_len, cols):
    t = p.shape[0]
    tm = 512
    hq = MLA_HEADS * HEAD_PAD
    hv = MLA_HEADS * V_HEAD
    qlora = lw["w_uq"].shape[0]
    kvlora = lw["w_uk"].shape[0]
    const = lambda i: (0, 0)
    in_specs = [
        pl.BlockSpec((tm, qlora), lambda i: (i, cols["q_a"] // qlora)),
        pl.BlockSpec((tm, kvlora), lambda i: (i, cols["kv_a"] // kvlora)),
        pl.BlockSpec((tm, HEAD_PAD), lambda i: (i, cols["k_rope"] // HEAD_PAD)),
        pl.BlockSpec((1, qlora), const),
        pl.BlockSpec((1, kvlora), const),
        pl.BlockSpec((1, hq), const),
        pl.BlockSpec((1, hq), const),
        pl.BlockSpec((qlora, hq), const),
        pl.BlockSpec((kvlora, hq), const),
        pl.BlockSpec((kvlora, hv), const),
    ]
    args = [p, p, p, lw["q_a_norm_g"], lw["kv_a_norm_g"], lw["q_norm_g"], lw["k_norm_g"],
            lw["w_uq"], lw["w_uk"], lw["w_uv"]]
    if rope is not None:
        nblk = seq_len // tm
        in_specs += [
            pl.BlockSpec((tm, HEAD_PAD), lambda i: (i, cols["k_rope_sw"] // HEAD_PAD)),
            pl.BlockSpec((1, hq), const), pl.BlockSpec((1, hq), const),
            pl.BlockSpec((qlora, hq), const),
            pl.BlockSpec((tm, HEAD_PAD), lambda i: (i % nblk, 0)),
            pl.BlockSpec((tm, HEAD_PAD), lambda i: (i % nblk, 0)),
        ]
        args += [p, lw["q_norm_g_sw"], lw["k_norm_g_sw"], lw["w_uq_sw"]] + list(rope)
    q_mult = LOG2E * QK_HEAD ** -0.5
    return pl.pallas_call(
        functools.partial(_qkprep_kernel, use_rope=rope is not None, q_mult=q_mult),
        out_shape=(jax.ShapeDtypeStruct((t, hq), BF16), jax.ShapeDtypeStruct((t, hq), BF16),
                   jax.ShapeDtypeStruct((t, hv), BF16), jax.ShapeDtypeStruct((t, kvlora), F32)),
        grid=(t // tm,),
        in_specs=in_specs,
        out_specs=(pl.BlockSpec((tm, hq), lambda i: (i, 0)), pl.BlockSpec((tm, hq), lambda i: (i, 0)),
                   pl.BlockSpec((tm, hv), lambda i: (i, 0)), pl.BlockSpec((tm, kvlora), lambda i: (i, 0))),
        compiler_params=_cparams(("parallel",), 40),
        name="qk_prep",
    )(*args)


def _ctxprep_kernel(ckv_ref, kpe_ref, gk_ref, wuk_ref, wuv_ref, k_out, v_out):
    ckv_b = ckv_ref[...].astype(BF16)
    k = _dot(ckv_b, wuk_ref[...])
    k_out[...] = _heads_norm(k, gk_ref[...], 1.0, shared=kpe_ref[...]).astype(BF16)
    v_out[...] = _dot(ckv_b, wuv_ref[...]).astype(BF16)


def _ctxprep_call(ckv, kpe, lw):
    t, kvlora = ckv.shape
    tm = 512
    hq = MLA_HEADS * HEAD_PAD
    hv = MLA_HEADS * V_HEAD
    const = lambda i: (0, 0)
    return pl.pallas_call(
        _ctxprep_kernel,
        out_shape=(jax.ShapeDtypeStruct((t, hq), BF16), jax.ShapeDtypeStruct((t, hv), BF16)),
        grid=(t // tm,),
        in_specs=[
            pl.BlockSpec((tm, kvlora), lambda i: (i, 0)),
            pl.BlockSpec((tm, HEAD_PAD), lambda i: (i, 0)),
            pl.BlockSpec((1, hq), const),
            pl.BlockSpec((kvlora, hq), const),
            pl.BlockSpec((kvlora, hv), const),
        ],
        out_specs=(pl.BlockSpec((tm, hq), lambda i: (i, 0)), pl.BlockSpec((tm, hv), lambda i: (i, 0))),
        compiler_params=_cparams(("parallel",), 40),
        name="ctx_prep",
    )(ckv, kpe, lw["k_norm_g"], lw["w_uk"], lw["w_uv"])


def _attn_kernel(*refs, has_ctx, heads):
    if has_ctx:
        q_ref, ko_ref, vo_ref, kc_ref, vc_ref, o_ref = refs
    else:
        q_ref, ko_ref, vo_ref, o_ref = refs
    contract_last = (((1,), (1,)), ((), ()))
    lane = lax.broadcasted_iota(jnp.int32, (q_ref.shape[0], 2 * V_HEAD), 1)
    for pair in range(heads // 2):
        vsl = slice(pair * 2 * V_HEAD, (pair + 1) * 2 * V_HEAD)
        outs = []
        for hh in range(2):
            h = 2 * pair + hh
            sl = slice(h * HEAD_PAD, (h + 1) * HEAD_PAD)
            q = q_ref[:, sl]
            s_o = lax.dot_general(q, ko_ref[:, sl], contract_last, preferred_element_type=F32)
            m = jnp.max(s_o, axis=-1, keepdims=True)
            if has_ctx:
                s_c = lax.dot_general(q, kc_ref[:, sl], contract_last, preferred_element_type=F32)
                m = jnp.maximum(m, jnp.max(s_c, axis=-1, keepdims=True))
            p_o = jnp.exp2(s_o - m)
            l = jnp.sum(p_o, axis=-1, keepdims=True)
            acc = _dot(p_o.astype(BF16), vo_ref[:, vsl])
            if has_ctx:
                p_c = jnp.exp2(s_c - m)
                l = l + jnp.sum(p_c, axis=-1, keepdims=True)
                acc = acc + _dot(p_c.astype(BF16), vc_ref[:, vsl])
            outs.append(acc / l)
        o_ref[:, vsl] = jnp.where(lane < V_HEAD, outs[0], outs[1]).astype(BF16)


def _attn_call(q, k, v, kc, vc, batch, seq_len, ctx_len):
    t = q.shape[0]
    tq = 256
    heads = 8
    nq = seq_len // tq
    npair = MLA_HEADS // heads
    qw = heads * HEAD_PAD
    vw = heads * V_HEAD
    in_specs = [
        pl.BlockSpec((tq, qw), lambda b, h, i: (b * nq + i, h)),
        pl.BlockSpec((seq_len, qw), lambda b, h, i: (b, h)),
        pl.BlockSpec((seq_len, vw), lambda b, h, i: (b, h)),
    ]
    args = [q, k, v]
    if kc is not None:
        in_specs += [pl.BlockSpec((ctx_len, qw), lambda b, h, i: (b, h)),
                     pl.BlockSpec((ctx_len, vw), lambda b, h, i: (b, h))]
        args += [kc, vc]
    return pl.pallas_call(
        functools.partial(_attn_kernel, has_ctx=kc is not None, heads=heads),
        out_shape=jax.ShapeDtypeStruct((t, MLA_HEADS * V_HEAD), BF16),
        grid=(batch, npair, nq),
        in_specs=in_specs,
        out_specs=pl.BlockSpec((tq, vw), lambda b, h, i: (b * nq + i, h)),
        compiler_params=_cparams(("parallel", "parallel", "arbitrary"), 48),
        name="attention",
    )(*args)


def _convpool_kernel(cu_ref, bg_ref, cg_ref, pu_ref, cw_ref, cb_ref, pw_ref, ps_ref, zb_ref, zd_ref):
    n = cu_ref.shape[0]
    row = lax.broadcasted_iota(jnp.int32, cu_ref.shape, 0)

    def down(x, k):
        return jnp.where(row >= k, pltpu.roll(x, k, axis=0), 0.0)

    def up(x, k):
        return jnp.where(row < n - k, pltpu.roll(x, n - k, axis=0), 0.0)

    v = cg_ref[...] * cu_ref[...]
    conv = (down(v, 1) * cw_ref[0:1, :] + v * cw_ref[1:2, :] + up(v, 1) * cw_ref[2:3, :]
            + cb_ref[...])
    zb_ref[...] = (bg_ref[...] * conv).astype(BF16)

    u = pu_ref[...]
    rowf = row.astype(F32)
    grp = pl.program_id(1)
    for gi, win in enumerate(POOL_WINDOWS):
        @pl.when(grp == gi)
        def _(win=win):
            half = win // 2
            fwd = u
            bwd = u
            k = 1
            while k < half:
                fwd = fwd + up(fwd, k)
                bwd = bwd + down(bwd, k)
                k *= 2
            total = fwd + down(bwd, 1)
            cnt = jnp.minimum(rowf + half, float(n)) - jnp.maximum(rowf - half, 0.0)
            mean = total / cnt - u
            zd_ref[...] = (_dot(mean.astype(BF16), pw_ref[0]) * ps_ref[...]).astype(BF16)


def _convpool_call(p, lw, batch, seq_len, cols):
    t = p.shape[0]
    cw = LANES
    nblk = lw["conv_w"].shape[1] // cw

    def pspec(name):
        base = cols[name] // cw
        return pl.BlockSpec((seq_len, cw), lambda b, g: (b, base + g))

    vec = lambda b, g: (0, g)
    return pl.pallas_call(
        _convpool_kernel,
        out_shape=(jax.ShapeDtypeStruct((t, nblk * cw), BF16), jax.ShapeDtypeStruct((t, nblk * cw), BF16)),
        grid=(batch, nblk),
        in_specs=[pspec("conv_u"), pspec("conv_bg"), pspec("conv_cg"), pspec("pool_u"),
                  pl.BlockSpec((3, cw), vec), pl.BlockSpec((1, cw), vec),
                  pl.BlockSpec((1, cw, cw), lambda b, g: (g, 0, 0)), pl.BlockSpec((1, cw), vec)],
        out_specs=(pl.BlockSpec((seq_len, cw), lambda b, g: (b, g)),
                   pl.BlockSpec((seq_len, cw), lambda b, g: (b, g))),
        compiler_params=_cparams(("parallel", "parallel"), 40),
        name="conv_pool",
    )(p, p, p, p, lw["conv_w"], lw["conv_b"], lw["pool_w"], lw["pool_scale"])


SSM_R = 8
PAIR_W = 2 * LANES


def _ssm_kernel(u_ref, h0_ref, a_ref, mb_ref, m1_ref, m2_ref, y_ref, hf_ref, xs_scr, st_scr, *, steps, nchunks):
    d = pl.program_id(0)
    c = pl.program_id(2)
    width = st_scr.shape[1]
    npair = width // PAIR_W
    rows = steps * SUBLANES

    @pl.when(c == 0)
    def _():
        st_scr[...] = h0_ref[0]

    u = u_ref[...]
    for pp in range(npair):
        sl = slice(pp * PAIR_W, (pp + 1) * PAIR_W)
        xs_scr[:, sl] = _dot(u[:, sl], mb_ref[0, pp])

    gw = width // 2
    for part in range(width // gw):
        base = part * gw
        offs = [(base + k * PAIR_W, base + k * PAIR_W + LANES) for k in range(gw // PAIR_W)]
        coef = [(a_ref[0, :, r0:r0 + LANES], a_ref[0, :, i0:i0 + LANES]) for r0, i0 in offs]

        def body(t, carry, offs=offs, coef=coef):
            tt = t + d * (steps - 1 - 2 * t)
            row0 = pl.multiple_of(tt * SUBLANES, SUBLANES)
            new = []
            for (r0, i0), (a_re, a_im), x_re, x_im in zip(offs, coef, carry[0::2], carry[1::2]):
                n_re = (a_re * x_re - a_im * x_im) + xs_scr[pl.ds(row0, SUBLANES), r0:r0 + LANES]
                n_im = (a_re * x_im + a_im * x_re) + xs_scr[pl.ds(row0, SUBLANES), i0:i0 + LANES]
                xs_scr[pl.ds(row0, SUBLANES), r0:r0 + LANES] = x_re
                xs_scr[pl.ds(row0, SUBLANES), i0:i0 + LANES] = x_im
                new += [n_re, n_im]
            return tuple(new)

        init = []
        for r0, i0 in offs:
            init += [st_scr[:, r0:r0 + LANES], st_scr[:, i0:i0 + LANES]]
        final = lax.fori_loop(0, steps, body, tuple(init), unroll=2)
        for (r0, i0), x_re, x_im in zip(offs, final[0::2], final[1::2]):
            st_scr[:, r0:r0 + LANES] = x_re
            st_scr[:, i0:i0 + LANES] = x_im

    for pp in range(npair):
        sl = slice(pp * PAIR_W, (pp + 1) * PAIR_W)
        y_ref[0, :, sl] = _dot(xs_scr[:, sl].astype(BF16), m1_ref[0, pp]) + _dot(u[:, sl], m2_ref[0, pp])

    @pl.when(c == nchunks - 1)
    def _():
        hf_ref[0] = st_scr[...]


def _ssm_call(u4, h0, lw, nsteps):
    nrows, width = u4.shape
    nbg = nrows // (nsteps * SUBLANES)
    steps = min(nsteps, 64)
    nchunks = nsteps // steps
    npair = width // PAIR_W
    rows = steps * SUBLANES

    def cidx(d, b, c):
        return b * nchunks + c + d * (nchunks - 1 - 2 * c)

    wspec = pl.BlockSpec((1, npair, PAIR_W, PAIR_W), lambda d, b, c: (d, 0, 0, 0))
    return pl.pallas_call(
        functools.partial(_ssm_kernel, steps=steps, nchunks=nchunks),
        out_shape=(jax.ShapeDtypeStruct((2, nrows, width), F32),
                   jax.ShapeDtypeStruct((2, nbg * SUBLANES, width), F32)),
        grid=(2, nbg, nchunks),
        in_specs=[
            pl.BlockSpec((rows, width), lambda d, b, c: (cidx(d, b, c), 0)),
            pl.BlockSpec((1, SUBLANES, width), lambda d, b, c: (d, b, 0)),
            pl.BlockSpec((1, SUBLANES, width), lambda d, b, c: (d, 0, 0)),
            wspec, wspec, wspec,
        ],
        out_specs=(pl.BlockSpec((1, rows, width), lambda d, b, c: (d, cidx(d, b, c), 0)),
                   pl.BlockSpec((1, SUBLANES, width), lambda d, b, c: (d, b, 0))),
        scratch_shapes=[pltpu.VMEM((steps * SUBLANES, width), F32), pltpu.VMEM((SUBLANES, width), F32)],
        compiler_params=_cparams(("arbitrary", "arbitrary", "arbitrary"), 52),
        name="s5_scan",
    )(u4, h0, lw["ssm_a"], lw["ssm_mb"], lw["ssm_m1"], lw["ssm_m2"])


def _merge_kernel(h_ref, za_ref, zb_ref, yf_ref, yb_ref, su_ref, sd_ref, zd_ref,
                  wg0, wg1, wg2, wg3, wa, wb, wca, wcg, wd, o_ref, y_scr):
    @pl.when(pl.program_id(1) == 0)
    def _():
        y_scr[...] = ((yf_ref[0] + yb_ref[0]) + sd_ref[...] * su_ref[...]).astype(BF16)

    h = h_ref[...]
    y = y_scr[...]

    def gate(w_ref):
        return jax.nn.sigmoid(_dot(h, w_ref[...]))

    merged = gate(wg0) * _dot(za_ref[...], wa[...])
    merged = merged + gate(wg1) * _dot(zb_ref[...], wb[...])
    merged = merged + gate(wg2) * (_dot(y, wca[...]) * jax.nn.sigmoid(_dot(y, wcg[...])))
    merged = merged + gate(wg3) * _dot(zd_ref[...], wd[...])
    o_ref[...] = merged.astype(BF16)


def _merge_call(h, p, za, zb, y2, zd, lw, cols):
    t, d = h.shape
    bw = za.shape[1]
    tm, tn = 512, 512
    nj = d // tn
    row = lambda i, j: (i, 0)
    colj = lambda i, j: (0, j)
    su_blk = cols["ssm_u"] // bw

    def gspec(k):
        return pl.BlockSpec((d, tn), lambda i, j: (0, k * nj + j))

    in_specs = [
        pl.BlockSpec((tm, d), row),
        pl.BlockSpec((tm, bw), row), pl.BlockSpec((tm, bw), row),
        pl.BlockSpec((1, tm, bw), lambda i, j: (0, i, 0)), pl.BlockSpec((1, tm, bw), lambda i, j: (1, i, 0)),
        pl.BlockSpec((tm, bw), lambda i, j: (i, su_blk)),
        pl.BlockSpec((1, bw), lambda i, j: (0, 0)),
        pl.BlockSpec((tm, bw), row),
        gspec(0), gspec(1), gspec(2), gspec(3),
        pl.BlockSpec((bw, tn), colj), pl.BlockSpec((bw, tn), colj),
        pl.BlockSpec((bw, tn), colj), pl.BlockSpec((bw, tn), lambda i, j: (0, nj + j)),
        pl.BlockSpec((bw, tn), colj),
    ]
    wg = lw["w_gate"]
    return pl.pallas_call(
        _merge_kernel,
        out_shape=jax.ShapeDtypeStruct((t, d), BF16),
        grid=(t // tm, nj),
        in_specs=in_specs,
        out_specs=pl.BlockSpec((tm, tn), lambda i, j: (i, j)),
        scratch_shapes=[pltpu.VMEM((tm, bw), BF16)],
        compiler_params=_cparams(("parallel", "arbitrary"), 48),
        name="branch_merge",
    )(h, za, zb, y2, y2, p, lw["ssm_d"], zd,
      wg, wg, wg, wg, lw["w_mla_o"], lw["w_conv_o"], lw["w_glu"], lw["w_glu"], lw["w_pool_o"])


def _outproj_kernel(x_ref, m_ref, gt_ref, sh_ref, sc_ref, g_ref, wo_ref, x1_ref, h2_ref):
    x1 = x_ref[...] + gt_ref[0] * _dot(m_ref[...], wo_ref[...])
    x1_ref[...] = x1
    h2_ref[...] = _modnorm(x1, g_ref[...], sc_ref[0], sh_ref[0]).astype(BF16)


def _outproj_call(x, merged, mod, lw, seq_len, per_batch):
    t, d = x.shape
    tm = 512
    mspec = functools.partial(_mod_spec, d=d, tm=tm, seq_len=seq_len, per_batch=per_batch)
    row = lambda i: (i, 0)
    return pl.pallas_call(
        _outproj_kernel,
        out_shape=(jax.ShapeDtypeStruct((t, d), F32), jax.ShapeDtypeStruct((t, d), BF16)),
        grid=(t // tm,),
        in_specs=[
            pl.BlockSpec((tm, d), row), pl.BlockSpec((tm, d), row),
            mspec(2), mspec(3), mspec(4),
            pl.BlockSpec((1, d), lambda i: (0, 0)),
            pl.BlockSpec((d, d), lambda i: (0, 0), pipeline_mode=pl.Buffered(1)),
        ],
        out_specs=(pl.BlockSpec((tm, d), row), pl.BlockSpec((tm, d), row)),
        compiler_params=_cparams(("parallel",), 48),
        name="out_proj",
    )(x, merged, mod, mod, mod, lw["norm2_g"], lw["w_o"])


def _mlp_kernel(x_ref, h_ref, gt_ref, w1_ref, w2_ref, o_ref):
    @pl.when(pl.program_id(1) == 0)
    def _():
        o_ref[...] = x_ref[...]

    hid = jnp.square(jnp.maximum(_dot(h_ref[...], w1_ref[...]), 0.0))
    o_ref[...] += gt_ref[0] * _dot(hid.astype(BF16), w2_ref[...])


def _mlp_call(x, h, mod, lw, seq_len, per_batch):
    t, d = x.shape
    hidden = lw["w_mlp1"].shape[1]
    tm, tk = 512, 1024
    mspec = functools.partial(_mod_spec, d=d, tm=tm, seq_len=seq_len, per_batch=per_batch)
    row = lambda i, k: (i, 0)
    return pl.pallas_call(
        _mlp_kernel,
        out_shape=jax.ShapeDtypeStruct((t, d), F32),
        grid=(t // tm, hidden // tk),
        in_specs=[
            pl.BlockSpec((tm, d), row), pl.BlockSpec((tm, d), row), mspec(5),
            pl.BlockSpec((d, tk), lambda i, k: (0, k)),
            pl.BlockSpec((tk, d), lambda i, k: (k, 0)),
        ],
        out_specs=pl.BlockSpec((tm, d), row),
        compiler_params=_cparams(("parallel", "arbitrary"), 48),
        name="mlp",
    )(x, h, mod, lw["w_mlp1"], lw["w_mlp2"])


def _pad_heads(w, per_head, lo):
    lead = w.shape[:-1]
    w = w.reshape(lead + (MLA_HEADS, per_head))
    w = jnp.pad(w, [(0, 0)] * len(lead) + [(0, 0), (lo, HEAD_PAD - lo - per_head)])
    return w.reshape(lead + (MLA_HEADS * HEAD_PAD,))


_ROPE_PARTNER = tuple((i // 16) * 16 + (i % 16 + 8) % 16 for i in range(ROPE_DIM))


def _swap_heads(w):
    lead = w.shape[:-1]
    w = w.reshape(lead + (MLA_HEADS, QK_HEAD))[..., QK_NOPE:][..., jnp.array(_ROPE_PARTNER)]
    w = jnp.pad(w, [(0, 0)] * len(lead) + [(0, 0), (QK_NOPE, HEAD_PAD - QK_HEAD)])
    return w.reshape(lead + (MLA_HEADS * HEAD_PAD,))


def _prep_ssm(lam_re, lam_im, log_step, b_re, b_im, c_re, c_im):
    r = SSM_R
    ndir, g, n, cg = b_re.shape
    npair = g // 2
    step = jnp.exp(log_step)[..., None]
    p = jnp.arange(r + 1, dtype=F32).reshape(-1, 1, 1, 1)
    mag = jnp.exp(p * (lam_re * step))
    pw_re = mag * jnp.cos(p * (lam_im * step))
    pw_im = mag * jnp.sin(p * (lam_im * step))
    ar, ai = pw_re[1], pw_im[1]
    den = lam_re * lam_re + lam_im * lam_im
    qr = ((ar - 1.0) * lam_re + ai * lam_im) / den
    qi = (ai * lam_re - (ar - 1.0) * lam_im) / den
    bb_re = qr[..., None] * b_re - qi[..., None] * b_im
    bb_im = qr[..., None] * b_im + qi[..., None] * b_re
    eye = jnp.eye(2, dtype=F32)

    def sel(t, powers):
        return jnp.stack([t[powers[dd], dd] for dd in range(ndir)])

    t_re = pw_re[..., None] * bb_re - pw_im[..., None] * bb_im
    t_im = pw_re[..., None] * bb_im + pw_im[..., None] * bb_re
    pj = jnp.stack([jnp.arange(r)[::-1], jnp.arange(r)])

    def inc_block(t):
        return jnp.einsum("djpgnc,gh->dpjgchn", t.reshape(ndir, r, npair, 2, n, cg), eye)

    mb = jnp.stack([inc_block(sel(t_re, pj)), inc_block(sel(t_im, pj))], axis=5)
    mb = mb.reshape(ndir, npair, r * 2 * cg, 2 * 2 * n)

    cl_re = c_re * pw_re[:, :, :, None, :] - c_im * pw_im[:, :, :, None, :]
    cl_im = c_re * pw_im[:, :, :, None, :] + c_im * pw_re[:, :, :, None, :]
    p1 = jnp.stack([jnp.arange(1, r + 1), jnp.arange(r, 0, -1)])

    def out_block(t):
        return jnp.einsum("djpgcn,gh->dphnjgc", t.reshape(ndir, r, npair, 2, cg, n), eye)

    m1 = jnp.stack([out_block(sel(cl_re, p1)), out_block(-sel(cl_im, p1))], axis=2)
    m1 = m1.reshape(ndir, npair, 2 * 2 * n, r * 2 * cg)

    hi = lax.Precision.HIGHEST
    k2 = (jnp.einsum("pdgcn,dgnk->pdgck", cl_re[:r], bb_re, precision=hi)
          - jnp.einsum("pdgcn,dgnk->pdgck", cl_im[:r], bb_im, precision=hi))
    diff = jnp.arange(r)[None, :] - jnp.arange(r)[:, None]
    taps = []
    for dd, sign in enumerate((1, -1)):
        causal = (sign * diff >= 0)[..., None, None, None]
        kk = jnp.where(causal, k2[jnp.abs(diff), dd], 0.0).reshape(r, r, npair, 2, cg, cg)
        taps.append(jnp.einsum("ijpgck,gh->pihkjgc", kk, eye))
    m2 = jnp.stack(taps).reshape(ndir, npair, r * 2 * cg, r * 2 * cg)

    a = jnp.stack([pw_re[r].reshape(ndir, npair, 2 * n), pw_im[r].reshape(ndir, npair, 2 * n)], axis=2)
    a = jnp.broadcast_to(a.reshape(ndir, 1, npair * 4 * n), (ndir, SUBLANES, npair * 4 * n))
    return a, mb.astype(BF16), m1.astype(BF16), m2.astype(BF16)


def _prep_layer(l, prm, sizes):
    q_lora, kv_lora, conv_w, ssm_w, pool_w_ = sizes
    w_in = prm["w_in"][l]
    d = w_in.shape[0]
    offs = {}
    o = 0
    for name, width in (("q_a", q_lora), ("kv_a", kv_lora), ("k_rope", ROPE_DIM), ("conv_u", conv_w),
                        ("conv_bg", conv_w), ("conv_cg", conv_w), ("ssm_u", ssm_w), ("pool_u", pool_w_)):
        offs[name] = (o, o + width)
        o += width
    gate0 = o
    take = lambda name: w_in[:, offs[name][0]:offs[name][1]]
    rope_lanes = ((0, 0), (QK_NOPE, HEAD_PAD - QK_HEAD))
    kr_pad = jnp.pad(take("k_rope"), rope_lanes)
    kr_sw = jnp.pad(take("k_rope")[:, jnp.array(_ROPE_PARTNER)], rope_lanes)
    pieces = [("q_a", take("q_a")), ("conv_u", take("conv_u")), ("conv_bg", take("conv_bg")),
              ("conv_cg", take("conv_cg")), ("ssm_u", take("ssm_u")), ("pool_u", take("pool_u")),
              ("kv_a", take("kv_a")), ("k_rope", kr_pad), ("k_rope_sw", kr_sw)]
    cols = {}
    o = 0
    for name, w in pieces:
        cols[name] = o
        o += w.shape[1]
    w_small = jnp.concatenate([w for _, w in pieces], axis=1).astype(BF16)

    w_ukv = prm["w_ukv"][l].reshape(kv_lora, MLA_HEADS, QK_NOPE + V_HEAD)
    w_uk = _pad_heads(w_ukv[:, :, :QK_NOPE].reshape(kv_lora, -1), QK_NOPE, 0)
    w_uv = w_ukv[:, :, QK_NOPE:].reshape(kv_lora, -1)

    ssm_a, ssm_mb, ssm_m1, ssm_m2 = _prep_ssm(
        prm["ssm_lam_re"][l], prm["ssm_lam_im"][l], prm["ssm_log_step"][l], prm["ssm_b_re"][l],
        prm["ssm_b_im"][l], prm["ssm_c_re"][l], prm["ssm_c_im"][l])

    row = lambda v: v.reshape(1, -1)
    lw = dict(
        w_small=w_small, w_gate=w_in[:, gate0:].astype(BF16),
        norm1_g=row(prm["norm1_g"][l]), norm2_g=row(prm["norm2_g"][l]),
        q_a_norm_g=row(prm["q_a_norm_g"][l]), kv_a_norm_g=row(prm["kv_a_norm_g"][l]),
        q_norm_g=row(_pad_heads(jnp.tile(prm["q_norm_g"][l], MLA_HEADS), QK_HEAD, 0)),
        k_norm_g=row(_pad_heads(jnp.tile(prm["k_norm_g"][l], MLA_HEADS), QK_HEAD, 0)),
        q_norm_g_sw=row(_swap_heads(jnp.tile(prm["q_norm_g"][l], MLA_HEADS))),
        k_norm_g_sw=row(_swap_heads(jnp.tile(prm["k_norm_g"][l], MLA_HEADS))),
        w_uq=_pad_heads(prm["w_uq"][l], QK_HEAD, 0).astype(BF16),
        w_uq_sw=_swap_heads(prm["w_uq"][l]).astype(BF16),
        w_uk=w_uk.astype(BF16), w_uv=w_uv.astype(BF16),
        w_mla_o=prm["w_mla_o"][l].astype(BF16),
        conv_w=prm["conv_w"][l], conv_b=row(prm["conv_b"][l]),
        w_conv_o=prm["w_conv_o"][l].astype(BF16),
        ssm_a=ssm_a, ssm_mb=ssm_mb, ssm_m1=ssm_m1, ssm_m2=ssm_m2,
        ssm_d=row(prm["ssm_d"][l]), w_glu=prm["w_glu"][l].astype(BF16),
        pool_w=prm["pool_w"][l].astype(BF16), pool_scale=row(prm["pool_scale"][l]),
        w_pool_o=prm["w_pool_o"][l].astype(BF16), w_o=prm["w_o"][l].astype(BF16),
        w_mlp1=prm["w_mlp1"][l].astype(BF16), w_mlp2=prm["w_mlp2"][l].astype(BF16),
    )
    return lw, cols


def _rope_tables(seq_len):
    half = ROPE_DIM // 2
    inv_freq = ROPE_THETA ** (-jnp.arange(0, half, 2, dtype=F32) / half)
    t = jnp.arange(seq_len)
    ang_r = (t // GRID_W).astype(F32)[:, None] * inv_freq
    ang_c = (t % GRID_W).astype(F32)[:, None] * inv_freq
    ones_lo = jnp.ones((seq_len, QK_NOPE), F32)
    zeros_lo = jnp.zeros((seq_len, QK_NOPE), F32)
    tail = jnp.zeros((seq_len, HEAD_PAD - QK_HEAD), F32)
    c = jnp.concatenate([ones_lo, jnp.cos(ang_r), jnp.cos(ang_r), jnp.cos(ang_c), jnp.cos(ang_c), tail], axis=1)
    s = jnp.concatenate([zeros_lo, -jnp.sin(ang_r), jnp.sin(ang_r), -jnp.sin(ang_c), jnp.sin(ang_c), tail], axis=1)
    return c, s


def _states_to_cols(st):
    b, ndir, _, g, n = st.shape
    st = st.reshape(b, ndir, 2, g // 2, 2 * n)
    return jnp.transpose(st, (1, 0, 3, 2, 4)).reshape(ndir, b, 2 * g * n)


def _cols_to_states(hf, g, n):
    ndir, b, _ = hf.shape
    hf = hf.reshape(ndir, b, g // 2, 2, 2 * n)
    return jnp.transpose(hf, (1, 0, 3, 2, 4)).reshape(b, ndir, 2, g, n)


def _ssm_pack(u, batch, seq_len):
    w = u.shape[1]
    pw = 2 * SSM_GROUP_CH
    u = u.reshape(batch // SUBLANES, SUBLANES, seq_len // SSM_R, SSM_R, w // pw, pw)
    u = jnp.transpose(u, (0, 2, 1, 4, 3, 5))
    return u.reshape(batch * seq_len // SSM_R, SSM_R * w)


def _ssm_unpack(y4, batch, seq_len):
    w = y4.shape[-1] // SSM_R
    pw = 2 * SSM_GROUP_CH
    y = y4.reshape(2, batch // SUBLANES, seq_len // SSM_R, SUBLANES, w // pw, SSM_R, pw)
    y = jnp.transpose(y, (0, 1, 3, 2, 5, 4, 6))
    return y.reshape(2, batch * seq_len, w)


def _mixer_layer(x, mod, lw, cols, batch, seq_len, per_batch, rope, ctx):
    p, h1 = _inproj_call(x, mod, lw["norm1_g"], lw["w_small"], seq_len, per_batch)
    q, k, v, ckv = _qkprep_call(p, lw, rope, seq_len, cols)
    if ctx is not None:
        ctx_ckv, ctx_kpe, h0 = ctx
        kc, vc = _ctxprep_call(ctx_ckv, ctx_kpe, lw)
        ctx_len = ctx_ckv.shape[0] // batch
    else:
        kc = vc = None
        ctx_len = 0
        h0 = jnp.zeros((2, batch, lw["ssm_a"].shape[-1]), F32)
    za = _attn_call(q, k, v, kc, vc, batch, seq_len, ctx_len)
    zb, zd = _convpool_call(p, lw, batch, seq_len, cols)

    w = zb.shape[1]
    su = cols["ssm_u"]
    u4 = _ssm_pack(p[:, su:su + w], batch, seq_len).astype(BF16)
    y4, hf = _ssm_call(u4, h0, lw, seq_len // SSM_R)
    y2 = _ssm_unpack(y4, batch, seq_len)

    merged = _merge_call(h1, p, za, zb, y2, zd, lw, cols)
    x1, h2 = _outproj_call(x, merged, mod, lw, seq_len, per_batch)
    x2 = _mlp_call(x1, h2, mod, lw, seq_len, per_batch)
    kr = cols["k_rope"] + QK_NOPE
    return x2, ckv, p[:, kr:kr + ROPE_DIM], hf


def kernel(x_prompt, x_sample, c, cache_ckv, cache_krope, state_ssm, c_ctx, w_ada, b_ada, norm1_g, norm2_g, w_in, q_a_norm_g, kv_a_norm_g, w_uq, w_ukv, q_norm_g, k_norm_g, w_mla_o, conv_w, conv_b, w_conv_o, ssm_lam_re, ssm_lam_im, ssm_log_step, ssm_b_re, ssm_b_im, ssm_c_re, ssm_c_im, ssm_d, w_glu, pool_w, pool_scale, w_pool_o, w_o, w_mlp1, w_mlp2):
    prm = dict(norm1_g=norm1_g, norm2_g=norm2_g, w_in=w_in, q_a_norm_g=q_a_norm_g, kv_a_norm_g=kv_a_norm_g,
               w_uq=w_uq, w_ukv=w_ukv, q_norm_g=q_norm_g, k_norm_g=k_norm_g, w_mla_o=w_mla_o,
               conv_w=conv_w, conv_b=conv_b, w_conv_o=w_conv_o, ssm_lam_re=ssm_lam_re, ssm_lam_im=ssm_lam_im,
               ssm_log_step=ssm_log_step, ssm_b_re=ssm_b_re, ssm_b_im=ssm_b_im, ssm_c_re=ssm_c_re,
               ssm_c_im=ssm_c_im, ssm_d=ssm_d, w_glu=w_glu, pool_w=pool_w, pool_scale=pool_scale,
               w_pool_o=w_pool_o, w_o=w_o, w_mlp1=w_mlp1, w_mlp2=w_mlp2)
    depth = w_in.shape[0]
    bp, lp, d = x_prompt.shape
    bs, ls, _ = x_sample.shape
    past = cache_ckv.shape[2]
    g, n = state_ssm.shape[-2:]
    sizes = (w_uq.shape[1], w_ukv.shape[1], conv_w.shape[-1], ssm_d.shape[-1], pool_scale.shape[-1])

    rows = -(-(1 + bs) // SUBLANES) * SUBLANES
    cvec = jnp.zeros((rows, d), F32).at[0].set(c_ctx).at[1:1 + bs].set(c)
    mods = _ada_call(cvec, w_ada, b_ada)
    rope = _rope_tables(ls)

    yp = x_prompt.reshape(bp * lp, d)
    ys = x_sample.reshape(bs * ls, d)
    ckv_list, krope_list, ssm_list = [], [], []
    for l in range(depth):
        lw, cols = _prep_layer(l, prm, sizes)
        mod_ctx = mods[l, 0:1].reshape(6, 1, d)
        mod_lat = mods[l, 1:1 + bs].reshape(bs * 6, 1, d)
        yp, ckv_l, krope_l, hf_l = _mixer_layer(yp, mod_ctx, lw, cols, bp, lp, False, None, None)
        ckv_list.append(ckv_l.reshape(bp, lp, -1))
        krope_list.append(krope_l.reshape(bp, lp, -1))
        ssm_list.append(_cols_to_states(hf_l, g, n))
        ctx_kpe = jnp.pad(cache_krope[:, l].reshape(bs * past, ROPE_DIM),
                          ((0, 0), (QK_NOPE, HEAD_PAD - QK_HEAD)))
        ctx = (cache_ckv[:, l].reshape(bs * past, -1), ctx_kpe, _states_to_cols(state_ssm[:, l]))
        ys, _, _, _ = _mixer_layer(ys, mod_lat, lw, cols, bs, ls, True, rope, ctx)
    return (yp.reshape(bp, lp, d), ys.reshape(bs, ls, d), jnp.stack(ckv_list, axis=1),
            jnp.stack(krope_list, axis=1), jnp.stack(ssm_list, axis=1))
```

```python
import functools
import math

import jax
import jax.numpy as jnp
from jax import lax
from jax.experimental import pallas as pl
from jax.experimental.pallas import tpu as pltpu

F32 = jnp.float32
BF16 = jnp.bfloat16

GRID_W = 64
N_BRANCH = 4
MLA_HEADS = 8
QK_NOPE = 64
ROPE_DIM = 32
QK_HEAD = QK_NOPE + ROPE_DIM
V_HEAD = 64
ROPE_THETA = 10000.0
SSM_GROUP_CH = 16
POOL_WINDOWS = (2, 4, 8, 16)
EPS = 1e-6
LOG2E = 1.4426950408889634

LANES = 128
SUBLANES = 8
HEAD_PAD = LANES
MIB = 1024 * 1024


def _cparams(sem, vmem_mib):
    return pltpu.CompilerParams(dimension_semantics=sem, vmem_limit_bytes=vmem_mib * MIB)


def _dot(a, b):
    return jnp.dot(a, b, preferred_element_type=F32)


def _mod_spec(chunk, d, tm, seq_len, per_batch):
    def imap(i, *_):
        row = (i * tm) // seq_len if per_batch else 0
        return (row * 6 + chunk, 0, 0)
    return pl.BlockSpec((1, 1, d), imap)


def _ada_kernel(c_ref, w_ref, b_ref, o_ref):
    cv = c_ref[...]
    s = cv * jax.nn.sigmoid(cv)
    o_ref[0] = _dot(s.astype(BF16), w_ref[0].astype(BF16)) + b_ref[0]


def _ada_call(cvec, w_ada, b_ada):
    depth, d, n = w_ada.shape
    rows = cvec.shape[0]
    tn = 1024
    return pl.pallas_call(
        _ada_kernel,
        out_shape=jax.ShapeDtypeStruct((depth, rows, n), F32),
        grid=(depth, n // tn),
        in_specs=[
            pl.BlockSpec((rows, d), lambda l, j: (0, 0)),
            pl.BlockSpec((1, d, tn), lambda l, j: (l, 0, j)),
            pl.BlockSpec((1, 1, tn), lambda l, j: (l, 0, j)),
        ],
        out_specs=pl.BlockSpec((1, rows, tn), lambda l, j: (l, 0, j)),
        compiler_params=_cparams(("parallel", "parallel"), 40),
        name="ada_mod",
    )(cvec, w_ada, b_ada.reshape(depth, 1, n))


def _modnorm(x, g, scale, shift):
    ms = jnp.mean(x * x, axis=-1, keepdims=True)
    return (x * lax.rsqrt(ms + EPS) * g) * (1.0 + scale) + shift


def _inproj_kernel(x_ref, sh_ref, sc_ref, g_ref, w_ref, o_ref, h_ref):
    h = _modnorm(x_ref[...], g_ref[...], sc_ref[0], sh_ref[0]).astype(BF16)
    h_ref[...] = h
    o_ref[...] = _dot(h, w_ref[...])


def _inproj_call(x, mod, g, w, seq_len, per_batch):
    t, d = x.shape
    n = w.shape[1]
    tm = 512
    mspec = functools.partial(_mod_spec, d=d, tm=tm, seq_len=seq_len, per_batch=per_batch)
    return pl.pallas_call(
        _inproj_kernel,
        out_shape=(jax.ShapeDtypeStruct((t, n), F32), jax.ShapeDtypeStruct((t, d), BF16)),
        grid=(t // tm,),
        in_specs=[
            pl.BlockSpec((tm, d), lambda i: (i, 0)),
            mspec(0), mspec(1),
            pl.BlockSpec((1, d), lambda i: (0, 0)),
            pl.BlockSpec((d, n), lambda i: (0, 0), pipeline_mode=pl.Buffered(1)),
        ],
        out_specs=(pl.BlockSpec((tm, n), lambda i: (i, 0)), pl.BlockSpec((tm, d), lambda i: (i, 0))),
        compiler_params=_cparams(("parallel",), 52),
        name="in_proj",
    )(x, mod, mod, g, w)


def _rms(x, g, width):
    ms = jnp.sum(x * x, axis=-1, keepdims=True) * (1.0 / width)
    return x * lax.rsqrt(ms + EPS) * g


def _heads_norm(x, g, mult, shared=None, rope=None):
    outs = []
    for h in range(MLA_HEADS):
        sl = slice(h * HEAD_PAD, (h + 1) * HEAD_PAD)
        xh = x[:, sl] if shared is None else x[:, sl] + shared
        ms = jnp.sum(xh * xh, axis=-1, keepdims=True) * (1.0 / QK_HEAD)
        r = lax.rsqrt(ms + EPS)
        y = xh * r * g[:, sl]
        if rope is not None:
            xs, shared_s, gs, cos, sin = rope
            part = shared_s if xs is None else xs[:, sl]
            y = y * cos + (part * r * gs[:, sl]) * sin
        if mult != 1.0:
            y = y * mult
        outs.append(y)
    return jnp.concatenate(outs, axis=1)


def _qkprep_kernel(*refs, use_rope, q_mult):
    if use_rope:
        (qa_ref, kva_ref, kpe_ref, gqa_ref, gkva_ref, gq_ref, gk_ref, wuq_ref, wuk_ref, wuv_ref,
         kpes_ref, gqs_ref, gks_ref, wuqs_ref, c_ref, s_ref, q_out, k_out, v_out, ckv_out) = refs
    else:
        (qa_ref, kva_ref, kpe_ref, gqa_ref, gkva_ref, gq_ref, gk_ref, wuq_ref, wuk_ref, wuv_ref,
         q_out, k_out, v_out, ckv_out) = refs
    qa = qa_ref[...]
    qa_n = _rms(qa, gqa_ref[...], qa.shape[-1]).astype(BF16)
    q = _dot(qa_n, wuq_ref[...])
    q_rope = (_dot(qa_n, wuqs_ref[...]), None, gqs_ref[...], c_ref[...], s_ref[...]) if use_rope else None
    q_out[...] = _heads_norm(q, gq_ref[...], q_mult, rope=q_rope).astype(BF16)

    kva = kva_ref[...]
    ckv = _rms(kva, gkva_ref[...], kva.shape[-1])
    ckv_out[...] = ckv
    ckv_b = ckv.astype(BF16)
    k = _dot(ckv_b, wuk_ref[...])
    k_rope = (None, kpes_ref[...], gks_ref[...], c_ref[...], s_ref[...]) if use_rope else None
    k_out[...] = _heads_norm(k, gk_ref[...], 1.0, shared=kpe_ref[...], rope=k_rope).astype(BF16)
    v_out[...] = _dot(ckv_b, wuv_ref[...]).astype(BF16)


def _qkprep_call(p, lw, rope, seq_len, cols):
    t = p.shape[0]
    tm = 512
    hq = MLA_HEADS * HEAD_PAD
    hv = MLA_HEADS * V_HEAD
    qlora = lw["w_uq"].shape[0]
    kvlora = lw["w_uk"].shape[0]
    const = lambda i: (0, 0)
    in_specs = [
        pl.BlockSpec((tm, qlora), lambda i: (i, cols["q_a"] // qlora)),
        pl.BlockSpec((tm, kvlora), lambda i: (i, cols["kv_a"] // kvlora)),
        pl.BlockSpec((tm, HEAD_PAD), lambda i: (i, cols["k_rope"] // HEAD_PAD)),
        pl.BlockSpec((1, qlora), const),
        pl.BlockSpec((1, kvlora), const),
        pl.BlockSpec((1, hq), const),
        pl.BlockSpec((1, hq), const),
        pl.BlockSpec((qlora, hq), const),
        pl.BlockSpec((kvlora, hq), const),
        pl.BlockSpec((kvlora, hv), const),
    ]
    args = [p, p, p, lw["q_a_norm_g"], lw["kv_a_norm_g"], lw["q_norm_g"], lw["k_norm_g"],
            lw["w_uq"], lw["w_uk"], lw["w_uv"]]
    if rope is not None:
        nblk = seq_len // tm
        in_specs += [
            pl.BlockSpec((tm, HEAD_PAD), lambda i: (i, cols["k_rope_sw"] // HEAD_PAD)),
            pl.BlockSpec((1, hq), const), pl.BlockSpec((1, hq), const),
            pl.BlockSpec((qlora, hq), const),
            pl.BlockSpec((tm, HEAD_PAD), lambda i: (i % nblk, 0)),
            pl.BlockSpec((tm, HEAD_PAD), lambda i: (i % nblk, 0)),
        ]
        args += [p, lw["q_norm_g_sw"], lw["k_norm_g_sw"], lw["w_uq_sw"]] + list(rope)
    q_mult = LOG2E * QK_HEAD ** -0.5
    return pl.pallas_call(
        functools.partial(_qkprep_kernel, use_rope=rope is not None, q_mult=q_mult),
        out_shape=(jax.ShapeDtypeStruct((t, hq), BF16), jax.ShapeDtypeStruct((t, hq), BF16),
                   jax.ShapeDtypeStruct((t, hv), BF16), jax.ShapeDtypeStruct((t, kvlora), F32)),
        grid=(t // tm,),
        in_specs=in_specs,
        out_specs=(pl.BlockSpec((tm, hq), lambda i: (i, 0)), pl.BlockSpec((tm, hq), lambda i: (i, 0)),
                   pl.BlockSpec((tm, hv), lambda i: (i, 0)), pl.BlockSpec((tm, kvlora), lambda i: (i, 0))),
        compiler_params=_cparams(("parallel",), 40),
        name="qk_prep",
    )(*args)


def _ctxprep_kernel(ckv_ref, kpe_ref, gk_ref, wuk_ref, wuv_ref, k_out, v_out):
    ckv_b = ckv_ref[...].astype(BF16)
    k = _dot(ckv_b, wuk_ref[...])
    k_out[...] = _heads_norm(k, gk_ref[...], 1.0, shared=kpe_ref[...]).astype(BF16)
    v_out[...] = _dot(ckv_b, wuv_ref[...]).astype(BF16)


def _ctxprep_call(ckv, kpe, lw):
    t, kvlora = ckv.shape
    tm = 512
    hq = MLA_HEADS * HEAD_PAD
    hv = MLA_HEADS * V_HEAD
    const = lambda i: (0, 0)
    return pl.pallas_call(
        _ctxprep_kernel,
        out_shape=(jax.ShapeDtypeStruct((t, hq), BF16), jax.ShapeDtypeStruct((t, hv), BF16)),
        grid=(t // tm,),
        in_specs=[
            pl.BlockSpec((tm, kvlora), lambda i: (i, 0)),
            pl.BlockSpec((tm, HEAD_PAD), lambda i: (i, 0)),
            pl.BlockSpec((1, hq), const),
            pl.BlockSpec((kvlora, hq), const),
            pl.BlockSpec((kvlora, hv), const),
        ],
        out_specs=(pl.BlockSpec((tm, hq), lambda i: (i, 0)), pl.BlockSpec((tm, hv), lambda i: (i, 0))),
        compiler_params=_cparams(("parallel",), 40),
        name="ctx_prep",
    )(ckv, kpe, lw["k_norm_g"], lw["w_uk"], lw["w_uv"])


def _attn_kernel(*refs, has_ctx, heads):
    if has_ctx:
        q_ref, ko_ref, vo_ref, kc_ref, vc_ref, o_ref = refs
    else:
        q_ref, ko_ref, vo_ref, o_ref = refs
    contract_last = (((1,), (1,)), ((), ()))
    lane = lax.broadcasted_iota(jnp.int32, (q_ref.shape[0], 2 * V_HEAD), 1)
    for pair in range(heads // 2):
        vsl = slice(pair * 2 * V_HEAD, (pair + 1) * 2 * V_HEAD)
        outs = []
        for hh in range(2):
            h = 2 * pair + hh
            sl = slice(h * HEAD_PAD, (h + 1) * HEAD_PAD)
            q = q_ref[:, sl]
            s_o = lax.dot_general(q, ko_ref[:, sl], contract_last, preferred_element_type=F32)
            m = jnp.max(s_o, axis=-1, keepdims=True)
            if has_ctx:
                s_c = lax.dot_general(q, kc_ref[:, sl], contract_last, preferred_element_type=F32)
                m = jnp.maximum(m, jnp.max(s_c, axis=-1, keepdims=True))
            p_o = jnp.exp2(s_o - m)
            l = jnp.sum(p_o, axis=-1, keepdims=True)
            acc = _dot(p_o.astype(BF16), vo_ref[:, vsl])
            if has_ctx:
                p_c = jnp.exp2(s_c - m)
                l = l + jnp.sum(p_c, axis=-1, keepdims=True)
                acc = acc + _dot(p_c.astype(BF16), vc_ref[:, vsl])
            outs.append(acc / l)
        o_ref[:, vsl] = jnp.where(lane < V_HEAD, outs[0], outs[1]).astype(BF16)


def _attn_call(q, k, v, kc, vc, batch, seq_len, ctx_len):
    t = q.shape[0]
    tq = 256
    heads = 8
    nq = seq_len // tq
    npair = MLA_HEADS // heads
    qw = heads * HEAD_PAD
    vw = heads * V_HEAD
    in_specs = [
        pl.BlockSpec((tq, qw), lambda b, h, i: (b * nq + i, h)),
        pl.BlockSpec((seq_len, qw), lambda b, h, i: (b, h)),
        pl.BlockSpec((seq_len, vw), lambda b, h, i: (b, h)),
    ]
    args = [q, k, v]
    if kc is not None:
        in_specs += [pl.BlockSpec((ctx_len, qw), lambda b, h, i: (b, h)),
                     pl.BlockSpec((ctx_len, vw), lambda b, h, i: (b, h))]
        args += [kc, vc]
    return pl.pallas_call(
        functools.partial(_attn_kernel, has_ctx=kc is not None, heads=heads),
        out_shape=jax.ShapeDtypeStruct((t, MLA_HEADS * V_HEAD), BF16),
        grid=(batch, npair, nq),
        in_specs=in_specs,
        out_specs=pl.BlockSpec((tq, vw), lambda b, h, i: (b * nq + i, h)),
        compiler_params=_cparams(("parallel", "parallel", "arbitrary"), 48),
        name="attention",
    )(*args)


def _convpool_kernel(cu_ref, bg_ref, cg_ref, pu_ref, cw_ref, cb_ref, pw_ref, ps_ref, zb_ref, zd_ref):
    n = cu_ref.shape[0]
    row = lax.broadcasted_iota(jnp.int32, cu_ref.shape, 0)

    def down(x, k):
        return jnp.where(row >= k, pltpu.roll(x, k, axis=0), 0.0)

    def up(x, k):
        return jnp.where(row < n - k, pltpu.roll(x, n - k, axis=0), 0.0)

    v = cg_ref[...] * cu_ref[...]
    conv = (down(v, 1) * cw_ref[0:1, :] + v * cw_ref[1:2, :] + up(v, 1) * cw_ref[2:3, :]
            + cb_ref[...])
    zb_ref[...] = (bg_ref[...] * conv).astype(BF16)

    u = pu_ref[...]
    rowf = row.astype(F32)
    grp = pl.program_id(1)
    for gi, win in enumerate(POOL_WINDOWS):
        @pl.when(grp == gi)
        def _(win=win):
            half = win // 2
            fwd = u
            bwd = u
            k = 1
            while k < half:
                fwd = fwd + up(fwd, k)
                bwd = bwd + down(bwd, k)
                k *= 2
            total = fwd + down(bwd, 1)
            cnt = jnp.minimum(rowf + half, float(n)) - jnp.maximum(rowf - half, 0.0)
            mean = total / cnt - u
            zd_ref[...] = (_dot(mean.astype(BF16), pw_ref[0]) * ps_ref[...]).astype(BF16)


def _convpool_call(p, lw, batch, seq_len, cols):
    t = p.shape[0]
    cw = LANES
    nblk = lw["conv_w"].shape[1] // cw

    def pspec(name):
        base = cols[name] // cw
        return pl.BlockSpec((seq_len, cw), lambda b, g: (b, base + g))

    vec = lambda b, g: (0, g)
    return pl.pallas_call(
        _convpool_kernel,
        out_shape=(jax.ShapeDtypeStruct((t, nblk * cw), BF16), jax.ShapeDtypeStruct((t, nblk * cw), BF16)),
        grid=(batch, nblk),
        in_specs=[pspec("conv_u"), pspec("conv_bg"), pspec("conv_cg"), pspec("pool_u"),
                  pl.BlockSpec((3, cw), vec), pl.BlockSpec((1, cw), vec),
                  pl.BlockSpec((1, cw, cw), lambda b, g: (g, 0, 0)), pl.BlockSpec((1, cw), vec)],
        out_specs=(pl.BlockSpec((seq_len, cw), lambda b, g: (b, g)),
                   pl.BlockSpec((seq_len, cw), lambda b, g: (b, g))),
        compiler_params=_cparams(("parallel", "parallel"), 40),
        name="conv_pool",
    )(p, p, p, p, lw["conv_w"], lw["conv_b"], lw["pool_w"], lw["pool_scale"])


SSM_R = 4
SSM_GB = LANES // SSM_GROUP_CH


def _ssm_kernel(*refs, steps, nchunks, nblk):
    u_refs = refs[:nblk]
    h0_ref, a_ref, mb_ref, m1_ref, m2_ref = refs[nblk:nblk + 5]
    y_refs = refs[nblk + 5:2 * nblk + 5]
    hf_ref, slab_scr, xs_scr, st_scr = refs[2 * nblk + 5:]
    d = pl.program_id(0)
    c = pl.program_id(2)
    width = st_scr.shape[1]
    bw = width // nblk
    uw = SSM_R * LANES

    @pl.when(c == 0)
    def _():
        st_scr[...] = h0_ref[0]

    for blk in range(nblk):
        for j in range(SSM_R):
            for b in range(SUBLANES):
                slab_scr[blk * SSM_R + j, pl.ds(b, steps, stride=SUBLANES), :] = (
                    u_refs[blk][b, pl.ds(j, steps, stride=SSM_R), :])

    def packed(blk):
        return jnp.concatenate([slab_scr[blk * SSM_R + j] for j in range(SSM_R)], axis=1).astype(BF16)

    for blk in range(nblk):
        xs_scr[:, blk * bw:(blk + 1) * bw] = _dot(packed(blk), mb_ref[0, blk])

    gw = width // 2
    for part in range(width // gw):
        offs = []
        for blk in range(part * nblk // 2, (part + 1) * nblk // 2):
            offs += [(blk * bw + k * LANES, blk * bw + bw // 2 + k * LANES) for k in range(bw // 2 // LANES)]
        coef = [(a_ref[0, :, r0:r0 + LANES], a_ref[0, :, i0:i0 + LANES]) for r0, i0 in offs]

        def body(t, carry, offs=offs, coef=coef):
            tt = t + d * (steps - 1 - 2 * t)
            row0 = pl.multiple_of(tt * SUBLANES, SUBLANES)
            new = []
            for (r0, i0), (a_re, a_im), x_re, x_im in zip(offs, coef, carry[0::2], carry[1::2]):
                n_re = (a_re * x_re - a_im * x_im) + xs_scr[pl.ds(row0, SUBLANES), r0:r0 + LANES]
                n_im = (a_re * x_im + a_im * x_re) + xs_scr[pl.ds(row0, SUBLANES), i0:i0 + LANES]
                xs_scr[pl.ds(row0, SUBLANES), r0:r0 + LANES] = x_re
                xs_scr[pl.ds(row0, SUBLANES), i0:i0 + LANES] = x_im
                new += [n_re, n_im]
            return tuple(new)

        init = []
        for r0, i0 in offs:
            init += [st_scr[:, r0:r0 + LANES], st_scr[:, i0:i0 + LANES]]
        final = lax.fori_loop(0, steps, body, tuple(init), unroll=2)
        for (r0, i0), x_re, x_im in zip(offs, final[0::2], final[1::2]):
            st_scr[:, r0:r0 + LANES] = x_re
            st_scr[:, i0:i0 + LANES] = x_im

    for blk in range(nblk):
        y = (_dot(xs_scr[:, blk * bw:(blk + 1) * bw].astype(BF16), m1_ref[0, blk])
             + _dot(packed(blk), m2_ref[0, blk]))
        for j in range(SSM_R):
            slab_scr[blk * SSM_R + j] = y[:, j * LANES:(j + 1) * LANES]
        for j in range(SSM_R):
            for b in range(SUBLANES):
                y_refs[blk][0, b, pl.ds(j, steps, stride=SSM_R), :] = (
                    slab_scr[blk * SSM_R + j, pl.ds(b, steps, stride=SUBLANES), :])

    @pl.when(c == nchunks - 1)
    def _():
        hf_ref[0] = st_scr[...]


def _ssm_call(p3, h0, lw, col0):
    batch, seq_len, _ = p3.shape
    width = h0.shape[-1]
    nblk = lw["ssm_mb"].shape[1]
    steps = min(seq_len // SSM_R, 64)
    tokens = steps * SSM_R
    nchunks = seq_len // tokens
    nbg = batch // SUBLANES
    rows = steps * SUBLANES

    def cidx(d, c):
        return c + d * (nchunks - 1 - 2 * c)

    def uspec(blk):
        return pl.BlockSpec((SUBLANES, tokens, LANES), lambda d, b, c: (b, cidx(d, c), col0 // LANES + blk))

    def wspec(arr):
        return pl.BlockSpec((1,) + arr.shape[1:], lambda d, b, c: (d, 0, 0, 0), pipeline_mode=pl.Buffered(1))

    yshape = jax.ShapeDtypeStruct((2, batch, seq_len, LANES), F32)
    yspec = pl.BlockSpec((1, SUBLANES, tokens, LANES), lambda d, b, c: (d, b, cidx(d, c), 0))
    outs = pl.pallas_call(
        functools.partial(_ssm_kernel, steps=steps, nchunks=nchunks, nblk=nblk),
        out_shape=(yshape,) * nblk + (jax.ShapeDtypeStruct((2, batch, width), F32),),
        grid=(2, nbg, nchunks),
        in_specs=[uspec(blk) for blk in range(nblk)] + [
            pl.BlockSpec((1, SUBLANES, width), lambda d, b, c: (d, b, 0)),
            pl.BlockSpec((1, SUBLANES, width), lambda d, b, c: (d, 0, 0)),
            wspec(lw["ssm_mb"]), wspec(lw["ssm_m1"]), wspec(lw["ssm_m2"]),
        ],
        out_specs=(yspec,) * nblk + (pl.BlockSpec((1, SUBLANES, width), lambda d, b, c: (d, b, 0)),),
        scratch_shapes=[pltpu.VMEM((nblk * SSM_R, rows, LANES), F32), pltpu.VMEM((rows, width), F32),
                        pltpu.VMEM((SUBLANES, width), F32)],
        compiler_params=_cparams(("arbitrary", "arbitrary", "arbitrary"), 52),
        name="s5_scan",
    )(*([p3] * nblk), h0, lw["ssm_a"], lw["ssm_mb"], lw["ssm_m1"], lw["ssm_m2"])
    return outs[:nblk], outs[nblk]


def _merge_kernel(*refs, nblk):
    (h_ref, za_ref, zb_ref, su_ref, sd_ref, zd_ref, wg0, wg1, wg2, wg3, wa, wb, wca, wcg, wd) = refs[:15]
    yf_refs = refs[15:15 + nblk]
    yb_refs = refs[15 + nblk:15 + 2 * nblk]
    o_ref, y_scr = refs[15 + 2 * nblk:]

    @pl.when(pl.program_id(1) == 0)
    def _():
        y_both = jnp.concatenate([f[0] + b[0] for f, b in zip(yf_refs, yb_refs)], axis=1)
        y_scr[...] = (y_both + sd_ref[...] * su_ref[...]).astype(BF16)

    h = h_ref[...]
    y = y_scr[...]

    def gate(w_ref):
        return jax.nn.sigmoid(_dot(h, w_ref[...]))

    merged = gate(wg0) * _dot(za_ref[...], wa[...])
    merged = merged + gate(wg1) * _dot(zb_ref[...], wb[...])
    merged = merged + gate(wg2) * (_dot(y, wca[...]) * jax.nn.sigmoid(_dot(y, wcg[...])))
    merged = merged + gate(wg3) * _dot(zd_ref[...], wd[...])
    o_ref[...] = merged.astype(BF16)


def _merge_call(h, p, za, zb, ys, zd, lw, cols):
    t, d = h.shape
    bw = za.shape[1]
    tm, tn = 512, 512
    nj = d // tn
    nblk = len(ys)
    row = lambda i, j: (i, 0)
    colj = lambda i, j: (0, j)
    su_blk = cols["ssm_u"] // bw

    def gspec(k):
        return pl.BlockSpec((d, tn), lambda i, j: (0, k * nj + j))

    def yspec(direction):
        return pl.BlockSpec((1, tm, LANES), lambda i, j: (direction, i, 0))

    in_specs = [
        pl.BlockSpec((tm, d), row),
        pl.BlockSpec((tm, bw), row), pl.BlockSpec((tm, bw), row),
        pl.BlockSpec((tm, bw), lambda i, j: (i, su_blk)),
        pl.BlockSpec((1, bw), lambda i, j: (0, 0)),
        pl.BlockSpec((tm, bw), row),
        gspec(0), gspec(1), gspec(2), gspec(3),
        pl.BlockSpec((bw, tn), colj), pl.BlockSpec((bw, tn), colj),
        pl.BlockSpec((bw, tn), colj), pl.BlockSpec((bw, tn), lambda i, j: (0, nj + j)),
        pl.BlockSpec((bw, tn), colj),
    ] + [yspec(0)] * nblk + [yspec(1)] * nblk
    wg = lw["w_gate"]
    return pl.pallas_call(
        functools.partial(_merge_kernel, nblk=nblk),
        out_shape=jax.ShapeDtypeStruct((t, d), BF16),
        grid=(t // tm, nj),
        in_specs=in_specs,
        out_specs=pl.BlockSpec((tm, tn), lambda i, j: (i, j)),
        scratch_shapes=[pltpu.VMEM((tm, bw), BF16)],
        compiler_params=_cparams(("parallel", "arbitrary"), 48),
        name="branch_merge",
    )(h, za, zb, p, lw["ssm_d"], zd,
      wg, wg, wg, wg, lw["w_mla_o"], lw["w_conv_o"], lw["w_glu"], lw["w_glu"], lw["w_pool_o"], *ys, *ys)


def _outproj_kernel(x_ref, m_ref, gt_ref, sh_ref, sc_ref, g_ref, wo_ref, x1_ref, h2_ref):
    x1 = x_ref[...] + gt_ref[0] * _dot(m_ref[...], wo_ref[...])
    x1_ref[...] = x1
    h2_ref[...] = _modnorm(x1, g_ref[...], sc_ref[0], sh_ref[0]).astype(BF16)


def _outproj_call(x, merged, mod, lw, seq_len, per_batch):
    t, d = x.shape
    tm = 512
    mspec = functools.partial(_mod_spec, d=d, tm=tm, seq_len=seq_len, per_batch=per_batch)
    row = lambda i: (i, 0)
    return pl.pallas_call(
        _outproj_kernel,
        out_shape=(jax.ShapeDtypeStruct((t, d), F32), jax.ShapeDtypeStruct((t, d), BF16)),
        grid=(t // tm,),
        in_specs=[
            pl.BlockSpec((tm, d), row), pl.BlockSpec((tm, d), row),
            mspec(2), mspec(3), mspec(4),
            pl.BlockSpec((1, d), lambda i: (0, 0)),
            pl.BlockSpec((d, d), lambda i: (0, 0), pipeline_mode=pl.Buffered(1)),
        ],
        out_specs=(pl.BlockSpec((tm, d), row), pl.BlockSpec((tm, d), row)),
        compiler_params=_cparams(("parallel",), 48),
        name="out_proj",
    )(x, merged, mod, mod, mod, lw["norm2_g"], lw["w_o"])


def _mlp_kernel(x_ref, h_ref, gt_ref, w1_ref, w2_ref, o_ref):
    @pl.when(pl.program_id(1) == 0)
    def _():
        o_ref[...] = x_ref[...]

    hid = jnp.square(jnp.maximum(_dot(h_ref[...], w1_ref[...]), 0.0))
    o_ref[...] += gt_ref[0] * _dot(hid.astype(BF16), w2_ref[...])


def _mlp_call(x, h, mod, lw, seq_len, per_batch):
    t, d = x.shape
    hidden = lw["w_mlp1"].shape[1]
    tm, tk = 512, 1024
    mspec = functools.partial(_mod_spec, d=d, tm=tm, seq_len=seq_len, per_batch=per_batch)
    row = lambda i, k: (i, 0)
    return pl.pallas_call(
        _mlp_kernel,
        out_shape=jax.ShapeDtypeStruct((t, d), F32),
        grid=(t // tm, hidden // tk),
        in_specs=[
            pl.BlockSpec((tm, d), row), pl.BlockSpec((tm, d), row), mspec(5),
            pl.BlockSpec((d, tk), lambda i, k: (0, k)),
            pl.BlockSpec((tk, d), lambda i, k: (k, 0)),
        ],
        out_specs=pl.BlockSpec((tm, d), row),
        compiler_params=_cparams(("parallel", "arbitrary"), 48),
        name="mlp",
    )(x, h, mod, lw["w_mlp1"], lw["w_mlp2"])


def _pad_heads(w, per_head, lo):
    lead = w.shape[:-1]
    w = w.reshape(lead + (MLA_HEADS, per_head))
    w = jnp.pad(w, [(0, 0)] * len(lead) + [(0, 0), (lo, HEAD_PAD - lo - per_head)])
    return w.reshape(lead + (MLA_HEADS * HEAD_PAD,))


_ROPE_PARTNER = tuple((i // 16) * 16 + (i % 16 + 8) % 16 for i in range(ROPE_DIM))


def _swap_heads(w):
    lead = w.shape[:-1]
    w = w.reshape(lead + (MLA_HEADS, QK_HEAD))[..., QK_NOPE:][..., jnp.array(_ROPE_PARTNER)]
    w = jnp.pad(w, [(0, 0)] * len(lead) + [(0, 0), (QK_NOPE, HEAD_PAD - QK_HEAD)])
    return w.reshape(lead + (MLA_HEADS * HEAD_PAD,))


def _prep_ssm(lam_re, lam_im, log_step, b_re, b_im, c_re, c_im):
    r = SSM_R
    ndir, g, n, cg = b_re.shape
    gb = SSM_GB
    nblk = g // gb
    step = jnp.exp(log_step)[..., None]
    p = jnp.arange(r + 1, dtype=F32).reshape(-1, 1, 1, 1)
    mag = jnp.exp(p * (lam_re * step))
    pw_re = mag * jnp.cos(p * (lam_im * step))
    pw_im = mag * jnp.sin(p * (lam_im * step))
    ar, ai = pw_re[1], pw_im[1]
    den = lam_re * lam_re + lam_im * lam_im
    qr = ((ar - 1.0) * lam_re + ai * lam_im) / den
    qi = (ai * lam_re - (ar - 1.0) * lam_im) / den
    bb_re = qr[..., None] * b_re - qi[..., None] * b_im
    bb_im = qr[..., None] * b_im + qi[..., None] * b_re
    eye = jnp.eye(gb, dtype=F32)

    def sel(t, powers):
        return jnp.stack([t[powers[dd], dd] for dd in range(ndir)])

    t_re = pw_re[..., None] * bb_re - pw_im[..., None] * bb_im
    t_im = pw_re[..., None] * bb_im + pw_im[..., None] * bb_re
    pj = jnp.stack([jnp.arange(r)[::-1], jnp.arange(r)])

    def inc_block(t):
        return jnp.einsum("djpgnc,gh->dpjgchn", t.reshape(ndir, r, nblk, gb, n, cg), eye)

    mb = jnp.stack([inc_block(sel(t_re, pj)), inc_block(sel(t_im, pj))], axis=5)
    mb = mb.reshape(ndir, nblk, r * gb * cg, 2 * gb * n)

    cl_re = c_re * pw_re[:, :, :, None, :] - c_im * pw_im[:, :, :, None, :]
    cl_im = c_re * pw_im[:, :, :, None, :] + c_im * pw_re[:, :, :, None, :]
    p1 = jnp.stack([jnp.arange(1, r + 1), jnp.arange(r, 0, -1)])

    def out_block(t):
        return jnp.einsum("djpgcn,gh->dphnjgc", t.reshape(ndir, r, nblk, gb, cg, n), eye)

    m1 = jnp.stack([out_block(sel(cl_re, p1)), out_block(-sel(cl_im, p1))], axis=2)
    m1 = m1.reshape(ndir, nblk, 2 * gb * n, r * gb * cg)

    hi = lax.Precision.HIGHEST
    k2 = (jnp.einsum("pdgcn,dgnk->pdgck", cl_re[:r], bb_re, precision=hi)
          - jnp.einsum("pdgcn,dgnk->pdgck", cl_im[:r], bb_im, precision=hi))
    diff = jnp.arange(r)[None, :] - jnp.arange(r)[:, None]
    taps = []
    for dd, sign in enumerate((1, -1)):
        causal = (sign * diff >= 0)[..., None, None, None]
        kk = jnp.where(causal, k2[jnp.abs(diff), dd], 0.0).reshape(r, r, nblk, gb, cg, cg)
        taps.append(jnp.einsum("ijpgck,gh->pihkjgc", kk, eye))
    m2 = jnp.stack(taps).reshape(ndir, nblk, r * gb * cg, r * gb * cg)

    a = jnp.stack([pw_re[r].reshape(ndir, nblk, gb * n), pw_im[r].reshape(ndir, nblk, gb * n)], axis=2)
    a = jnp.broadcast_to(a.reshape(ndir, 1, 2 * g * n), (ndir, SUBLANES, 2 * g * n))
    return a, mb.astype(BF16), m1.astype(BF16), m2.astype(BF16)


def _prep_layer(l, prm, sizes):
    q_lora, kv_lora, conv_w, ssm_w, pool_w_ = sizes
    w_in = prm["w_in"][l]
    d = w_in.shape[0]
    offs = {}
    o = 0
    for name, width in (("q_a", q_lora), ("kv_a", kv_lora), ("k_rope", ROPE_DIM), ("conv_u", conv_w),
                        ("conv_bg", conv_w), ("conv_cg", conv_w), ("ssm_u", ssm_w), ("pool_u", pool_w_)):
        offs[name] = (o, o + width)
        o += width
    gate0 = o
    take = lambda name: w_in[:, offs[name][0]:offs[name][1]]
    rope_lanes = ((0, 0), (QK_NOPE, HEAD_PAD - QK_HEAD))
    kr_pad = jnp.pad(take("k_rope"), rope_lanes)
    kr_sw = jnp.pad(take("k_rope")[:, jnp.array(_ROPE_PARTNER)], rope_lanes)
    pieces = [("q_a", take("q_a")), ("conv_u", take("conv_u")), ("conv_bg", take("conv_bg")),
              ("conv_cg", take("conv_cg")), ("ssm_u", take("ssm_u")), ("pool_u", take("pool_u")),
              ("kv_a", take("kv_a")), ("k_rope", kr_pad), ("k_rope_sw", kr_sw)]
    cols = {}
    o = 0
    for name, w in pieces:
        cols[name] = o
        o += w.shape[1]
    w_small = jnp.concatenate([w for _, w in pieces], axis=1).astype(BF16)

    w_ukv = prm["w_ukv"][l].reshape(kv_lora, MLA_HEADS, QK_NOPE + V_HEAD)
    w_uk = _pad_heads(w_ukv[:, :, :QK_NOPE].reshape(kv_lora, -1), QK_NOPE, 0)
    w_uv = w_ukv[:, :, QK_NOPE:].reshape(kv_lora, -1)

    ssm_a, ssm_mb, ssm_m1, ssm_m2 = _prep_ssm(
        prm["ssm_lam_re"][l], prm["ssm_lam_im"][l], prm["ssm_log_step"][l], prm["ssm_b_re"][l],
        prm["ssm_b_im"][l], prm["ssm_c_re"][l], prm["ssm_c_im"][l])

    row = lambda v: v.reshape(1, -1)
    lw = dict(
        w_small=w_small, w_gate=w_in[:, gate0:].astype(BF16),
        norm1_g=row(prm["norm1_g"][l]), norm2_g=row(prm["norm2_g"][l]),
        q_a_norm_g=row(prm["q_a_norm_g"][l]), kv_a_norm_g=row(prm["kv_a_norm_g"][l]),
        q_norm_g=row(_pad_heads(jnp.tile(prm["q_norm_g"][l], MLA_HEADS), QK_HEAD, 0)),
        k_norm_g=row(_pad_heads(jnp.tile(prm["k_norm_g"][l], MLA_HEADS), QK_HEAD, 0)),
        q_norm_g_sw=row(_swap_heads(jnp.tile(prm["q_norm_g"][l], MLA_HEADS))),
        k_norm_g_sw=row(_swap_heads(jnp.tile(prm["k_norm_g"][l], MLA_HEADS))),
        w_uq=_pad_heads(prm["w_uq"][l], QK_HEAD, 0).astype(BF16),
        w_uq_sw=_swap_heads(prm["w_uq"][l]).astype(BF16),
        w_uk=w_uk.astype(BF16), w_uv=w_uv.astype(BF16),
        w_mla_o=prm["w_mla_o"][l].astype(BF16),
        conv_w=prm["conv_w"][l], conv_b=row(prm["conv_b"][l]),
        w_conv_o=prm["w_conv_o"][l].astype(BF16),
        ssm_a=ssm_a, ssm_mb=ssm_mb, ssm_m1=ssm_m1, ssm_m2=ssm_m2,
        ssm_d=row(prm["ssm_d"][l]), w_glu=prm["w_glu"][l].astype(BF16),
        pool_w=prm["pool_w"][l].astype(BF16), pool_scale=row(prm["pool_scale"][l]),
        w_pool_o=prm["w_pool_o"][l].astype(BF16), w_o=prm["w_o"][l].astype(BF16),
        w_mlp1=prm["w_mlp1"][l].astype(BF16), w_mlp2=prm["w_mlp2"][l].astype(BF16),
    )
    return lw, cols


def _rope_tables(seq_len):
    half = ROPE_DIM // 2
    inv_freq = ROPE_THETA ** (-jnp.arange(0, half, 2, dtype=F32) / half)
    t = jnp.arange(seq_len)
    ang_r = (t // GRID_W).astype(F32)[:, None] * inv_freq
    ang_c = (t % GRID_W).astype(F32)[:, None] * inv_freq
    ones_lo = jnp.ones((seq_len, QK_NOPE), F32)
    zeros_lo = jnp.zeros((seq_len, QK_NOPE), F32)
    tail = jnp.zeros((seq_len, HEAD_PAD - QK_HEAD), F32)
    c = jnp.concatenate([ones_lo, jnp.cos(ang_r), jnp.cos(ang_r), jnp.cos(ang_c), jnp.cos(ang_c), tail], axis=1)
    s = jnp.concatenate([zeros_lo, -jnp.sin(ang_r), jnp.sin(ang_r), -jnp.sin(ang_c), jnp.sin(ang_c), tail], axis=1)
    return c, s


def _states_to_cols(st):
    b, ndir, _, g, n = st.shape
    st = st.reshape(b, ndir, 2, g // SSM_GB, SSM_GB * n)
    return jnp.transpose(st, (1, 0, 3, 2, 4)).reshape(ndir, b, 2 * g * n)


def _cols_to_states(hf, g, n):
    ndir, b, _ = hf.shape
    hf = hf.reshape(ndir, b, g // SSM_GB, 2, SSM_GB * n)
    return jnp.transpose(hf, (1, 0, 3, 2, 4)).reshape(b, ndir, 2, g, n)


def _mixer_layer(x, mod, lw, cols, batch, seq_len, per_batch, rope, ctx):
    p, h1 = _inproj_call(x, mod, lw["norm1_g"], lw["w_small"], seq_len, per_batch)
    q, k, v, ckv = _qkprep_call(p, lw, rope, seq_len, cols)
    if ctx is not None:
        ctx_ckv, ctx_kpe, h0 = ctx
        kc, vc = _ctxprep_call(ctx_ckv, ctx_kpe, lw)
        ctx_len = ctx_ckv.shape[0] // batch
    else:
        kc = vc = None
        ctx_len = 0
        h0 = jnp.zeros((2, batch, lw["ssm_a"].shape[-1]), F32)
    za = _attn_call(q, k, v, kc, vc, batch, seq_len, ctx_len)
    zb, zd = _convpool_call(p, lw, batch, seq_len, cols)

    ys, hf = _ssm_call(p.reshape(batch, seq_len, -1), h0, lw, cols["ssm_u"])
    ys = [y.reshape(2, batch * seq_len, LANES) for y in ys]
    merged = _merge_call(h1, p, za, zb, ys, zd, lw, cols)
    x1, h2 = _outproj_call(x, merged, mod, lw, seq_len, per_batch)
    x2 = _mlp_call(x1, h2, mod, lw, seq_len, per_batch)
    kr = cols["k_rope"] + QK_NOPE
    return x2, ckv, p[:, kr:kr + ROPE_DIM], hf


def kernel(x_prompt, x_sample, c, cache_ckv, cache_krope, state_ssm, c_ctx, w_ada, b_ada, norm1_g, norm2_g, w_in, q_a_norm_g, kv_a_norm_g, w_uq, w_ukv, q_norm_g, k_norm_g, w_mla_o, conv_w, conv_b, w_conv_o, ssm_lam_re, ssm_lam_im, ssm_log_step, ssm_b_re, ssm_b_im, ssm_c_re, ssm_c_im, ssm_d, w_glu, pool_w, pool_scale, w_pool_o, w_o, w_mlp1, w_mlp2):
    prm = dict(norm1_g=norm1_g, norm2_g=norm2_g, w_in=w_in, q_a_norm_g=q_a_norm_g, kv_a_norm_g=kv_a_norm_g,
               w_uq=w_uq, w_ukv=w_ukv, q_norm_g=q_norm_g, k_norm_g=k_norm_g, w_mla_o=w_mla_o,
               conv_w=conv_w, conv_b=conv_b, w_conv_o=w_conv_o, ssm_lam_re=ssm_lam_re, ssm_lam_im=ssm_lam_im,
               ssm_log_step=ssm_log_step, ssm_b_re=ssm_b_re, ssm_b_im=ssm_b_im, ssm_c_re=ssm_c_re,
               ssm_c_im=ssm_c_im, ssm_d=ssm_d, w_glu=w_glu, pool_w=pool_w, pool_scale=pool_scale,
               w_pool_o=w_pool_o, w_o=w_o, w_mlp1=w_mlp1, w_mlp2=w_mlp2)
    depth = w_in.shape[0]
    bp, lp, d = x_prompt.shape
    bs, ls, _ = x_sample.shape
    past = cache_ckv.shape[2]
    g, n = state_ssm.shape[-2:]
    sizes = (w_uq.shape[1], w_ukv.shape[1], conv_w.shape[-1], ssm_d.shape[-1], pool_scale.shape[-1])

    rows = -(-(1 + bs) // SUBLANES) * SUBLANES
    cvec = jnp.zeros((rows, d), F32).at[0].set(c_ctx).at[1:1 + bs].set(c)
    mods = _ada_call(cvec, w_ada, b_ada)
    rope = _rope_tables(ls)

    yp = x_prompt.reshape(bp * lp, d)
    ys = x_sample.reshape(bs * ls, d)
    ckv_list, krope_list, ssm_list = [], [], []
    for l in range(depth):
        lw, cols = _prep_layer(l, prm, sizes)
        mod_ctx = mods[l, 0:1].reshape(6, 1, d)
        mod_lat = mods[l, 1:1 + bs].reshape(bs * 6, 1, d)
        yp, ckv_l, krope_l, hf_l = _mixer_layer(yp, mod_ctx, lw, cols, bp, lp, False, None, None)
        ckv_list.append(ckv_l.reshape(bp, lp, -1))
        krope_list.append(krope_l.reshape(bp, lp, -1))
        ssm_list.append(_cols_to_states(hf_l, g, n))
        ctx_kpe = jnp.pad(cache_krope[:, l].reshape(bs * past, ROPE_DIM),
                          ((0, 0), (QK_NOPE, HEAD_PAD - QK_HEAD)))
        ctx = (cache_ckv[:, l].reshape(bs * past, -1), ctx_kpe, _states_to_cols(state_ssm[:, l]))
        ys, _, _, _ = _mixer_layer(ys, mod_lat, lw, cols, bs, ls, True, rope, ctx)
    return (yp.reshape(bp, lp, d), ys.reshape(bs, ls, d), jnp.stack(ckv_list, axis=1),
            jnp.stack(krope_list, axis=1), jnp.stack(ssm_list, axis=1))
```

```python
import functools
import math

import jax
import jax.numpy as jnp
from jax import lax
from jax.experimental import pallas as pl
from jax.experimental.pallas import tpu as pltpu

F32 = jnp.float32
BF16 = jnp.bfloat16

GRID_W = 64
N_BRANCH = 4
MLA_HEADS = 8
QK_NOPE = 64
ROPE_DIM = 32
QK_HEAD = QK_NOPE + ROPE_DIM
V_HEAD = 64
ROPE_THETA = 10000.0
SSM_GROUP_CH = 16
POOL_WINDOWS = (2, 4, 8, 16)
EPS = 1e-6
LOG2E = 1.4426950408889634

LANES = 128
SUBLANES = 8
HEAD_PAD = LANES
MIB = 1024 * 1024


def _cparams(sem, vmem_mib):
    return pltpu.CompilerParams(dimension_semantics=sem, vmem_limit_bytes=vmem_mib * MIB)


def _dot(a, b):
    return jnp.dot(a, b, preferred_element_type=F32)


def _mod_spec(chunk, d, tm, seq_len, per_batch):
    def imap(i, *_):
        row = (i * tm) // seq_len if per_batch else 0
        return (row * 6 + chunk, 0, 0)
    return pl.BlockSpec((1, 1, d), imap)


def _ada_kernel(c_ref, w_ref, b_ref, o_ref):
    cv = c_ref[...]
    s = cv * jax.nn.sigmoid(cv)
    o_ref[0] = _dot(s.astype(BF16), w_ref[0].astype(BF16)) + b_ref[0]


def _ada_call(cvec, w_ada, b_ada):
    depth, d, n = w_ada.shape
    rows = cvec.shape[0]
    tn = 1024
    return pl.pallas_call(
        _ada_kernel,
        out_shape=jax.ShapeDtypeStruct((depth, rows, n), F32),
        grid=(depth, n // tn),
        in_specs=[
            pl.BlockSpec((rows, d), lambda l, j: (0, 0)),
            pl.BlockSpec((1, d, tn), lambda l, j: (l, 0, j)),
            pl.BlockSpec((1, 1, tn), lambda l, j: (l, 0, j)),
        ],
        out_specs=pl.BlockSpec((1, rows, tn), lambda l, j: (l, 0, j)),
        compiler_params=_cparams(("parallel", "parallel"), 40),
        name="ada_mod",
    )(cvec, w_ada, b_ada.reshape(depth, 1, n))


def _modnorm(x, g, scale, shift):
    ms = jnp.mean(x * x, axis=-1, keepdims=True)
    return (x * lax.rsqrt(ms + EPS) * g) * (1.0 + scale) + shift


def _inproj_kernel(x_ref, sh_ref, sc_ref, g_ref, w_ref, o_ref, h_ref):
    h = _modnorm(x_ref[...], g_ref[...], sc_ref[0], sh_ref[0]).astype(BF16)
    h_ref[...] = h
    o_ref[...] = _dot(h, w_ref[...])


def _inproj_call(x, mod, g, w, l, seq_len, per_batch):
    t, d = x.shape
    n = w.shape[2]
    tm = 512
    mspec = functools.partial(_mod_spec, d=d, tm=tm, seq_len=seq_len, per_batch=per_batch)
    return pl.pallas_call(
        _inproj_kernel,
        out_shape=(jax.ShapeDtypeStruct((t, n), F32), jax.ShapeDtypeStruct((t, d), BF16)),
        grid=(t // tm,),
        in_specs=[
            pl.BlockSpec((tm, d), lambda i: (i, 0)),
            mspec(0), mspec(1),
            pl.BlockSpec((1, d), lambda i: (0, 0)),
            pl.BlockSpec((None, d, n), lambda i: (l, 0, 0), pipeline_mode=pl.Buffered(1)),
        ],
        out_specs=(pl.BlockSpec((tm, n), lambda i: (i, 0)), pl.BlockSpec((tm, d), lambda i: (i, 0))),
        compiler_params=_cparams(("parallel",), 52),
        name="in_proj",
    )(x, mod, mod, g, w)


def _rms(x, g, width):
    ms = jnp.sum(x * x, axis=-1, keepdims=True) * (1.0 / width)
    return x * lax.rsqrt(ms + EPS) * g


def _heads_norm(x, g, mult, shared=None, rope=None):
    outs = []
    for h in range(MLA_HEADS):
        sl = slice(h * HEAD_PAD, (h + 1) * HEAD_PAD)
        xh = x[:, sl] if shared is None else x[:, sl] + shared
        ms = jnp.sum(xh * xh, axis=-1, keepdims=True) * (1.0 / QK_HEAD)
        r = lax.rsqrt(ms + EPS)
        y = xh * r * g[:, sl]
        if rope is not None:
            xs, shared_s, gs, cos, sin = rope
            part = shared_s if xs is None else xs[:, sl]
            y = y * cos + (part * r * gs[:, sl]) * sin
        if mult != 1.0:
            y = y * mult
        outs.append(y)
    return jnp.concatenate(outs, axis=1)


def _qkprep_kernel(*refs, use_rope, q_mult):
    if use_rope:
        (qa_ref, kva_ref, kpe_ref, gqa_ref, gkva_ref, gq_ref, gk_ref, wuq_ref, wuk_ref, wuv_ref,
         kpes_ref, gqs_ref, gks_ref, wuqs_ref, c_ref, s_ref, q_out, k_out, v_out, ckv_out) = refs
    else:
        (qa_ref, kva_ref, kpe_ref, gqa_ref, gkva_ref, gq_ref, gk_ref, wuq_ref, wuk_ref, wuv_ref,
         q_out, k_out, v_out, ckv_out) = refs
    qa = qa_ref[...]
    qa_n = _rms(qa, gqa_ref[...], qa.shape[-1]).astype(BF16)
    q = _dot(qa_n, wuq_ref[...])
    q_rope = (_dot(qa_n, wuqs_ref[...]), None, gqs_ref[...], c_ref[...], s_ref[...]) if use_rope else None
    q_out[...] = _heads_norm(q, gq_ref[...], q_mult, rope=q_rope).astype(BF16)

    kva = kva_ref[...]
    ckv = _rms(kva, gkva_ref[...], kva.shape[-1])
    ckv_out[...] = ckv
    ckv_b = ckv.astype(BF16)
    k = _dot(ckv_b, wuk_ref[...])
    k_rope = (None, kpes_ref[...], gks_ref[...], c_ref[...], s_ref[...]) if use_rope else None
    k_out[...] = _heads_norm(k, gk_ref[...], 1.0, shared=kpe_ref[...], rope=k_rope).astype(BF16)
    v_out[...] = _dot(ckv_b, wuv_ref[...]).astype(BF16)


def _qkprep_call(p, lw, rope, seq_len, cols):
    t = p.shape[0]
    tm = 512
    hq = MLA_HEADS * HEAD_PAD
    hv = MLA_HEADS * V_HEAD
    qlora = lw["w_uq"].shape[0]
    kvlora = lw["w_uk"].shape[0]
    const = lambda i: (0, 0)
    in_specs = [
        pl.BlockSpec((tm, qlora), lambda i: (i, cols["q_a"] // qlora)),
        pl.BlockSpec((tm, kvlora), lambda i: (i, cols["kv_a"] // kvlora)),
        pl.BlockSpec((tm, HEAD_PAD), lambda i: (i, cols["k_rope"] // HEAD_PAD)),
        pl.BlockSpec((1, qlora), const),
        pl.BlockSpec((1, kvlora), const),
        pl.BlockSpec((1, hq), const),
        pl.BlockSpec((1, hq), const),
        pl.BlockSpec((qlora, hq), const),
        pl.BlockSpec((kvlora, hq), const),
        pl.BlockSpec((kvlora, hv), const),
    ]
    args = [p, p, p, lw["q_a_norm_g"], lw["kv_a_norm_g"], lw["q_norm_g"], lw["k_norm_g"],
            lw["w_uq"], lw["w_uk"], lw["w_uv"]]
    if rope is not None:
        nblk = seq_len // tm
        in_specs += [
            pl.BlockSpec((tm, HEAD_PAD), lambda i: (i, cols["k_rope_sw"] // HEAD_PAD)),
            pl.BlockSpec((1, hq), const), pl.BlockSpec((1, hq), const),
            pl.BlockSpec((qlora, hq), const),
            pl.BlockSpec((tm, HEAD_PAD), lambda i: (i % nblk, 0)),
            pl.BlockSpec((tm, HEAD_PAD), lambda i: (i % nblk, 0)),
        ]
        args += [p, lw["q_norm_g_sw"], lw["k_norm_g_sw"], lw["w_uq_sw"]] + list(rope)
    q_mult = LOG2E * QK_HEAD ** -0.5
    return pl.pallas_call(
        functools.partial(_qkprep_kernel, use_rope=rope is not None, q_mult=q_mult),
        out_shape=(jax.ShapeDtypeStruct((t, hq), BF16), jax.ShapeDtypeStruct((t, hq), BF16),
                   jax.ShapeDtypeStruct((t, hv), BF16), jax.ShapeDtypeStruct((t, kvlora), F32)),
        grid=(t // tm,),
        in_specs=in_specs,
        out_specs=(pl.BlockSpec((tm, hq), lambda i: (i, 0)), pl.BlockSpec((tm, hq), lambda i: (i, 0)),
                   pl.BlockSpec((tm, hv), lambda i: (i, 0)), pl.BlockSpec((tm, kvlora), lambda i: (i, 0))),
        compiler_params=_cparams(("parallel",), 40),
        name="qk_prep",
    )(*args)


def _ctxprep_kernel(ckv_ref, kpe_ref, gk_ref, wuk_ref, wuv_ref, k_out, v_out):
    ckv_b = ckv_ref[...].astype(BF16)
    k = _dot(ckv_b, wuk_ref[...])
    k_out[...] = _heads_norm(k, gk_ref[...], 1.0, shared=kpe_ref[...]).astype(BF16)
    v_out[...] = _dot(ckv_b, wuv_ref[...]).astype(BF16)


def _ctxprep_call(ckv, kpe, lw):
    t, kvlora = ckv.shape
    tm = 512
    hq = MLA_HEADS * HEAD_PAD
    hv = MLA_HEADS * V_HEAD
    const = lambda i: (0, 0)
    return pl.pallas_call(
        _ctxprep_kernel,
        out_shape=(jax.ShapeDtypeStruct((t, hq), BF16), jax.ShapeDtypeStruct((t, hv), BF16)),
        grid=(t // tm,),
        in_specs=[
            pl.BlockSpec((tm, kvlora), lambda i: (i, 0)),
            pl.BlockSpec((tm, HEAD_PAD), lambda i: (i, 0)),
            pl.BlockSpec((1, hq), const),
            pl.BlockSpec((kvlora, hq), const),
            pl.BlockSpec((kvlora, hv), const),
        ],
        out_specs=(pl.BlockSpec((tm, hq), lambda i: (i, 0)), pl.BlockSpec((tm, hv), lambda i: (i, 0))),
        compiler_params=_cparams(("parallel",), 40),
        name="ctx_prep",
    )(ckv, kpe, lw["k_norm_g"], lw["w_uk"], lw["w_uv"])


def _attn_kernel(*refs, has_ctx, heads):
    if has_ctx:
        q_ref, ko_ref, vo_ref, kc_ref, vc_ref, o_ref = refs
    else:
        q_ref, ko_ref, vo_ref, o_ref = refs
    contract_last = (((1,), (1,)), ((), ()))
    lane = lax.broadcasted_iota(jnp.int32, (q_ref.shape[0], 2 * V_HEAD), 1)
    for pair in range(heads // 2):
        vsl = slice(pair * 2 * V_HEAD, (pair + 1) * 2 * V_HEAD)
        outs = []
        for hh in range(2):
            h = 2 * pair + hh
            sl = slice(h * HEAD_PAD, (h + 1) * HEAD_PAD)
            q = q_ref[:, sl]
            s_o = lax.dot_general(q, ko_ref[:, sl], contract_last, preferred_element_type=F32)
            m = jnp.max(s_o, axis=-1, keepdims=True)
            if has_ctx:
                s_c = lax.dot_general(q, kc_ref[:, sl], contract_last, preferred_element_type=F32)
                m = jnp.maximum(m, jnp.max(s_c, axis=-1, keepdims=True))
            p_o = jnp.exp2(s_o - m)
            l = jnp.sum(p_o, axis=-1, keepdims=True)
            acc = _dot(p_o.astype(BF16), vo_ref[:, vsl])
            if has_ctx:
                p_c = jnp.exp2(s_c - m)
                l = l + jnp.sum(p_c, axis=-1, keepdims=True)
                acc = acc + _dot(p_c.astype(BF16), vc_ref[:, vsl])
            outs.append(acc / l)
        o_ref[:, vsl] = jnp.where(lane < V_HEAD, outs[0], outs[1]).astype(BF16)


def _attn_call(q, k, v, kc, vc, batch, seq_len, ctx_len):
    t = q.shape[0]
    tq = 256
    heads = 8
    nq = seq_len // tq
    npair = MLA_HEADS // heads
    qw = heads * HEAD_PAD
    vw = heads * V_HEAD
    in_specs = [
        pl.BlockSpec((tq, qw), lambda b, h, i: (b * nq + i, h)),
        pl.BlockSpec((seq_len, qw), lambda b, h, i: (b, h)),
        pl.BlockSpec((seq_len, vw), lambda b, h, i: (b, h)),
    ]
    args = [q, k, v]
    if kc is not None:
        in_specs += [pl.BlockSpec((ctx_len, qw), lambda b, h, i: (b, h)),
                     pl.BlockSpec((ctx_len, vw), lambda b, h, i: (b, h))]
        args += [kc, vc]
    return pl.pallas_call(
        functools.partial(_attn_kernel, has_ctx=kc is not None, heads=heads),
        out_shape=jax.ShapeDtypeStruct((t, MLA_HEADS * V_HEAD), BF16),
        grid=(batch, npair, nq),
        in_specs=in_specs,
        out_specs=pl.BlockSpec((tq, vw), lambda b, h, i: (b * nq + i, h)),
        compiler_params=_cparams(("parallel", "parallel", "arbitrary"), 48),
        name="attention",
    )(*args)


def _convpool_kernel(cu_ref, bg_ref, cg_ref, pu_ref, cw_ref, cb_ref, pw_ref, ps_ref, zb_ref, zd_ref):
    n = cu_ref.shape[0]
    row = lax.broadcasted_iota(jnp.int32, cu_ref.shape, 0)

    def down(x, k):
        return jnp.where(row >= k, pltpu.roll(x, k, axis=0), 0.0)

    def up(x, k):
        return jnp.where(row < n - k, pltpu.roll(x, n - k, axis=0), 0.0)

    v = cg_ref[...] * cu_ref[...]
    conv = (down(v, 1) * cw_ref[0:1, :] + v * cw_ref[1:2, :] + up(v, 1) * cw_ref[2:3, :]
            + cb_ref[...])
    zb_ref[...] = (bg_ref[...] * conv).astype(BF16)

    u = pu_ref[...]
    rowf = row.astype(F32)
    grp = pl.program_id(1)
    for gi, win in enumerate(POOL_WINDOWS):
        @pl.when(grp == gi)
        def _(win=win):
            half = win // 2
            fwd = u
            bwd = u
            k = 1
            while k < half:
                fwd = fwd + up(fwd, k)
                bwd = bwd + down(bwd, k)
                k *= 2
            total = fwd + down(bwd, 1)
            cnt = jnp.minimum(rowf + half, float(n)) - jnp.maximum(rowf - half, 0.0)
            mean = total / cnt - u
            zd_ref[...] = (_dot(mean.astype(BF16), pw_ref[0]) * ps_ref[...]).astype(BF16)


def _convpool_call(p, lw, batch, seq_len, cols):
    t = p.shape[0]
    cw = LANES
    nblk = lw["conv_w"].shape[1] // cw

    def pspec(name):
        base = cols[name] // cw
        return pl.BlockSpec((seq_len, cw), lambda b, g: (b, base + g))

    vec = lambda b, g: (0, g)
    return pl.pallas_call(
        _convpool_kernel,
        out_shape=(jax.ShapeDtypeStruct((t, nblk * cw), BF16), jax.ShapeDtypeStruct((t, nblk * cw), BF16)),
        grid=(batch, nblk),
        in_specs=[pspec("conv_u"), pspec("conv_bg"), pspec("conv_cg"), pspec("pool_u"),
                  pl.BlockSpec((3, cw), vec), pl.BlockSpec((1, cw), vec),
                  pl.BlockSpec((1, cw, cw), lambda b, g: (g, 0, 0)), pl.BlockSpec((1, cw), vec)],
        out_specs=(pl.BlockSpec((seq_len, cw), lambda b, g: (b, g)),
                   pl.BlockSpec((seq_len, cw), lambda b, g: (b, g))),
        compiler_params=_cparams(("parallel", "parallel"), 40),
        name="conv_pool",
    )(p, p, p, p, lw["conv_w"], lw["conv_b"], lw["pool_w"], lw["pool_scale"])


SSM_R = 4
SSM_GB = LANES // SSM_GROUP_CH


def _ssm_kernel(*refs, steps, nchunks, nblk):
    u_refs = refs[:nblk]
    h0_ref, a_ref, mb_ref, m1_ref, m2_ref = refs[nblk:nblk + 5]
    y_refs = refs[nblk + 5:2 * nblk + 5]
    hf_ref, slab_scr, xs_scr, st_scr = refs[2 * nblk + 5:]
    d = pl.program_id(0)
    c = pl.program_id(2)
    width = st_scr.shape[1]
    bw = width // nblk

    @pl.when(c == 0)
    def _():
        st_scr[...] = h0_ref[0]

    for blk in range(nblk):
        for j in range(SSM_R):
            for b in range(SUBLANES):
                slab_scr[blk * SSM_R + j, pl.ds(b, steps, stride=SUBLANES), :] = (
                    u_refs[blk][b, pl.ds(j, steps, stride=SSM_R), :])

    def packed(blk):
        return jnp.concatenate([slab_scr[blk * SSM_R + j] for j in range(SSM_R)], axis=1).astype(BF16)

    for blk in range(nblk):
        xs_scr[:, blk * bw:(blk + 1) * bw] = _dot(packed(blk), mb_ref[0, blk])

    gw = width // 2
    for part in range(width // gw):
        offs = []
        for blk in range(part * nblk // 2, (part + 1) * nblk // 2):
            offs += [(blk * bw + k * LANES, blk * bw + bw // 2 + k * LANES) for k in range(bw // 2 // LANES)]
        coef = [(a_ref[0, :, r0:r0 + LANES], a_ref[0, :, i0:i0 + LANES]) for r0, i0 in offs]

        def body(t, carry, offs=offs, coef=coef):
            tt = t + d * (steps - 1 - 2 * t)
            row0 = pl.multiple_of(tt * SUBLANES, SUBLANES)
            new = []
            for (r0, i0), (a_re, a_im), x_re, x_im in zip(offs, coef, carry[0::2], carry[1::2]):
                n_re = (a_re * x_re - a_im * x_im) + xs_scr[pl.ds(row0, SUBLANES), r0:r0 + LANES]
                n_im = (a_re * x_im + a_im * x_re) + xs_scr[pl.ds(row0, SUBLANES), i0:i0 + LANES]
                xs_scr[pl.ds(row0, SUBLANES), r0:r0 + LANES] = x_re
                xs_scr[pl.ds(row0, SUBLANES), i0:i0 + LANES] = x_im
                new += [n_re, n_im]
            return tuple(new)

        init = []
        for r0, i0 in offs:
            init += [st_scr[:, r0:r0 + LANES], st_scr[:, i0:i0 + LANES]]
        final = lax.fori_loop(0, steps, body, tuple(init), unroll=2)
        for (r0, i0), x_re, x_im in zip(offs, final[0::2], final[1::2]):
            st_scr[:, r0:r0 + LANES] = x_re
            st_scr[:, i0:i0 + LANES] = x_im

    for blk in range(nblk):
        y = (_dot(xs_scr[:, blk * bw:(blk + 1) * bw].astype(BF16), m1_ref[0, blk])
             + _dot(packed(blk), m2_ref[0, blk]))
        for j in range(SSM_R):
            slab_scr[blk * SSM_R + j] = y[:, j * LANES:(j + 1) * LANES]
        for j in range(SSM_R):
            for b in range(SUBLANES):
                y_refs[blk][0, b, pl.ds(j, steps, stride=SSM_R), :] = (
                    slab_scr[blk * SSM_R + j, pl.ds(b, steps, stride=SUBLANES), :])

    @pl.when(c == nchunks - 1)
    def _():
        hf_ref[0] = st_scr[...]


def _ssm_call(p3, h0, lw, col0):
    batch, seq_len, _ = p3.shape
    width = h0.shape[-1]
    nblk = lw["ssm_mb"].shape[2]
    l = lw["l"]
    steps = min(seq_len // SSM_R, 64)
    tokens = steps * SSM_R
    nchunks = seq_len // tokens
    nbg = batch // SUBLANES
    rows = steps * SUBLANES

    def cidx(d, c):
        return c + d * (nchunks - 1 - 2 * c)

    def uspec(blk):
        return pl.BlockSpec((SUBLANES, tokens, LANES), lambda d, b, c: (b, cidx(d, c), col0 // LANES + blk))

    def wspec(arr):
        return pl.BlockSpec((None, 1) + arr.shape[2:], lambda d, b, c: (l, d, 0, 0, 0),
                            pipeline_mode=pl.Buffered(1))

    yshape = jax.ShapeDtypeStruct((2, batch, seq_len, LANES), F32)
    yspec = pl.BlockSpec((1, SUBLANES, tokens, LANES), lambda d, b, c: (d, b, cidx(d, c), 0))
    outs = pl.pallas_call(
        functools.partial(_ssm_kernel, steps=steps, nchunks=nchunks, nblk=nblk),
        out_shape=(yshape,) * nblk + (jax.ShapeDtypeStruct((2, batch, width), F32),),
        grid=(2, nbg, nchunks),
        in_specs=[uspec(blk) for blk in range(nblk)] + [
            pl.BlockSpec((1, SUBLANES, width), lambda d, b, c: (d, b, 0)),
            pl.BlockSpec((None, 1, SUBLANES, width), lambda d, b, c: (l, d, 0, 0)),
            wspec(lw["ssm_mb"]), wspec(lw["ssm_m1"]), wspec(lw["ssm_m2"]),
        ],
        out_specs=(yspec,) * nblk + (pl.BlockSpec((1, SUBLANES, width), lambda d, b, c: (d, b, 0)),),
        scratch_shapes=[pltpu.VMEM((nblk * SSM_R, rows, LANES), F32), pltpu.VMEM((rows, width), F32),
                        pltpu.VMEM((SUBLANES, width), F32)],
        compiler_params=_cparams(("arbitrary", "arbitrary", "arbitrary"), 52),
        name="s5_scan",
    )(*([p3] * nblk), h0, lw["ssm_a"], lw["ssm_mb"], lw["ssm_m1"], lw["ssm_m2"])
    return outs[:nblk], outs[nblk]


def _merge_kernel(*refs, nblk):
    (h_ref, za_ref, zb_ref, su_ref, sd_ref, zd_ref, wg0, wg1, wg2, wg3, wa, wb, wca, wcg, wd) = refs[:15]
    yf_refs = refs[15:15 + nblk]
    yb_refs = refs[15 + nblk:15 + 2 * nblk]
    o_ref, y_scr = refs[15 + 2 * nblk:]

    @pl.when(pl.program_id(1) == 0)
    def _():
        y_both = jnp.concatenate([f[0] + b[0] for f, b in zip(yf_refs, yb_refs)], axis=1)
        y_scr[...] = (y_both + sd_ref[...] * su_ref[...]).astype(BF16)

    h = h_ref[...]
    y = y_scr[...]

    def gate(w_ref):
        return jax.nn.sigmoid(_dot(h, w_ref[...]))

    merged = gate(wg0) * _dot(za_ref[...], wa[...])
    merged = merged + gate(wg1) * _dot(zb_ref[...], wb[...])
    merged = merged + gate(wg2) * (_dot(y, wca[...]) * jax.nn.sigmoid(_dot(y, wcg[...])))
    merged = merged + gate(wg3) * _dot(zd_ref[...], wd[...])
    o_ref[...] = merged.astype(BF16)


def _merge_call(h, p, za, zb, ys, zd, lw, cols):
    t, d = h.shape
    bw = za.shape[1]
    tm, tn = 512, 512
    nj = d // tn
    nblk = len(ys)
    row = lambda i, j: (i, 0)
    su_blk = cols["ssm_u"] // bw

    l = lw["l"]

    def gspec(k):
        return pl.BlockSpec((None, d, tn), lambda i, j: (l, 0, k * nj + j))

    def yspec(direction):
        return pl.BlockSpec((1, tm, LANES), lambda i, j: (direction, i, 0))

    wcol = pl.BlockSpec((None, bw, tn), lambda i, j: (l, 0, j))
    in_specs = [
        pl.BlockSpec((tm, d), row),
        pl.BlockSpec((tm, bw), row), pl.BlockSpec((tm, bw), row),
        pl.BlockSpec((tm, bw), lambda i, j: (i, su_blk)),
        pl.BlockSpec((1, bw), lambda i, j: (0, 0)),
        pl.BlockSpec((tm, bw), row),
        gspec(0), gspec(1), gspec(2), gspec(3),
        wcol, wcol, wcol, pl.BlockSpec((None, bw, tn), lambda i, j: (l, 0, nj + j)), wcol,
    ] + [yspec(0)] * nblk + [yspec(1)] * nblk
    wg = lw["w_gate"]
    return pl.pallas_call(
        functools.partial(_merge_kernel, nblk=nblk),
        out_shape=jax.ShapeDtypeStruct((t, d), BF16),
        grid=(t // tm, nj),
        in_specs=in_specs,
        out_specs=pl.BlockSpec((tm, tn), lambda i, j: (i, j)),
        scratch_shapes=[pltpu.VMEM((tm, bw), BF16)],
        compiler_params=_cparams(("parallel", "arbitrary"), 48),
        name="branch_merge",
    )(h, za, zb, p, lw["ssm_d"], zd,
      wg, wg, wg, wg, lw["w_mla_o"], lw["w_conv_o"], lw["w_glu"], lw["w_glu"], lw["w_pool_o"], *ys, *ys)


def _outproj_kernel(x_ref, m_ref, gt_ref, sh_ref, sc_ref, g_ref, wo_ref, x1_ref, h2_ref):
    x1 = x_ref[...] + gt_ref[0] * _dot(m_ref[...], wo_ref[...])
    x1_ref[...] = x1
    h2_ref[...] = _modnorm(x1, g_ref[...], sc_ref[0], sh_ref[0]).astype(BF16)


def _outproj_call(x, merged, mod, lw, seq_len, per_batch):
    t, d = x.shape
    tm = 512
    mspec = functools.partial(_mod_spec, d=d, tm=tm, seq_len=seq_len, per_batch=per_batch)
    row = lambda i: (i, 0)
    return pl.pallas_call(
        _outproj_kernel,
        out_shape=(jax.ShapeDtypeStruct((t, d), F32), jax.ShapeDtypeStruct((t, d), BF16)),
        grid=(t // tm,),
        in_specs=[
            pl.BlockSpec((tm, d), row), pl.BlockSpec((tm, d), row),
            mspec(2), mspec(3), mspec(4),
            pl.BlockSpec((1, d), lambda i: (0, 0)),
            pl.BlockSpec((None, d, d), lambda i: (lw["l"], 0, 0), pipeline_mode=pl.Buffered(1)),
        ],
        out_specs=(pl.BlockSpec((tm, d), row), pl.BlockSpec((tm, d), row)),
        compiler_params=_cparams(("parallel",), 48),
        name="out_proj",
    )(x, merged, mod, mod, mod, lw["norm2_g"], lw["w_o"])


def _mlp_kernel(x_ref, h_ref, gt_ref, w1_ref, w2_ref, o_ref):
    @pl.when(pl.program_id(1) == 0)
    def _():
        o_ref[...] = x_ref[...]

    hid = jnp.square(jnp.maximum(_dot(h_ref[...], w1_ref[...]), 0.0))
    o_ref[...] += gt_ref[0] * _dot(hid.astype(BF16), w2_ref[...])


def _mlp_call(x, h, mod, lw, seq_len, per_batch):
    t, d = x.shape
    hidden = lw["w_mlp1"].shape[2]
    tm, tk = 512, 1024
    l = lw["l"]
    mspec = functools.partial(_mod_spec, d=d, tm=tm, seq_len=seq_len, per_batch=per_batch)
    row = lambda i, k: (i, 0)
    return pl.pallas_call(
        _mlp_kernel,
        out_shape=jax.ShapeDtypeStruct((t, d), F32),
        grid=(t // tm, hidden // tk),
        in_specs=[
            pl.BlockSpec((tm, d), row), pl.BlockSpec((tm, d), row), mspec(5),
            pl.BlockSpec((None, d, tk), lambda i, k: (l, 0, k)),
            pl.BlockSpec((None, tk, d), lambda i, k: (l, k, 0)),
        ],
        out_specs=pl.BlockSpec((tm, d), row),
        compiler_params=_cparams(("parallel", "arbitrary"), 48),
        name="mlp",
    )(x, h, mod, lw["w_mlp1"], lw["w_mlp2"])


def _pad_heads(w, per_head, lo):
    lead = w.shape[:-1]
    w = w.reshape(lead + (MLA_HEADS, per_head))
    w = jnp.pad(w, [(0, 0)] * len(lead) + [(0, 0), (lo, HEAD_PAD - lo - per_head)])
    return w.reshape(lead + (MLA_HEADS * HEAD_PAD,))


_ROPE_PARTNER = tuple((i // 16) * 16 + (i % 16 + 8) % 16 for i in range(ROPE_DIM))


def _swap_heads(w):
    lead = w.shape[:-1]
    w = w.reshape(lead + (MLA_HEADS, QK_HEAD))[..., QK_NOPE:][..., jnp.array(_ROPE_PARTNER)]
    w = jnp.pad(w, [(0, 0)] * len(lead) + [(0, 0), (QK_NOPE, HEAD_PAD - QK_HEAD)])
    return w.reshape(lead + (MLA_HEADS * HEAD_PAD,))


def _prep_ssm(lam_re, lam_im, log_step, b_re, b_im, c_re, c_im):
    r, gb = SSM_R, SSM_GB
    nl, ndir, g, n, cg = b_re.shape
    ns = nl * ndir
    nblk = g // gb
    flat = lambda v: v.reshape((ns,) + v.shape[2:])
    lam_re, lam_im, log_step, b_re, b_im, c_re, c_im = map(
        flat, (lam_re, lam_im, log_step, b_re, b_im, c_re, c_im))
    step = jnp.exp(log_step)[..., None]
    pw = []
    for p in range(r + 1):
        mag = jnp.exp(p * (lam_re * step))
        pw.append((mag * jnp.cos(p * (lam_im * step)), mag * jnp.sin(p * (lam_im * step))))
    ar, ai = pw[1]
    den = lam_re * lam_re + lam_im * lam_im
    qr = ((ar - 1.0) * lam_re + ai * lam_im) / den
    qi = (ai * lam_re - (ar - 1.0) * lam_im) / den
    bb_re = qr[..., None] * b_re - qi[..., None] * b_im
    bb_im = qr[..., None] * b_im + qi[..., None] * b_re
    fwd = (jnp.arange(ns) % ndir == 0)[:, None, None]

    def power(p_fwd, p_bwd):
        return (jnp.where(fwd, pw[p_fwd][0], pw[p_bwd][0]), jnp.where(fwd, pw[p_fwd][1], pw[p_bwd][1]))

    rows_g = (jnp.arange(r * gb * cg) // cg) % gb
    cols_g = jnp.arange(gb * n) // n

    t_re, t_im = [], []
    for j in range(r):
        pr, pi = power(r - 1 - j, j)
        t_re.append(pr[..., None] * bb_re - pi[..., None] * bb_im)
        t_im.append(pr[..., None] * bb_im + pi[..., None] * bb_re)
    mask_b = (rows_g[:, None] == cols_g[None, :]).astype(F32)

    def inc_block(ts):
        t = jnp.stack(ts, axis=1).reshape(ns, r, nblk, gb, n, cg)
        t = jnp.transpose(t, (0, 2, 1, 3, 5, 4)).reshape(ns, nblk, r * gb * cg, n)
        return jnp.tile(t, (1, 1, 1, gb)) * mask_b

    mb = jnp.concatenate([inc_block(t_re), inc_block(t_im)], axis=-1)

    cl_re, cl_im = [], []
    for j in range(r):
        pr, pi = power(j + 1, r - j)
        cl_re.append(c_re * pr[:, :, None, :] - c_im * pi[:, :, None, :])
        cl_im.append(c_re * pi[:, :, None, :] + c_im * pr[:, :, None, :])

    def out_block(ts):
        t = jnp.stack(ts, axis=1).reshape(ns, r, nblk, gb, cg, n)
        t = jnp.transpose(t, (0, 2, 5, 1, 3, 4)).reshape(ns, nblk, n, r * gb * cg)
        return jnp.tile(t, (1, 1, gb, 1)) * mask_b.T

    m1 = jnp.concatenate([out_block(cl_re), -out_block(cl_im)], axis=2)

    hi = lax.Precision.HIGHEST
    lane_g = jnp.arange(gb * cg) // cg
    mask_k = (lane_g[:, None] == lane_g[None, :]).astype(F32)
    kd = []
    for p in range(r):
        pr, pi = pw[p]
        k2 = (jnp.einsum("sgcn,sgnk->sgck", c_re * pr[:, :, None, :] - c_im * pi[:, :, None, :], bb_re, precision=hi)
              - jnp.einsum("sgcn,sgnk->sgck", c_re * pi[:, :, None, :] + c_im * pr[:, :, None, :], bb_im, precision=hi))
        k2 = jnp.swapaxes(k2, -1, -2).reshape(ns, nblk, gb * cg, cg)
        kd.append(jnp.tile(k2, (1, 1, 1, gb)) * mask_k)
    fwd4 = fwd[..., None]
    zero = jnp.zeros_like(kd[0])
    m2_rows = []
    for i in range(r):
        blocks = []
        for j in range(r):
            if i == j:
                blocks.append(kd[0])
            elif j > i:
                blocks.append(jnp.where(fwd4, kd[j - i], zero))
            else:
                blocks.append(jnp.where(fwd4, zero, kd[i - j]))
        m2_rows.append(jnp.concatenate(blocks, axis=-1))
    m2 = jnp.concatenate(m2_rows, axis=2)

    a = jnp.stack([pw[r][0].reshape(ns, nblk, gb * n), pw[r][1].reshape(ns, nblk, gb * n)], axis=2)
    a = jnp.broadcast_to(a.reshape(ns, 1, 2 * g * n), (ns, SUBLANES, 2 * g * n))
    unflat = lambda v: v.reshape((nl, ndir) + v.shape[1:])
    return unflat(a), unflat(mb.astype(BF16)), unflat(m1.astype(BF16)), unflat(m2.astype(BF16))


def _prep_stacked(prm, sizes):
    q_lora, kv_lora, conv_w, ssm_w, pool_w_ = sizes
    w_in = prm["w_in"]
    offs = {}
    o = 0
    for name, width in (("q_a", q_lora), ("kv_a", kv_lora), ("k_rope", ROPE_DIM), ("conv_u", conv_w),
                        ("conv_bg", conv_w), ("conv_cg", conv_w), ("ssm_u", ssm_w), ("pool_u", pool_w_)):
        offs[name] = (o, o + width)
        o += width
    gate0 = o
    take = lambda name: w_in[:, :, offs[name][0]:offs[name][1]]
    rope_lanes = ((0, 0), (0, 0), (QK_NOPE, HEAD_PAD - QK_HEAD))
    kr_pad = jnp.pad(take("k_rope"), rope_lanes)
    kr_sw = jnp.pad(take("k_rope")[:, :, jnp.array(_ROPE_PARTNER)], rope_lanes)
    pieces = [("q_a", take("q_a")), ("conv_u", take("conv_u")), ("conv_bg", take("conv_bg")),
              ("conv_cg", take("conv_cg")), ("ssm_u", take("ssm_u")), ("pool_u", take("pool_u")),
              ("kv_a", take("kv_a")), ("k_rope", kr_pad), ("k_rope_sw", kr_sw)]
    cols = {}
    o = 0
    for name, w in pieces:
        cols[name] = o
        o += w.shape[2]
    big = dict(
        w_small=jnp.concatenate([w for _, w in pieces], axis=2).astype(BF16),
        w_gate=w_in[:, :, gate0:].astype(BF16),
    )
    for name in ("w_mla_o", "w_conv_o", "w_glu", "w_pool_o", "w_o", "w_mlp1", "w_mlp2"):
        big[name] = prm[name].astype(BF16)
    return big, cols


def _prep_layer(l, prm, sizes, big, ssm):
    kv_lora = sizes[1]
    w_ukv = prm["w_ukv"][l].reshape(kv_lora, MLA_HEADS, QK_NOPE + V_HEAD)
    w_uk = _pad_heads(w_ukv[:, :, :QK_NOPE].reshape(kv_lora, -1), QK_NOPE, 0)
    w_uv = w_ukv[:, :, QK_NOPE:].reshape(kv_lora, -1)
    ssm_a, ssm_mb, ssm_m1, ssm_m2 = ssm

    row = lambda v: v.reshape(1, -1)
    lw = dict(
        big, l=l,
        norm1_g=row(prm["norm1_g"][l]), norm2_g=row(prm["norm2_g"][l]),
        q_a_norm_g=row(prm["q_a_norm_g"][l]), kv_a_norm_g=row(prm["kv_a_norm_g"][l]),
        q_norm_g=row(_pad_heads(jnp.tile(prm["q_norm_g"][l], MLA_HEADS), QK_HEAD, 0)),
        k_norm_g=row(_pad_heads(jnp.tile(prm["k_norm_g"][l], MLA_HEADS), QK_HEAD, 0)),
        q_norm_g_sw=row(_swap_heads(jnp.tile(prm["q_norm_g"][l], MLA_HEADS))),
        k_norm_g_sw=row(_swap_heads(jnp.tile(prm["k_norm_g"][l], MLA_HEADS))),
        w_uq=_pad_heads(prm["w_uq"][l], QK_HEAD, 0).astype(BF16),
        w_uq_sw=_swap_heads(prm["w_uq"][l]).astype(BF16),
        w_uk=w_uk.astype(BF16), w_uv=w_uv.astype(BF16),
        conv_w=prm["conv_w"][l], conv_b=row(prm["conv_b"][l]),
        ssm_a=ssm_a, ssm_mb=ssm_mb, ssm_m1=ssm_m1, ssm_m2=ssm_m2,
        ssm_d=row(prm["ssm_d"][l]),
        pool_w=prm["pool_w"][l].astype(BF16), pool_scale=row(prm["pool_scale"][l]),
    )
    return lw


def _rope_tables(seq_len):
    half = ROPE_DIM // 2
    inv_freq = ROPE_THETA ** (-jnp.arange(0, half, 2, dtype=F32) / half)
    t = jnp.arange(seq_len)
    ang_r = (t // GRID_W).astype(F32)[:, None] * inv_freq
    ang_c = (t % GRID_W).astype(F32)[:, None] * inv_freq
    ones_lo = jnp.ones((seq_len, QK_NOPE), F32)
    zeros_lo = jnp.zeros((seq_len, QK_NOPE), F32)
    tail = jnp.zeros((seq_len, HEAD_PAD - QK_HEAD), F32)
    c = jnp.concatenate([ones_lo, jnp.cos(ang_r), jnp.cos(ang_r), jnp.cos(ang_c), jnp.cos(ang_c), tail], axis=1)
    s = jnp.concatenate([zeros_lo, -jnp.sin(ang_r), jnp.sin(ang_r), -jnp.sin(ang_c), jnp.sin(ang_c), tail], axis=1)
    return c, s


def _states_to_cols(st):
    b, ndir, _, g, n = st.shape
    st = st.reshape(b, ndir, 2, g // SSM_GB, SSM_GB * n)
    return jnp.transpose(st, (1, 0, 3, 2, 4)).reshape(ndir, b, 2 * g * n)


def _cols_to_states(hf, g, n):
    ndir, b, _ = hf.shape
    hf = hf.reshape(ndir, b, g // SSM_GB, 2, SSM_GB * n)
    return jnp.transpose(hf, (1, 0, 3, 2, 4)).reshape(b, ndir, 2, g, n)


def _mixer_layer(x, mod, lw, cols, batch, seq_len, per_batch, rope, ctx):
    p, h1 = _inproj_call(x, mod, lw["norm1_g"], lw["w_small"], lw["l"], seq_len, per_batch)
    q, k, v, ckv = _qkprep_call(p, lw, rope, seq_len, cols)
    if ctx is not None:
        ctx_ckv, ctx_kpe, h0 = ctx
        kc, vc = _ctxprep_call(ctx_ckv, ctx_kpe, lw)
        ctx_len = ctx_ckv.shape[0] // batch
    else:
        kc = vc = None
        ctx_len = 0
        h0 = jnp.zeros((2, batch, lw["ssm_a"].shape[-1]), F32)
    za = _attn_call(q, k, v, kc, vc, batch, seq_len, ctx_len)
    zb, zd = _convpool_call(p, lw, batch, seq_len, cols)

    ys, hf = _ssm_call(p.reshape(batch, seq_len, -1), h0, lw, cols["ssm_u"])
    ys = [y.reshape(2, batch * seq_len, LANES) for y in ys]
    merged = _merge_call(h1, p, za, zb, ys, zd, lw, cols)
    x1, h2 = _outproj_call(x, merged, mod, lw, seq_len, per_batch)
    x2 = _mlp_call(x1, h2, mod, lw, seq_len, per_batch)
    kr = cols["k_rope"] + QK_NOPE
    return x2, ckv, p[:, kr:kr + ROPE_DIM], hf


def kernel(x_prompt, x_sample, c, cache_ckv, cache_krope, state_ssm, c_ctx, w_ada, b_ada, norm1_g, norm2_g, w_in, q_a_norm_g, kv_a_norm_g, w_uq, w_ukv, q_norm_g, k_norm_g, w_mla_o, conv_w, conv_b, w_conv_o, ssm_lam_re, ssm_lam_im, ssm_log_step, ssm_b_re, ssm_b_im, ssm_c_re, ssm_c_im, ssm_d, w_glu, pool_w, pool_scale, w_pool_o, w_o, w_mlp1, w_mlp2):
    prm = dict(norm1_g=norm1_g, norm2_g=norm2_g, w_in=w_in, q_a_norm_g=q_a_norm_g, kv_a_norm_g=kv_a_norm_g,
               w_uq=w_uq, w_ukv=w_ukv, q_norm_g=q_norm_g, k_norm_g=k_norm_g, w_mla_o=w_mla_o,
               conv_w=conv_w, conv_b=conv_b, w_conv_o=w_conv_o, ssm_lam_re=ssm_lam_re, ssm_lam_im=ssm_lam_im,
               ssm_log_step=ssm_log_step, ssm_b_re=ssm_b_re, ssm_b_im=ssm_b_im, ssm_c_re=ssm_c_re,
               ssm_c_im=ssm_c_im, ssm_d=ssm_d, w_glu=w_glu, pool_w=pool_w, pool_scale=pool_scale,
               w_pool_o=w_pool_o, w_o=w_o, w_mlp1=w_mlp1, w_mlp2=w_mlp2)
    depth = w_in.shape[0]
    bp, lp, d = x_prompt.shape
    bs, ls, _ = x_sample.shape
    past = cache_ckv.shape[2]
    g, n = state_ssm.shape[-2:]
    sizes = (w_uq.shape[1], w_ukv.shape[1], conv_w.shape[-1], ssm_d.shape[-1], pool_scale.shape[-1])

    rows = -(-(1 + bs) // SUBLANES) * SUBLANES
    cvec = jnp.zeros((rows, d), F32).at[0].set(c_ctx).at[1:1 + bs].set(c)
    mods = _ada_call(cvec, w_ada, b_ada)
    rope = _rope_tables(ls)
    ssm = _prep_ssm(ssm_lam_re, ssm_lam_im, ssm_log_step, ssm_b_re, ssm_b_im, ssm_c_re, ssm_c_im)
    big, cols = _prep_stacked(prm, sizes)

    yp = x_prompt.reshape(bp * lp, d)
    ys = x_sample.reshape(bs * ls, d)
    ckv_list, krope_list, ssm_list = [], [], []
    for l in range(depth):
        lw = _prep_layer(l, prm, sizes, big, ssm)
        mod_ctx = mods[l, 0:1].reshape(6, 1, d)
        mod_lat = mods[l, 1:1 + bs].reshape(bs * 6, 1, d)
        yp, ckv_l, krope_l, hf_l = _mixer_layer(yp, mod_ctx, lw, cols, bp, lp, False, None, None)
        ckv_list.append(ckv_l.reshape(bp, lp, -1))
        krope_list.append(krope_l.reshape(bp, lp, -1))
        ssm_list.append(_cols_to_states(hf_l, g, n))
        ctx_kpe = jnp.pad(cache_krope[:, l].reshape(bs * past, ROPE_DIM),
                          ((0, 0), (QK_NOPE, HEAD_PAD - QK_HEAD)))
        ctx = (cache_ckv[:, l].reshape(bs * past, -1), ctx_kpe, _states_to_cols(state_ssm[:, l]))
        ys, _, _, _ = _mixer_layer(ys, mod_lat, lw, cols, bs, ls, True, rope, ctx)
    return (yp.reshape(bp, lp, d), ys.reshape(bs, ls, d), jnp.stack(ckv_list, axis=1),
            jnp.stack(krope_list, axis=1), jnp.stack(ssm_list, axis=1))
```

```python
import functools
import math

import jax
import jax.numpy as jnp
from jax import lax
from jax.experimental import pallas as pl
from jax.experimental.pallas import tpu as pltpu

F32 = jnp.float32
BF16 = jnp.bfloat16

GRID_W = 64
N_BRANCH = 4
MLA_HEADS = 8
QK_NOPE = 64
ROPE_DIM = 32
QK_HEAD = QK_NOPE + ROPE_DIM
V_HEAD = 64
ROPE_THETA = 10000.0
SSM_GROUP_CH = 16
POOL_WINDOWS = (2, 4, 8, 16)
EPS = 1e-6
LOG2E = 1.4426950408889634

LANES = 128
SUBLANES = 8
HEAD_PAD = LANES
MIB = 1024 * 1024


def _cparams(sem, vmem_mib):
    return pltpu.CompilerParams(dimension_semantics=sem, vmem_limit_bytes=vmem_mib * MIB)


def _dot(a, b):
    return jnp.dot(a, b, preferred_element_type=F32)


def _mod_spec(chunk, d, tm, seq_len, per_batch):
    def imap(i, *_):
        row = (i * tm) // seq_len if per_batch else 0
        return (row * 6 + chunk, 0, 0)
    return pl.BlockSpec((1, 1, d), imap)


def _ada_kernel(c_ref, w_ref, b_ref, o_ref):
    cv = c_ref[...]
    s = cv * jax.nn.sigmoid(cv)
    o_ref[0] = _dot(s.astype(BF16), w_ref[0].astype(BF16)) + b_ref[0]


def _ada_call(cvec, w_ada, b_ada):
    depth, d, n = w_ada.shape
    rows = cvec.shape[0]
    tn = 1024
    return pl.pallas_call(
        _ada_kernel,
        out_shape=jax.ShapeDtypeStruct((depth, rows, n), F32),
        grid=(depth, n // tn),
        in_specs=[
            pl.BlockSpec((rows, d), lambda l, j: (0, 0)),
            pl.BlockSpec((1, d, tn), lambda l, j: (l, 0, j)),
            pl.BlockSpec((1, 1, tn), lambda l, j: (l, 0, j)),
        ],
        out_specs=pl.BlockSpec((1, rows, tn), lambda l, j: (l, 0, j)),
        compiler_params=_cparams(("parallel", "parallel"), 40),
        name="ada_mod",
    )(cvec, w_ada, b_ada.reshape(depth, 1, n))


def _modnorm(x, g, scale, shift):
    ms = jnp.mean(x * x, axis=-1, keepdims=True)
    return (x * lax.rsqrt(ms + EPS) * g) * (1.0 + scale) + shift


def _inproj_kernel(x_ref, sh_ref, sc_ref, g_ref, w_ref, o_ref, h_ref):
    h = _modnorm(x_ref[...], g_ref[...], sc_ref[0], sh_ref[0]).astype(BF16)
    h_ref[...] = h
    o_ref[...] = _dot(h, w_ref[...])


def _inproj_call(x, mod, g, w, l, seq_len, per_batch):
    t, d = x.shape
    n = w.shape[2]
    tm = 512
    mspec = functools.partial(_mod_spec, d=d, tm=tm, seq_len=seq_len, per_batch=per_batch)
    return pl.pallas_call(
        _inproj_kernel,
        out_shape=(jax.ShapeDtypeStruct((t, n), F32), jax.ShapeDtypeStruct((t, d), BF16)),
        grid=(t // tm,),
        in_specs=[
            pl.BlockSpec((tm, d), lambda i: (i, 0)),
            mspec(0), mspec(1),
            pl.BlockSpec((1, d), lambda i: (0, 0)),
            pl.BlockSpec((None, d, n), lambda i: (l, 0, 0), pipeline_mode=pl.Buffered(1)),
        ],
        out_specs=(pl.BlockSpec((tm, n), lambda i: (i, 0)), pl.BlockSpec((tm, d), lambda i: (i, 0))),
        compiler_params=_cparams(("parallel",), 52),
        name="in_proj",
    )(x, mod, mod, g, w)


def _rms(x, g, width):
    ms = jnp.sum(x * x, axis=-1, keepdims=True) * (1.0 / width)
    return x * lax.rsqrt(ms + EPS) * g


def _heads_norm(x, g, mult, shared=None, rope=None):
    outs = []
    for h in range(MLA_HEADS):
        sl = slice(h * HEAD_PAD, (h + 1) * HEAD_PAD)
        xh = x[:, sl] if shared is None else x[:, sl] + shared
        ms = jnp.sum(xh * xh, axis=-1, keepdims=True) * (1.0 / QK_HEAD)
        r = lax.rsqrt(ms + EPS)
        y = xh * r * g[:, sl]
        if rope is not None:
            xs, shared_s, gs, cos, sin = rope
            part = shared_s if xs is None else xs[:, sl]
            y = y * cos + (part * r * gs[:, sl]) * sin
        if mult != 1.0:
            y = y * mult
        outs.append(y)
    return jnp.concatenate(outs, axis=1)


def _qkprep_kernel(*refs, use_rope, q_mult):
    if use_rope:
        (qa_ref, kva_ref, kpe_ref, gqa_ref, gkva_ref, gq_ref, gk_ref, wuq_ref, wuk_ref, wuv_ref,
         kpes_ref, gqs_ref, gks_ref, wuqs_ref, c_ref, s_ref, q_out, k_out, v_out, ckv_out) = refs
    else:
        (qa_ref, kva_ref, kpe_ref, gqa_ref, gkva_ref, gq_ref, gk_ref, wuq_ref, wuk_ref, wuv_ref,
         q_out, k_out, v_out, ckv_out) = refs
    qa = qa_ref[...]
    qa_n = _rms(qa, gqa_ref[...], qa.shape[-1]).astype(BF16)
    q = _dot(qa_n, wuq_ref[...])
    q_rope = (_dot(qa_n, wuqs_ref[...]), None, gqs_ref[...], c_ref[...], s_ref[...]) if use_rope else None
    q_out[...] = _heads_norm(q, gq_ref[...], q_mult, rope=q_rope).astype(BF16)

    kva = kva_ref[...]
    ckv = _rms(kva, gkva_ref[...], kva.shape[-1])
    ckv_out[...] = ckv
    ckv_b = ckv.astype(BF16)
    k = _dot(ckv_b, wuk_ref[...])
    k_rope = (None, kpes_ref[...], gks_ref[...], c_ref[...], s_ref[...]) if use_rope else None
    k_out[...] = _heads_norm(k, gk_ref[...], 1.0, shared=kpe_ref[...], rope=k_rope).astype(BF16)
    v_out[...] = _dot(ckv_b, wuv_ref[...]).astype(BF16)


def _qkprep_call(p, lw, rope, seq_len, cols):
    t = p.shape[0]
    tm = 512
    hq = MLA_HEADS * HEAD_PAD
    hv = MLA_HEADS * V_HEAD
    qlora = lw["w_uq"].shape[0]
    kvlora = lw["w_uk"].shape[0]
    const = lambda i: (0, 0)
    in_specs = [
        pl.BlockSpec((tm, qlora), lambda i: (i, cols["q_a"] // qlora)),
        pl.BlockSpec((tm, kvlora), lambda i: (i, cols["kv_a"] // kvlora)),
        pl.BlockSpec((tm, HEAD_PAD), lambda i: (i, cols["k_rope"] // HEAD_PAD)),
        pl.BlockSpec((1, qlora), const),
        pl.BlockSpec((1, kvlora), const),
        pl.BlockSpec((1, hq), const),
        pl.BlockSpec((1, hq), const),
        pl.BlockSpec((qlora, hq), const),
        pl.BlockSpec((kvlora, hq), const),
        pl.BlockSpec((kvlora, hv), const),
    ]
    args = [p, p, p, lw["q_a_norm_g"], lw["kv_a_norm_g"], lw["q_norm_g"], lw["k_norm_g"],
            lw["w_uq"], lw["w_uk"], lw["w_uv"]]
    if rope is not None:
        nblk = seq_len // tm
        in_specs += [
            pl.BlockSpec((tm, HEAD_PAD), lambda i: (i, cols["k_rope_sw"] // HEAD_PAD)),
            pl.BlockSpec((1, hq), const), pl.BlockSpec((1, hq), const),
            pl.BlockSpec((qlora, hq), const),
            pl.BlockSpec((tm, HEAD_PAD), lambda i: (i % nblk, 0)),
            pl.BlockSpec((tm, HEAD_PAD), lambda i: (i % nblk, 0)),
        ]
        args += [p, lw["q_norm_g_sw"], lw["k_norm_g_sw"], lw["w_uq_sw"]] + list(rope)
    q_mult = LOG2E * QK_HEAD ** -0.5
    return pl.pallas_call(
        functools.partial(_qkprep_kernel, use_rope=rope is not None, q_mult=q_mult),
        out_shape=(jax.ShapeDtypeStruct((t, hq), BF16), jax.ShapeDtypeStruct((t, hq), BF16),
                   jax.ShapeDtypeStruct((t, hv), BF16), jax.ShapeDtypeStruct((t, kvlora), F32)),
        grid=(t // tm,),
        in_specs=in_specs,
        out_specs=(pl.BlockSpec((tm, hq), lambda i: (i, 0)), pl.BlockSpec((tm, hq), lambda i: (i, 0)),
                   pl.BlockSpec((tm, hv), lambda i: (i, 0)), pl.BlockSpec((tm, kvlora), lambda i: (i, 0))),
        compiler_params=_cparams(("parallel",), 40),
        name="qk_prep",
    )(*args)


def _ctxprep_kernel(ckv_ref, kpe_ref, gk_ref, wuk_ref, wuv_ref, k_out, v_out):
    ckv_b = ckv_ref[...].astype(BF16)
    k = _dot(ckv_b, wuk_ref[...])
    k_out[...] = _heads_norm(k, gk_ref[...], 1.0, shared=kpe_ref[...]).astype(BF16)
    v_out[...] = _dot(ckv_b, wuv_ref[...]).astype(BF16)


def _ctxprep_call(ckv, kpe, lw):
    t, kvlora = ckv.shape
    tm = 512
    hq = MLA_HEADS * HEAD_PAD
    hv = MLA_HEADS * V_HEAD
    const = lambda i: (0, 0)
    return pl.pallas_call(
        _ctxprep_kernel,
        out_shape=(jax.ShapeDtypeStruct((t, hq), BF16), jax.ShapeDtypeStruct((t, hv), BF16)),
        grid=(t // tm,),
        in_specs=[
            pl.BlockSpec((tm, kvlora), lambda i: (i, 0)),
            pl.BlockSpec((tm, HEAD_PAD), lambda i: (i, 0)),
            pl.BlockSpec((1, hq), const),
            pl.BlockSpec((kvlora, hq), const),
            pl.BlockSpec((kvlora, hv), const),
        ],
        out_specs=(pl.BlockSpec((tm, hq), lambda i: (i, 0)), pl.BlockSpec((tm, hv), lambda i: (i, 0))),
        compiler_params=_cparams(("parallel",), 40),
        name="ctx_prep",
    )(ckv, kpe, lw["k_norm_g"], lw["w_uk"], lw["w_uv"])


def _attn_kernel(*refs, has_ctx, heads):
    if has_ctx:
        q_ref, ko_ref, vo_ref, kc_ref, vc_ref, o_ref = refs
    else:
        q_ref, ko_ref, vo_ref, o_ref = refs
    contract_last = (((1,), (1,)), ((), ()))
    lane = lax.broadcasted_iota(jnp.int32, (q_ref.shape[0], 2 * V_HEAD), 1)
    for pair in range(heads // 2):
        vsl = slice(pair * 2 * V_HEAD, (pair + 1) * 2 * V_HEAD)
        outs = []
        for hh in range(2):
            h = 2 * pair + hh
            sl = slice(h * HEAD_PAD, (h + 1) * HEAD_PAD)
            q = q_ref[:, sl]
            s_o = lax.dot_general(q, ko_ref[:, sl], contract_last, preferred_element_type=F32)
            m = jnp.max(s_o, axis=-1, keepdims=True)
            if has_ctx:
                s_c = lax.dot_general(q, kc_ref[:, sl], contract_last, preferred_element_type=F32)
                m = jnp.maximum(m, jnp.max(s_c, axis=-1, keepdims=True))
            p_o = jnp.exp2(s_o - m)
            l = jnp.sum(p_o, axis=-1, keepdims=True)
            acc = _dot(p_o.astype(BF16), vo_ref[:, vsl])
            if has_ctx:
                p_c = jnp.exp2(s_c - m)
                l = l + jnp.sum(p_c, axis=-1, keepdims=True)
                acc = acc + _dot(p_c.astype(BF16), vc_ref[:, vsl])
            outs.append(acc / l)
        o_ref[:, vsl] = jnp.where(lane < V_HEAD, outs[0], outs[1]).astype(BF16)


def _attn_call(q, k, v, kc, vc, batch, seq_len, ctx_len):
    t = q.shape[0]
    tq = min(seq_len, 512)
    heads = 8
    nq = seq_len // tq
    npair = MLA_HEADS // heads
    qw = heads * HEAD_PAD
    vw = heads * V_HEAD
    in_specs = [
        pl.BlockSpec((tq, qw), lambda b, h, i: (b * nq + i, h)),
        pl.BlockSpec((seq_len, qw), lambda b, h, i: (b, h)),
        pl.BlockSpec((seq_len, vw), lambda b, h, i: (b, h)),
    ]
    args = [q, k, v]
    if kc is not None:
        in_specs += [pl.BlockSpec((ctx_len, qw), lambda b, h, i: (b, h)),
                     pl.BlockSpec((ctx_len, vw), lambda b, h, i: (b, h))]
        args += [kc, vc]
    return pl.pallas_call(
        functools.partial(_attn_kernel, has_ctx=kc is not None, heads=heads),
        out_shape=jax.ShapeDtypeStruct((t, MLA_HEADS * V_HEAD), BF16),
        grid=(batch, npair, nq),
        in_specs=in_specs,
        out_specs=pl.BlockSpec((tq, vw), lambda b, h, i: (b * nq + i, h)),
        compiler_params=_cparams(("parallel", "parallel", "arbitrary"), 56),
        name="attention",
    )(*args)


def _convpool_kernel(cu_ref, bg_ref, cg_ref, pu_ref, cw_ref, cb_ref, pw_ref, ps_ref, zb_ref, zd_ref):
    n = cu_ref.shape[0]
    row = lax.broadcasted_iota(jnp.int32, cu_ref.shape, 0)

    def down(x, k):
        return jnp.where(row >= k, pltpu.roll(x, k, axis=0), 0.0)

    def up(x, k):
        return jnp.where(row < n - k, pltpu.roll(x, n - k, axis=0), 0.0)

    v = cg_ref[...] * cu_ref[...]
    conv = (down(v, 1) * cw_ref[0:1, :] + v * cw_ref[1:2, :] + up(v, 1) * cw_ref[2:3, :]
            + cb_ref[...])
    zb_ref[...] = (bg_ref[...] * conv).astype(BF16)

    u = pu_ref[...]
    rowf = row.astype(F32)
    grp = pl.program_id(1)
    for gi, win in enumerate(POOL_WINDOWS):
        @pl.when(grp == gi)
        def _(win=win):
            half = win // 2
            fwd = u
            bwd = u
            k = 1
            while k < half:
                fwd = fwd + up(fwd, k)
                bwd = bwd + down(bwd, k)
                k *= 2
            total = fwd + down(bwd, 1)
            cnt = jnp.minimum(rowf + half, float(n)) - jnp.maximum(rowf - half, 0.0)
            mean = total / cnt - u
            zd_ref[...] = (_dot(mean.astype(BF16), pw_ref[0]) * ps_ref[...]).astype(BF16)


def _convpool_call(p, lw, batch, seq_len, cols):
    t = p.shape[0]
    cw = LANES
    nblk = lw["conv_w"].shape[1] // cw

    def pspec(name):
        base = cols[name] // cw
        return pl.BlockSpec((seq_len, cw), lambda b, g: (b, base + g))

    vec = lambda b, g: (0, g)
    return pl.pallas_call(
        _convpool_kernel,
        out_shape=(jax.ShapeDtypeStruct((t, nblk * cw), BF16), jax.ShapeDtypeStruct((t, nblk * cw), BF16)),
        grid=(batch, nblk),
        in_specs=[pspec("conv_u"), pspec("conv_bg"), pspec("conv_cg"), pspec("pool_u"),
                  pl.BlockSpec((3, cw), vec), pl.BlockSpec((1, cw), vec),
                  pl.BlockSpec((1, cw, cw), lambda b, g: (g, 0, 0)), pl.BlockSpec((1, cw), vec)],
        out_specs=(pl.BlockSpec((seq_len, cw), lambda b, g: (b, g)),
                   pl.BlockSpec((seq_len, cw), lambda b, g: (b, g))),
        compiler_params=_cparams(("parallel", "parallel"), 40),
        name="conv_pool",
    )(p, p, p, p, lw["conv_w"], lw["conv_b"], lw["pool_w"], lw["pool_scale"])


SSM_R = 4
SSM_GB = LANES // SSM_GROUP_CH


def _ssm_kernel(*refs, steps, nchunks, nblk):
    u_refs = refs[:nblk]
    h0_ref, a_ref, mb_ref, m1_ref, m2_ref = refs[nblk:nblk + 5]
    y_refs = refs[nblk + 5:2 * nblk + 5]
    hf_ref, slab_scr, xs_scr, st_scr = refs[2 * nblk + 5:]
    d = pl.program_id(0)
    c = pl.program_id(2)
    width = st_scr.shape[1]
    bw = width // nblk

    @pl.when(c == 0)
    def _():
        st_scr[...] = h0_ref[0]

    for blk in range(nblk):
        for j in range(SSM_R):
            for b in range(SUBLANES):
                slab_scr[blk * SSM_R + j, pl.ds(b, steps, stride=SUBLANES), :] = (
                    u_refs[blk][b, pl.ds(j, steps, stride=SSM_R), :])

    def packed(blk):
        return jnp.concatenate([slab_scr[blk * SSM_R + j] for j in range(SSM_R)], axis=1).astype(BF16)

    for blk in range(nblk):
        xs_scr[:, blk * bw:(blk + 1) * bw] = _dot(packed(blk), mb_ref[0, blk])

    gw = width // 2
    for part in range(width // gw):
        offs = []
        for blk in range(part * nblk // 2, (part + 1) * nblk // 2):
            offs += [(blk * bw + k * LANES, blk * bw + bw // 2 + k * LANES) for k in range(bw // 2 // LANES)]
        coef = [(a_ref[0, :, r0:r0 + LANES], a_ref[0, :, i0:i0 + LANES]) for r0, i0 in offs]

        def body(t, carry, offs=offs, coef=coef):
            tt = t + d * (steps - 1 - 2 * t)
            row0 = pl.multiple_of(tt * SUBLANES, SUBLANES)
            new = []
            for (r0, i0), (a_re, a_im), x_re, x_im in zip(offs, coef, carry[0::2], carry[1::2]):
                n_re = (a_re * x_re - a_im * x_im) + xs_scr[pl.ds(row0, SUBLANES), r0:r0 + LANES]
                n_im = (a_re * x_im + a_im * x_re) + xs_scr[pl.ds(row0, SUBLANES), i0:i0 + LANES]
                xs_scr[pl.ds(row0, SUBLANES), r0:r0 + LANES] = x_re
                xs_scr[pl.ds(row0, SUBLANES), i0:i0 + LANES] = x_im
                new += [n_re, n_im]
            return tuple(new)

        init = []
        for r0, i0 in offs:
            init += [st_scr[:, r0:r0 + LANES], st_scr[:, i0:i0 + LANES]]
        final = lax.fori_loop(0, steps, body, tuple(init), unroll=2)
        for (r0, i0), x_re, x_im in zip(offs, final[0::2], final[1::2]):
            st_scr[:, r0:r0 + LANES] = x_re
            st_scr[:, i0:i0 + LANES] = x_im

    for blk in range(nblk):
        y = (_dot(xs_scr[:, blk * bw:(blk + 1) * bw].astype(BF16), m1_ref[0, blk])
             + _dot(packed(blk), m2_ref[0, blk]))
        for j in range(SSM_R):
            slab_scr[blk * SSM_R + j] = y[:, j * LANES:(j + 1) * LANES]
        for j in range(SSM_R):
            for b in range(SUBLANES):
                y_refs[blk][0, b, pl.ds(j, steps, stride=SSM_R), :] = (
                    slab_scr[blk * SSM_R + j, pl.ds(b, steps, stride=SUBLANES), :])

    @pl.when(c == nchunks - 1)
    def _():
        hf_ref[0] = st_scr[...]


def _ssm_call(p3, h0, lw, col0):
    batch, seq_len, _ = p3.shape
    width = h0.shape[-1]
    nblk = lw["ssm_mb"].shape[2]
    l = lw["l"]
    steps = min(seq_len // SSM_R, 64)
    tokens = steps * SSM_R
    nchunks = seq_len // tokens
    nbg = batch // SUBLANES
    rows = steps * SUBLANES

    def cidx(d, c):
        return c + d * (nchunks - 1 - 2 * c)

    def uspec(blk):
        return pl.BlockSpec((SUBLANES, tokens, LANES), lambda d, b, c: (b, cidx(d, c), col0 // LANES + blk))

    def wspec(arr):
        return pl.BlockSpec((None, 1) + arr.shape[2:], lambda d, b, c: (l, d, 0, 0, 0),
                            pipeline_mode=pl.Buffered(1))

    yshape = jax.ShapeDtypeStruct((2, batch, seq_len, LANES), F32)
    yspec = pl.BlockSpec((1, SUBLANES, tokens, LANES), lambda d, b, c: (d, b, cidx(d, c), 0))
    outs = pl.pallas_call(
        functools.partial(_ssm_kernel, steps=steps, nchunks=nchunks, nblk=nblk),
        out_shape=(yshape,) * nblk + (jax.ShapeDtypeStruct((2, batch, width), F32),),
        grid=(2, nbg, nchunks),
        in_specs=[uspec(blk) for blk in range(nblk)] + [
            pl.BlockSpec((1, SUBLANES, width), lambda d, b, c: (d, b, 0)),
            pl.BlockSpec((None, 1, SUBLANES, width), lambda d, b, c: (l, d, 0, 0)),
            wspec(lw["ssm_mb"]), wspec(lw["ssm_m1"]), wspec(lw["ssm_m2"]),
        ],
        out_specs=(yspec,) * nblk + (pl.BlockSpec((1, SUBLANES, width), lambda d, b, c: (d, b, 0)),),
        scratch_shapes=[pltpu.VMEM((nblk * SSM_R, rows, LANES), F32), pltpu.VMEM((rows, width), F32),
                        pltpu.VMEM((SUBLANES, width), F32)],
        compiler_params=_cparams(("arbitrary", "arbitrary", "arbitrary"), 52),
        name="s5_scan",
    )(*([p3] * nblk), h0, lw["ssm_a"], lw["ssm_mb"], lw["ssm_m1"], lw["ssm_m2"])
    return outs[:nblk], outs[nblk]


def _merge_kernel(*refs, nblk):
    (h_ref, za_ref, zb_ref, su_ref, sd_ref, zd_ref, wg0, wg1, wg2, wg3, wa, wb, wca, wcg, wd) = refs[:15]
    yf_refs = refs[15:15 + nblk]
    yb_refs = refs[15 + nblk:15 + 2 * nblk]
    o_ref, y_scr = refs[15 + 2 * nblk:]

    @pl.when(pl.program_id(1) == 0)
    def _():
        y_both = jnp.concatenate([f[0] + b[0] for f, b in zip(yf_refs, yb_refs)], axis=1)
        y_scr[...] = (y_both + sd_ref[...] * su_ref[...]).astype(BF16)

    h = h_ref[...]
    y = y_scr[...]

    def gate(w_ref):
        return jax.nn.sigmoid(_dot(h, w_ref[...]))

    merged = gate(wg0) * _dot(za_ref[...], wa[...])
    merged = merged + gate(wg1) * _dot(zb_ref[...], wb[...])
    merged = merged + gate(wg2) * (_dot(y, wca[...]) * jax.nn.sigmoid(_dot(y, wcg[...])))
    merged = merged + gate(wg3) * _dot(zd_ref[...], wd[...])
    o_ref[...] = merged.astype(BF16)


def _merge_call(h, p, za, zb, ys, zd, lw, cols):
    t, d = h.shape
    bw = za.shape[1]
    tm, tn = 512, 512
    nj = d // tn
    nblk = len(ys)
    row = lambda i, j: (i, 0)
    su_blk = cols["ssm_u"] // bw

    l = lw["l"]

    def gspec(k):
        return pl.BlockSpec((None, d, tn), lambda i, j: (l, 0, k * nj + j))

    def yspec(direction):
        return pl.BlockSpec((1, tm, LANES), lambda i, j: (direction, i, 0))

    wcol = pl.BlockSpec((None, bw, tn), lambda i, j: (l, 0, j))
    in_specs = [
        pl.BlockSpec((tm, d), row),
        pl.BlockSpec((tm, bw), row), pl.BlockSpec((tm, bw), row),
        pl.BlockSpec((tm, bw), lambda i, j: (i, su_blk)),
        pl.BlockSpec((1, bw), lambda i, j: (0, 0)),
        pl.BlockSpec((tm, bw), row),
        gspec(0), gspec(1), gspec(2), gspec(3),
        wcol, wcol, wcol, pl.BlockSpec((None, bw, tn), lambda i, j: (l, 0, nj + j)), wcol,
    ] + [yspec(0)] * nblk + [yspec(1)] * nblk
    wg = lw["w_gate"]
    return pl.pallas_call(
        functools.partial(_merge_kernel, nblk=nblk),
        out_shape=jax.ShapeDtypeStruct((t, d), BF16),
        grid=(t // tm, nj),
        in_specs=in_specs,
        out_specs=pl.BlockSpec((tm, tn), lambda i, j: (i, j)),
        scratch_shapes=[pltpu.VMEM((tm, bw), BF16)],
        compiler_params=_cparams(("parallel", "arbitrary"), 48),
        name="branch_merge",
    )(h, za, zb, p, lw["ssm_d"], zd,
      wg, wg, wg, wg, lw["w_mla_o"], lw["w_conv_o"], lw["w_glu"], lw["w_glu"], lw["w_pool_o"], *ys, *ys)


def _outproj_kernel(x_ref, m_ref, gt_ref, sh_ref, sc_ref, g_ref, wo_ref, x1_ref, h2_ref):
    x1 = x_ref[...] + gt_ref[0] * _dot(m_ref[...], wo_ref[...])
    x1_ref[...] = x1
    h2_ref[...] = _modnorm(x1, g_ref[...], sc_ref[0], sh_ref[0]).astype(BF16)


def _outproj_call(x, merged, mod, lw, seq_len, per_batch):
    t, d = x.shape
    tm = 512
    mspec = functools.partial(_mod_spec, d=d, tm=tm, seq_len=seq_len, per_batch=per_batch)
    row = lambda i: (i, 0)
    return pl.pallas_call(
        _outproj_kernel,
        out_shape=(jax.ShapeDtypeStruct((t, d), F32), jax.ShapeDtypeStruct((t, d), BF16)),
        grid=(t // tm,),
        in_specs=[
            pl.BlockSpec((tm, d), row), pl.BlockSpec((tm, d), row),
            mspec(2), mspec(3), mspec(4),
            pl.BlockSpec((1, d), lambda i: (0, 0)),
            pl.BlockSpec((None, d, d), lambda i: (lw["l"], 0, 0), pipeline_mode=pl.Buffered(1)),
        ],
        out_specs=(pl.BlockSpec((tm, d), row), pl.BlockSpec((tm, d), row)),
        compiler_params=_cparams(("parallel",), 48),
        name="out_proj",
    )(x, merged, mod, mod, mod, lw["norm2_g"], lw["w_o"])


def _mlp_kernel(x_ref, h_ref, gt_ref, w1_ref, w2_ref, o_ref):
    @pl.when(pl.program_id(1) == 0)
    def _():
        o_ref[...] = x_ref[...]

    hid = jnp.square(jnp.maximum(_dot(h_ref[...], w1_ref[...]), 0.0))
    o_ref[...] += gt_ref[0] * _dot(hid.astype(BF16), w2_ref[...])


def _mlp_call(x, h, mod, lw, seq_len, per_batch):
    t, d = x.shape
    hidden = lw["w_mlp1"].shape[2]
    tm, tk = 512, 1024
    l = lw["l"]
    mspec = functools.partial(_mod_spec, d=d, tm=tm, seq_len=seq_len, per_batch=per_batch)
    row = lambda i, k: (i, 0)
    return pl.pallas_call(
        _mlp_kernel,
        out_shape=jax.ShapeDtypeStruct((t, d), F32),
        grid=(t // tm, hidden // tk),
        in_specs=[
            pl.BlockSpec((tm, d), row), pl.BlockSpec((tm, d), row), mspec(5),
            pl.BlockSpec((None, d, tk), lambda i, k: (l, 0, k)),
            pl.BlockSpec((None, tk, d), lambda i, k: (l, k, 0)),
        ],
        out_specs=pl.BlockSpec((tm, d), row),
        compiler_params=_cparams(("parallel", "arbitrary"), 48),
        name="mlp",
    )(x, h, mod, lw["w_mlp1"], lw["w_mlp2"])


def _pad_heads(w, per_head, lo):
    lead = w.shape[:-1]
    w = w.reshape(lead + (MLA_HEADS, per_head))
    w = jnp.pad(w, [(0, 0)] * len(lead) + [(0, 0), (lo, HEAD_PAD - lo - per_head)])
    return w.reshape(lead + (MLA_HEADS * HEAD_PAD,))


_ROPE_PARTNER = tuple((i // 16) * 16 + (i % 16 + 8) % 16 for i in range(ROPE_DIM))


def _swap_heads(w):
    lead = w.shape[:-1]
    w = w.reshape(lead + (MLA_HEADS, QK_HEAD))[..., QK_NOPE:][..., jnp.array(_ROPE_PARTNER)]
    w = jnp.pad(w, [(0, 0)] * len(lead) + [(0, 0), (QK_NOPE, HEAD_PAD - QK_HEAD)])
    return w.reshape(lead + (MLA_HEADS * HEAD_PAD,))


def _prep_ssm(lam_re, lam_im, log_step, b_re, b_im, c_re, c_im):
    r, gb = SSM_R, SSM_GB
    nl, ndir, g, n, cg = b_re.shape
    ns = nl * ndir
    nblk = g // gb
    flat = lambda v: v.reshape((ns,) + v.shape[2:])
    lam_re, lam_im, log_step, b_re, b_im, c_re, c_im = map(
        flat, (lam_re, lam_im, log_step, b_re, b_im, c_re, c_im))
    step = jnp.exp(log_step)[..., None]
    pw = []
    for p in range(r + 1):
        mag = jnp.exp(p * (lam_re * step))
        pw.append((mag * jnp.cos(p * (lam_im * step)), mag * jnp.sin(p * (lam_im * step))))
    ar, ai = pw[1]
    den = lam_re * lam_re + lam_im * lam_im
    qr = ((ar - 1.0) * lam_re + ai * lam_im) / den
    qi = (ai * lam_re - (ar - 1.0) * lam_im) / den
    bb_re = qr[..., None] * b_re - qi[..., None] * b_im
    bb_im = qr[..., None] * b_im + qi[..., None] * b_re
    fwd = (jnp.arange(ns) % ndir == 0)[:, None, None]

    def power(p_fwd, p_bwd):
        return (jnp.where(fwd, pw[p_fwd][0], pw[p_bwd][0]), jnp.where(fwd, pw[p_fwd][1], pw[p_bwd][1]))

    rows_g = (jnp.arange(r * gb * cg) // cg) % gb
    cols_g = jnp.arange(gb * n) // n

    t_re, t_im = [], []
    for j in range(r):
        pr, pi = power(r - 1 - j, j)
        t_re.append(pr[..., None] * bb_re - pi[..., None] * bb_im)
        t_im.append(pr[..., None] * bb_im + pi[..., None] * bb_re)
    mask_b = (rows_g[:, None] == cols_g[None, :]).astype(F32)

    def inc_block(ts):
        t = jnp.stack(ts, axis=1).reshape(ns, r, nblk, gb, n, cg)
        t = jnp.transpose(t, (0, 2, 1, 3, 5, 4)).reshape(ns, nblk, r * gb * cg, n)
        return jnp.tile(t, (1, 1, 1, gb)) * mask_b

    mb = jnp.concatenate([inc_block(t_re), inc_block(t_im)], axis=-1)

    cl_re, cl_im = [], []
    for j in range(r):
        pr, pi = power(j + 1, r - j)
        cl_re.append(c_re * pr[:, :, None, :] - c_im * pi[:, :, None, :])
        cl_im.append(c_re * pi[:, :, None, :] + c_im * pr[:, :, None, :])

    def out_block(ts):
        t = jnp.stack(ts, axis=1).reshape(ns, r, nblk, gb, cg, n)
        t = jnp.transpose(t, (0, 2, 5, 1, 3, 4)).reshape(ns, nblk, n, r * gb * cg)
        return jnp.tile(t, (1, 1, gb, 1)) * mask_b.T

    m1 = jnp.concatenate([out_block(cl_re), -out_block(cl_im)], axis=2)

    hi = lax.Precision.HIGHEST
    lane_g = jnp.arange(gb * cg) // cg
    mask_k = (lane_g[:, None] == lane_g[None, :]).astype(F32)
    kd = []
    for p in range(r):
        pr, pi = pw[p]
        k2 = (jnp.einsum("sgcn,sgnk->sgck", c_re * pr[:, :, None, :] - c_im * pi[:, :, None, :], bb_re, precision=hi)
              - jnp.einsum("sgcn,sgnk->sgck", c_re * pi[:, :, None, :] + c_im * pr[:, :, None, :], bb_im, precision=hi))
        k2 = jnp.swapaxes(k2, -1, -2).reshape(ns, nblk, gb * cg, cg)
        kd.append(jnp.tile(k2, (1, 1, 1, gb)) * mask_k)
    fwd4 = fwd[..., None]
    zero = jnp.zeros_like(kd[0])
    m2_rows = []
    for i in range(r):
        blocks = []
        for j in range(r):
            if i == j:
                blocks.append(kd[0])
            elif j > i:
                blocks.append(jnp.where(fwd4, kd[j - i], zero))
            else:
                blocks.append(jnp.where(fwd4, zero, kd[i - j]))
        m2_rows.append(jnp.concatenate(blocks, axis=-1))
    m2 = jnp.concatenate(m2_rows, axis=2)

    a = jnp.stack([pw[r][0].reshape(ns, nblk, gb * n), pw[r][1].reshape(ns, nblk, gb * n)], axis=2)
    a = jnp.broadcast_to(a.reshape(ns, 1, 2 * g * n), (ns, SUBLANES, 2 * g * n))
    unflat = lambda v: v.reshape((nl, ndir) + v.shape[1:])
    return unflat(a), unflat(mb.astype(BF16)), unflat(m1.astype(BF16)), unflat(m2.astype(BF16))


def _lane_window_kernel(a_ref, b_ref, o_ref, *, shift):
    x = jnp.concatenate([a_ref[...], b_ref[...]], axis=1)
    o_ref[...] = pltpu.roll(x, x.shape[1] - shift, axis=1)[:, :o_ref.shape[1]].astype(BF16)


def _window_bf16_call(w, col0, n_out):
    nl, k, n = w.shape
    shift = col0 % LANES
    base = col0 - shift
    cb = 2 * LANES
    assert shift > 0 and base % cb == 0 and n_out % cb == 0 and col0 + n_out <= n
    return pl.pallas_call(
        functools.partial(_lane_window_kernel, shift=shift),
        out_shape=jax.ShapeDtypeStruct((nl, k, n_out), BF16),
        grid=(nl, n_out // cb),
        in_specs=[pl.BlockSpec((None, k, cb), lambda l, j: (l, 0, base // cb + j)),
                  pl.BlockSpec((None, k, LANES), lambda l, j: (l, 0, (base + (j + 1) * cb) // LANES))],
        out_specs=pl.BlockSpec((None, k, cb), lambda l, j: (l, 0, j)),
        compiler_params=_cparams(("parallel", "parallel"), 40),
        name="gate_weights",
    )(w, w)


def _prep_stacked(prm, sizes):
    q_lora, kv_lora, conv_w, ssm_w, pool_w_ = sizes
    w_in = prm["w_in"]
    offs = {}
    o = 0
    for name, width in (("q_a", q_lora), ("kv_a", kv_lora), ("k_rope", ROPE_DIM), ("conv_u", conv_w),
                        ("conv_bg", conv_w), ("conv_cg", conv_w), ("ssm_u", ssm_w), ("pool_u", pool_w_)):
        offs[name] = (o, o + width)
        o += width
    gate0 = o
    take = lambda name: w_in[:, :, offs[name][0]:offs[name][1]]
    rope_lanes = ((0, 0), (0, 0), (QK_NOPE, HEAD_PAD - QK_HEAD))
    kr_pad = jnp.pad(take("k_rope"), rope_lanes)
    kr_sw = jnp.pad(take("k_rope")[:, :, jnp.array(_ROPE_PARTNER)], rope_lanes)
    pieces = [("q_a", take("q_a")), ("conv_u", take("conv_u")), ("conv_bg", take("conv_bg")),
              ("conv_cg", take("conv_cg")), ("ssm_u", take("ssm_u")), ("pool_u", take("pool_u")),
              ("kv_a", take("kv_a")), ("k_rope", kr_pad), ("k_rope_sw", kr_sw)]
    cols = {}
    o = 0
    for name, w in pieces:
        cols[name] = o
        o += w.shape[2]
    big = dict(
        w_small=jnp.concatenate([w for _, w in pieces], axis=2).astype(BF16),
        w_gate=_window_bf16_call(w_in, gate0, w_in.shape[2] - gate0),
    )
    for name in ("w_mla_o", "w_conv_o", "w_glu", "w_pool_o", "w_o", "w_mlp1", "w_mlp2"):
        big[name] = prm[name].astype(BF16)
    return big, cols


def _prep_layer(l, prm, sizes, big, ssm):
    kv_lora = sizes[1]
    w_ukv = prm["w_ukv"][l].reshape(kv_lora, MLA_HEADS, QK_NOPE + V_HEAD)
    w_uk = _pad_heads(w_ukv[:, :, :QK_NOPE].reshape(kv_lora, -1), QK_NOPE, 0)
    w_uv = w_ukv[:, :, QK_NOPE:].reshape(kv_lora, -1)
    ssm_a, ssm_mb, ssm_m1, ssm_m2 = ssm

    row = lambda v: v.reshape(1, -1)
    lw = dict(
        big, l=l,
        norm1_g=row(prm["norm1_g"][l]), norm2_g=row(prm["norm2_g"][l]),
        q_a_norm_g=row(prm["q_a_norm_g"][l]), kv_a_norm_g=row(prm["kv_a_norm_g"][l]),
        q_norm_g=row(_pad_heads(jnp.tile(prm["q_norm_g"][l], MLA_HEADS), QK_HEAD, 0)),
        k_norm_g=row(_pad_heads(jnp.tile(prm["k_norm_g"][l], MLA_HEADS), QK_HEAD, 0)),
        q_norm_g_sw=row(_swap_heads(jnp.tile(prm["q_norm_g"][l], MLA_HEADS))),
        k_norm_g_sw=row(_swap_heads(jnp.tile(prm["k_norm_g"][l], MLA_HEADS))),
        w_uq=_pad_heads(prm["w_uq"][l], QK_HEAD, 0).astype(BF16),
        w_uq_sw=_swap_heads(prm["w_uq"][l]).astype(BF16),
        w_uk=w_uk.astype(BF16), w_uv=w_uv.astype(BF16),
        conv_w=prm["conv_w"][l], conv_b=row(prm["conv_b"][l]),
        ssm_a=ssm_a, ssm_mb=ssm_mb, ssm_m1=ssm_m1, ssm_m2=ssm_m2,
        ssm_d=row(prm["ssm_d"][l]),
        pool_w=prm["pool_w"][l].astype(BF16), pool_scale=row(prm["pool_scale"][l]),
    )
    return lw


def _rope_tables(seq_len):
    half = ROPE_DIM // 2
    inv_freq = ROPE_THETA ** (-jnp.arange(0, half, 2, dtype=F32) / half)
    t = jnp.arange(seq_len)
    ang_r = (t // GRID_W).astype(F32)[:, None] * inv_freq
    ang_c = (t % GRID_W).astype(F32)[:, None] * inv_freq
    ones_lo = jnp.ones((seq_len, QK_NOPE), F32)
    zeros_lo = jnp.zeros((seq_len, QK_NOPE), F32)
    tail = jnp.zeros((seq_len, HEAD_PAD - QK_HEAD), F32)
    c = jnp.concatenate([ones_lo, jnp.cos(ang_r), jnp.cos(ang_r), jnp.cos(ang_c), jnp.cos(ang_c), tail], axis=1)
    s = jnp.concatenate([zeros_lo, -jnp.sin(ang_r), jnp.sin(ang_r), -jnp.sin(ang_c), jnp.sin(ang_c), tail], axis=1)
    return c, s


def _states_to_cols(st):
    b, ndir, _, g, n = st.shape
    st = st.reshape(b, ndir, 2, g // SSM_GB, SSM_GB * n)
    return jnp.transpose(st, (1, 0, 3, 2, 4)).reshape(ndir, b, 2 * g * n)


def _cols_to_states(hf, g, n):
    ndir, b, _ = hf.shape
    hf = hf.reshape(ndir, b, g // SSM_GB, 2, SSM_GB * n)
    return jnp.transpose(hf, (1, 0, 3, 2, 4)).reshape(b, ndir, 2, g, n)


def _mixer_layer(x, mod, lw, cols, batch, seq_len, per_batch, rope, ctx):
    p, h1 = _inproj_call(x, mod, lw["norm1_g"], lw["w_small"], lw["l"], seq_len, per_batch)
    q, k, v, ckv = _qkprep_call(p, lw, rope, seq_len, cols)
    if ctx is not None:
        ctx_ckv, ctx_kpe, h0 = ctx
        kc, vc = _ctxprep_call(ctx_ckv, ctx_kpe, lw)
        ctx_len = ctx_ckv.shape[0] // batch
    else:
        kc = vc = None
        ctx_len = 0
        h0 = jnp.zeros((2, batch, lw["ssm_a"].shape[-1]), F32)
    za = _attn_call(q, k, v, kc, vc, batch, seq_len, ctx_len)
    zb, zd = _convpool_call(p, lw, batch, seq_len, cols)

    ys, hf = _ssm_call(p.reshape(batch, seq_len, -1), h0, lw, cols["ssm_u"])
    ys = [y.reshape(2, batch * seq_len, LANES) for y in ys]
    merged = _merge_call(h1, p, za, zb, ys, zd, lw, cols)
    x1, h2 = _outproj_call(x, merged, mod, lw, seq_len, per_batch)
    x2 = _mlp_call(x1, h2, mod, lw, seq_len, per_batch)
    kr = cols["k_rope"] + QK_NOPE
    return x2, ckv, p[:, kr:kr + ROPE_DIM], hf


def kernel(x_prompt, x_sample, c, cache_ckv, cache_krope, state_ssm, c_ctx, w_ada, b_ada, norm1_g, norm2_g, w_in, q_a_norm_g, kv_a_norm_g, w_uq, w_ukv, q_norm_g, k_norm_g, w_mla_o, conv_w, conv_b, w_conv_o, ssm_lam_re, ssm_lam_im, ssm_log_step, ssm_b_re, ssm_b_im, ssm_c_re, ssm_c_im, ssm_d, w_glu, pool_w, pool_scale, w_pool_o, w_o, w_mlp1, w_mlp2):
    prm = dict(norm1_g=norm1_g, norm2_g=norm2_g, w_in=w_in, q_a_norm_g=q_a_norm_g, kv_a_norm_g=kv_a_norm_g,
               w_uq=w_uq, w_ukv=w_ukv, q_norm_g=q_norm_g, k_norm_g=k_norm_g, w_mla_o=w_mla_o,
               conv_w=conv_w, conv_b=conv_b, w_conv_o=w_conv_o, ssm_lam_re=ssm_lam_re, ssm_lam_im=ssm_lam_im,
               ssm_log_step=ssm_log_step, ssm_b_re=ssm_b_re, ssm_b_im=ssm_b_im, ssm_c_re=ssm_c_re,
               ssm_c_im=ssm_c_im, ssm_d=ssm_d, w_glu=w_glu, pool_w=pool_w, pool_scale=pool_scale,
               w_pool_o=w_pool_o, w_o=w_o, w_mlp1=w_mlp1, w_mlp2=w_mlp2)
    depth = w_in.shape[0]
    bp, lp, d = x_prompt.shape
    bs, ls, _ = x_sample.shape
    past = cache_ckv.shape[2]
    g, n = state_ssm.shape[-2:]
    sizes = (w_uq.shape[1], w_ukv.shape[1], conv_w.shape[-1], ssm_d.shape[-1], pool_scale.shape[-1])

    rows = -(-(1 + bs) // SUBLANES) * SUBLANES
    cvec = jnp.zeros((rows, d), F32).at[0].set(c_ctx).at[1:1 + bs].set(c)
    mods = _ada_call(cvec, w_ada, b_ada)
    rope = _rope_tables(ls)
    ssm = _prep_ssm(ssm_lam_re, ssm_lam_im, ssm_log_step, ssm_b_re, ssm_b_im, ssm_c_re, ssm_c_im)
    big, cols = _prep_stacked(prm, sizes)

    yp = x_prompt.reshape(bp * lp, d)
    ys = x_sample.reshape(bs * ls, d)
    ckv_list, krope_list, ssm_list = [], [], []
    for l in range(depth):
        lw = _prep_layer(l, prm, sizes, big, ssm)
        mod_ctx = mods[l, 0:1].reshape(6, 1, d)
        mod_lat = mods[l, 1:1 + bs].reshape(bs * 6, 1, d)
        yp, ckv_l, krope_l, hf_l = _mixer_layer(yp, mod_ctx, lw, cols, bp, lp, False, None, None)
        ckv_list.append(ckv_l.reshape(bp, lp, -1))
        krope_list.append(krope_l.reshape(bp, lp, -1))
        ssm_list.append(_cols_to_states(hf_l, g, n))
        ctx_kpe = jnp.pad(cache_krope[:, l].reshape(bs * past, ROPE_DIM),
                          ((0, 0), (QK_NOPE, HEAD_PAD - QK_HEAD)))
        ctx = (cache_ckv[:, l].reshape(bs * past, -1), ctx_kpe, _states_to_cols(state_ssm[:, l]))
        ys, _, _, _ = _mixer_layer(ys, mod_lat, lw, cols, bs, ls, True, rope, ctx)
    return (yp.reshape(bp, lp, d), ys.reshape(bs, ls, d), jnp.stack(ckv_list, axis=1),
            jnp.stack(krope_list, axis=1), jnp.stack(ssm_list, axis=1))
```

```python
import functools
import math

import jax
import jax.numpy as jnp
from jax import lax
from jax.experimental import pallas as pl
from jax.experimental.pallas import tpu as pltpu

F32 = jnp.float32
BF16 = jnp.bfloat16

GRID_W = 64
N_BRANCH = 4
MLA_HEADS = 8
QK_NOPE = 64
ROPE_DIM = 32
QK_HEAD = QK_NOPE + ROPE_DIM
V_HEAD = 64
ROPE_THETA = 10000.0
SSM_GROUP_CH = 16
POOL_WINDOWS = (2, 4, 8, 16)
EPS = 1e-6
LOG2E = 1.4426950408889634

LANES = 128
SUBLANES = 8
HEAD_PAD = LANES
MIB = 1024 * 1024


def _cparams(sem, vmem_mib):
    return pltpu.CompilerParams(dimension_semantics=sem, vmem_limit_bytes=vmem_mib * MIB)


def _dot(a, b):
    return jnp.dot(a, b, preferred_element_type=F32)


def _mod_spec(chunk, d, tm, seq_len, per_batch):
    def imap(i, *_):
        row = (i * tm) // seq_len if per_batch else 0
        return (row * 6 + chunk, 0, 0)
    return pl.BlockSpec((1, 1, d), imap)


def _ada_kernel(c_ref, w_ref, b_ref, o_ref):
    cv = c_ref[...]
    s = cv * jax.nn.sigmoid(cv)
    o_ref[0] = _dot(s.astype(BF16), w_ref[0].astype(BF16)) + b_ref[0]


def _ada_call(cvec, w_ada, b_ada):
    depth, d, n = w_ada.shape
    rows = cvec.shape[0]
    tn = 1024
    return pl.pallas_call(
        _ada_kernel,
        out_shape=jax.ShapeDtypeStruct((depth, rows, n), F32),
        grid=(depth, n // tn),
        in_specs=[
            pl.BlockSpec((rows, d), lambda l, j: (0, 0)),
            pl.BlockSpec((1, d, tn), lambda l, j: (l, 0, j)),
            pl.BlockSpec((1, 1, tn), lambda l, j: (l, 0, j)),
        ],
        out_specs=pl.BlockSpec((1, rows, tn), lambda l, j: (l, 0, j)),
        compiler_params=_cparams(("parallel", "parallel"), 40),
        name="ada_mod",
    )(cvec, w_ada, b_ada.reshape(depth, 1, n))


def _modnorm(x, g, scale, shift):
    ms = jnp.mean(x * x, axis=-1, keepdims=True)
    return (x * lax.rsqrt(ms + EPS) * g) * (1.0 + scale) + shift


def _rms(x, g, width):
    ms = jnp.sum(x * x, axis=-1, keepdims=True) * (1.0 / width)
    return x * lax.rsqrt(ms + EPS) * g


def _heads_norm(x, g, mult, shared=None, rope=None):
    outs = []
    for h in range(MLA_HEADS):
        sl = slice(h * HEAD_PAD, (h + 1) * HEAD_PAD)
        xh = x[:, sl] if shared is None else x[:, sl] + shared
        ms = jnp.sum(xh * xh, axis=-1, keepdims=True) * (1.0 / QK_HEAD)
        r = lax.rsqrt(ms + EPS)
        y = xh * r * g[:, sl]
        if rope is not None:
            xs, shared_s, gs, cos, sin = rope
            part = shared_s if xs is None else xs[:, sl]
            y = y * cos + (part * r * gs[:, sl]) * sin
        if mult != 1.0:
            y = y * mult
        outs.append(y)
    return jnp.concatenate(outs, axis=1)


def _inproj_kernel(*refs, use_rope, q_mult, cols, n_keep):
    (x_ref, sh_ref, sc_ref, g_ref, w_ref, gqa_ref, gkva_ref, gq_ref, gk_ref,
     wuq_ref, wuk_ref, wuv_ref) = refs[:12]
    if use_rope:
        gqs_ref, gks_ref, wuqs_ref, c_ref, s_ref = refs[12:17]
    p_out, h_out, q_out, k_out, v_out, ckv_out = refs[-6:]

    h = _modnorm(x_ref[...], g_ref[...], sc_ref[0], sh_ref[0]).astype(BF16)
    h_out[...] = h
    att0 = cols["k_rope"]
    res = _dot(h, w_ref[:, att0:])
    p_out[:, :att0] = _dot(h, w_ref[:, :att0])
    p_out[:, att0:] = res[:, :n_keep - att0]

    def col(name, width):
        return res[:, cols[name] - att0:cols[name] - att0 + width]

    qa = col("q_a", wuq_ref.shape[0])
    qa_n = _rms(qa, gqa_ref[...], qa.shape[-1]).astype(BF16)
    q = _dot(qa_n, wuq_ref[...])
    q_rope = (_dot(qa_n, wuqs_ref[...]), None, gqs_ref[...], c_ref[...], s_ref[...]) if use_rope else None
    q_out[...] = _heads_norm(q, gq_ref[...], q_mult, rope=q_rope).astype(BF16)

    kva = col("kv_a", wuk_ref.shape[0])
    ckv = _rms(kva, gkva_ref[...], kva.shape[-1])
    ckv_out[...] = ckv
    ckv_b = ckv.astype(BF16)
    k = _dot(ckv_b, wuk_ref[...])
    k_rope = (None, col("k_rope_sw", HEAD_PAD), gks_ref[...], c_ref[...], s_ref[...]) if use_rope else None
    k_out[...] = _heads_norm(k, gk_ref[...], 1.0, shared=col("k_rope", HEAD_PAD), rope=k_rope).astype(BF16)
    v_out[...] = _dot(ckv_b, wuv_ref[...]).astype(BF16)


def _inproj_call(x, mod, lw, rope, seq_len, per_batch, cols):
    t, d = x.shape
    w = lw["w_small"]
    n = w.shape[2]
    n_keep = cols["k_rope"] + HEAD_PAD
    tm = 512
    hq = MLA_HEADS * HEAD_PAD
    hv = MLA_HEADS * V_HEAD
    qlora = lw["w_uq"].shape[0]
    kvlora = lw["w_uk"].shape[0]
    l = lw["l"]
    mspec = functools.partial(_mod_spec, d=d, tm=tm, seq_len=seq_len, per_batch=per_batch)
    const = lambda i: (0, 0)
    row = lambda i: (i, 0)
    in_specs = [
        pl.BlockSpec((tm, d), row), mspec(0), mspec(1),
        pl.BlockSpec((1, d), const),
        pl.BlockSpec((None, d, n), lambda i: (l, 0, 0), pipeline_mode=pl.Buffered(1)),
        pl.BlockSpec((1, qlora), const), pl.BlockSpec((1, kvlora), const),
        pl.BlockSpec((1, hq), const), pl.BlockSpec((1, hq), const),
        pl.BlockSpec((qlora, hq), const), pl.BlockSpec((kvlora, hq), const), pl.BlockSpec((kvlora, hv), const),
    ]
    args = [x, mod, mod, lw["norm1_g"], w, lw["q_a_norm_g"], lw["kv_a_norm_g"], lw["q_norm_g"],
            lw["k_norm_g"], lw["w_uq"], lw["w_uk"], lw["w_uv"]]
    if rope is not None:
        nblk = seq_len // tm
        in_specs += [
            pl.BlockSpec((1, hq), const), pl.BlockSpec((1, hq), const),
            pl.BlockSpec((qlora, hq), const),
            pl.BlockSpec((tm, HEAD_PAD), lambda i: (i % nblk, 0)),
            pl.BlockSpec((tm, HEAD_PAD), lambda i: (i % nblk, 0)),
        ]
        args += [lw["q_norm_g_sw"], lw["k_norm_g_sw"], lw["w_uq_sw"]] + list(rope)
    q_mult = LOG2E * QK_HEAD ** -0.5
    return pl.pallas_call(
        functools.partial(_inproj_kernel, use_rope=rope is not None, q_mult=q_mult, cols=cols, n_keep=n_keep),
        out_shape=(jax.ShapeDtypeStruct((t, n_keep), F32), jax.ShapeDtypeStruct((t, d), BF16),
                   jax.ShapeDtypeStruct((t, hq), BF16), jax.ShapeDtypeStruct((t, hq), BF16),
                   jax.ShapeDtypeStruct((t, hv), BF16), jax.ShapeDtypeStruct((t, kvlora), F32)),
        grid=(t // tm,),
        in_specs=in_specs,
        out_specs=(pl.BlockSpec((tm, n_keep), row), pl.BlockSpec((tm, d), row),
                   pl.BlockSpec((tm, hq), row), pl.BlockSpec((tm, hq), row),
                   pl.BlockSpec((tm, hv), row), pl.BlockSpec((tm, kvlora), row)),
        compiler_params=_cparams(("parallel",), 56),
        name="in_proj",
    )(*args)


def _ctxprep_kernel(ckv_ref, kpe_ref, gk_ref, wuk_ref, wuv_ref, k_out, v_out):
    ckv_b = ckv_ref[...].astype(BF16)
    k = _dot(ckv_b, wuk_ref[...])
    k_out[...] = _heads_norm(k, gk_ref[...], 1.0, shared=kpe_ref[...]).astype(BF16)
    v_out[...] = _dot(ckv_b, wuv_ref[...]).astype(BF16)


def _ctxprep_call(ckv, kpe, lw):
    t, kvlora = ckv.shape
    tm = 512
    hq = MLA_HEADS * HEAD_PAD
    hv = MLA_HEADS * V_HEAD
    const = lambda i: (0, 0)
    return pl.pallas_call(
        _ctxprep_kernel,
        out_shape=(jax.ShapeDtypeStruct((t, hq), BF16), jax.ShapeDtypeStruct((t, hv), BF16)),
        grid=(t // tm,),
        in_specs=[
            pl.BlockSpec((tm, kvlora), lambda i: (i, 0)),
            pl.BlockSpec((tm, HEAD_PAD), lambda i: (i, 0)),
            pl.BlockSpec((1, hq), const),
            pl.BlockSpec((kvlora, hq), const),
            pl.BlockSpec((kvlora, hv), const),
        ],
        out_specs=(pl.BlockSpec((tm, hq), lambda i: (i, 0)), pl.BlockSpec((tm, hv), lambda i: (i, 0))),
        compiler_params=_cparams(("parallel",), 40),
        name="ctx_prep",
    )(ckv, kpe, lw["k_norm_g"], lw["w_uk"], lw["w_uv"])


def _attn_kernel(*refs, has_ctx, heads):
    if has_ctx:
        q_ref, ko_ref, vo_ref, kc_ref, vc_ref, o_ref = refs
    else:
        q_ref, ko_ref, vo_ref, o_ref = refs
    contract_last = (((1,), (1,)), ((), ()))
    lane = lax.broadcasted_iota(jnp.int32, (q_ref.shape[0], 2 * V_HEAD), 1)
    for pair in range(heads // 2):
        vsl = slice(pair * 2 * V_HEAD, (pair + 1) * 2 * V_HEAD)
        outs = []
        for hh in range(2):
            h = 2 * pair + hh
            sl = slice(h * HEAD_PAD, (h + 1) * HEAD_PAD)
            q = q_ref[:, sl]
            s_o = lax.dot_general(q, ko_ref[:, sl], contract_last, preferred_element_type=F32)
            m = jnp.max(s_o, axis=-1, keepdims=True)
            if has_ctx:
                s_c = lax.dot_general(q, kc_ref[:, sl], contract_last, preferred_element_type=F32)
                m = jnp.maximum(m, jnp.max(s_c, axis=-1, keepdims=True))
            p_o = jnp.exp2(s_o - m)
            l = jnp.sum(p_o, axis=-1, keepdims=True)
            acc = _dot(p_o.astype(BF16), vo_ref[:, vsl])
            if has_ctx:
                p_c = jnp.exp2(s_c - m)
                l = l + jnp.sum(p_c, axis=-1, keepdims=True)
                acc = acc + _dot(p_c.astype(BF16), vc_ref[:, vsl])
            outs.append(acc / l)
        o_ref[:, vsl] = jnp.where(lane < V_HEAD, outs[0], outs[1]).astype(BF16)


def _attn_call(q, k, v, kc, vc, batch, seq_len, ctx_len):
    t = q.shape[0]
    tq = min(seq_len, 512)
    heads = 8
    nq = seq_len // tq
    npair = MLA_HEADS // heads
    qw = heads * HEAD_PAD
    vw = heads * V_HEAD
    in_specs = [
        pl.BlockSpec((tq, qw), lambda b, h, i: (b * nq + i, h)),
        pl.BlockSpec((seq_len, qw), lambda b, h, i: (b, h)),
        pl.BlockSpec((seq_len, vw), lambda b, h, i: (b, h)),
    ]
    args = [q, k, v]
    if kc is not None:
        in_specs += [pl.BlockSpec((ctx_len, qw), lambda b, h, i: (b, h)),
                     pl.BlockSpec((ctx_len, vw), lambda b, h, i: (b, h))]
        args += [kc, vc]
    return pl.pallas_call(
        functools.partial(_attn_kernel, has_ctx=kc is not None, heads=heads),
        out_shape=jax.ShapeDtypeStruct((t, MLA_HEADS * V_HEAD), BF16),
        grid=(batch, npair, nq),
        in_specs=in_specs,
        out_specs=pl.BlockSpec((tq, vw), lambda b, h, i: (b * nq + i, h)),
        compiler_params=_cparams(("parallel", "parallel", "arbitrary"), 56),
        name="attention",
    )(*args)


def _convpool_kernel(cu_ref, bg_ref, cg_ref, pu_ref, cw_ref, cb_ref, pw_ref, ps_ref, zb_ref, zd_ref):
    n = cu_ref.shape[0]
    gps = pw_ref.shape[0]
    edge = max(POOL_WINDOWS) // 2
    assert edge % SUBLANES == 0 and n >= 4 * edge

    def shifts(rows, zero_fill):
        def down(x, k):
            y = pltpu.roll(x, k, axis=0)
            if zero_fill:
                y = jnp.where(lax.broadcasted_iota(jnp.int32, x.shape, 0) >= k, y, 0.0)
            return y

        def up(x, k):
            y = pltpu.roll(x, rows - k, axis=0)
            if zero_fill:
                y = jnp.where(lax.broadcasted_iota(jnp.int32, x.shape, 0) < rows - k, y, 0.0)
            return y
        return down, up

    def conv_rows(r0, rows, zero_fill):
        down, up = shifts(rows, zero_fill)
        v = cg_ref[r0:r0 + rows, :] * cu_ref[r0:r0 + rows, :]
        conv = (down(v, 1) * cw_ref[0:1, :] + v * cw_ref[1:2, :] + up(v, 1) * cw_ref[2:3, :]
                + cb_ref[...])
        return (bg_ref[r0:r0 + rows, :] * conv).astype(BF16)

    def pool_rows(r0, rows, zero_fill, gg, half):
        down, up = shifts(rows, zero_fill)
        sl = slice(gg * LANES, (gg + 1) * LANES)
        u = pu_ref[r0:r0 + rows, sl]
        fwd = u
        bwd = u
        k = 1
        while k < half:
            fwd = fwd + up(fwd, k)
            bwd = bwd + down(bwd, k)
            k *= 2
        total = fwd + down(bwd, 1)
        rowf = (lax.broadcasted_iota(jnp.int32, u.shape, 0) + r0).astype(F32)
        cnt = jnp.minimum(rowf + half, float(n)) - jnp.maximum(rowf - half, 0.0)
        mean = total / cnt - u
        return (_dot(mean.astype(BF16), pw_ref[gg]) * ps_ref[:, sl]).astype(BF16)

    zb_ref[...] = conv_rows(0, n, False)
    zb_ref[0:edge, :] = conv_rows(0, 2 * edge, True)[0:edge]
    zb_ref[n - edge:n, :] = conv_rows(n - 2 * edge, 2 * edge, True)[edge:]

    step = pl.program_id(1)
    for first in range(0, len(POOL_WINDOWS), gps):
        @pl.when(step * gps == first)
        def _(first=first):
            for gg in range(gps):
                sl = slice(gg * LANES, (gg + 1) * LANES)
                half = POOL_WINDOWS[first + gg] // 2
                zd_ref[:, sl] = pool_rows(0, n, False, gg, half)
                zd_ref[0:edge, sl] = pool_rows(0, 2 * edge, True, gg, half)[0:edge]
                zd_ref[n - edge:n, sl] = pool_rows(n - 2 * edge, 2 * edge, True, gg, half)[edge:]


def _convpool_call(p, lw, batch, seq_len, cols):
    t = p.shape[0]
    gps = 2
    cw = gps * LANES
    nblk = lw["conv_w"].shape[1] // cw

    def pspec(name):
        base = cols[name] // cw
        return pl.BlockSpec((seq_len, cw), lambda b, g: (b, base + g))

    vec = lambda b, g: (0, g)
    return pl.pallas_call(
        _convpool_kernel,
        out_shape=(jax.ShapeDtypeStruct((t, nblk * cw), BF16), jax.ShapeDtypeStruct((t, nblk * cw), BF16)),
        grid=(batch, nblk),
        in_specs=[pspec("conv_u"), pspec("conv_bg"), pspec("conv_cg"), pspec("pool_u"),
                  pl.BlockSpec((3, cw), vec), pl.BlockSpec((1, cw), vec),
                  pl.BlockSpec((gps, LANES, LANES), lambda b, g: (g, 0, 0)), pl.BlockSpec((1, cw), vec)],
        out_specs=(pl.BlockSpec((seq_len, cw), lambda b, g: (b, g)),
                   pl.BlockSpec((seq_len, cw), lambda b, g: (b, g))),
        compiler_params=_cparams(("parallel", "parallel"), 40),
        name="conv_pool",
    )(p, p, p, p, lw["conv_w"], lw["conv_b"], lw["pool_w"], lw["pool_scale"])


SSM_R = 4
SSM_GB = LANES // SSM_GROUP_CH


def _ssm_kernel(*refs, steps, nchunks, nblk):
    u_refs = refs[:nblk]
    h0_ref, a_ref, mb_ref, m1_ref, m2_ref = refs[nblk:nblk + 5]
    y_refs = refs[nblk + 5:2 * nblk + 5]
    hf_ref, slab_scr, xs_scr, st_scr = refs[2 * nblk + 5:]
    d = pl.program_id(0)
    c = pl.program_id(2)
    width = st_scr.shape[1]
    bw = width // nblk

    @pl.when(c == 0)
    def _():
        st_scr[...] = h0_ref[0]

    for blk in range(nblk):
        for j in range(SSM_R):
            for b in range(SUBLANES):
                slab_scr[blk * SSM_R + j, pl.ds(b, steps, stride=SUBLANES), :] = (
                    u_refs[blk][b, pl.ds(j, steps, stride=SSM_R), :])

    def packed(blk):
        return jnp.concatenate([slab_scr[blk * SSM_R + j] for j in range(SSM_R)], axis=1).astype(BF16)

    for blk in range(nblk):
        xs_scr[:, blk * bw:(blk + 1) * bw] = _dot(packed(blk), mb_ref[0, blk])

    gw = width // 2
    for part in range(width // gw):
        offs = []
        for blk in range(part * nblk // 2, (part + 1) * nblk // 2):
            offs += [(blk * bw + k * LANES, blk * bw + bw // 2 + k * LANES) for k in range(bw // 2 // LANES)]
        coef = [(a_ref[0, :, r0:r0 + LANES], a_ref[0, :, i0:i0 + LANES]) for r0, i0 in offs]

        def body(t, carry, offs=offs, coef=coef):
            tt = t + d * (steps - 1 - 2 * t)
            row0 = pl.multiple_of(tt * SUBLANES, SUBLANES)
            new = []
            for (r0, i0), (a_re, a_im), x_re, x_im in zip(offs, coef, carry[0::2], carry[1::2]):
                n_re = (a_re * x_re - a_im * x_im) + xs_scr[pl.ds(row0, SUBLANES), r0:r0 + LANES]
                n_im = (a_re * x_im + a_im * x_re) + xs_scr[pl.ds(row0, SUBLANES), i0:i0 + LANES]
                xs_scr[pl.ds(row0, SUBLANES), r0:r0 + LANES] = x_re
                xs_scr[pl.ds(row0, SUBLANES), i0:i0 + LANES] = x_im
                new += [n_re, n_im]
            return tuple(new)

        init = []
        for r0, i0 in offs:
            init += [st_scr[:, r0:r0 + LANES], st_scr[:, i0:i0 + LANES]]
        final = lax.fori_loop(0, steps, body, tuple(init), unroll=2)
        for (r0, i0), x_re, x_im in zip(offs, final[0::2], final[1::2]):
            st_scr[:, r0:r0 + LANES] = x_re
            st_scr[:, i0:i0 + LANES] = x_im

    for blk in range(nblk):
        y = (_dot(xs_scr[:, blk * bw:(blk + 1) * bw].astype(BF16), m1_ref[0, blk])
             + _dot(packed(blk), m2_ref[0, blk]))
        for j in range(SSM_R):
            slab_scr[blk * SSM_R + j] = y[:, j * LANES:(j + 1) * LANES]
        for j in range(SSM_R):
            for b in range(SUBLANES):
                y_refs[blk][0, b, pl.ds(j, steps, stride=SSM_R), :] = (
                    slab_scr[blk * SSM_R + j, pl.ds(b, steps, stride=SUBLANES), :])

    @pl.when(c == nchunks - 1)
    def _():
        hf_ref[0] = st_scr[...]


def _ssm_call(p3, h0, lw, col0):
    batch, seq_len, _ = p3.shape
    width = h0.shape[-1]
    nblk = lw["ssm_mb"].shape[2]
    l = lw["l"]
    steps = min(seq_len // SSM_R, 64)
    tokens = steps * SSM_R
    nchunks = seq_len // tokens
    nbg = batch // SUBLANES
    rows = steps * SUBLANES

    def cidx(d, c):
        return c + d * (nchunks - 1 - 2 * c)

    def uspec(blk):
        return pl.BlockSpec((SUBLANES, tokens, LANES), lambda d, b, c: (b, cidx(d, c), col0 // LANES + blk))

    def wspec(arr):
        return pl.BlockSpec((None, 1) + arr.shape[2:], lambda d, b, c: (l, d, 0, 0, 0),
                            pipeline_mode=pl.Buffered(1))

    yshape = jax.ShapeDtypeStruct((2, batch, seq_len, LANES), F32)
    yspec = pl.BlockSpec((1, SUBLANES, tokens, LANES), lambda d, b, c: (d, b, cidx(d, c), 0))
    outs = pl.pallas_call(
        functools.partial(_ssm_kernel, steps=steps, nchunks=nchunks, nblk=nblk),
        out_shape=(yshape,) * nblk + (jax.ShapeDtypeStruct((2, batch, width), F32),),
        grid=(2, nbg, nchunks),
        in_specs=[uspec(blk) for blk in range(nblk)] + [
            pl.BlockSpec((1, SUBLANES, width), lambda d, b, c: (d, b, 0)),
            pl.BlockSpec((None, 1, SUBLANES, width), lambda d, b, c: (l, d, 0, 0)),
            wspec(lw["ssm_mb"]), wspec(lw["ssm_m1"]), wspec(lw["ssm_m2"]),
        ],
        out_specs=(yspec,) * nblk + (pl.BlockSpec((1, SUBLANES, width), lambda d, b, c: (d, b, 0)),),
        scratch_shapes=[pltpu.VMEM((nblk * SSM_R, rows, LANES), F32), pltpu.VMEM((rows, width), F32),
                        pltpu.VMEM((SUBLANES, width), F32)],
        compiler_params=_cparams(("arbitrary", "arbitrary", "arbitrary"), 52),
        name="s5_scan",
    )(*([p3] * nblk), h0, lw["ssm_a"], lw["ssm_mb"], lw["ssm_m1"], lw["ssm_m2"])
    return outs[:nblk], outs[nblk]


def _merge_kernel(*refs, nblk):
    (h_ref, za_ref, zb_ref, su_ref, sd_ref, zd_ref, wg0, wg1, wg2, wg3, wa, wb, wca, wcg, wd) = refs[:15]
    yf_refs = refs[15:15 + nblk]
    yb_refs = refs[15 + nblk:15 + 2 * nblk]
    o_ref, y_scr = refs[15 + 2 * nblk:]

    @pl.when(pl.program_id(1) == 0)
    def _():
        y_both = jnp.concatenate([f[0] + b[0] for f, b in zip(yf_refs, yb_refs)], axis=1)
        y_scr[...] = (y_both + sd_ref[...] * su_ref[...]).astype(BF16)

    h = h_ref[...]
    y = y_scr[...]

    def gate(w_ref):
        return jax.nn.sigmoid(_dot(h, w_ref[...]))

    merged = gate(wg0) * _dot(za_ref[...], wa[...])
    merged = merged + gate(wg1) * _dot(zb_ref[...], wb[...])
    merged = merged + gate(wg2) * (_dot(y, wca[...]) * jax.nn.sigmoid(_dot(y, wcg[...])))
    merged = merged + gate(wg3) * _dot(zd_ref[...], wd[...])
    o_ref[...] = merged.astype(BF16)


def _merge_call(h, p, za, zb, ys, zd, lw, cols):
    t, d = h.shape
    bw = za.shape[1]
    tm, tn = 512, 512
    nj = d // tn
    nblk = len(ys)
    row = lambda i, j: (i, 0)
    su_blk = cols["ssm_u"] // bw

    l = lw["l"]

    def gspec(k):
        return pl.BlockSpec((None, d, tn), lambda i, j: (l, 0, k * nj + j))

    def yspec(direction):
        return pl.BlockSpec((1, tm, LANES), lambda i, j: (direction, i, 0))

    wcol = pl.BlockSpec((None, bw, tn), lambda i, j: (l, 0, j))
    in_specs = [
        pl.BlockSpec((tm, d), row),
        pl.BlockSpec((tm, bw), row), pl.BlockSpec((tm, bw), row),
        pl.BlockSpec((tm, bw), lambda i, j: (i, su_blk)),
        pl.BlockSpec((1, bw), lambda i, j: (0, 0)),
        pl.BlockSpec((tm, bw), row),
        gspec(0), gspec(1), gspec(2), gspec(3),
        wcol, wcol, wcol, pl.BlockSpec((None, bw, tn), lambda i, j: (l, 0, nj + j)), wcol,
    ] + [yspec(0)] * nblk + [yspec(1)] * nblk
    wg = lw["w_gate"]
    return pl.pallas_call(
        functools.partial(_merge_kernel, nblk=nblk),
        out_shape=jax.ShapeDtypeStruct((t, d), BF16),
        grid=(t // tm, nj),
        in_specs=in_specs,
        out_specs=pl.BlockSpec((tm, tn), lambda i, j: (i, j)),
        scratch_shapes=[pltpu.VMEM((tm, bw), BF16)],
        compiler_params=_cparams(("parallel", "arbitrary"), 48),
        name="branch_merge",
    )(h, za, zb, p, lw["ssm_d"], zd,
      wg, wg, wg, wg, lw["w_mla_o"], lw["w_conv_o"], lw["w_glu"], lw["w_glu"], lw["w_pool_o"], *ys, *ys)


def _outproj_kernel(x_ref, m_ref, gt_ref, sh_ref, sc_ref, g_ref, wo_ref, x1_ref, h2_ref):
    x1 = x_ref[...] + gt_ref[0] * _dot(m_ref[...], wo_ref[...])
    x1_ref[...] = x1
    h2_ref[...] = _modnorm(x1, g_ref[...], sc_ref[0], sh_ref[0]).astype(BF16)


def _outproj_call(x, merged, mod, lw, seq_len, per_batch):
    t, d = x.shape
    tm = 512
    mspec = functools.partial(_mod_spec, d=d, tm=tm, seq_len=seq_len, per_batch=per_batch)
    row = lambda i: (i, 0)
    return pl.pallas_call(
        _outproj_kernel,
        out_shape=(jax.ShapeDtypeStruct((t, d), F32), jax.ShapeDtypeStruct((t, d), BF16)),
        grid=(t // tm,),
        in_specs=[
            pl.BlockSpec((tm, d), row), pl.BlockSpec((tm, d), row),
            mspec(2), mspec(3), mspec(4),
            pl.BlockSpec((1, d), lambda i: (0, 0)),
            pl.BlockSpec((None, d, d), lambda i: (lw["l"], 0, 0), pipeline_mode=pl.Buffered(1)),
        ],
        out_specs=(pl.BlockSpec((tm, d), row), pl.BlockSpec((tm, d), row)),
        compiler_params=_cparams(("parallel",), 48),
        name="out_proj",
    )(x, merged, mod, mod, mod, lw["norm2_g"], lw["w_o"])


def _mlp_kernel(x_ref, h_ref, gt_ref, w1_ref, w2_ref, o_ref):
    @pl.when(pl.program_id(1) == 0)
    def _():
        o_ref[...] = x_ref[...]

    hid = jnp.square(jnp.maximum(_dot(h_ref[...], w1_ref[...]), 0.0))
    o_ref[...] += gt_ref[0] * _dot(hid.astype(BF16), w2_ref[...])


def _mlp_call(x, h, mod, lw, seq_len, per_batch):
    t, d = x.shape
    hidden = lw["w_mlp1"].shape[2]
    tm, tk = 512, 1024
    l = lw["l"]
    mspec = functools.partial(_mod_spec, d=d, tm=tm, seq_len=seq_len, per_batch=per_batch)
    row = lambda i, k: (i, 0)
    return pl.pallas_call(
        _mlp_kernel,
        out_shape=jax.ShapeDtypeStruct((t, d), F32),
        grid=(t // tm, hidden // tk),
        in_specs=[
            pl.BlockSpec((tm, d), row), pl.BlockSpec((tm, d), row), mspec(5),
            pl.BlockSpec((None, d, tk), lambda i, k: (l, 0, k)),
            pl.BlockSpec((None, tk, d), lambda i, k: (l, k, 0)),
        ],
        out_specs=pl.BlockSpec((tm, d), row),
        compiler_params=_cparams(("parallel", "arbitrary"), 48),
        name="mlp",
    )(x, h, mod, lw["w_mlp1"], lw["w_mlp2"])


def _pad_heads(w, per_head, lo):
    lead = w.shape[:-1]
    w = w.reshape(lead + (MLA_HEADS, per_head))
    w = jnp.pad(w, [(0, 0)] * len(lead) + [(0, 0), (lo, HEAD_PAD - lo - per_head)])
    return w.reshape(lead + (MLA_HEADS * HEAD_PAD,))


_ROPE_PARTNER = tuple((i // 16) * 16 + (i % 16 + 8) % 16 for i in range(ROPE_DIM))


def _swap_heads(w):
    lead = w.shape[:-1]
    w = w.reshape(lead + (MLA_HEADS, QK_HEAD))[..., QK_NOPE:][..., jnp.array(_ROPE_PARTNER)]
    w = jnp.pad(w, [(0, 0)] * len(lead) + [(0, 0), (QK_NOPE, HEAD_PAD - QK_HEAD)])
    return w.reshape(lead + (MLA_HEADS * HEAD_PAD,))


def _prep_ssm(lam_re, lam_im, log_step, b_re, b_im, c_re, c_im):
    r, gb = SSM_R, SSM_GB
    nl, ndir, g, n, cg = b_re.shape
    ns = nl * ndir
    nblk = g // gb
    flat = lambda v: v.reshape((ns,) + v.shape[2:])
    lam_re, lam_im, log_step, b_re, b_im, c_re, c_im = map(
        flat, (lam_re, lam_im, log_step, b_re, b_im, c_re, c_im))
    step = jnp.exp(log_step)[..., None]
    pw = []
    for p in range(r + 1):
        mag = jnp.exp(p * (lam_re * step))
        pw.append((mag * jnp.cos(p * (lam_im * step)), mag * jnp.sin(p * (lam_im * step))))
    ar, ai = pw[1]
    den = lam_re * lam_re + lam_im * lam_im
    qr = ((ar - 1.0) * lam_re + ai * lam_im) / den
    qi = (ai * lam_re - (ar - 1.0) * lam_im) / den
    bb_re = qr[..., None] * b_re - qi[..., None] * b_im
    bb_im = qr[..., None] * b_im + qi[..., None] * b_re
    fwd = (jnp.arange(ns) % ndir == 0)[:, None, None]

    def power(p_fwd, p_bwd):
        return (jnp.where(fwd, pw[p_fwd][0], pw[p_bwd][0]), jnp.where(fwd, pw[p_fwd][1], pw[p_bwd][1]))

    rows_g = (jnp.arange(r * gb * cg) // cg) % gb
    cols_g = jnp.arange(gb * n) // n

    t_re, t_im = [], []
    for j in range(r):
        pr, pi = power(r - 1 - j, j)
        t_re.append(pr[..., None] * bb_re - pi[..., None] * bb_im)
        t_im.append(pr[..., None] * bb_im + pi[..., None] * bb_re)
    mask_b = (rows_g[:, None] == cols_g[None, :]).astype(F32)

    def inc_block(ts):
        t = jnp.stack(ts, axis=1).reshape(ns, r, nblk, gb, n, cg)
        t = jnp.transpose(t, (0, 2, 1, 3, 5, 4)).reshape(ns, nblk, r * gb * cg, n)
        return jnp.tile(t, (1, 1, 1, gb)) * mask_b

    mb = jnp.concatenate([inc_block(t_re), inc_block(t_im)], axis=-1)

    cl_re, cl_im = [], []
    for j in range(r):
        pr, pi = power(j + 1, r - j)
        cl_re.append(c_re * pr[:, :, None, :] - c_im * pi[:, :, None, :])
        cl_im.append(c_re * pi[:, :, None, :] + c_im * pr[:, :, None, :])

    def out_block(ts):
        t = jnp.stack(ts, axis=1).reshape(ns, r, nblk, gb, cg, n)
        t = jnp.transpose(t, (0, 2, 5, 1, 3, 4)).reshape(ns, nblk, n, r * gb * cg)
        return jnp.tile(t, (1, 1, gb, 1)) * mask_b.T

    m1 = jnp.concatenate([out_block(cl_re), -out_block(cl_im)], axis=2)

    hi = lax.Precision.HIGHEST
    lane_g = jnp.arange(gb * cg) // cg
    mask_k = (lane_g[:, None] == lane_g[None, :]).astype(F32)
    kd = []
    for p in range(r):
        pr, pi = pw[p]
        k2 = (jnp.einsum("sgcn,sgnk->sgck", c_re * pr[:, :, None, :] - c_im * pi[:, :, None, :], bb_re, precision=hi)
              - jnp.einsum("sgcn,sgnk->sgck", c_re * pi[:, :, None, :] + c_im * pr[:, :, None, :], bb_im, precision=hi))
        k2 = jnp.swapaxes(k2, -1, -2).reshape(ns, nblk, gb * cg, cg)
        kd.append(jnp.tile(k2, (1, 1, 1, gb)) * mask_k)
    fwd4 = fwd[..., None]
    zero = jnp.zeros_like(kd[0])
    m2_rows = []
    for i in range(r):
        blocks = []
        for j in range(r):
            if i == j:
                blocks.append(kd[0])
            elif j > i:
                blocks.append(jnp.where(fwd4, kd[j - i], zero))
            else:
                blocks.append(jnp.where(fwd4, zero, kd[i - j]))
        m2_rows.append(jnp.concatenate(blocks, axis=-1))
    m2 = jnp.concatenate(m2_rows, axis=2)

    a = jnp.stack([pw[r][0].reshape(ns, nblk, gb * n), pw[r][1].reshape(ns, nblk, gb * n)], axis=2)
    a = jnp.broadcast_to(a.reshape(ns, 1, 2 * g * n), (ns, SUBLANES, 2 * g * n))
    unflat = lambda v: v.reshape((nl, ndir) + v.shape[1:])
    return unflat(a), unflat(mb.astype(BF16)), unflat(m1.astype(BF16)), unflat(m2.astype(BF16))


def _prep_stacked(prm, sizes):
    q_lora, kv_lora, conv_w, ssm_w, pool_w_ = sizes
    w_in = prm["w_in"]
    offs = {}
    o = 0
    for name, width in (("q_a", q_lora), ("kv_a", kv_lora), ("k_rope", ROPE_DIM), ("conv_u", conv_w),
                        ("conv_bg", conv_w), ("conv_cg", conv_w), ("ssm_u", ssm_w), ("pool_u", pool_w_)):
        offs[name] = (o, o + width)
        o += width
    gate0 = o
    take = lambda name: w_in[:, :, offs[name][0]:offs[name][1]]
    rope_lanes = ((0, 0), (0, 0), (QK_NOPE, HEAD_PAD - QK_HEAD))
    kr_pad = jnp.pad(take("k_rope"), rope_lanes)
    kr_sw = jnp.pad(take("k_rope")[:, :, jnp.array(_ROPE_PARTNER)], rope_lanes)
    pieces = [("conv_u", take("conv_u")), ("conv_bg", take("conv_bg")), ("conv_cg", take("conv_cg")),
              ("ssm_u", take("ssm_u")), ("pool_u", take("pool_u")), ("k_rope", kr_pad),
              ("k_rope_sw", kr_sw), ("q_a", take("q_a")), ("kv_a", take("kv_a"))]
    cols = {}
    o = 0
    for name, w in pieces:
        cols[name] = o
        o += w.shape[2]
    big = dict(
        w_small=jnp.concatenate([w for _, w in pieces], axis=2).astype(BF16),
        w_gate=w_in[:, :, gate0:].astype(BF16),
    )
    for name in ("w_mla_o", "w_conv_o", "w_glu", "w_pool_o", "w_o", "w_mlp1", "w_mlp2"):
        big[name] = prm[name].astype(BF16)
    return big, cols


def _prep_layer(l, prm, sizes, big, ssm):
    kv_lora = sizes[1]
    w_ukv = prm["w_ukv"][l].reshape(kv_lora, MLA_HEADS, QK_NOPE + V_HEAD)
    w_uk = _pad_heads(w_ukv[:, :, :QK_NOPE].reshape(kv_lora, -1), QK_NOPE, 0)
    w_uv = w_ukv[:, :, QK_NOPE:].reshape(kv_lora, -1)
    ssm_a, ssm_mb, ssm_m1, ssm_m2 = ssm

    row = lambda v: v.reshape(1, -1)
    lw = dict(
        big, l=l,
        norm1_g=row(prm["norm1_g"][l]), norm2_g=row(prm["norm2_g"][l]),
        q_a_norm_g=row(prm["q_a_norm_g"][l]), kv_a_norm_g=row(prm["kv_a_norm_g"][l]),
        q_norm_g=row(_pad_heads(jnp.tile(prm["q_norm_g"][l], MLA_HEADS), QK_HEAD, 0)),
        k_norm_g=row(_pad_heads(jnp.tile(prm["k_norm_g"][l], MLA_HEADS), QK_HEAD, 0)),
        q_norm_g_sw=row(_swap_heads(jnp.tile(prm["q_norm_g"][l], MLA_HEADS))),
        k_norm_g_sw=row(_swap_heads(jnp.tile(prm["k_norm_g"][l], MLA_HEADS))),
        w_uq=_pad_heads(prm["w_uq"][l], QK_HEAD, 0).astype(BF16),
        w_uq_sw=_swap_heads(prm["w_uq"][l]).astype(BF16),
        w_uk=w_uk.astype(BF16), w_uv=w_uv.astype(BF16),
        conv_w=prm["conv_w"][l], conv_b=row(prm["conv_b"][l]),
        ssm_a=ssm_a, ssm_mb=ssm_mb, ssm_m1=ssm_m1, ssm_m2=ssm_m2,
        ssm_d=row(prm["ssm_d"][l]),
        pool_w=prm["pool_w"][l].astype(BF16), pool_scale=row(prm["pool_scale"][l]),
    )
    return lw


def _rope_tables(seq_len):
    half = ROPE_DIM // 2
    inv_freq = ROPE_THETA ** (-jnp.arange(0, half, 2, dtype=F32) / half)
    t = jnp.arange(seq_len)
    ang_r = (t // GRID_W).astype(F32)[:, None] * inv_freq
    ang_c = (t % GRID_W).astype(F32)[:, None] * inv_freq
    ones_lo = jnp.ones((seq_len, QK_NOPE), F32)
    zeros_lo = jnp.zeros((seq_len, QK_NOPE), F32)
    tail = jnp.zeros((seq_len, HEAD_PAD - QK_HEAD), F32)
    c = jnp.concatenate([ones_lo, jnp.cos(ang_r), jnp.cos(ang_r), jnp.cos(ang_c), jnp.cos(ang_c), tail], axis=1)
    s = jnp.concatenate([zeros_lo, -jnp.sin(ang_r), jnp.sin(ang_r), -jnp.sin(ang_c), jnp.sin(ang_c), tail], axis=1)
    return c, s


def _states_to_cols(st):
    b, ndir, _, g, n = st.shape
    st = st.reshape(b, ndir, 2, g // SSM_GB, SSM_GB * n)
    return jnp.transpose(st, (1, 0, 3, 2, 4)).reshape(ndir, b, 2 * g * n)


def _cols_to_states(hf, g, n):
    ndir, b, _ = hf.shape
    hf = hf.reshape(ndir, b, g // SSM_GB, 2, SSM_GB * n)
    return jnp.transpose(hf, (1, 0, 3, 2, 4)).reshape(b, ndir, 2, g, n)


def _mixer_layer(x, mod, lw, cols, batch, seq_len, per_batch, rope, ctx):
    p, h1, q, k, v, ckv = _inproj_call(x, mod, lw, rope, seq_len, per_batch, cols)
    if ctx is not None:
        ctx_ckv, ctx_kpe, h0 = ctx
        kc, vc = _ctxprep_call(ctx_ckv, ctx_kpe, lw)
        ctx_len = ctx_ckv.shape[0] // batch
    else:
        kc = vc = None
        ctx_len = 0
        h0 = jnp.zeros((2, batch, lw["ssm_a"].shape[-1]), F32)
    za = _attn_call(q, k, v, kc, vc, batch, seq_len, ctx_len)
    zb, zd = _convpool_call(p, lw, batch, seq_len, cols)

    ys, hf = _ssm_call(p.reshape(batch, seq_len, -1), h0, lw, cols["ssm_u"])
    ys = [y.reshape(2, batch * seq_len, LANES) for y in ys]
    merged = _merge_call(h1, p, za, zb, ys, zd, lw, cols)
    x1, h2 = _outproj_call(x, merged, mod, lw, seq_len, per_batch)
    x2 = _mlp_call(x1, h2, mod, lw, seq_len, per_batch)
    kr = cols["k_rope"] + QK_NOPE
    return x2, ckv, p[:, kr:kr + ROPE_DIM], hf


def kernel(x_prompt, x_sample, c, cache_ckv, cache_krope, state_ssm, c_ctx, w_ada, b_ada, norm1_g, norm2_g, w_in, q_a_norm_g, kv_a_norm_g, w_uq, w_ukv, q_norm_g, k_norm_g, w_mla_o, conv_w, conv_b, w_conv_o, ssm_lam_re, ssm_lam_im, ssm_log_step, ssm_b_re, ssm_b_im, ssm_c_re, ssm_c_im, ssm_d, w_glu, pool_w, pool_scale, w_pool_o, w_o, w_mlp1, w_mlp2):
    prm = dict(norm1_g=norm1_g, norm2_g=norm2_g, w_in=w_in, q_a_norm_g=q_a_norm_g, kv_a_norm_g=kv_a_norm_g,
               w_uq=w_uq, w_ukv=w_ukv, q_norm_g=q_norm_g, k_norm_g=k_norm_g, w_mla_o=w_mla_o,
               conv_w=conv_w, conv_b=conv_b, w_conv_o=w_conv_o, ssm_lam_re=ssm_lam_re, ssm_lam_im=ssm_lam_im,
               ssm_log_step=ssm_log_step, ssm_b_re=ssm_b_re, ssm_b_im=ssm_b_im, ssm_c_re=ssm_c_re,
               ssm_c_im=ssm_c_im, ssm_d=ssm_d, w_glu=w_glu, pool_w=pool_w, pool_scale=pool_scale,
               w_pool_o=w_pool_o, w_o=w_o, w_mlp1=w_mlp1, w_mlp2=w_mlp2)
    depth = w_in.shape[0]
    bp, lp, d = x_prompt.shape
    bs, ls, _ = x_sample.shape
    past = cache_ckv.shape[2]
    g, n = state_ssm.shape[-2:]
    sizes = (w_uq.shape[1], w_ukv.shape[1], conv_w.shape[-1], ssm_d.shape[-1], pool_scale.shape[-1])

    rows = -(-(1 + bs) // SUBLANES) * SUBLANES
    cvec = jnp.zeros((rows, d), F32).at[0].set(c_ctx).at[1:1 + bs].set(c)
    mods = _ada_call(cvec, w_ada, b_ada)
    rope = _rope_tables(ls)
    ssm = _prep_ssm(ssm_lam_re, ssm_lam_im, ssm_log_step, ssm_b_re, ssm_b_im, ssm_c_re, ssm_c_im)
    big, cols = _prep_stacked(prm, sizes)

    yp = x_prompt.reshape(bp * lp, d)
    ys = x_sample.reshape(bs * ls, d)
    ckv_list, krope_list, ssm_list = [], [], []
    for l in range(depth):
        lw = _prep_layer(l, prm, sizes, big, ssm)
        mod_ctx = mods[l, 0:1].reshape(6, 1, d)
        mod_lat = mods[l, 1:1 + bs].reshape(bs * 6, 1, d)
        yp, ckv_l, krope_l, hf_l = _mixer_layer(yp, mod_ctx, lw, cols, bp, lp, False, None, None)
        ckv_list.append(ckv_l.reshape(bp, lp, -1))
        krope_list.append(krope_l.reshape(bp, lp, -1))
        ssm_list.append(_cols_to_states(hf_l, g, n))
        ctx_kpe = jnp.pad(cache_krope[:, l].reshape(bs * past, ROPE_DIM),
                          ((0, 0), (QK_NOPE, HEAD_PAD - QK_HEAD)))
        ctx = (cache_ckv[:, l].reshape(bs * past, -1), ctx_kpe, _states_to_cols(state_ssm[:, l]))
        ys, _, _, _ = _mixer_layer(ys, mod_lat, lw, cols, bs, ls, True, rope, ctx)
    return (yp.reshape(bp, lp, d), ys.reshape(bs, ls, d), jnp.stack(ckv_list, axis=1),
            jnp.stack(krope_list, axis=1), jnp.stack(ssm_list, axis=1))
```

```python
import functools
import math

import jax
import jax.numpy as jnp
from jax import lax
from jax.experimental import pallas as pl
from jax.experimental.pallas import tpu as pltpu

F32 = jnp.float32
BF16 = jnp.bfloat16

GRID_W = 64
N_BRANCH = 4
MLA_HEADS = 8
QK_NOPE = 64
ROPE_DIM = 32
QK_HEAD = QK_NOPE + ROPE_DIM
V_HEAD = 64
ROPE_THETA = 10000.0
SSM_GROUP_CH = 16
POOL_WINDOWS = (2, 4, 8, 16)
EPS = 1e-6
LOG2E = 1.4426950408889634

LANES = 128
SUBLANES = 8
HEAD_PAD = LANES
MIB = 1024 * 1024


def _cparams(sem, vmem_mib):
    return pltpu.CompilerParams(dimension_semantics=sem, vmem_limit_bytes=vmem_mib * MIB)


def _dot(a, b):
    return jnp.dot(a, b, preferred_element_type=F32)


def _dot_t(a, bt):
    return lax.dot_general(a, bt, (((1,), (1,)), ((), ())), preferred_element_type=F32)


def _mod_spec(chunk, d, tm, seq_len, per_batch):
    def imap(i, *_):
        row = (i * tm) // seq_len if per_batch else 0
        return (row * 6 + chunk, 0, 0)
    return pl.BlockSpec((1, 1, d), imap)


def _ada_kernel(c_ref, w_ref, b_ref, o_ref):
    cv = c_ref[...]
    s = cv * jax.nn.sigmoid(cv)
    o_ref[0] = _dot(s.astype(BF16), w_ref[0].astype(BF16)) + b_ref[0]


def _ada_call(cvec, w_ada, b_ada):
    depth, d, n = w_ada.shape
    rows = cvec.shape[0]
    tn = 1024
    return pl.pallas_call(
        _ada_kernel,
        out_shape=jax.ShapeDtypeStruct((depth, rows, n), F32),
        grid=(depth, n // tn),
        in_specs=[
            pl.BlockSpec((rows, d), lambda l, j: (0, 0)),
            pl.BlockSpec((1, d, tn), lambda l, j: (l, 0, j)),
            pl.BlockSpec((1, 1, tn), lambda l, j: (l, 0, j)),
        ],
        out_specs=pl.BlockSpec((1, rows, tn), lambda l, j: (l, 0, j)),
        compiler_params=_cparams(("parallel", "parallel"), 40),
        name="ada_mod",
    )(cvec, w_ada, b_ada.reshape(depth, 1, n))


def _modnorm(x, g, scale, shift):
    ms = jnp.mean(x * x, axis=-1, keepdims=True)
    return (x * lax.rsqrt(ms + EPS) * g) * (1.0 + scale) + shift


def _rms(x, g, width):
    ms = jnp.sum(x * x, axis=-1, keepdims=True) * (1.0 / width)
    return x * lax.rsqrt(ms + EPS) * g


def _heads_norm(x, g, mult, shared=None, rope=None):
    outs = []
    for h in range(MLA_HEADS):
        sl = slice(h * HEAD_PAD, (h + 1) * HEAD_PAD)
        xh = x[:, sl] if shared is None else x[:, sl] + shared
        ms = jnp.sum(xh * xh, axis=-1, keepdims=True) * (1.0 / QK_HEAD)
        r = lax.rsqrt(ms + EPS)
        y = xh * r * g[:, sl]
        if rope is not None:
            xs, shared_s, gs, cos, sin = rope
            part = shared_s if xs is None else xs[:, sl]
            y = y * cos + (part * r * gs[:, sl]) * sin
        if mult != 1.0:
            y = y * mult
        outs.append(y)
    return jnp.concatenate(outs, axis=1)


def _inproj_kernel(*refs, use_rope, q_mult, cols, n_keep):
    (x_ref, sh_ref, sc_ref, g_ref, wmix_ref, watt_ref, gqa_ref, gkva_ref, gq_ref, gk_ref,
     wuq_ref, wuk_ref, wuv_ref) = refs[:13]
    if use_rope:
        gqs_ref, gks_ref, wuqs_ref, c_ref, s_ref = refs[13:18]
    p_out, h_out, q_out, k_out, v_out, ckv_out = refs[-6:]

    h = _modnorm(x_ref[...], g_ref[...], sc_ref[0], sh_ref[0]).astype(BF16)
    h_out[...] = h
    att0 = cols["k_rope"]
    res = _dot_t(h, watt_ref[...])
    p_out[:, :att0] = _dot_t(h, wmix_ref[...])
    p_out[:, att0:] = res[:, :n_keep - att0]

    def col(name, width):
        return res[:, cols[name] - att0:cols[name] - att0 + width]

    qa = col("q_a", wuq_ref.shape[0])
    qa_n = _rms(qa, gqa_ref[...], qa.shape[-1]).astype(BF16)
    q = _dot(qa_n, wuq_ref[...])
    q_rope = (_dot(qa_n, wuqs_ref[...]), None, gqs_ref[...], c_ref[...], s_ref[...]) if use_rope else None
    q_out[...] = _heads_norm(q, gq_ref[...], q_mult, rope=q_rope).astype(BF16)

    kva = col("kv_a", wuk_ref.shape[0])
    ckv = _rms(kva, gkva_ref[...], kva.shape[-1])
    ckv_out[...] = ckv
    ckv_b = ckv.astype(BF16)
    k = _dot(ckv_b, wuk_ref[...])
    k_rope = (None, col("k_rope_sw", HEAD_PAD), gks_ref[...], c_ref[...], s_ref[...]) if use_rope else None
    k_out[...] = _heads_norm(k, gk_ref[...], 1.0, shared=col("k_rope", HEAD_PAD), rope=k_rope).astype(BF16)
    v_out[...] = _dot(ckv_b, wuv_ref[...]).astype(BF16)


def _inproj_call(x, mod, lw, rope, seq_len, per_batch, cols):
    t, d = x.shape
    att0 = cols["k_rope"]
    n_att = lw["w_att"].shape[1]
    n_keep = att0 + HEAD_PAD
    tm = 512
    hq = MLA_HEADS * HEAD_PAD
    hv = MLA_HEADS * V_HEAD
    qlora = lw["w_uq"].shape[0]
    kvlora = lw["w_uk"].shape[0]
    l = lw["l"]
    mspec = functools.partial(_mod_spec, d=d, tm=tm, seq_len=seq_len, per_batch=per_batch)
    const = lambda i: (0, 0)
    row = lambda i: (i, 0)
    in_specs = [
        pl.BlockSpec((tm, d), row), mspec(0), mspec(1),
        pl.BlockSpec((1, d), const),
        pl.BlockSpec((None, att0, d), lambda i: (l, 0, 0), pipeline_mode=pl.Buffered(1)),
        pl.BlockSpec((None, n_att, d), lambda i: (l, 0, 0), pipeline_mode=pl.Buffered(1)),
        pl.BlockSpec((1, qlora), const), pl.BlockSpec((1, kvlora), const),
        pl.BlockSpec((1, hq), const), pl.BlockSpec((1, hq), const),
        pl.BlockSpec((qlora, hq), const), pl.BlockSpec((kvlora, hq), const), pl.BlockSpec((kvlora, hv), const),
    ]
    args = [x, mod, mod, lw["norm1_g"], lw["w_tail"], lw["w_att"], lw["q_a_norm_g"], lw["kv_a_norm_g"], lw["q_norm_g"],
            lw["k_norm_g"], lw["w_uq"], lw["w_uk"], lw["w_uv"]]
    if rope is not None:
        nblk = seq_len // tm
        in_specs += [
            pl.BlockSpec((1, hq), const), pl.BlockSpec((1, hq), const),
            pl.BlockSpec((qlora, hq), const),
            pl.BlockSpec((tm, HEAD_PAD), lambda i: (i % nblk, 0)),
            pl.BlockSpec((tm, HEAD_PAD), lambda i: (i % nblk, 0)),
        ]
        args += [lw["q_norm_g_sw"], lw["k_norm_g_sw"], lw["w_uq_sw"]] + list(rope)
    q_mult = LOG2E * QK_HEAD ** -0.5
    return pl.pallas_call(
        functools.partial(_inproj_kernel, use_rope=rope is not None, q_mult=q_mult, cols=cols, n_keep=n_keep),
        out_shape=(jax.ShapeDtypeStruct((t, n_keep), F32), jax.ShapeDtypeStruct((t, d), BF16),
                   jax.ShapeDtypeStruct((t, hq), BF16), jax.ShapeDtypeStruct((t, hq), BF16),
                   jax.ShapeDtypeStruct((t, hv), BF16), jax.ShapeDtypeStruct((t, kvlora), F32)),
        grid=(t // tm,),
        in_specs=in_specs,
        out_specs=(pl.BlockSpec((tm, n_keep), row), pl.BlockSpec((tm, d), row),
                   pl.BlockSpec((tm, hq), row), pl.BlockSpec((tm, hq), row),
                   pl.BlockSpec((tm, hv), row), pl.BlockSpec((tm, kvlora), row)),
        compiler_params=_cparams(("parallel",), 56),
        name="in_proj",
    )(*args)


def _ctxprep_kernel(ckv_ref, kpe_ref, gk_ref, wuk_ref, wuv_ref, k_out, v_out):
    ckv_b = ckv_ref[...].astype(BF16)
    k = _dot(ckv_b, wuk_ref[...])
    k_out[...] = _heads_norm(k, gk_ref[...], 1.0, shared=kpe_ref[...]).astype(BF16)
    v_out[...] = _dot(ckv_b, wuv_ref[...]).astype(BF16)


def _ctxprep_call(ckv, kpe, lw):
    t, kvlora = ckv.shape
    tm = 512
    hq = MLA_HEADS * HEAD_PAD
    hv = MLA_HEADS * V_HEAD
    const = lambda i: (0, 0)
    return pl.pallas_call(
        _ctxprep_kernel,
        out_shape=(jax.ShapeDtypeStruct((t, hq), BF16), jax.ShapeDtypeStruct((t, hv), BF16)),
        grid=(t // tm,),
        in_specs=[
            pl.BlockSpec((tm, kvlora), lambda i: (i, 0)),
            pl.BlockSpec((tm, HEAD_PAD), lambda i: (i, 0)),
            pl.BlockSpec((1, hq), const),
            pl.BlockSpec((kvlora, hq), const),
            pl.BlockSpec((kvlora, hv), const),
        ],
        out_specs=(pl.BlockSpec((tm, hq), lambda i: (i, 0)), pl.BlockSpec((tm, hv), lambda i: (i, 0))),
        compiler_params=_cparams(("parallel",), 40),
        name="ctx_prep",
    )(ckv, kpe, lw["k_norm_g"], lw["w_uk"], lw["w_uv"])


def _attn_kernel(*refs, has_ctx, heads):
    if has_ctx:
        q_ref, ko_ref, vo_ref, kc_ref, vc_ref, o_ref = refs
    else:
        q_ref, ko_ref, vo_ref, o_ref = refs
    contract_last = (((1,), (1,)), ((), ()))
    lane = lax.broadcasted_iota(jnp.int32, (q_ref.shape[0], 2 * V_HEAD), 1)
    for pair in range(heads // 2):
        vsl = slice(pair * 2 * V_HEAD, (pair + 1) * 2 * V_HEAD)
        outs = []
        for hh in range(2):
            h = 2 * pair + hh
            sl = slice(h * HEAD_PAD, (h + 1) * HEAD_PAD)
            q = q_ref[:, sl]
            s_o = lax.dot_general(q, ko_ref[:, sl], contract_last, preferred_element_type=F32)
            m = jnp.max(s_o, axis=-1, keepdims=True)
            if has_ctx:
                s_c = lax.dot_general(q, kc_ref[:, sl], contract_last, preferred_element_type=F32)
                m = jnp.maximum(m, jnp.max(s_c, axis=-1, keepdims=True))
            p_o = jnp.exp2(s_o - m)
            l = jnp.sum(p_o, axis=-1, keepdims=True)
            acc = _dot(p_o.astype(BF16), vo_ref[:, vsl])
            if has_ctx:
                p_c = jnp.exp2(s_c - m)
                l = l + jnp.sum(p_c, axis=-1, keepdims=True)
                acc = acc + _dot(p_c.astype(BF16), vc_ref[:, vsl])
            outs.append(acc / l)
        o_ref[:, vsl] = jnp.where(lane < V_HEAD, outs[0], outs[1]).astype(BF16)


def _attn_call(q, k, v, kc, vc, batch, seq_len, ctx_len):
    t = q.shape[0]
    tq = min(seq_len, 512)
    heads = 8
    nq = seq_len // tq
    npair = MLA_HEADS // heads
    qw = heads * HEAD_PAD
    vw = heads * V_HEAD
    in_specs = [
        pl.BlockSpec((tq, qw), lambda b, h, i: (b * nq + i, h)),
        pl.BlockSpec((seq_len, qw), lambda b, h, i: (b, h)),
        pl.BlockSpec((seq_len, vw), lambda b, h, i: (b, h)),
    ]
    args = [q, k, v]
    if kc is not None:
        in_specs += [pl.BlockSpec((ctx_len, qw), lambda b, h, i: (b, h)),
                     pl.BlockSpec((ctx_len, vw), lambda b, h, i: (b, h))]
        args += [kc, vc]
    return pl.pallas_call(
        functools.partial(_attn_kernel, has_ctx=kc is not None, heads=heads),
        out_shape=jax.ShapeDtypeStruct((t, MLA_HEADS * V_HEAD), BF16),
        grid=(batch, npair, nq),
        in_specs=in_specs,
        out_specs=pl.BlockSpec((tq, vw), lambda b, h, i: (b * nq + i, h)),
        compiler_params=_cparams(("parallel", "parallel", "arbitrary"), 56),
        name="attention",
    )(*args)


def _convpool_kernel(cu_ref, bg_ref, cg_ref, pu_ref, cw_ref, cb_ref, pw_ref, ps_ref, zb_ref, zd_ref):
    n = cu_ref.shape[0]
    gps = pw_ref.shape[0]
    edge = max(POOL_WINDOWS) // 2
    assert edge % SUBLANES == 0 and n >= 4 * edge

    def shifts(rows, zero_fill):
        def down(x, k):
            y = pltpu.roll(x, k, axis=0)
            if zero_fill:
                y = jnp.where(lax.broadcasted_iota(jnp.int32, x.shape, 0) >= k, y, 0.0)
            return y

        def up(x, k):
            y = pltpu.roll(x, rows - k, axis=0)
            if zero_fill:
                y = jnp.where(lax.broadcasted_iota(jnp.int32, x.shape, 0) < rows - k, y, 0.0)
            return y
        return down, up

    def conv_rows(r0, rows, zero_fill):
        down, up = shifts(rows, zero_fill)
        v = cg_ref[r0:r0 + rows, :] * cu_ref[r0:r0 + rows, :]
        conv = (down(v, 1) * cw_ref[0:1, :] + v * cw_ref[1:2, :] + up(v, 1) * cw_ref[2:3, :]
                + cb_ref[...])
        return (bg_ref[r0:r0 + rows, :] * conv).astype(BF16)

    def pool_rows(r0, rows, zero_fill, gg, half):
        down, up = shifts(rows, zero_fill)
        sl = slice(gg * LANES, (gg + 1) * LANES)
        u = pu_ref[r0:r0 + rows, sl]
        fwd = u
        bwd = u
        k = 1
        while k < half:
            fwd = fwd + up(fwd, k)
            bwd = bwd + down(bwd, k)
            k *= 2
        total = fwd + down(bwd, 1)
        rowf = (lax.broadcasted_iota(jnp.int32, u.shape, 0) + r0).astype(F32)
        cnt = jnp.minimum(rowf + half, float(n)) - jnp.maximum(rowf - half, 0.0)
        mean = total / cnt - u
        return (_dot(mean.astype(BF16), pw_ref[gg]) * ps_ref[:, sl]).astype(BF16)

    zb_ref[...] = conv_rows(0, n, False)
    zb_ref[0:edge, :] = conv_rows(0, 2 * edge, True)[0:edge]
    zb_ref[n - edge:n, :] = conv_rows(n - 2 * edge, 2 * edge, True)[edge:]

    step = pl.program_id(1)
    for first in range(0, len(POOL_WINDOWS), gps):
        @pl.when(step * gps == first)
        def _(first=first):
            for gg in range(gps):
                sl = slice(gg * LANES, (gg + 1) * LANES)
                half = POOL_WINDOWS[first + gg] // 2
                zd_ref[:, sl] = pool_rows(0, n, False, gg, half)
                zd_ref[0:edge, sl] = pool_rows(0, 2 * edge, True, gg, half)[0:edge]
                zd_ref[n - edge:n, sl] = pool_rows(n - 2 * edge, 2 * edge, True, gg, half)[edge:]


def _convpool_call(p, lw, batch, seq_len, cols):
    t = p.shape[0]
    gps = 2
    cw = gps * LANES
    nblk = lw["conv_w"].shape[1] // cw

    def pspec(name):
        base = cols[name] // cw
        return pl.BlockSpec((seq_len, cw), lambda b, g: (b, base + g))

    vec = lambda b, g: (0, g)
    return pl.pallas_call(
        _convpool_kernel,
        out_shape=(jax.ShapeDtypeStruct((t, nblk * cw), BF16), jax.ShapeDtypeStruct((t, nblk * cw), BF16)),
        grid=(batch, nblk),
        in_specs=[pspec("conv_u"), pspec("conv_bg"), pspec("conv_cg"), pspec("pool_u"),
                  pl.BlockSpec((3, cw), vec), pl.BlockSpec((1, cw), vec),
                  pl.BlockSpec((gps, LANES, LANES), lambda b, g: (g, 0, 0)), pl.BlockSpec((1, cw), vec)],
        out_specs=(pl.BlockSpec((seq_len, cw), lambda b, g: (b, g)),
                   pl.BlockSpec((seq_len, cw), lambda b, g: (b, g))),
        compiler_params=_cparams(("parallel", "parallel"), 40),
        name="conv_pool",
    )(p, p, p, p, lw["conv_w"], lw["conv_b"], lw["pool_w"], lw["pool_scale"])


SSM_R = 4
SSM_GB = LANES // SSM_GROUP_CH


def _ssm_kernel(*refs, steps, nchunks, nblk):
    u_refs = refs[:nblk]
    h0_ref, a_ref, mb_ref, m1_ref, m2_ref = refs[nblk:nblk + 5]
    y_refs = refs[nblk + 5:2 * nblk + 5]
    hf_ref, slab_scr, xs_scr, st_scr = refs[2 * nblk + 5:]
    d = pl.program_id(0)
    c = pl.program_id(2)
    width = st_scr.shape[1]
    bw = width // nblk

    @pl.when(c == 0)
    def _():
        st_scr[...] = h0_ref[0]

    for blk in range(nblk):
        for j in range(SSM_R):
            for b in range(SUBLANES):
                slab_scr[blk * SSM_R + j, pl.ds(b, steps, stride=SUBLANES), :] = (
                    u_refs[blk][b, pl.ds(j, steps, stride=SSM_R), :])

    def packed(blk):
        return jnp.concatenate([slab_scr[blk * SSM_R + j] for j in range(SSM_R)], axis=1).astype(BF16)

    for blk in range(nblk):
        xs_scr[:, blk * bw:(blk + 1) * bw] = _dot(packed(blk), mb_ref[0, blk])

    gw = width // 2
    for part in range(width // gw):
        offs = []
        for blk in range(part * nblk // 2, (part + 1) * nblk // 2):
            offs += [(blk * bw + k * LANES, blk * bw + bw // 2 + k * LANES) for k in range(bw // 2 // LANES)]
        coef = [(a_ref[0, :, r0:r0 + LANES], a_ref[0, :, i0:i0 + LANES]) for r0, i0 in offs]

        def body(t, carry, offs=offs, coef=coef):
            tt = t + d * (steps - 1 - 2 * t)
            row0 = pl.multiple_of(tt * SUBLANES, SUBLANES)
            new = []
            for (r0, i0), (a_re, a_im), x_re, x_im in zip(offs, coef, carry[0::2], carry[1::2]):
                n_re = (a_re * x_re - a_im * x_im) + xs_scr[pl.ds(row0, SUBLANES), r0:r0 + LANES]
                n_im = (a_re * x_im + a_im * x_re) + xs_scr[pl.ds(row0, SUBLANES), i0:i0 + LANES]
                xs_scr[pl.ds(row0, SUBLANES), r0:r0 + LANES] = x_re
                xs_scr[pl.ds(row0, SUBLANES), i0:i0 + LANES] = x_im
                new += [n_re, n_im]
            return tuple(new)

        init = []
        for r0, i0 in offs:
            init += [st_scr[:, r0:r0 + LANES], st_scr[:, i0:i0 + LANES]]
        final = lax.fori_loop(0, steps, body, tuple(init), unroll=2)
        for (r0, i0), x_re, x_im in zip(offs, final[0::2], final[1::2]):
            st_scr[:, r0:r0 + LANES] = x_re
            st_scr[:, i0:i0 + LANES] = x_im

    for blk in range(nblk):
        y = (_dot(xs_scr[:, blk * bw:(blk + 1) * bw].astype(BF16), m1_ref[0, blk])
             + _dot(packed(blk), m2_ref[0, blk]))
        for j in range(SSM_R):
            slab_scr[blk * SSM_R + j] = y[:, j * LANES:(j + 1) * LANES]
        for j in range(SSM_R):
            for b in range(SUBLANES):
                y_refs[blk][0, b, pl.ds(j, steps, stride=SSM_R), :] = (
                    slab_scr[blk * SSM_R + j, pl.ds(b, steps, stride=SUBLANES), :])

    @pl.when(c == nchunks - 1)
    def _():
        hf_ref[0] = st_scr[...]


def _ssm_call(p3, h0, lw, col0):
    batch, seq_len, _ = p3.shape
    width = h0.shape[-1]
    nblk = lw["ssm_mb"].shape[2]
    l = lw["l"]
    steps = min(seq_len // SSM_R, 64)
    tokens = steps * SSM_R
    nchunks = seq_len // tokens
    nbg = batch // SUBLANES
    rows = steps * SUBLANES

    def cidx(d, c):
        return c + d * (nchunks - 1 - 2 * c)

    def uspec(blk):
        return pl.BlockSpec((SUBLANES, tokens, LANES), lambda d, b, c: (b, cidx(d, c), col0 // LANES + blk))

    def wspec(arr):
        return pl.BlockSpec((None, 1) + arr.shape[2:], lambda d, b, c: (l, d, 0, 0, 0),
                            pipeline_mode=pl.Buffered(1))

    yshape = jax.ShapeDtypeStruct((2, batch, seq_len, LANES), F32)
    yspec = pl.BlockSpec((1, SUBLANES, tokens, LANES), lambda d, b, c: (d, b, cidx(d, c), 0))
    outs = pl.pallas_call(
        functools.partial(_ssm_kernel, steps=steps, nchunks=nchunks, nblk=nblk),
        out_shape=(yshape,) * nblk + (jax.ShapeDtypeStruct((2, batch, width), F32),),
        grid=(2, nbg, nchunks),
        in_specs=[uspec(blk) for blk in range(nblk)] + [
            pl.BlockSpec((1, SUBLANES, width), lambda d, b, c: (d, b, 0)),
            pl.BlockSpec((None, 1, SUBLANES, width), lambda d, b, c: (l, d, 0, 0)),
            wspec(lw["ssm_mb"]), wspec(lw["ssm_m1"]), wspec(lw["ssm_m2"]),
        ],
        out_specs=(yspec,) * nblk + (pl.BlockSpec((1, SUBLANES, width), lambda d, b, c: (d, b, 0)),),
        scratch_shapes=[pltpu.VMEM((nblk * SSM_R, rows, LANES), F32), pltpu.VMEM((rows, width), F32),
                        pltpu.VMEM((SUBLANES, width), F32)],
        compiler_params=_cparams(("arbitrary", "arbitrary", "arbitrary"), 52),
        name="s5_scan",
    )(*([p3] * nblk), h0, lw["ssm_a"], lw["ssm_mb"], lw["ssm_m1"], lw["ssm_m2"])
    return outs[:nblk], outs[nblk]


def _merge_kernel(*refs, nblk):
    (h_ref, za_ref, zb_ref, su_ref, sd_ref, zd_ref, wg0, wg1, wg2, wg3, wa, wb, wca, wcg, wd) = refs[:15]
    yf_refs = refs[15:15 + nblk]
    yb_refs = refs[15 + nblk:15 + 2 * nblk]
    o_ref, y_scr = refs[15 + 2 * nblk:]

    @pl.when(pl.program_id(1) == 0)
    def _():
        y_both = jnp.concatenate([f[0] + b[0] for f, b in zip(yf_refs, yb_refs)], axis=1)
        y_scr[...] = (y_both + sd_ref[...] * su_ref[...]).astype(BF16)

    h = h_ref[...]
    y = y_scr[...]

    def gate(w_ref):
        return jax.nn.sigmoid(_dot_t(h, w_ref[...]))

    merged = gate(wg0) * _dot(za_ref[...], wa[...])
    merged = merged + gate(wg1) * _dot(zb_ref[...], wb[...])
    merged = merged + gate(wg2) * (_dot(y, wca[...]) * jax.nn.sigmoid(_dot(y, wcg[...])))
    merged = merged + gate(wg3) * _dot(zd_ref[...], wd[...])
    o_ref[...] = merged.astype(BF16)


def _merge_call(h, p, za, zb, ys, zd, lw, cols):
    t, d = h.shape
    bw = za.shape[1]
    tm, tn = 512, 512
    nj = d // tn
    nblk = len(ys)
    row = lambda i, j: (i, 0)
    su_blk = cols["ssm_u"] // bw

    l = lw["l"]

    gate_blk0 = cols["k_rope"] // tn

    def gspec(k):
        return pl.BlockSpec((None, tn, d), lambda i, j: (l, gate_blk0 + k * nj + j, 0))

    def yspec(direction):
        return pl.BlockSpec((1, tm, LANES), lambda i, j: (direction, i, 0))

    wcol = pl.BlockSpec((None, bw, tn), lambda i, j: (l, 0, j))
    in_specs = [
        pl.BlockSpec((tm, d), row),
        pl.BlockSpec((tm, bw), row), pl.BlockSpec((tm, bw), row),
        pl.BlockSpec((tm, bw), lambda i, j: (i, su_blk)),
        pl.BlockSpec((1, bw), lambda i, j: (0, 0)),
        pl.BlockSpec((tm, bw), row),
        gspec(0), gspec(1), gspec(2), gspec(3),
        wcol, wcol, wcol, pl.BlockSpec((None, bw, tn), lambda i, j: (l, 0, nj + j)), wcol,
    ] + [yspec(0)] * nblk + [yspec(1)] * nblk
    wg = lw["w_tail"]
    return pl.pallas_call(
        functools.partial(_merge_kernel, nblk=nblk),
        out_shape=jax.ShapeDtypeStruct((t, d), BF16),
        grid=(t // tm, nj),
        in_specs=in_specs,
        out_specs=pl.BlockSpec((tm, tn), lambda i, j: (i, j)),
        scratch_shapes=[pltpu.VMEM((tm, bw), BF16)],
        compiler_params=_cparams(("parallel", "arbitrary"), 48),
        name="branch_merge",
    )(h, za, zb, p, lw["ssm_d"], zd,
      wg, wg, wg, wg, lw["w_mla_o"], lw["w_conv_o"], lw["w_glu"], lw["w_glu"], lw["w_pool_o"], *ys, *ys)


def _outproj_kernel(x_ref, m_ref, gt_ref, sh_ref, sc_ref, g_ref, wo_ref, x1_ref, h2_ref):
    x1 = x_ref[...] + gt_ref[0] * _dot(m_ref[...], wo_ref[...])
    x1_ref[...] = x1
    h2_ref[...] = _modnorm(x1, g_ref[...], sc_ref[0], sh_ref[0]).astype(BF16)


def _outproj_call(x, merged, mod, lw, seq_len, per_batch):
    t, d = x.shape
    tm = 512
    mspec = functools.partial(_mod_spec, d=d, tm=tm, seq_len=seq_len, per_batch=per_batch)
    row = lambda i: (i, 0)
    return pl.pallas_call(
        _outproj_kernel,
        out_shape=(jax.ShapeDtypeStruct((t, d), F32), jax.ShapeDtypeStruct((t, d), BF16)),
        grid=(t // tm,),
        in_specs=[
            pl.BlockSpec((tm, d), row), pl.BlockSpec((tm, d), row),
            mspec(2), mspec(3), mspec(4),
            pl.BlockSpec((1, d), lambda i: (0, 0)),
            pl.BlockSpec((None, d, d), lambda i: (lw["l"], 0, 0), pipeline_mode=pl.Buffered(1)),
        ],
        out_specs=(pl.BlockSpec((tm, d), row), pl.BlockSpec((tm, d), row)),
        compiler_params=_cparams(("parallel",), 48),
        name="out_proj",
    )(x, merged, mod, mod, mod, lw["norm2_g"], lw["w_o"])


def _mlp_kernel(x_ref, h_ref, gt_ref, w1_ref, w2_ref, o_ref):
    @pl.when(pl.program_id(1) == 0)
    def _():
        o_ref[...] = x_ref[...]

    hid = jnp.square(jnp.maximum(_dot(h_ref[...], w1_ref[...]), 0.0))
    o_ref[...] += gt_ref[0] * _dot(hid.astype(BF16), w2_ref[...])


def _mlp_call(x, h, mod, lw, seq_len, per_batch):
    t, d = x.shape
    hidden = lw["w_mlp1"].shape[2]
    tm, tk = 512, 1024
    l = lw["l"]
    mspec = functools.partial(_mod_spec, d=d, tm=tm, seq_len=seq_len, per_batch=per_batch)
    row = lambda i, k: (i, 0)
    return pl.pallas_call(
        _mlp_kernel,
        out_shape=jax.ShapeDtypeStruct((t, d), F32),
        grid=(t // tm, hidden // tk),
        in_specs=[
            pl.BlockSpec((tm, d), row), pl.BlockSpec((tm, d), row), mspec(5),
            pl.BlockSpec((None, d, tk), lambda i, k: (l, 0, k)),
            pl.BlockSpec((None, tk, d), lambda i, k: (l, k, 0)),
        ],
        out_specs=pl.BlockSpec((tm, d), row),
        compiler_params=_cparams(("parallel", "arbitrary"), 48),
        name="mlp",
    )(x, h, mod, lw["w_mlp1"], lw["w_mlp2"])


def _pad_heads(w, per_head, lo):
    lead = w.shape[:-1]
    w = w.reshape(lead + (MLA_HEADS, per_head))
    w = jnp.pad(w, [(0, 0)] * len(lead) + [(0, 0), (lo, HEAD_PAD - lo - per_head)])
    return w.reshape(lead + (MLA_HEADS * HEAD_PAD,))


_ROPE_PARTNER = tuple((i // 16) * 16 + (i % 16 + 8) % 16 for i in range(ROPE_DIM))


def _swap_heads(w):
    lead = w.shape[:-1]
    w = w.reshape(lead + (MLA_HEADS, QK_HEAD))[..., QK_NOPE:][..., jnp.array(_ROPE_PARTNER)]
    w = jnp.pad(w, [(0, 0)] * len(lead) + [(0, 0), (QK_NOPE, HEAD_PAD - QK_HEAD)])
    return w.reshape(lead + (MLA_HEADS * HEAD_PAD,))


def _prep_ssm(lam_re, lam_im, log_step, b_re, b_im, c_re, c_im):
    r, gb = SSM_R, SSM_GB
    nl, ndir, g, n, cg = b_re.shape
    ns = nl * ndir
    nblk = g // gb
    flat = lambda v: v.reshape((ns,) + v.shape[2:])
    lam_re, lam_im, log_step, b_re, b_im, c_re, c_im = map(
        flat, (lam_re, lam_im, log_step, b_re, b_im, c_re, c_im))
    step = jnp.exp(log_step)[..., None]
    pw = []
    for p in range(r + 1):
        mag = jnp.exp(p * (lam_re * step))
        pw.append((mag * jnp.cos(p * (lam_im * step)), mag * jnp.sin(p * (lam_im * step))))
    ar, ai = pw[1]
    den = lam_re * lam_re + lam_im * lam_im
    qr = ((ar - 1.0) * lam_re + ai * lam_im) / den
    qi = (ai * lam_re - (ar - 1.0) * lam_im) / den
    bb_re = qr[..., None] * b_re - qi[..., None] * b_im
    bb_im = qr[..., None] * b_im + qi[..., None] * b_re
    fwd = (jnp.arange(ns) % ndir == 0)[:, None, None]

    def power(p_fwd, p_bwd):
        return (jnp.where(fwd, pw[p_fwd][0], pw[p_bwd][0]), jnp.where(fwd, pw[p_fwd][1], pw[p_bwd][1]))

    rows_g = (jnp.arange(r * gb * cg) // cg) % gb
    cols_g = jnp.arange(gb * n) // n

    t_re, t_im = [], []
    for j in range(r):
        pr, pi = power(r - 1 - j, j)
        t_re.append(pr[..., None] * bb_re - pi[..., None] * bb_im)
        t_im.append(pr[..., None] * bb_im + pi[..., None] * bb_re)
    mask_b = (rows_g[:, None] == cols_g[None, :]).astype(F32)

    def inc_block(ts):
        t = jnp.stack(ts, axis=1).reshape(ns, r, nblk, gb, n, cg)
        t = jnp.transpose(t, (0, 2, 1, 3, 5, 4)).reshape(ns, nblk, r * gb * cg, n)
        return jnp.tile(t, (1, 1, 1, gb)) * mask_b

    mb = jnp.concatenate([inc_block(t_re), inc_block(t_im)], axis=-1)

    cl_re, cl_im = [], []
    for j in range(r):
        pr, pi = power(j + 1, r - j)
        cl_re.append(c_re * pr[:, :, None, :] - c_im * pi[:, :, None, :])
        cl_im.append(c_re * pi[:, :, None, :] + c_im * pr[:, :, None, :])

    def out_block(ts):
        t = jnp.stack(ts, axis=1).reshape(ns, r, nblk, gb, cg, n)
        t = jnp.transpose(t, (0, 2, 5, 1, 3, 4)).reshape(ns, nblk, n, r * gb * cg)
        return jnp.tile(t, (1, 1, gb, 1)) * mask_b.T

    m1 = jnp.concatenate([out_block(cl_re), -out_block(cl_im)], axis=2)

    hi = lax.Precision.HIGHEST
    lane_g = jnp.arange(gb * cg) // cg
    mask_k = (lane_g[:, None] == lane_g[None, :]).astype(F32)
    kd = []
    for p in range(r):
        pr, pi = pw[p]
        k2 = (jnp.einsum("sgcn,sgnk->sgck", c_re * pr[:, :, None, :] - c_im * pi[:, :, None, :], bb_re, precision=hi)
              - jnp.einsum("sgcn,sgnk->sgck", c_re * pi[:, :, None, :] + c_im * pr[:, :, None, :], bb_im, precision=hi))
        k2 = jnp.swapaxes(k2, -1, -2).reshape(ns, nblk, gb * cg, cg)
        kd.append(jnp.tile(k2, (1, 1, 1, gb)) * mask_k)
    fwd4 = fwd[..., None]
    zero = jnp.zeros_like(kd[0])
    m2_rows = []
    for i in range(r):
        blocks = []
        for j in range(r):
            if i == j:
                blocks.append(kd[0])
            elif j > i:
                blocks.append(jnp.where(fwd4, kd[j - i], zero))
            else:
                blocks.append(jnp.where(fwd4, zero, kd[i - j]))
        m2_rows.append(jnp.concatenate(blocks, axis=-1))
    m2 = jnp.concatenate(m2_rows, axis=2)

    a = jnp.stack([pw[r][0].reshape(ns, nblk, gb * n), pw[r][1].reshape(ns, nblk, gb * n)], axis=2)
    a = jnp.broadcast_to(a.reshape(ns, 1, 2 * g * n), (ns, SUBLANES, 2 * g * n))
    unflat = lambda v: v.reshape((nl, ndir) + v.shape[1:])
    return unflat(a), unflat(mb.astype(BF16)), unflat(m1.astype(BF16)), unflat(m2.astype(BF16))


def _prep_stacked(prm, sizes):
    q_lora, kv_lora, conv_w, ssm_w, pool_w_ = sizes
    w_in_t = jnp.swapaxes(prm["w_in"], 1, 2)
    tail0 = q_lora + kv_lora + ROPE_DIM
    w_tail = w_in_t[:, tail0:, :].astype(BF16)
    head = w_in_t[:, :tail0, :]
    cols = {}
    o = 0
    for name, width in (("conv_u", conv_w), ("conv_bg", conv_w), ("conv_cg", conv_w), ("ssm_u", ssm_w),
                        ("pool_u", pool_w_), ("k_rope", HEAD_PAD), ("k_rope_sw", HEAD_PAD), ("q_a", q_lora),
                        ("kv_a", kv_lora)):
        cols[name] = o
        o += width
    k_rope = head[:, q_lora + kv_lora:tail0, :]
    rope_lanes = ((0, 0), (QK_NOPE, HEAD_PAD - QK_HEAD), (0, 0))
    w_att = jnp.concatenate([jnp.pad(k_rope, rope_lanes),
                             jnp.pad(k_rope[:, jnp.array(_ROPE_PARTNER), :], rope_lanes),
                             head[:, :q_lora + kv_lora, :]], axis=1).astype(BF16)
    big = dict(w_tail=w_tail, w_att=w_att)
    for name in ("w_mla_o", "w_conv_o", "w_glu", "w_pool_o", "w_o", "w_mlp1", "w_mlp2"):
        big[name] = prm[name].astype(BF16)
    return big, cols


def _prep_layer(l, prm, sizes, big, ssm):
    kv_lora = sizes[1]
    w_ukv = prm["w_ukv"][l].reshape(kv_lora, MLA_HEADS, QK_NOPE + V_HEAD)
    w_uk = _pad_heads(w_ukv[:, :, :QK_NOPE].reshape(kv_lora, -1), QK_NOPE, 0)
    w_uv = w_ukv[:, :, QK_NOPE:].reshape(kv_lora, -1)
    ssm_a, ssm_mb, ssm_m1, ssm_m2 = ssm

    row = lambda v: v.reshape(1, -1)
    lw = dict(
        big, l=l,
        norm1_g=row(prm["norm1_g"][l]), norm2_g=row(prm["norm2_g"][l]),
        q_a_norm_g=row(prm["q_a_norm_g"][l]), kv_a_norm_g=row(prm["kv_a_norm_g"][l]),
        q_norm_g=row(_pad_heads(jnp.tile(prm["q_norm_g"][l], MLA_HEADS), QK_HEAD, 0)),
        k_norm_g=row(_pad_heads(jnp.tile(prm["k_norm_g"][l], MLA_HEADS), QK_HEAD, 0)),
        q_norm_g_sw=row(_swap_heads(jnp.tile(prm["q_norm_g"][l], MLA_HEADS))),
        k_norm_g_sw=row(_swap_heads(jnp.tile(prm["k_norm_g"][l], MLA_HEADS))),
        w_uq=_pad_heads(prm["w_uq"][l], QK_HEAD, 0).astype(BF16),
        w_uq_sw=_swap_heads(prm["w_uq"][l]).astype(BF16),
        w_uk=w_uk.astype(BF16), w_uv=w_uv.astype(BF16),
        conv_w=prm["conv_w"][l], conv_b=row(prm["conv_b"][l]),
        ssm_a=ssm_a, ssm_mb=ssm_mb, ssm_m1=ssm_m1, ssm_m2=ssm_m2,
        ssm_d=row(prm["ssm_d"][l]),
        pool_w=prm["pool_w"][l].astype(BF16), pool_scale=row(prm["pool_scale"][l]),
    )
    return lw


def _rope_tables(seq_len):
    half = ROPE_DIM // 2
    inv_freq = ROPE_THETA ** (-jnp.arange(0, half, 2, dtype=F32) / half)
    t = jnp.arange(seq_len)
    ang_r = (t // GRID_W).astype(F32)[:, None] * inv_freq
    ang_c = (t % GRID_W).astype(F32)[:, None] * inv_freq
    ones_lo = jnp.ones((seq_len, QK_NOPE), F32)
    zeros_lo = jnp.zeros((seq_len, QK_NOPE), F32)
    tail = jnp.zeros((seq_len, HEAD_PAD - QK_HEAD), F32)
    c = jnp.concatenate([ones_lo, jnp.cos(ang_r), jnp.cos(ang_r), jnp.cos(ang_c), jnp.cos(ang_c), tail], axis=1)
    s = jnp.concatenate([zeros_lo, -jnp.sin(ang_r), jnp.sin(ang_r), -jnp.sin(ang_c), jnp.sin(ang_c), tail], axis=1)
    return c, s


def _states_to_cols(st):
    b, ndir, _, g, n = st.shape
    st = st.reshape(b, ndir, 2, g // SSM_GB, SSM_GB * n)
    return jnp.transpose(st, (1, 0, 3, 2, 4)).reshape(ndir, b, 2 * g * n)


def _cols_to_states(hf, g, n):
    ndir, b, _ = hf.shape
    hf = hf.reshape(ndir, b, g // SSM_GB, 2, SSM_GB * n)
    return jnp.transpose(hf, (1, 0, 3, 2, 4)).reshape(b, ndir, 2, g, n)


def _mixer_layer(x, mod, lw, cols, batch, seq_len, per_batch, rope, ctx):
    p, h1, q, k, v, ckv = _inproj_call(x, mod, lw, rope, seq_len, per_batch, cols)
    if ctx is not None:
        ctx_ckv, ctx_kpe, h0 = ctx
        kc, vc = _ctxprep_call(ctx_ckv, ctx_kpe, lw)
        ctx_len = ctx_ckv.shape[0] // batch
    else:
        kc = vc = None
        ctx_len = 0
        h0 = jnp.zeros((2, batch, lw["ssm_a"].shape[-1]), F32)
    za = _attn_call(q, k, v, kc, vc, batch, seq_len, ctx_len)
    zb, zd = _convpool_call(p, lw, batch, seq_len, cols)

    ys, hf = _ssm_call(p.reshape(batch, seq_len, -1), h0, lw, cols["ssm_u"])
    ys = [y.reshape(2, batch * seq_len, LANES) for y in ys]
    merged = _merge_call(h1, p, za, zb, ys, zd, lw, cols)
    x1, h2 = _outproj_call(x, merged, mod, lw, seq_len, per_batch)
    x2 = _mlp_call(x1, h2, mod, lw, seq_len, per_batch)
    kr = cols["k_rope"] + QK_NOPE
    return x2, ckv, p[:, kr:kr + ROPE_DIM], hf


def kernel(x_prompt, x_sample, c, cache_ckv, cache_krope, state_ssm, c_ctx, w_ada, b_ada, norm1_g, norm2_g, w_in, q_a_norm_g, kv_a_norm_g, w_uq, w_ukv, q_norm_g, k_norm_g, w_mla_o, conv_w, conv_b, w_conv_o, ssm_lam_re, ssm_lam_im, ssm_log_step, ssm_b_re, ssm_b_im, ssm_c_re, ssm_c_im, ssm_d, w_glu, pool_w, pool_scale, w_pool_o, w_o, w_mlp1, w_mlp2):
    prm = dict(norm1_g=norm1_g, norm2_g=norm2_g, w_in=w_in, q_a_norm_g=q_a_norm_g, kv_a_norm_g=kv_a_norm_g,
               w_uq=w_uq, w_ukv=w_ukv, q_norm_g=q_norm_g, k_norm_g=k_norm_g, w_mla_o=w_mla_o,
               conv_w=conv_w, conv_b=conv_b, w_conv_o=w_conv_o, ssm_lam_re=ssm_lam_re, ssm_lam_im=ssm_lam_im,
               ssm_log_step=ssm_log_step, ssm_b_re=ssm_b_re, ssm_b_im=ssm_b_im, ssm_c_re=ssm_c_re,
               ssm_c_im=ssm_c_im, ssm_d=ssm_d, w_glu=w_glu, pool_w=pool_w, pool_scale=pool_scale,
               w_pool_o=w_pool_o, w_o=w_o, w_mlp1=w_mlp1, w_mlp2=w_mlp2)
    depth = w_in.shape[0]
    bp, lp, d = x_prompt.shape
    bs, ls, _ = x_sample.shape
    past = cache_ckv.shape[2]
    g, n = state_ssm.shape[-2:]
    sizes = (w_uq.shape[1], w_ukv.shape[1], conv_w.shape[-1], ssm_d.shape[-1], pool_scale.shape[-1])

    rows = -(-(1 + bs) // SUBLANES) * SUBLANES
    cvec = jnp.zeros((rows, d), F32).at[0].set(c_ctx).at[1:1 + bs].set(c)
    mods = _ada_call(cvec, w_ada, b_ada)
    rope = _rope_tables(ls)
    ssm = _prep_ssm(ssm_lam_re, ssm_lam_im, ssm_log_step, ssm_b_re, ssm_b_im, ssm_c_re, ssm_c_im)
    big, cols = _prep_stacked(prm, sizes)

    yp = x_prompt.reshape(bp * lp, d)
    ys = x_sample.reshape(bs * ls, d)
    ckv_list, krope_list, ssm_list = [], [], []
    for l in range(depth):
        lw = _prep_layer(l, prm, sizes, big, ssm)
        mod_ctx = mods[l, 0:1].reshape(6, 1, d)
        mod_lat = mods[l, 1:1 + bs].reshape(bs * 6, 1, d)
        yp, ckv_l, krope_l, hf_l = _mixer_layer(yp, mod_ctx, lw, cols, bp, lp, False, None, None)
        ckv_list.append(ckv_l.reshape(bp, lp, -1))
        krope_list.append(krope_l.reshape(bp, lp, -1))
        ssm_list.append(_cols_to_states(hf_l, g, n))
        ctx_kpe = jnp.pad(cache_krope[:, l].reshape(bs * past, ROPE_DIM),
                          ((0, 0), (QK_NOPE, HEAD_PAD - QK_HEAD)))
        ctx = (cache_ckv[:, l].reshape(bs * past, -1), ctx_kpe, _states_to_cols(state_ssm[:, l]))
        ys, _, _, _ = _mixer_layer(ys, mod_lat, lw, cols, bs, ls, True, rope, ctx)
    return (yp.reshape(bp, lp, d), ys.reshape(bs, ls, d), jnp.stack(ckv_list, axis=1),
            jnp.stack(krope_list, axis=1), jnp.stack(ssm_list, axis=1))
```

```python
import functools
import math

import jax
import jax.numpy as jnp
from jax import lax
from jax.experimental import pallas as pl
from jax.experimental.pallas import tpu as pltpu

F32 = jnp.float32
BF16 = jnp.bfloat16

GRID_W = 64
N_BRANCH = 4
MLA_HEADS = 8
QK_NOPE = 64
ROPE_DIM = 32
QK_HEAD = QK_NOPE + ROPE_DIM
V_HEAD = 64
ROPE_THETA = 10000.0
SSM_GROUP_CH = 16
POOL_WINDOWS = (2, 4, 8, 16)
EPS = 1e-6
LOG2E = 1.4426950408889634

LANES = 128
SUBLANES = 8
HEAD_PAD = LANES
MIB = 1024 * 1024


def _cparams(sem, vmem_mib):
    return pltpu.CompilerParams(dimension_semantics=sem, vmem_limit_bytes=vmem_mib * MIB)


def _dot(a, b):
    return jnp.dot(a, b, preferred_element_type=F32)


def _dot_t(a, bt):
    return lax.dot_general(a, bt, (((1,), (1,)), ((), ())), preferred_element_type=F32)


def _mod_spec(chunk, d, tm, seq_len, per_batch):
    def imap(i, *_):
        row = (i * tm) // seq_len if per_batch else 0
        return (row * 6 + chunk, 0, 0)
    return pl.BlockSpec((1, 1, d), imap)


def _ada_kernel(c_ref, w_ref, b_ref, o_ref):
    cv = c_ref[...]
    s = cv * jax.nn.sigmoid(cv)
    o_ref[0] = _dot(s.astype(BF16), w_ref[0].astype(BF16)) + b_ref[0]


def _ada_call(cvec, w_ada, b_ada):
    depth, d, n = w_ada.shape
    rows = cvec.shape[0]
    tn = 1024
    return pl.pallas_call(
        _ada_kernel,
        out_shape=jax.ShapeDtypeStruct((depth, rows, n), F32),
        grid=(depth, n // tn),
        in_specs=[
            pl.BlockSpec((rows, d), lambda l, j: (0, 0)),
            pl.BlockSpec((1, d, tn), lambda l, j: (l, 0, j)),
            pl.BlockSpec((1, 1, tn), lambda l, j: (l, 0, j)),
        ],
        out_specs=pl.BlockSpec((1, rows, tn), lambda l, j: (l, 0, j)),
        compiler_params=_cparams(("parallel", "parallel"), 40),
        name="ada_mod",
    )(cvec, w_ada, b_ada.reshape(depth, 1, n))


def _modnorm(x, g, scale, shift):
    ms = jnp.mean(x * x, axis=-1, keepdims=True)
    return (x * lax.rsqrt(ms + EPS) * g) * (1.0 + scale) + shift


def _rms(x, g, width):
    ms = jnp.sum(x * x, axis=-1, keepdims=True) * (1.0 / width)
    return x * lax.rsqrt(ms + EPS) * g


def _heads_norm(x, g, mult, shared=None, rope=None):
    t_main = g * mult
    if rope is not None:
        xs, shared_s, gs, cos, sin = rope
        t_main = cos * t_main
        t_part = sin * (gs * mult)
    outs = []
    for h in range(MLA_HEADS):
        sl = slice(h * HEAD_PAD, (h + 1) * HEAD_PAD)
        xh = x[:, sl] if shared is None else x[:, sl] + shared
        ms = jnp.sum(xh * xh, axis=-1, keepdims=True) * (1.0 / QK_HEAD)
        y = xh * t_main
        if rope is not None:
            y = y + (shared_s if xs is None else xs[:, sl]) * t_part
        outs.append(y * lax.rsqrt(ms + EPS))
    return jnp.concatenate(outs, axis=1)


def _inproj_kernel(*refs, use_rope, q_mult, cols, n_keep):
    (x_ref, sh_ref, sc_ref, g_ref, wmix_ref, watt_ref, gqa_ref, gkva_ref, gq_ref, gk_ref,
     wuq_ref, wuk_ref, wuv_ref) = refs[:13]
    if use_rope:
        gqs_ref, gks_ref, wuqs_ref, c_ref, s_ref = refs[13:18]
    p_out, h_out, q_out, k_out, v_out, ckv_out = refs[-6:]
    att0 = cols["k_rope"]
    tm = x_ref.shape[0]
    nsub = 2

    for sub in range(nsub):
        rs = slice(sub * tm // nsub, (sub + 1) * tm // nsub)
        h = _modnorm(x_ref[rs, :], g_ref[...], sc_ref[0], sh_ref[0]).astype(BF16)
        h_out[rs, :] = h
        res = _dot_t(h, watt_ref[...])
        p_out[rs, :att0] = _dot_t(h, wmix_ref[...])
        p_out[rs, att0:] = res[:, :n_keep - att0]

        def col(name, width, res=res):
            return res[:, cols[name] - att0:cols[name] - att0 + width]

        rope_tabs = (c_ref[rs, :], s_ref[rs, :]) if use_rope else None
        qa = col("q_a", wuq_ref.shape[0])
        qa_n = _rms(qa, gqa_ref[...], qa.shape[-1]).astype(BF16)
        q = _dot(qa_n, wuq_ref[...])
        q_rope = (_dot(qa_n, wuqs_ref[...]), None, gqs_ref[...]) + rope_tabs if use_rope else None
        q_out[rs, :] = _heads_norm(q, gq_ref[...], q_mult, rope=q_rope).astype(BF16)

        kva = col("kv_a", wuk_ref.shape[0])
        ckv = _rms(kva, gkva_ref[...], kva.shape[-1])
        ckv_out[rs, :] = ckv
        ckv_b = ckv.astype(BF16)
        k = _dot(ckv_b, wuk_ref[...])
        k_rope = (None, col("k_rope_sw", HEAD_PAD), gks_ref[...]) + rope_tabs if use_rope else None
        k_out[rs, :] = _heads_norm(k, gk_ref[...], 1.0, shared=col("k_rope", HEAD_PAD), rope=k_rope).astype(BF16)
        v_out[rs, :] = _dot(ckv_b, wuv_ref[...]).astype(BF16)


def _inproj_call(x, mod, lw, rope, seq_len, per_batch, cols):
    t, d = x.shape
    att0 = cols["k_rope"]
    n_att = lw["w_att"].shape[1]
    n_keep = att0 + HEAD_PAD
    tm = 512
    hq = MLA_HEADS * HEAD_PAD
    hv = MLA_HEADS * V_HEAD
    qlora = lw["w_uq"].shape[0]
    kvlora = lw["w_uk"].shape[0]
    l = lw["l"]
    mspec = functools.partial(_mod_spec, d=d, tm=tm, seq_len=seq_len, per_batch=per_batch)
    const = lambda i: (0, 0)
    row = lambda i: (i, 0)
    in_specs = [
        pl.BlockSpec((tm, d), row), mspec(0), mspec(1),
        pl.BlockSpec((1, d), const),
        pl.BlockSpec((None, att0, d), lambda i: (l, 0, 0), pipeline_mode=pl.Buffered(1)),
        pl.BlockSpec((None, n_att, d), lambda i: (l, 0, 0), pipeline_mode=pl.Buffered(1)),
        pl.BlockSpec((1, qlora), const), pl.BlockSpec((1, kvlora), const),
        pl.BlockSpec((1, HEAD_PAD), const), pl.BlockSpec((1, HEAD_PAD), const),
        pl.BlockSpec((qlora, hq), const), pl.BlockSpec((kvlora, hq), const), pl.BlockSpec((kvlora, hv), const),
    ]
    args = [x, mod, mod, lw["norm1_g"], lw["w_tail"], lw["w_att"], lw["q_a_norm_g"], lw["kv_a_norm_g"], lw["q_norm_g"],
            lw["k_norm_g"], lw["w_uq"], lw["w_uk"], lw["w_uv"]]
    if rope is not None:
        nblk = seq_len // tm
        in_specs += [
            pl.BlockSpec((1, HEAD_PAD), const), pl.BlockSpec((1, HEAD_PAD), const),
            pl.BlockSpec((qlora, hq), const),
            pl.BlockSpec((tm, HEAD_PAD), lambda i: (i % nblk, 0)),
            pl.BlockSpec((tm, HEAD_PAD), lambda i: (i % nblk, 0)),
        ]
        args += [lw["q_norm_g_sw"], lw["k_norm_g_sw"], lw["w_uq_sw"]] + list(rope)
    q_mult = LOG2E * QK_HEAD ** -0.5
    return pl.pallas_call(
        functools.partial(_inproj_kernel, use_rope=rope is not None, q_mult=q_mult, cols=cols, n_keep=n_keep),
        out_shape=(jax.ShapeDtypeStruct((t, n_keep), F32), jax.ShapeDtypeStruct((t, d), BF16),
                   jax.ShapeDtypeStruct((t, hq), BF16), jax.ShapeDtypeStruct((t, hq), BF16),
                   jax.ShapeDtypeStruct((t, hv), BF16), jax.ShapeDtypeStruct((t, kvlora), F32)),
        grid=(t // tm,),
        in_specs=in_specs,
        out_specs=(pl.BlockSpec((tm, n_keep), row), pl.BlockSpec((tm, d), row),
                   pl.BlockSpec((tm, hq), row), pl.BlockSpec((tm, hq), row),
                   pl.BlockSpec((tm, hv), row), pl.BlockSpec((tm, kvlora), row)),
        compiler_params=_cparams(("parallel",), 56),
        name="in_proj",
    )(*args)


def _ctxprep_kernel(ckv_ref, kpe_ref, gk_ref, wuk_ref, wuv_ref, k_out, v_out):
    ckv_b = ckv_ref[...].astype(BF16)
    k = _dot(ckv_b, wuk_ref[...])
    k_out[...] = _heads_norm(k, gk_ref[...], 1.0, shared=kpe_ref[...]).astype(BF16)
    v_out[...] = _dot(ckv_b, wuv_ref[...]).astype(BF16)


def _ctxprep_call(ckv, kpe, lw):
    t, kvlora = ckv.shape
    tm = 512
    hq = MLA_HEADS * HEAD_PAD
    hv = MLA_HEADS * V_HEAD
    const = lambda i: (0, 0)
    return pl.pallas_call(
        _ctxprep_kernel,
        out_shape=(jax.ShapeDtypeStruct((t, hq), BF16), jax.ShapeDtypeStruct((t, hv), BF16)),
        grid=(t // tm,),
        in_specs=[
            pl.BlockSpec((tm, kvlora), lambda i: (i, 0)),
            pl.BlockSpec((tm, HEAD_PAD), lambda i: (i, 0)),
            pl.BlockSpec((1, HEAD_PAD), const),
            pl.BlockSpec((kvlora, hq), const),
            pl.BlockSpec((kvlora, hv), const),
        ],
        out_specs=(pl.BlockSpec((tm, hq), lambda i: (i, 0)), pl.BlockSpec((tm, hv), lambda i: (i, 0))),
        compiler_params=_cparams(("parallel",), 40),
        name="ctx_prep",
    )(ckv, kpe, lw["k_norm_g"], lw["w_uk"], lw["w_uv"])


def _attn_kernel(*refs, has_ctx, heads):
    if has_ctx:
        q_ref, ko_ref, vo_ref, kc_ref, vc_ref, o_ref = refs
    else:
        q_ref, ko_ref, vo_ref, o_ref = refs
    contract_last = (((1,), (1,)), ((), ()))
    lane = lax.broadcasted_iota(jnp.int32, (q_ref.shape[0], 2 * V_HEAD), 1)
    for pair in range(heads // 2):
        vsl = slice(pair * 2 * V_HEAD, (pair + 1) * 2 * V_HEAD)
        outs = []
        for hh in range(2):
            h = 2 * pair + hh
            sl = slice(h * HEAD_PAD, (h + 1) * HEAD_PAD)
            q = q_ref[:, sl]
            s_o = lax.dot_general(q, ko_ref[:, sl], contract_last, preferred_element_type=F32)
            m = jnp.max(s_o, axis=-1, keepdims=True)
            if has_ctx:
                s_c = lax.dot_general(q, kc_ref[:, sl], contract_last, preferred_element_type=F32)
                m = jnp.maximum(m, jnp.max(s_c, axis=-1, keepdims=True))
            p_o = jnp.exp2(s_o - m)
            l = jnp.sum(p_o, axis=-1, keepdims=True)
            acc = _dot(p_o.astype(BF16), vo_ref[:, vsl])
            if has_ctx:
                p_c = jnp.exp2(s_c - m)
                l = l + jnp.sum(p_c, axis=-1, keepdims=True)
                acc = acc + _dot(p_c.astype(BF16), vc_ref[:, vsl])
            outs.append(acc / l)
        o_ref[:, vsl] = jnp.where(lane < V_HEAD, outs[0], outs[1]).astype(BF16)


def _attn_call(q, k, v, kc, vc, batch, seq_len, ctx_len):
    t = q.shape[0]
    tq = min(seq_len, 512)
    heads = 8
    nq = seq_len // tq
    npair = MLA_HEADS // heads
    qw = heads * HEAD_PAD
    vw = heads * V_HEAD
    in_specs = [
        pl.BlockSpec((tq, qw), lambda b, h, i: (b * nq + i, h)),
        pl.BlockSpec((seq_len, qw), lambda b, h, i: (b, h)),
        pl.BlockSpec((seq_len, vw), lambda b, h, i: (b, h)),
    ]
    args = [q, k, v]
    if kc is not None:
        in_specs += [pl.BlockSpec((ctx_len, qw), lambda b, h, i: (b, h)),
                     pl.BlockSpec((ctx_len, vw), lambda b, h, i: (b, h))]
        args += [kc, vc]
    return pl.pallas_call(
        functools.partial(_attn_kernel, has_ctx=kc is not None, heads=heads),
        out_shape=jax.ShapeDtypeStruct((t, MLA_HEADS * V_HEAD), BF16),
        grid=(batch, npair, nq),
        in_specs=in_specs,
        out_specs=pl.BlockSpec((tq, vw), lambda b, h, i: (b * nq + i, h)),
        compiler_params=_cparams(("parallel", "parallel", "arbitrary"), 56),
        name="attention",
    )(*args)


def _convpool_kernel(cu_ref, bg_ref, cg_ref, pu_ref, cw_ref, cb_ref, pw_ref, ps_ref, zb_ref, zd_ref):
    n = cu_ref.shape[0]
    gps = pw_ref.shape[0]
    edge = max(POOL_WINDOWS) // 2
    assert edge % SUBLANES == 0 and n >= 4 * edge

    def shifts(rows, zero_fill):
        def down(x, k):
            y = pltpu.roll(x, k, axis=0)
            if zero_fill:
                y = jnp.where(lax.broadcasted_iota(jnp.int32, x.shape, 0) >= k, y, 0.0)
            return y

        def up(x, k):
            y = pltpu.roll(x, rows - k, axis=0)
            if zero_fill:
                y = jnp.where(lax.broadcasted_iota(jnp.int32, x.shape, 0) < rows - k, y, 0.0)
            return y
        return down, up

    def conv_rows(r0, rows, zero_fill):
        down, up = shifts(rows, zero_fill)
        v = cg_ref[r0:r0 + rows, :] * cu_ref[r0:r0 + rows, :]
        conv = (down(v, 1) * cw_ref[0:1, :] + v * cw_ref[1:2, :] + up(v, 1) * cw_ref[2:3, :]
                + cb_ref[...])
        return (bg_ref[r0:r0 + rows, :] * conv).astype(BF16)

    def pool_rows(r0, rows, zero_fill, gg, half):
        down, up = shifts(rows, zero_fill)
        sl = slice(gg * LANES, (gg + 1) * LANES)
        u = pu_ref[r0:r0 + rows, sl]
        fwd = u
        bwd = u
        k = 1
        while k < half:
            fwd = fwd + up(fwd, k)
            bwd = bwd + down(bwd, k)
            k *= 2
        total = fwd + down(bwd, 1)
        rowf = (lax.broadcasted_iota(jnp.int32, u.shape, 0) + r0).astype(F32)
        cnt = jnp.minimum(rowf + half, float(n)) - jnp.maximum(rowf - half, 0.0)
        mean = total / cnt - u
        return (_dot(mean.astype(BF16), pw_ref[gg]) * ps_ref[:, sl]).astype(BF16)

    zb_ref[...] = conv_rows(0, n, False)
    zb_ref[0:edge, :] = conv_rows(0, 2 * edge, True)[0:edge]
    zb_ref[n - edge:n, :] = conv_rows(n - 2 * edge, 2 * edge, True)[edge:]

    step = pl.program_id(1)
    for first in range(0, len(POOL_WINDOWS), gps):
        @pl.when(step * gps == first)
        def _(first=first):
            for gg in range(gps):
                sl = slice(gg * LANES, (gg + 1) * LANES)
                half = POOL_WINDOWS[first + gg] // 2
                zd_ref[:, sl] = pool_rows(0, n, False, gg, half)
                zd_ref[0:edge, sl] = pool_rows(0, 2 * edge, True, gg, half)[0:edge]
                zd_ref[n - edge:n, sl] = pool_rows(n - 2 * edge, 2 * edge, True, gg, half)[edge:]


def _convpool_call(p, lw, batch, seq_len, cols):
    t = p.shape[0]
    gps = 2
    cw = gps * LANES
    nblk = lw["conv_w"].shape[1] // cw

    def pspec(name):
        base = cols[name] // cw
        return pl.BlockSpec((seq_len, cw), lambda b, g: (b, base + g))

    vec = lambda b, g: (0, g)
    return pl.pallas_call(
        _convpool_kernel,
        out_shape=(jax.ShapeDtypeStruct((t, nblk * cw), BF16), jax.ShapeDtypeStruct((t, nblk * cw), BF16)),
        grid=(batch, nblk),
        in_specs=[pspec("conv_u"), pspec("conv_bg"), pspec("conv_cg"), pspec("pool_u"),
                  pl.BlockSpec((3, cw), vec), pl.BlockSpec((1, cw), vec),
                  pl.BlockSpec((gps, LANES, LANES), lambda b, g: (g, 0, 0)), pl.BlockSpec((1, cw), vec)],
        out_specs=(pl.BlockSpec((seq_len, cw), lambda b, g: (b, g)),
                   pl.BlockSpec((seq_len, cw), lambda b, g: (b, g))),
        compiler_params=_cparams(("parallel", "parallel"), 40),
        name="conv_pool",
    )(p, p, p, p, lw["conv_w"], lw["conv_b"], lw["pool_w"], lw["pool_scale"])


SSM_R = 4
SSM_GB = LANES // SSM_GROUP_CH


def _ssm_kernel(*refs, steps, nchunks, nblk):
    u_refs = refs[:nblk]
    h0_ref, a_ref, mb_ref, m1_ref, m2_ref = refs[nblk:nblk + 5]
    y_refs = refs[nblk + 5:2 * nblk + 5]
    hf_ref, slab_scr, xs_scr, st_scr = refs[2 * nblk + 5:]
    d = pl.program_id(0)
    c = pl.program_id(2)
    width = st_scr.shape[1]
    bw = width // nblk

    @pl.when(c == 0)
    def _():
        st_scr[...] = h0_ref[0]

    for blk in range(nblk):
        for j in range(SSM_R):
            for b in range(SUBLANES):
                slab_scr[blk * SSM_R + j, pl.ds(b, steps, stride=SUBLANES), :] = (
                    u_refs[blk][b, pl.ds(j, steps, stride=SSM_R), :])

    def packed(blk):
        return jnp.concatenate([slab_scr[blk * SSM_R + j] for j in range(SSM_R)], axis=1).astype(BF16)

    for blk in range(nblk):
        xs_scr[:, blk * bw:(blk + 1) * bw] = _dot(packed(blk), mb_ref[0, blk])

    gw = width // 2
    for part in range(width // gw):
        offs = []
        for blk in range(part * nblk // 2, (part + 1) * nblk // 2):
            offs += [(blk * bw + k * LANES, blk * bw + bw // 2 + k * LANES) for k in range(bw // 2 // LANES)]
        coef = [(a_ref[0, :, r0:r0 + LANES], a_ref[0, :, i0:i0 + LANES]) for r0, i0 in offs]

        def body(t, carry, offs=offs, coef=coef):
            tt = t + d * (steps - 1 - 2 * t)
            row0 = pl.multiple_of(tt * SUBLANES, SUBLANES)
            new = []
            for (r0, i0), (a_re, a_im), x_re, x_im in zip(offs, coef, carry[0::2], carry[1::2]):
                n_re = (a_re * x_re - a_im * x_im) + xs_scr[pl.ds(row0, SUBLANES), r0:r0 + LANES]
                n_im = (a_re * x_im + a_im * x_re) + xs_scr[pl.ds(row0, SUBLANES), i0:i0 + LANES]
                xs_scr[pl.ds(row0, SUBLANES), r0:r0 + LANES] = x_re
                xs_scr[pl.ds(row0, SUBLANES), i0:i0 + LANES] = x_im
                new += [n_re, n_im]
            return tuple(new)

        init = []
        for r0, i0 in offs:
            init += [st_scr[:, r0:r0 + LANES], st_scr[:, i0:i0 + LANES]]
        final = lax.fori_loop(0, steps, body, tuple(init), unroll=2)
        for (r0, i0), x_re, x_im in zip(offs, final[0::2], final[1::2]):
            st_scr[:, r0:r0 + LANES] = x_re
            st_scr[:, i0:i0 + LANES] = x_im

    for blk in range(nblk):
        y = (_dot(xs_scr[:, blk * bw:(blk + 1) * bw].astype(BF16), m1_ref[0, blk])
             + _dot(packed(blk), m2_ref[0, blk]))
        for j in range(SSM_R):
            slab_scr[blk * SSM_R + j] = y[:, j * LANES:(j + 1) * LANES]
        for j in range(SSM_R):
            for b in range(SUBLANES):
                y_refs[blk][0, b, pl.ds(j, steps, stride=SSM_R), :] = (
                    slab_scr[blk * SSM_R + j, pl.ds(b, steps, stride=SUBLANES), :])

    @pl.when(c == nchunks - 1)
    def _():
        hf_ref[0] = st_scr[...]


def _ssm_call(p3, h0, lw, col0):
    batch, seq_len, _ = p3.shape
    width = h0.shape[-1]
    nblk = lw["ssm_mb"].shape[2]
    l = lw["l"]
    steps = min(seq_len // SSM_R, 64)
    tokens = steps * SSM_R
    nchunks = seq_len // tokens
    nbg = batch // SUBLANES
    rows = steps * SUBLANES

    def cidx(d, c):
        return c + d * (nchunks - 1 - 2 * c)

    def uspec(blk):
        return pl.BlockSpec((SUBLANES, tokens, LANES), lambda d, b, c: (b, cidx(d, c), col0 // LANES + blk))

    def wspec(arr):
        return pl.BlockSpec((None, 1) + arr.shape[2:], lambda d, b, c: (l, d, 0, 0, 0),
                            pipeline_mode=pl.Buffered(1))

    yshape = jax.ShapeDtypeStruct((2, batch, seq_len, LANES), F32)
    yspec = pl.BlockSpec((1, SUBLANES, tokens, LANES), lambda d, b, c: (d, b, cidx(d, c), 0))
    outs = pl.pallas_call(
        functools.partial(_ssm_kernel, steps=steps, nchunks=nchunks, nblk=nblk),
        out_shape=(yshape,) * nblk + (jax.ShapeDtypeStruct((2, batch, width), F32),),
        grid=(2, nbg, nchunks),
        in_specs=[uspec(blk) for blk in range(nblk)] + [
            pl.BlockSpec((1, SUBLANES, width), lambda d, b, c: (d, b, 0)),
            pl.BlockSpec((None, 1, SUBLANES, width), lambda d, b, c: (l, d, 0, 0)),
            wspec(lw["ssm_mb"]), wspec(lw["ssm_m1"]), wspec(lw["ssm_m2"]),
        ],
        out_specs=(yspec,) * nblk + (pl.BlockSpec((1, SUBLANES, width), lambda d, b, c: (d, b, 0)),),
        scratch_shapes=[pltpu.VMEM((nblk * SSM_R, rows, LANES), F32), pltpu.VMEM((rows, width), F32),
                        pltpu.VMEM((SUBLANES, width), F32)],
        compiler_params=_cparams(("arbitrary", "arbitrary", "arbitrary"), 52),
        name="s5_scan",
    )(*([p3] * nblk), h0, lw["ssm_a"], lw["ssm_mb"], lw["ssm_m1"], lw["ssm_m2"])
    return outs[:nblk], outs[nblk]


def _merge_kernel(*refs, nblk):
    (h_ref, za_ref, zb_ref, su_ref, sd_ref, zd_ref, wg0, wg1, wg2, wg3, wa, wb, wca, wcg, wd) = refs[:15]
    yf_refs = refs[15:15 + nblk]
    yb_refs = refs[15 + nblk:15 + 2 * nblk]
    o_ref, y_scr = refs[15 + 2 * nblk:]

    @pl.when(pl.program_id(1) == 0)
    def _():
        y_both = jnp.concatenate([f[0] + b[0] for f, b in zip(yf_refs, yb_refs)], axis=1)
        y_scr[...] = (y_both + sd_ref[...] * su_ref[...]).astype(BF16)

    tm = h_ref.shape[0]
    nsub = 2
    for sub in range(nsub):
        rs = slice(sub * tm // nsub, (sub + 1) * tm // nsub)
        h = h_ref[rs, :]
        y = y_scr[rs, :]

        def gate(w_ref, h=h):
            return jax.nn.sigmoid(_dot_t(h, w_ref[...]))

        merged = gate(wg0) * _dot(za_ref[rs, :], wa[...])
        merged = merged + gate(wg1) * _dot(zb_ref[rs, :], wb[...])
        merged = merged + gate(wg2) * (_dot(y, wca[...]) * jax.nn.sigmoid(_dot(y, wcg[...])))
        merged = merged + gate(wg3) * _dot(zd_ref[rs, :], wd[...])
        o_ref[rs, :] = merged.astype(BF16)


def _merge_call(h, p, za, zb, ys, zd, lw, cols):
    t, d = h.shape
    bw = za.shape[1]
    tm, tn = 512, 512
    nj = d // tn
    nblk = len(ys)
    row = lambda i, j: (i, 0)
    su_blk = cols["ssm_u"] // bw

    l = lw["l"]

    gate_blk0 = cols["k_rope"] // tn

    def gspec(k):
        return pl.BlockSpec((None, tn, d), lambda i, j: (l, gate_blk0 + k * nj + j, 0))

    def yspec(direction):
        return pl.BlockSpec((1, tm, LANES), lambda i, j: (direction, i, 0))

    wcol = pl.BlockSpec((None, bw, tn), lambda i, j: (l, 0, j))
    in_specs = [
        pl.BlockSpec((tm, d), row),
        pl.BlockSpec((tm, bw), row), pl.BlockSpec((tm, bw), row),
        pl.BlockSpec((tm, bw), lambda i, j: (i, su_blk)),
        pl.BlockSpec((1, bw), lambda i, j: (0, 0)),
        pl.BlockSpec((tm, bw), row),
        gspec(0), gspec(1), gspec(2), gspec(3),
        wcol, wcol, wcol, pl.BlockSpec((None, bw, tn), lambda i, j: (l, 0, nj + j)), wcol,
    ] + [yspec(0)] * nblk + [yspec(1)] * nblk
    wg = lw["w_tail"]
    return pl.pallas_call(
        functools.partial(_merge_kernel, nblk=nblk),
        out_shape=jax.ShapeDtypeStruct((t, d), BF16),
        grid=(t // tm, nj),
        in_specs=in_specs,
        out_specs=pl.BlockSpec((tm, tn), lambda i, j: (i, j)),
        scratch_shapes=[pltpu.VMEM((tm, bw), BF16)],
        compiler_params=_cparams(("parallel", "arbitrary"), 48),
        name="branch_merge",
    )(h, za, zb, p, lw["ssm_d"], zd,
      wg, wg, wg, wg, lw["w_mla_o"], lw["w_conv_o"], lw["w_glu"], lw["w_glu"], lw["w_pool_o"], *ys, *ys)


def _outproj_kernel(x_ref, m_ref, gt_ref, sh_ref, sc_ref, g_ref, wo_ref, x1_ref, h2_ref):
    tm = x_ref.shape[0]
    nsub = 2
    for sub in range(nsub):
        rs = slice(sub * tm // nsub, (sub + 1) * tm // nsub)
        x1 = x_ref[rs, :] + gt_ref[0] * _dot(m_ref[rs, :], wo_ref[...])
        x1_ref[rs, :] = x1
        h2_ref[rs, :] = _modnorm(x1, g_ref[...], sc_ref[0], sh_ref[0]).astype(BF16)


def _outproj_call(x, merged, mod, lw, seq_len, per_batch):
    t, d = x.shape
    tm = 512
    mspec = functools.partial(_mod_spec, d=d, tm=tm, seq_len=seq_len, per_batch=per_batch)
    row = lambda i: (i, 0)
    return pl.pallas_call(
        _outproj_kernel,
        out_shape=(jax.ShapeDtypeStruct((t, d), F32), jax.ShapeDtypeStruct((t, d), BF16)),
        grid=(t // tm,),
        in_specs=[
            pl.BlockSpec((tm, d), row), pl.BlockSpec((tm, d), row),
            mspec(2), mspec(3), mspec(4),
            pl.BlockSpec((1, d), lambda i: (0, 0)),
            pl.BlockSpec((None, d, d), lambda i: (lw["l"], 0, 0), pipeline_mode=pl.Buffered(1)),
        ],
        out_specs=(pl.BlockSpec((tm, d), row), pl.BlockSpec((tm, d), row)),
        compiler_params=_cparams(("parallel",), 48),
        name="out_proj",
    )(x, merged, mod, mod, mod, lw["norm2_g"], lw["w_o"])


def _mlp_kernel(x_ref, h_ref, gt_ref, w1_ref, w2_ref, o_ref):
    @pl.when(pl.program_id(1) == 0)
    def _():
        o_ref[...] = x_ref[...]

    hid = jnp.square(jnp.maximum(_dot(h_ref[...], w1_ref[...]), 0.0))
    o_ref[...] += gt_ref[0] * _dot(hid.astype(BF16), w2_ref[...])


def _mlp_call(x, h, mod, lw, seq_len, per_batch):
    t, d = x.shape
    hidden = lw["w_mlp1"].shape[2]
    tm, tk = 512, 1024
    l = lw["l"]
    mspec = functools.partial(_mod_spec, d=d, tm=tm, seq_len=seq_len, per_batch=per_batch)
    row = lambda i, k: (i, 0)
    return pl.pallas_call(
        _mlp_kernel,
        out_shape=jax.ShapeDtypeStruct((t, d), F32),
        grid=(t // tm, hidden // tk),
        in_specs=[
            pl.BlockSpec((tm, d), row), pl.BlockSpec((tm, d), row), mspec(5),
            pl.BlockSpec((None, d, tk), lambda i, k: (l, 0, k)),
            pl.BlockSpec((None, tk, d), lambda i, k: (l, k, 0)),
        ],
        out_specs=pl.BlockSpec((tm, d), row),
        compiler_params=_cparams(("parallel", "arbitrary"), 48),
        name="mlp",
    )(x, h, mod, lw["w_mlp1"], lw["w_mlp2"])


def _pad_heads(w, per_head, lo):
    lead = w.shape[:-1]
    w = w.reshape(lead + (MLA_HEADS, per_head))
    w = jnp.pad(w, [(0, 0)] * len(lead) + [(0, 0), (lo, HEAD_PAD - lo - per_head)])
    return w.reshape(lead + (MLA_HEADS * HEAD_PAD,))


_ROPE_PARTNER = tuple((i // 16) * 16 + (i % 16 + 8) % 16 for i in range(ROPE_DIM))


def _swap_heads(w):
    lead = w.shape[:-1]
    w = w.reshape(lead + (MLA_HEADS, QK_HEAD))[..., QK_NOPE:][..., jnp.array(_ROPE_PARTNER)]
    w = jnp.pad(w, [(0, 0)] * len(lead) + [(0, 0), (QK_NOPE, HEAD_PAD - QK_HEAD)])
    return w.reshape(lead + (MLA_HEADS * HEAD_PAD,))


def _prep_ssm(lam_re, lam_im, log_step, b_re, b_im, c_re, c_im):
    r, gb = SSM_R, SSM_GB
    nl, ndir, g, n, cg = b_re.shape
    ns = nl * ndir
    nblk = g // gb
    flat = lambda v: v.reshape((ns,) + v.shape[2:])
    lam_re, lam_im, log_step, b_re, b_im, c_re, c_im = map(
        flat, (lam_re, lam_im, log_step, b_re, b_im, c_re, c_im))
    step = jnp.exp(log_step)[..., None]
    pw = []
    for p in range(r + 1):
        mag = jnp.exp(p * (lam_re * step))
        pw.append((mag * jnp.cos(p * (lam_im * step)), mag * jnp.sin(p * (lam_im * step))))
    ar, ai = pw[1]
    den = lam_re * lam_re + lam_im * lam_im
    qr = ((ar - 1.0) * lam_re + ai * lam_im) / den
    qi = (ai * lam_re - (ar - 1.0) * lam_im) / den
    bb_re = qr[..., None] * b_re - qi[..., None] * b_im
    bb_im = qr[..., None] * b_im + qi[..., None] * b_re
    fwd = (jnp.arange(ns) % ndir == 0)[:, None, None]

    def power(p_fwd, p_bwd):
        return (jnp.where(fwd, pw[p_fwd][0], pw[p_bwd][0]), jnp.where(fwd, pw[p_fwd][1], pw[p_bwd][1]))

    rows_g = (jnp.arange(r * gb * cg) // cg) % gb
    cols_g = jnp.arange(gb * n) // n

    t_re, t_im = [], []
    for j in range(r):
        pr, pi = power(r - 1 - j, j)
        t_re.append(pr[..., None] * bb_re - pi[..., None] * bb_im)
        t_im.append(pr[..., None] * bb_im + pi[..., None] * bb_re)
    mask_b = (rows_g[:, None] == cols_g[None, :]).astype(F32)

    def inc_block(ts):
        t = jnp.stack(ts, axis=1).reshape(ns, r, nblk, gb, n, cg)
        t = jnp.transpose(t, (0, 2, 1, 3, 5, 4)).reshape(ns, nblk, r * gb * cg, n)
        return jnp.tile(t, (1, 1, 1, gb)) * mask_b

    mb = jnp.concatenate([inc_block(t_re), inc_block(t_im)], axis=-1)

    cl_re, cl_im = [], []
    for j in range(r):
        pr, pi = power(j + 1, r - j)
        cl_re.append(c_re * pr[:, :, None, :] - c_im * pi[:, :, None, :])
        cl_im.append(c_re * pi[:, :, None, :] + c_im * pr[:, :, None, :])

    def out_block(ts):
        t = jnp.stack(ts, axis=1).reshape(ns, r, nblk, gb, cg, n)
        t = jnp.transpose(t, (0, 2, 5, 1, 3, 4)).reshape(ns, nblk, n, r * gb * cg)
        return jnp.tile(t, (1, 1, gb, 1)) * mask_b.T

    m1 = jnp.concatenate([out_block(cl_re), -out_block(cl_im)], axis=2)

    hi = lax.Precision.HIGHEST
    lane_g = jnp.arange(gb * cg) // cg
    mask_k = (lane_g[:, None] == lane_g[None, :]).astype(F32)
    kd = []
    for p in range(r):
        pr, pi = pw[p]
        k2 = (jnp.einsum("sgcn,sgnk->sgck", c_re * pr[:, :, None, :] - c_im * pi[:, :, None, :], bb_re, precision=hi)
              - jnp.einsum("sgcn,sgnk->sgck", c_re * pi[:, :, None, :] + c_im * pr[:, :, None, :], bb_im, precision=hi))
        k2 = jnp.swapaxes(k2, -1, -2).reshape(ns, nblk, gb * cg, cg)
        kd.append(jnp.tile(k2, (1, 1, 1, gb)) * mask_k)
    fwd4 = fwd[..., None]
    zero = jnp.zeros_like(kd[0])
    m2_rows = []
    for i in range(r):
        blocks = []
        for j in range(r):
            if i == j:
                blocks.append(kd[0])
            elif j > i:
                blocks.append(jnp.where(fwd4, kd[j - i], zero))
            else:
                blocks.append(jnp.where(fwd4, zero, kd[i - j]))
        m2_rows.append(jnp.concatenate(blocks, axis=-1))
    m2 = jnp.concatenate(m2_rows, axis=2)

    a = jnp.stack([pw[r][0].reshape(ns, nblk, gb * n), pw[r][1].reshape(ns, nblk, gb * n)], axis=2)
    a = jnp.broadcast_to(a.reshape(ns, 1, 2 * g * n), (ns, SUBLANES, 2 * g * n))
    unflat = lambda v: v.reshape((nl, ndir) + v.shape[1:])
    return unflat(a), unflat(mb.astype(BF16)), unflat(m1.astype(BF16)), unflat(m2.astype(BF16))


def _prep_stacked(prm, sizes):
    q_lora, kv_lora, conv_w, ssm_w, pool_w_ = sizes
    w_in_t = jnp.swapaxes(prm["w_in"], 1, 2)
    tail0 = q_lora + kv_lora + ROPE_DIM
    w_tail = w_in_t[:, tail0:, :].astype(BF16)
    head = w_in_t[:, :tail0, :]
    cols = {}
    o = 0
    for name, width in (("conv_u", conv_w), ("conv_bg", conv_w), ("conv_cg", conv_w), ("ssm_u", ssm_w),
                        ("pool_u", pool_w_), ("k_rope", HEAD_PAD), ("k_rope_sw", HEAD_PAD), ("q_a", q_lora),
                        ("kv_a", kv_lora)):
        cols[name] = o
        o += width
    k_rope = head[:, q_lora + kv_lora:tail0, :]
    rope_lanes = ((0, 0), (QK_NOPE, HEAD_PAD - QK_HEAD), (0, 0))
    w_att = jnp.concatenate([jnp.pad(k_rope, rope_lanes),
                             jnp.pad(k_rope[:, jnp.array(_ROPE_PARTNER), :], rope_lanes),
                             head[:, :q_lora + kv_lora, :]], axis=1).astype(BF16)
    big = dict(w_tail=w_tail, w_att=w_att)
    for name in ("w_mla_o", "w_conv_o", "w_glu", "w_pool_o", "w_o", "w_mlp1", "w_mlp2"):
        big[name] = prm[name].astype(BF16)
    return big, cols


def _prep_layer(l, prm, sizes, big, ssm):
    kv_lora = sizes[1]
    w_ukv = prm["w_ukv"][l].reshape(kv_lora, MLA_HEADS, QK_NOPE + V_HEAD)
    w_uk = _pad_heads(w_ukv[:, :, :QK_NOPE].reshape(kv_lora, -1), QK_NOPE, 0)
    w_uv = w_ukv[:, :, QK_NOPE:].reshape(kv_lora, -1)
    ssm_a, ssm_mb, ssm_m1, ssm_m2 = ssm

    row = lambda v: v.reshape(1, -1)
    head_gain = lambda g: row(jnp.pad(g, (0, HEAD_PAD - QK_HEAD)))
    partner_gain = lambda g: row(jnp.pad(g[QK_NOPE:][jnp.array(_ROPE_PARTNER)], (QK_NOPE, HEAD_PAD - QK_HEAD)))
    lw = dict(
        big, l=l,
        norm1_g=row(prm["norm1_g"][l]), norm2_g=row(prm["norm2_g"][l]),
        q_a_norm_g=row(prm["q_a_norm_g"][l]), kv_a_norm_g=row(prm["kv_a_norm_g"][l]),
        q_norm_g=head_gain(prm["q_norm_g"][l]), k_norm_g=head_gain(prm["k_norm_g"][l]),
        q_norm_g_sw=partner_gain(prm["q_norm_g"][l]), k_norm_g_sw=partner_gain(prm["k_norm_g"][l]),
        w_uq=_pad_heads(prm["w_uq"][l], QK_HEAD, 0).astype(BF16),
        w_uq_sw=_swap_heads(prm["w_uq"][l]).astype(BF16),
        w_uk=w_uk.astype(BF16), w_uv=w_uv.astype(BF16),
        conv_w=prm["conv_w"][l], conv_b=row(prm["conv_b"][l]),
        ssm_a=ssm_a, ssm_mb=ssm_mb, ssm_m1=ssm_m1, ssm_m2=ssm_m2,
        ssm_d=row(prm["ssm_d"][l]),
        pool_w=prm["pool_w"][l].astype(BF16), pool_scale=row(prm["pool_scale"][l]),
    )
    return lw


def _rope_tables(seq_len):
    half = ROPE_DIM // 2
    inv_freq = ROPE_THETA ** (-jnp.arange(0, half, 2, dtype=F32) / half)
    t = jnp.arange(seq_len)
    ang_r = (t // GRID_W).astype(F32)[:, None] * inv_freq
    ang_c = (t % GRID_W).astype(F32)[:, None] * inv_freq
    ones_lo = jnp.ones((seq_len, QK_NOPE), F32)
    zeros_lo = jnp.zeros((seq_len, QK_NOPE), F32)
    tail = jnp.zeros((seq_len, HEAD_PAD - QK_HEAD), F32)
    c = jnp.concatenate([ones_lo, jnp.cos(ang_r), jnp.cos(ang_r), jnp.cos(ang_c), jnp.cos(ang_c), tail], axis=1)
    s = jnp.concatenate([zeros_lo, -jnp.sin(ang_r), jnp.sin(ang_r), -jnp.sin(ang_c), jnp.sin(ang_c), tail], axis=1)
    return c, s


def _states_to_cols(st):
    b, ndir, _, g, n = st.shape
    st = st.reshape(b, ndir, 2, g // SSM_GB, SSM_GB * n)
    return jnp.transpose(st, (1, 0, 3, 2, 4)).reshape(ndir, b, 2 * g * n)


def _cols_to_states(hf, g, n):
    ndir, b, _ = hf.shape
    hf = hf.reshape(ndir, b, g // SSM_GB, 2, SSM_GB * n)
    return jnp.transpose(hf, (1, 0, 3, 2, 4)).reshape(b, ndir, 2, g, n)


def _mixer_layer(x, mod, lw, cols, batch, seq_len, per_batch, rope, ctx):
    p, h1, q, k, v, ckv = _inproj_call(x, mod, lw, rope, seq_len, per_batch, cols)
    if ctx is not None:
        ctx_ckv, ctx_kpe, h0 = ctx
        kc, vc = _ctxprep_call(ctx_ckv, ctx_kpe, lw)
        ctx_len = ctx_ckv.shape[0] // batch
    else:
        kc = vc = None
        ctx_len = 0
        h0 = jnp.zeros((2, batch, lw["ssm_a"].shape[-1]), F32)
    za = _attn_call(q, k, v, kc, vc, batch, seq_len, ctx_len)
    zb, zd = _convpool_call(p, lw, batch, seq_len, cols)

    ys, hf = _ssm_call(p.reshape(batch, seq_len, -1), h0, lw, cols["ssm_u"])
    ys = [y.reshape(2, batch * seq_len, LANES) for y in ys]
    merged = _merge_call(h1, p, za, zb, ys, zd, lw, cols)
    x1, h2 = _outproj_call(x, merged, mod, lw, seq_len, per_batch)
    x2 = _mlp_call(x1, h2, mod, lw, seq_len, per_batch)
    kr = cols["k_rope"] + QK_NOPE
    return x2, ckv, p[:, kr:kr + ROPE_DIM], hf


def kernel(x_prompt, x_sample, c, cache_ckv, cache_krope, state_ssm, c_ctx, w_ada, b_ada, norm1_g, norm2_g, w_in, q_a_norm_g, kv_a_norm_g, w_uq, w_ukv, q_norm_g, k_norm_g, w_mla_o, conv_w, conv_b, w_conv_o, ssm_lam_re, ssm_lam_im, ssm_log_step, ssm_b_re, ssm_b_im, ssm_c_re, ssm_c_im, ssm_d, w_glu, pool_w, pool_scale, w_pool_o, w_o, w_mlp1, w_mlp2):
    prm = dict(norm1_g=norm1_g, norm2_g=norm2_g, w_in=w_in, q_a_norm_g=q_a_norm_g, kv_a_norm_g=kv_a_norm_g,
               w_uq=w_uq, w_ukv=w_ukv, q_norm_g=q_norm_g, k_norm_g=k_norm_g, w_mla_o=w_mla_o,
               conv_w=conv_w, conv_b=conv_b, w_conv_o=w_conv_o, ssm_lam_re=ssm_lam_re, ssm_lam_im=ssm_lam_im,
               ssm_log_step=ssm_log_step, ssm_b_re=ssm_b_re, ssm_b_im=ssm_b_im, ssm_c_re=ssm_c_re,
               ssm_c_im=ssm_c_im, ssm_d=ssm_d, w_glu=w_glu, pool_w=pool_w, pool_scale=pool_scale,
               w_pool_o=w_pool_o, w_o=w_o, w_mlp1=w_mlp1, w_mlp2=w_mlp2)
    depth = w_in.shape[0]
    bp, lp, d = x_prompt.shape
    bs, ls, _ = x_sample.shape
    past = cache_ckv.shape[2]
    g, n = state_ssm.shape[-2:]
    sizes = (w_uq.shape[1], w_ukv.shape[1], conv_w.shape[-1], ssm_d.shape[-1], pool_scale.shape[-1])

    rows = -(-(1 + bs) // SUBLANES) * SUBLANES
    cvec = jnp.zeros((rows, d), F32).at[0].set(c_ctx).at[1:1 + bs].set(c)
    mods = _ada_call(cvec, w_ada, b_ada)
    rope = _rope_tables(ls)
    ssm = _prep_ssm(ssm_lam_re, ssm_lam_im, ssm_log_step, ssm_b_re, ssm_b_im, ssm_c_re, ssm_c_im)
    big, cols = _prep_stacked(prm, sizes)

    yp = x_prompt.reshape(bp * lp, d)
    ys = x_sample.reshape(bs * ls, d)
    ckv_list, krope_list, ssm_list = [], [], []
    for l in range(depth):
        lw = _prep_layer(l, prm, sizes, big, ssm)
        mod_ctx = mods[l, 0:1].reshape(6, 1, d)
        mod_lat = mods[l, 1:1 + bs].reshape(bs * 6, 1, d)
        yp, ckv_l, krope_l, hf_l = _mixer_layer(yp, mod_ctx, lw, cols, bp, lp, False, None, None)
        ckv_list.append(ckv_l.reshape(bp, lp, -1))
        krope_list.append(krope_l.reshape(bp, lp, -1))
        ssm_list.append(_cols_to_states(hf_l, g, n))
        ctx_kpe = jnp.pad(cache_krope[:, l].reshape(bs * past, ROPE_DIM),
                          ((0, 0), (QK_NOPE, HEAD_PAD - QK_HEAD)))
        ctx = (cache_ckv[:, l].reshape(bs * past, -1), ctx_kpe, _states_to_cols(state_ssm[:, l]))
        ys, _, _, _ = _mixer_layer(ys, mod_lat, lw, cols, bs, ls, True, rope, ctx)
    return (yp.reshape(bp, lp, d), ys.reshape(bs, ls, d), jnp.stack(ckv_list, axis=1),
            jnp.stack(krope_list, axis=1), jnp.stack(ssm_list, axis=1))
```

```python
import functools

import jax
import jax.numpy as jnp
from jax import lax
from jax.experimental import pallas as pl
from jax.experimental.pallas import tpu as pltpu

F32 = jnp.float32
BF16 = jnp.bfloat16

GRID_W = 64
MLA_HEADS = 8
QK_NOPE = 64
ROPE_DIM = 32
QK_HEAD = QK_NOPE + ROPE_DIM
V_HEAD = 64
ROPE_THETA = 10000.0
SSM_GROUP_CH = 16
POOL_WINDOWS = (2, 4, 8, 16)
EPS = 1e-6
LOG2E = 1.4426950408889634

LANES = 128
SUBLANES = 8
HEAD_PAD = LANES
MIB = 1024 * 1024


def _cparams(sem, vmem_mib):
    return pltpu.CompilerParams(dimension_semantics=sem, vmem_limit_bytes=vmem_mib * MIB)


def _dot(a, b):
    return jnp.dot(a, b, preferred_element_type=F32)


def _dot_t(a, bt):
    return lax.dot_general(a, bt, (((1,), (1,)), ((), ())), preferred_element_type=F32)


def _mod_spec(chunk, d, tm, seq_len, per_batch):
    def imap(i, *_):
        row = (i * tm) // seq_len if per_batch else 0
        return (row * 6 + chunk, 0, 0)
    return pl.BlockSpec((1, 1, d), imap)


def _ada_kernel(c_ref, w_ref, b_ref, o_ref):
    cv = c_ref[...]
    s = cv * jax.nn.sigmoid(cv)
    o_ref[0] = _dot(s.astype(BF16), w_ref[0].astype(BF16)) + b_ref[0]


def _ada_call(cvec, w_ada, b_ada):
    depth, d, n = w_ada.shape
    rows = cvec.shape[0]
    tn = 1024
    return pl.pallas_call(
        _ada_kernel,
        out_shape=jax.ShapeDtypeStruct((depth, rows, n), F32),
        grid=(depth, n // tn),
        in_specs=[
            pl.BlockSpec((rows, d), lambda l, j: (0, 0)),
            pl.BlockSpec((1, d, tn), lambda l, j: (l, 0, j)),
            pl.BlockSpec((1, 1, tn), lambda l, j: (l, 0, j)),
        ],
        out_specs=pl.BlockSpec((1, rows, tn), lambda l, j: (l, 0, j)),
        compiler_params=_cparams(("parallel", "parallel"), 40),
        name="ada_mod",
    )(cvec, w_ada, b_ada.reshape(depth, 1, n))


def _modnorm(x, g, scale, shift):
    ms = jnp.mean(x * x, axis=-1, keepdims=True)
    return (x * lax.rsqrt(ms + EPS) * g) * (1.0 + scale) + shift


def _rms(x, g, width):
    ms = jnp.sum(x * x, axis=-1, keepdims=True) * (1.0 / width)
    return x * lax.rsqrt(ms + EPS) * g


def _heads_norm(x, g, mult, shared=None, rope=None):
    t_main = g * mult
    if rope is not None:
        xs, shared_s, gs, cos, sin = rope
        t_main = cos * t_main
        t_part = sin * (gs * mult)
    outs = []
    for h in range(MLA_HEADS):
        sl = slice(h * HEAD_PAD, (h + 1) * HEAD_PAD)
        xh = x[:, sl] if shared is None else x[:, sl] + shared
        ms = jnp.sum(xh * xh, axis=-1, keepdims=True) * (1.0 / QK_HEAD)
        y = xh * t_main
        if rope is not None:
            y = y + (shared_s if xs is None else xs[:, sl]) * t_part
        outs.append(y * lax.rsqrt(ms + EPS))
    return jnp.concatenate(outs, axis=1)


def _inproj_kernel(*refs, use_rope, q_mult, cols, n_keep):
    (x_ref, sh_ref, sc_ref, g_ref, wmix_ref, watt_ref, gqa_ref, gkva_ref, gq_ref, gk_ref,
     wuq_ref, wuk_ref, wuv_ref) = refs[:13]
    if use_rope:
        gqs_ref, gks_ref, wuqs_ref, c_ref, s_ref = refs[13:18]
    p_out, h_out, q_out, k_out, v_out, ckv_out = refs[-6:]
    att0 = cols["k_rope"]
    tm = x_ref.shape[0]
    nsub = 2

    for sub in range(nsub):
        rs = slice(sub * tm // nsub, (sub + 1) * tm // nsub)
        h = _modnorm(x_ref[rs, :], g_ref[...], sc_ref[0], sh_ref[0]).astype(BF16)
        h_out[rs, :] = h
        res = _dot_t(h, watt_ref[...])
        p_out[rs, :att0] = _dot_t(h, wmix_ref[...])
        p_out[rs, att0:] = res[:, :n_keep - att0]

        def col(name, width, res=res):
            return res[:, cols[name] - att0:cols[name] - att0 + width]

        rope_tabs = (c_ref[rs, :], s_ref[rs, :]) if use_rope else None
        qa = col("q_a", wuq_ref.shape[0])
        qa_n = _rms(qa, gqa_ref[...], qa.shape[-1]).astype(BF16)
        q = _dot(qa_n, wuq_ref[...])
        q_rope = (_dot(qa_n, wuqs_ref[...]), None, gqs_ref[...]) + rope_tabs if use_rope else None
        q_out[rs, :] = _heads_norm(q, gq_ref[...], q_mult, rope=q_rope).astype(BF16)

        kva = col("kv_a", wuk_ref.shape[0])
        ckv = _rms(kva, gkva_ref[...], kva.shape[-1])
        ckv_out[rs, :] = ckv
        ckv_b = ckv.astype(BF16)
        k = _dot(ckv_b, wuk_ref[...])
        k_rope = (None, col("k_rope_sw", HEAD_PAD), gks_ref[...]) + rope_tabs if use_rope else None
        k_out[rs, :] = _heads_norm(k, gk_ref[...], 1.0, shared=col("k_rope", HEAD_PAD), rope=k_rope).astype(BF16)
        v_out[rs, :] = _dot(ckv_b, wuv_ref[...]).astype(BF16)


def _inproj_call(x, mod, lw, rope, seq_len, per_batch, cols):
    t, d = x.shape
    att0 = cols["k_rope"]
    n_att = lw["w_att"].shape[1]
    n_keep = att0 + HEAD_PAD
    tm = 512
    hq = MLA_HEADS * HEAD_PAD
    hv = MLA_HEADS * V_HEAD
    qlora = lw["w_uq"].shape[0]
    kvlora = lw["w_uk"].shape[0]
    l = lw["l"]
    mspec = functools.partial(_mod_spec, d=d, tm=tm, seq_len=seq_len, per_batch=per_batch)
    const = lambda i: (0, 0)
    row = lambda i: (i, 0)
    in_specs = [
        pl.BlockSpec((tm, d), row), mspec(0), mspec(1),
        pl.BlockSpec((1, d), const),
        pl.BlockSpec((None, att0, d), lambda i: (l, 0, 0), pipeline_mode=pl.Buffered(1)),
        pl.BlockSpec((None, n_att, d), lambda i: (l, 0, 0), pipeline_mode=pl.Buffered(1)),
        pl.BlockSpec((1, qlora), const), pl.BlockSpec((1, kvlora), const),
        pl.BlockSpec((1, HEAD_PAD), const), pl.BlockSpec((1, HEAD_PAD), const),
        pl.BlockSpec((qlora, hq), const), pl.BlockSpec((kvlora, hq), const), pl.BlockSpec((kvlora, hv), const),
    ]
    args = [x, mod, mod, lw["norm1_g"], lw["w_tail"], lw["w_att"], lw["q_a_norm_g"], lw["kv_a_norm_g"], lw["q_norm_g"],
            lw["k_norm_g"], lw["w_uq"], lw["w_uk"], lw["w_uv"]]
    if rope is not None:
        nblk = seq_len // tm
        in_specs += [
            pl.BlockSpec((1, HEAD_PAD), const), pl.BlockSpec((1, HEAD_PAD), const),
            pl.BlockSpec((qlora, hq), const),
            pl.BlockSpec((tm, HEAD_PAD), lambda i: (i % nblk, 0)),
            pl.BlockSpec((tm, HEAD_PAD), lambda i: (i % nblk, 0)),
        ]
        args += [lw["q_norm_g_sw"], lw["k_norm_g_sw"], lw["w_uq_sw"]] + list(rope)
    q_mult = LOG2E * QK_HEAD ** -0.5
    return pl.pallas_call(
        functools.partial(_inproj_kernel, use_rope=rope is not None, q_mult=q_mult, cols=cols, n_keep=n_keep),
        out_shape=(jax.ShapeDtypeStruct((t, n_keep), F32), jax.ShapeDtypeStruct((t, d), BF16),
                   jax.ShapeDtypeStruct((t, hq), BF16), jax.ShapeDtypeStruct((t, hq), BF16),
                   jax.ShapeDtypeStruct((t, hv), BF16), jax.ShapeDtypeStruct((t, kvlora), F32)),
        grid=(t // tm,),
        in_specs=in_specs,
        out_specs=(pl.BlockSpec((tm, n_keep), row), pl.BlockSpec((tm, d), row),
                   pl.BlockSpec((tm, hq), row), pl.BlockSpec((tm, hq), row),
                   pl.BlockSpec((tm, hv), row), pl.BlockSpec((tm, kvlora), row)),
        compiler_params=_cparams(("parallel",), 56),
        name="in_proj",
    )(*args)


def _ctxprep_kernel(ckv_ref, kpe_ref, gk_ref, wuk_ref, wuv_ref, k_out, v_out):
    ckv_b = ckv_ref[...].astype(BF16)
    k = _dot(ckv_b, wuk_ref[...])
    k_out[...] = _heads_norm(k, gk_ref[...], 1.0, shared=kpe_ref[...]).astype(BF16)
    v_out[...] = _dot(ckv_b, wuv_ref[...]).astype(BF16)


def _ctxprep_call(ckv, kpe, lw):
    t, kvlora = ckv.shape
    tm = 512
    hq = MLA_HEADS * HEAD_PAD
    hv = MLA_HEADS * V_HEAD
    const = lambda i: (0, 0)
    return pl.pallas_call(
        _ctxprep_kernel,
        out_shape=(jax.ShapeDtypeStruct((t, hq), BF16), jax.ShapeDtypeStruct((t, hv), BF16)),
        grid=(t // tm,),
        in_specs=[
            pl.BlockSpec((tm, kvlora), lambda i: (i, 0)),
            pl.BlockSpec((tm, HEAD_PAD), lambda i: (i, 0)),
            pl.BlockSpec((1, HEAD_PAD), const),
            pl.BlockSpec((kvlora, hq), const),
            pl.BlockSpec((kvlora, hv), const),
        ],
        out_specs=(pl.BlockSpec((tm, hq), lambda i: (i, 0)), pl.BlockSpec((tm, hv), lambda i: (i, 0))),
        compiler_params=_cparams(("parallel",), 40),
        name="ctx_prep",
    )(ckv, kpe, lw["k_norm_g"], lw["w_uk"], lw["w_uv"])


def _attn_kernel(*refs, has_ctx, heads):
    if has_ctx:
        q_ref, ko_ref, vo_ref, kc_ref, vc_ref, o_ref = refs
    else:
        q_ref, ko_ref, vo_ref, o_ref = refs
    contract_last = (((1,), (1,)), ((), ()))
    lane = lax.broadcasted_iota(jnp.int32, (q_ref.shape[0], 2 * V_HEAD), 1)
    for pair in range(heads // 2):
        vsl = slice(pair * 2 * V_HEAD, (pair + 1) * 2 * V_HEAD)
        outs = []
        for hh in range(2):
            h = 2 * pair + hh
            sl = slice(h * HEAD_PAD, (h + 1) * HEAD_PAD)
            q = q_ref[:, sl]
            s_o = lax.dot_general(q, ko_ref[:, sl], contract_last, preferred_element_type=F32)
            m = jnp.max(s_o, axis=-1, keepdims=True)
            if has_ctx:
                s_c = lax.dot_general(q, kc_ref[:, sl], contract_last, preferred_element_type=F32)
                m = jnp.maximum(m, jnp.max(s_c, axis=-1, keepdims=True))
            p_o = jnp.exp2(s_o - m)
            l = jnp.sum(p_o, axis=-1, keepdims=True)
            acc = _dot(p_o.astype(BF16), vo_ref[:, vsl])
            if has_ctx:
                p_c = jnp.exp2(s_c - m)
                l = l + jnp.sum(p_c, axis=-1, keepdims=True)
                acc = acc + _dot(p_c.astype(BF16), vc_ref[:, vsl])
            outs.append(acc / l)
        o_ref[:, vsl] = jnp.where(lane < V_HEAD, outs[0], outs[1]).astype(BF16)


def _attn_call(q, k, v, kc, vc, batch, seq_len, ctx_len):
    t = q.shape[0]
    tq = min(seq_len, 512)
    heads = 8
    nq = seq_len // tq
    npair = MLA_HEADS // heads
    qw = heads * HEAD_PAD
    vw = heads * V_HEAD
    in_specs = [
        pl.BlockSpec((tq, qw), lambda b, h, i: (b * nq + i, h)),
        pl.BlockSpec((seq_len, qw), lambda b, h, i: (b, h)),
        pl.BlockSpec((seq_len, vw), lambda b, h, i: (b, h)),
    ]
    args = [q, k, v]
    if kc is not None:
        in_specs += [pl.BlockSpec((ctx_len, qw), lambda b, h, i: (b, h)),
                     pl.BlockSpec((ctx_len, vw), lambda b, h, i: (b, h))]
        args += [kc, vc]
    return pl.pallas_call(
        functools.partial(_attn_kernel, has_ctx=kc is not None, heads=heads),
        out_shape=jax.ShapeDtypeStruct((t, MLA_HEADS * V_HEAD), BF16),
        grid=(batch, npair, nq),
        in_specs=in_specs,
        out_specs=pl.BlockSpec((tq, vw), lambda b, h, i: (b * nq + i, h)),
        compiler_params=_cparams(("parallel", "parallel", "arbitrary"), 56),
        name="attention",
    )(*args)


def _convpool_kernel(cu_ref, bg_ref, cg_ref, pu_ref, cw_ref, cb_ref, pw_ref, ps_ref, zb_ref, zd_ref):
    n = cu_ref.shape[0]
    gps = pw_ref.shape[0]
    edge = max(POOL_WINDOWS) // 2
    assert edge % SUBLANES == 0 and n >= 4 * edge

    def shifts(rows, zero_fill):
        def down(x, k):
            y = pltpu.roll(x, k, axis=0)
            if zero_fill:
                y = jnp.where(lax.broadcasted_iota(jnp.int32, x.shape, 0) >= k, y, 0.0)
            return y

        def up(x, k):
            y = pltpu.roll(x, rows - k, axis=0)
            if zero_fill:
                y = jnp.where(lax.broadcasted_iota(jnp.int32, x.shape, 0) < rows - k, y, 0.0)
            return y
        return down, up

    def conv_rows(r0, rows, zero_fill):
        down, up = shifts(rows, zero_fill)
        v = cg_ref[r0:r0 + rows, :] * cu_ref[r0:r0 + rows, :]
        conv = (down(v, 1) * cw_ref[0:1, :] + v * cw_ref[1:2, :] + up(v, 1) * cw_ref[2:3, :]
                + cb_ref[...])
        return (bg_ref[r0:r0 + rows, :] * conv).astype(BF16)

    def pool_rows(r0, rows, zero_fill, gg, half):
        down, up = shifts(rows, zero_fill)
        sl = slice(gg * LANES, (gg + 1) * LANES)
        u = pu_ref[r0:r0 + rows, sl]
        fwd = u
        bwd = u
        k = 1
        while k < half:
            fwd = fwd + up(fwd, k)
            bwd = bwd + down(bwd, k)
            k *= 2
        total = fwd + down(bwd, 1)
        rowf = (lax.broadcasted_iota(jnp.int32, u.shape, 0) + r0).astype(F32)
        cnt = jnp.minimum(rowf + half, float(n)) - jnp.maximum(rowf - half, 0.0)
        mean = total / cnt - u
        return (_dot(mean.astype(BF16), pw_ref[gg]) * ps_ref[:, sl]).astype(BF16)

    zb_ref[...] = conv_rows(0, n, False)
    zb_ref[0:edge, :] = conv_rows(0, 2 * edge, True)[0:edge]
    zb_ref[n - edge:n, :] = conv_rows(n - 2 * edge, 2 * edge, True)[edge:]

    step = pl.program_id(1)
    for first in range(0, len(POOL_WINDOWS), gps):
        @pl.when(step * gps == first)
        def _(first=first):
            for gg in range(gps):
                sl = slice(gg * LANES, (gg + 1) * LANES)
                half = POOL_WINDOWS[first + gg] // 2
                zd_ref[:, sl] = pool_rows(0, n, False, gg, half)
                zd_ref[0:edge, sl] = pool_rows(0, 2 * edge, True, gg, half)[0:edge]
                zd_ref[n - edge:n, sl] = pool_rows(n - 2 * edge, 2 * edge, True, gg, half)[edge:]


def _convpool_call(p, lw, batch, seq_len, cols):
    t = p.shape[0]
    gps = 2
    cw = gps * LANES
    nblk = lw["conv_w"].shape[1] // cw

    def pspec(name):
        base = cols[name] // cw
        return pl.BlockSpec((seq_len, cw), lambda b, g: (b, base + g))

    vec = lambda b, g: (0, g)
    return pl.pallas_call(
        _convpool_kernel,
        out_shape=(jax.ShapeDtypeStruct((t, nblk * cw), BF16), jax.ShapeDtypeStruct((t, nblk * cw), BF16)),
        grid=(batch, nblk),
        in_specs=[pspec("conv_u"), pspec("conv_bg"), pspec("conv_cg"), pspec("pool_u"),
                  pl.BlockSpec((3, cw), vec), pl.BlockSpec((1, cw), vec),
                  pl.BlockSpec((gps, LANES, LANES), lambda b, g: (g, 0, 0)), pl.BlockSpec((1, cw), vec)],
        out_specs=(pl.BlockSpec((seq_len, cw), lambda b, g: (b, g)),
                   pl.BlockSpec((seq_len, cw), lambda b, g: (b, g))),
        compiler_params=_cparams(("parallel", "parallel"), 40),
        name="conv_pool",
    )(p, p, p, p, lw["conv_w"], lw["conv_b"], lw["pool_w"], lw["pool_scale"])


SSM_R = 4
SSM_GB = LANES // SSM_GROUP_CH


def _ssm_kernel(*refs, steps, nchunks, nblk):
    u_refs = refs[:nblk]
    h0_ref, a_ref, mb_ref, m1_ref, m2_ref = refs[nblk:nblk + 5]
    y_refs = refs[nblk + 5:2 * nblk + 5]
    hf_ref, slab_scr, xs_scr, st_scr = refs[2 * nblk + 5:]
    d = pl.program_id(0)
    c = pl.program_id(2)
    width = st_scr.shape[1]
    bw = width // nblk

    @pl.when(c == 0)
    def _():
        st_scr[...] = h0_ref[0]

    for blk in range(nblk):
        for j in range(SSM_R):
            for b in range(SUBLANES):
                slab_scr[blk * SSM_R + j, pl.ds(b, steps, stride=SUBLANES), :] = (
                    u_refs[blk][b, pl.ds(j, steps, stride=SSM_R), :])

    def packed(blk):
        return jnp.concatenate([slab_scr[blk * SSM_R + j] for j in range(SSM_R)], axis=1).astype(BF16)

    for blk in range(nblk):
        xs_scr[:, blk * bw:(blk + 1) * bw] = _dot(packed(blk), mb_ref[0, blk])

    gw = width // 2
    for part in range(width // gw):
        offs = []
        for blk in range(part * nblk // 2, (part + 1) * nblk // 2):
            offs += [(blk * bw + k * LANES, blk * bw + bw // 2 + k * LANES) for k in range(bw // 2 // LANES)]
        coef = [(a_ref[0, :, r0:r0 + LANES], a_ref[0, :, i0:i0 + LANES]) for r0, i0 in offs]

        def body(t, carry, offs=offs, coef=coef):
            tt = t + d * (steps - 1 - 2 * t)
            row0 = pl.multiple_of(tt * SUBLANES, SUBLANES)
            new = []
            for (r0, i0), (a_re, a_im), x_re, x_im in zip(offs, coef, carry[0::2], carry[1::2]):
                n_re = (a_re * x_re - a_im * x_im) + xs_scr[pl.ds(row0, SUBLANES), r0:r0 + LANES]
                n_im = (a_re * x_im + a_im * x_re) + xs_scr[pl.ds(row0, SUBLANES), i0:i0 + LANES]
                xs_scr[pl.ds(row0, SUBLANES), r0:r0 + LANES] = x_re
                xs_scr[pl.ds(row0, SUBLANES), i0:i0 + LANES] = x_im
                new += [n_re, n_im]
            return tuple(new)

        init = []
        for r0, i0 in offs:
            init += [st_scr[:, r0:r0 + LANES], st_scr[:, i0:i0 + LANES]]
        final = lax.fori_loop(0, steps, body, tuple(init), unroll=2)
        for (r0, i0), x_re, x_im in zip(offs, final[0::2], final[1::2]):
            st_scr[:, r0:r0 + LANES] = x_re
            st_scr[:, i0:i0 + LANES] = x_im

    for blk in range(nblk):
        y = (_dot(xs_scr[:, blk * bw:(blk + 1) * bw].astype(BF16), m1_ref[0, blk])
             + _dot(packed(blk), m2_ref[0, blk]))
        for j in range(SSM_R):
            slab_scr[blk * SSM_R + j] = y[:, j * LANES:(j + 1) * LANES]
        for j in range(SSM_R):
            for b in range(SUBLANES):
                y_refs[blk][0, b, pl.ds(j, steps, stride=SSM_R), :] = (
                    slab_scr[blk * SSM_R + j, pl.ds(b, steps, stride=SUBLANES), :])

    @pl.when(c == nchunks - 1)
    def _():
        hf_ref[0] = st_scr[...]


def _ssm_call(p3, h0, lw, col0):
    batch, seq_len, _ = p3.shape
    width = h0.shape[-1]
    nblk = lw["ssm_mb"].shape[2]
    l = lw["l"]
    steps = min(seq_len // SSM_R, 64)
    tokens = steps * SSM_R
    nchunks = seq_len // tokens
    nbg = batch // SUBLANES
    rows = steps * SUBLANES

    def cidx(d, c):
        return c + d * (nchunks - 1 - 2 * c)

    def uspec(blk):
        return pl.BlockSpec((SUBLANES, tokens, LANES), lambda d, b, c: (b, cidx(d, c), col0 // LANES + blk))

    def wspec(arr):
        return pl.BlockSpec((None, 1) + arr.shape[2:], lambda d, b, c: (l, d, 0, 0, 0),
                            pipeline_mode=pl.Buffered(1))

    yshape = jax.ShapeDtypeStruct((2, batch, seq_len, LANES), F32)
    yspec = pl.BlockSpec((1, SUBLANES, tokens, LANES), lambda d, b, c: (d, b, cidx(d, c), 0))
    outs = pl.pallas_call(
        functools.partial(_ssm_kernel, steps=steps, nchunks=nchunks, nblk=nblk),
        out_shape=(yshape,) * nblk + (jax.ShapeDtypeStruct((2, batch, width), F32),),
        grid=(2, nbg, nchunks),
        in_specs=[uspec(blk) for blk in range(nblk)] + [
            pl.BlockSpec((1, SUBLANES, width), lambda d, b, c: (d, b, 0)),
            pl.BlockSpec((None, 1, SUBLANES, width), lambda d, b, c: (l, d, 0, 0)),
            wspec(lw["ssm_mb"]), wspec(lw["ssm_m1"]), wspec(lw["ssm_m2"]),
        ],
        out_specs=(yspec,) * nblk + (pl.BlockSpec((1, SUBLANES, width), lambda d, b, c: (d, b, 0)),),
        scratch_shapes=[pltpu.VMEM((nblk * SSM_R, rows, LANES), F32), pltpu.VMEM((rows, width), F32),
                        pltpu.VMEM((SUBLANES, width), F32)],
        compiler_params=_cparams(("arbitrary", "arbitrary", "arbitrary"), 52),
        name="s5_scan",
    )(*([p3] * nblk), h0, lw["ssm_a"], lw["ssm_mb"], lw["ssm_m1"], lw["ssm_m2"])
    return outs[:nblk], outs[nblk]


def _merge_kernel(*refs, nblk, ncast):
    (h_ref, za_ref, zb_ref, su_ref, sd_ref, zd_ref, wg0, wg1, wg2, wg3, wa, wb, wca, wcg, wd) = refs[:15]
    yf_refs = refs[15:15 + nblk]
    yb_refs = refs[15 + nblk:15 + 2 * nblk]
    cast_in = refs[15 + 2 * nblk:15 + 2 * nblk + ncast]
    o_ref = refs[15 + 2 * nblk + ncast]
    cast_out = refs[16 + 2 * nblk + ncast:16 + 2 * nblk + 2 * ncast]
    y_scr = refs[-1]

    for src, dst in zip(cast_in, cast_out):
        dst[...] = src[...].astype(BF16)

    @pl.when(pl.program_id(1) == 0)
    def _():
        y_both = jnp.concatenate([f[0] + b[0] for f, b in zip(yf_refs, yb_refs)], axis=1)
        y_scr[...] = (y_both + sd_ref[...] * su_ref[...]).astype(BF16)

    tm = h_ref.shape[0]
    nsub = 2
    for sub in range(nsub):
        rs = slice(sub * tm // nsub, (sub + 1) * tm // nsub)
        h = h_ref[rs, :]
        y = y_scr[rs, :]

        def gate(w_ref, h=h):
            return jax.nn.sigmoid(_dot_t(h, w_ref[...]))

        merged = gate(wg0) * _dot(za_ref[rs, :], wa[...])
        merged = merged + gate(wg1) * _dot(zb_ref[rs, :], wb[...])
        merged = merged + gate(wg2) * (_dot(y, wca[...]) * jax.nn.sigmoid(_dot(y, wcg[...])))
        merged = merged + gate(wg3) * _dot(zd_ref[rs, :], wd[...])
        o_ref[rs, :] = merged.astype(BF16)


def _merge_call(h, p, za, zb, ys, zd, lw, cols, cast=()):
    t, d = h.shape
    bw = za.shape[1]
    tm, tn = 512, 512
    nj = d // tn
    nblk = len(ys)
    row = lambda i, j: (i, 0)
    su_blk = cols["ssm_u"] // bw

    l = lw["l"]

    gate_blk0 = cols["k_rope"] // tn

    def gspec(k):
        return pl.BlockSpec((None, tn, d), lambda i, j: (l, gate_blk0 + k * nj + j, 0))

    def yspec(direction):
        return pl.BlockSpec((1, tm, LANES), lambda i, j: (direction, i, 0))

    wcol = pl.BlockSpec((None, bw, tn), lambda i, j: (l, 0, j))
    in_specs = [
        pl.BlockSpec((tm, d), row),
        pl.BlockSpec((tm, bw), row), pl.BlockSpec((tm, bw), row),
        pl.BlockSpec((tm, bw), lambda i, j: (i, su_blk)),
        pl.BlockSpec((1, bw), lambda i, j: (0, 0)),
        pl.BlockSpec((tm, bw), row),
        gspec(0), gspec(1), gspec(2), gspec(3),
        wcol, wcol, wcol, pl.BlockSpec((None, bw, tn), lambda i, j: (l, 0, nj + j)), wcol,
    ] + [yspec(0)] * nblk + [yspec(1)] * nblk
    out_shape = [jax.ShapeDtypeStruct((t, d), BF16)]
    out_specs = [pl.BlockSpec((tm, tn), lambda i, j: (i, j))]
    nsteps = (t // tm) * nj
    for w in cast:
        rows, n = w.shape[1] // nsteps, w.shape[2]
        assert rows * nsteps == w.shape[1] and rows % (2 * SUBLANES) == 0
        in_specs.append(pl.BlockSpec((None, rows, n), lambda i, j: (l, i * nj + j, 0)))
        out_shape.append(jax.ShapeDtypeStruct(w.shape[1:], BF16))
        out_specs.append(pl.BlockSpec((rows, n), lambda i, j: (i * nj + j, 0)))
    wg = lw["w_tail"]
    outs = pl.pallas_call(
        functools.partial(_merge_kernel, nblk=nblk, ncast=len(cast)),
        out_shape=tuple(out_shape),
        grid=(t // tm, nj),
        in_specs=in_specs,
        out_specs=tuple(out_specs),
        scratch_shapes=[pltpu.VMEM((tm, bw), BF16)],
        compiler_params=_cparams(("parallel", "arbitrary"), 48),
        name="branch_merge",
    )(h, za, zb, p, lw["ssm_d"], zd,
      wg, wg, wg, wg, lw["w_mla_o"], lw["w_conv_o"], lw["w_glu"], lw["w_glu"], lw["w_pool_o"], *ys, *ys, *cast)
    return outs[0], tuple(outs[1:])


def _outproj_kernel(x_ref, m_ref, gt_ref, sh_ref, sc_ref, g_ref, wo_ref, x1_ref, h2_ref):
    tm = x_ref.shape[0]
    nsub = 2
    for sub in range(nsub):
        rs = slice(sub * tm // nsub, (sub + 1) * tm // nsub)
        x1 = x_ref[rs, :] + gt_ref[0] * _dot(m_ref[rs, :], wo_ref[...])
        x1_ref[rs, :] = x1
        h2_ref[rs, :] = _modnorm(x1, g_ref[...], sc_ref[0], sh_ref[0]).astype(BF16)


def _outproj_call(x, merged, mod, lw, seq_len, per_batch):
    t, d = x.shape
    tm = 512
    mspec = functools.partial(_mod_spec, d=d, tm=tm, seq_len=seq_len, per_batch=per_batch)
    row = lambda i: (i, 0)
    return pl.pallas_call(
        _outproj_kernel,
        out_shape=(jax.ShapeDtypeStruct((t, d), F32), jax.ShapeDtypeStruct((t, d), BF16)),
        grid=(t // tm,),
        in_specs=[
            pl.BlockSpec((tm, d), row), pl.BlockSpec((tm, d), row),
            mspec(2), mspec(3), mspec(4),
            pl.BlockSpec((1, d), lambda i: (0, 0)),
            pl.BlockSpec((None, d, d), lambda i: (lw["l"], 0, 0), pipeline_mode=pl.Buffered(1)),
        ],
        out_specs=(pl.BlockSpec((tm, d), row), pl.BlockSpec((tm, d), row)),
        compiler_params=_cparams(("parallel",), 48),
        name="out_proj",
    )(x, merged, mod, mod, mod, lw["norm2_g"], lw["w_o"])


def _mlp_kernel(x_ref, h_ref, gt_ref, w1_ref, w2_ref, o_ref):
    @pl.when(pl.program_id(1) == 0)
    def _():
        o_ref[...] = x_ref[...]

    hid = jnp.square(jnp.maximum(_dot(h_ref[...], w1_ref[...]), 0.0))
    o_ref[...] += gt_ref[0] * _dot(hid.astype(BF16), w2_ref[...])


def _mlp_call(x, h, mod, lw, seq_len, per_batch):
    t, d = x.shape
    hidden = lw["w_mlp1"].shape[2]
    tm, tk = 512, 1024
    l = lw["l_mlp"]
    mspec = functools.partial(_mod_spec, d=d, tm=tm, seq_len=seq_len, per_batch=per_batch)
    row = lambda i, k: (i, 0)
    return pl.pallas_call(
        _mlp_kernel,
        out_shape=jax.ShapeDtypeStruct((t, d), F32),
        grid=(t // tm, hidden // tk),
        in_specs=[
            pl.BlockSpec((tm, d), row), pl.BlockSpec((tm, d), row), mspec(5),
            pl.BlockSpec((None, d, tk), lambda i, k: (l, 0, k)),
            pl.BlockSpec((None, tk, d), lambda i, k: (l, k, 0)),
        ],
        out_specs=pl.BlockSpec((tm, d), row),
        compiler_params=_cparams(("parallel", "arbitrary"), 48),
        name="mlp",
    )(x, h, mod, lw["w_mlp1"], lw["w_mlp2"])


def _pad_heads(w, per_head, lo):
    lead = w.shape[:-1]
    w = w.reshape(lead + (MLA_HEADS, per_head))
    w = jnp.pad(w, [(0, 0)] * len(lead) + [(0, 0), (lo, HEAD_PAD - lo - per_head)])
    return w.reshape(lead + (MLA_HEADS * HEAD_PAD,))


_ROPE_PARTNER = tuple((i // 16) * 16 + (i % 16 + 8) % 16 for i in range(ROPE_DIM))


def _swap_heads(w):
    lead = w.shape[:-1]
    w = w.reshape(lead + (MLA_HEADS, QK_HEAD))[..., QK_NOPE:][..., jnp.array(_ROPE_PARTNER)]
    w = jnp.pad(w, [(0, 0)] * len(lead) + [(0, 0), (QK_NOPE, HEAD_PAD - QK_HEAD)])
    return w.reshape(lead + (MLA_HEADS * HEAD_PAD,))


def _prep_ssm(lam_re, lam_im, log_step, b_re, b_im, c_re, c_im):
    r, gb = SSM_R, SSM_GB
    nl, ndir, g, n, cg = b_re.shape
    ns = nl * ndir
    nblk = g // gb
    flat = lambda v: v.reshape((ns,) + v.shape[2:])
    lam_re, lam_im, log_step, b_re, b_im, c_re, c_im = map(
        flat, (lam_re, lam_im, log_step, b_re, b_im, c_re, c_im))
    step = jnp.exp(log_step)[..., None]
    pw = []
    for p in range(r + 1):
        mag = jnp.exp(p * (lam_re * step))
        pw.append((mag * jnp.cos(p * (lam_im * step)), mag * jnp.sin(p * (lam_im * step))))
    ar, ai = pw[1]
    den = lam_re * lam_re + lam_im * lam_im
    qr = ((ar - 1.0) * lam_re + ai * lam_im) / den
    qi = (ai * lam_re - (ar - 1.0) * lam_im) / den
    bb_re = qr[..., None] * b_re - qi[..., None] * b_im
    bb_im = qr[..., None] * b_im + qi[..., None] * b_re
    fwd = (jnp.arange(ns) % ndir == 0)[:, None, None]

    def power(p_fwd, p_bwd):
        return (jnp.where(fwd, pw[p_fwd][0], pw[p_bwd][0]), jnp.where(fwd, pw[p_fwd][1], pw[p_bwd][1]))

    rows_g = (jnp.arange(r * gb * cg) // cg) % gb
    cols_g = jnp.arange(gb * n) // n

    t_re, t_im = [], []
    for j in range(r):
        pr, pi = power(r - 1 - j, j)
        t_re.append(pr[..., None] * bb_re - pi[..., None] * bb_im)
        t_im.append(pr[..., None] * bb_im + pi[..., None] * bb_re)
    mask_b = (rows_g[:, None] == cols_g[None, :]).astype(F32)

    def inc_block(ts):
        t = jnp.stack(ts, axis=1).reshape(ns, r, nblk, gb, n, cg)
        t = jnp.transpose(t, (0, 2, 1, 3, 5, 4)).reshape(ns, nblk, r * gb * cg, n)
        return jnp.tile(t, (1, 1, 1, gb)) * mask_b

    mb = jnp.concatenate([inc_block(t_re), inc_block(t_im)], axis=-1)

    cl_re, cl_im = [], []
    for j in range(r):
        pr, pi = power(j + 1, r - j)
        cl_re.append(c_re * pr[:, :, None, :] - c_im * pi[:, :, None, :])
        cl_im.append(c_re * pi[:, :, None, :] + c_im * pr[:, :, None, :])

    def out_block(ts):
        t = jnp.stack(ts, axis=1).reshape(ns, r, nblk, gb, cg, n)
        t = jnp.transpose(t, (0, 2, 5, 1, 3, 4)).reshape(ns, nblk, n, r * gb * cg)
        return jnp.tile(t, (1, 1, gb, 1)) * mask_b.T

    m1 = jnp.concatenate([out_block(cl_re), -out_block(cl_im)], axis=2)

    hi = lax.Precision.HIGHEST
    lane_g = jnp.arange(gb * cg) // cg
    mask_k = (lane_g[:, None] == lane_g[None, :]).astype(F32)
    kd = []
    for p in range(r):
        pr, pi = pw[p]
        k2 = (jnp.einsum("sgcn,sgnk->sgck", c_re * pr[:, :, None, :] - c_im * pi[:, :, None, :], bb_re, precision=hi)
              - jnp.einsum("sgcn,sgnk->sgck", c_re * pi[:, :, None, :] + c_im * pr[:, :, None, :], bb_im, precision=hi))
        k2 = jnp.swapaxes(k2, -1, -2).reshape(ns, nblk, gb * cg, cg)
        kd.append(jnp.tile(k2, (1, 1, 1, gb)) * mask_k)
    fwd4 = fwd[..., None]
    zero = jnp.zeros_like(kd[0])
    m2_rows = []
    for i in range(r):
        blocks = []
        for j in range(r):
            if i == j:
                blocks.append(kd[0])
            elif j > i:
                blocks.append(jnp.where(fwd4, kd[j - i], zero))
            else:
                blocks.append(jnp.where(fwd4, zero, kd[i - j]))
        m2_rows.append(jnp.concatenate(blocks, axis=-1))
    m2 = jnp.concatenate(m2_rows, axis=2)

    a = jnp.stack([pw[r][0].reshape(ns, nblk, gb * n), pw[r][1].reshape(ns, nblk, gb * n)], axis=2)
    a = jnp.broadcast_to(a.reshape(ns, 1, 2 * g * n), (ns, SUBLANES, 2 * g * n))
    unflat = lambda v: v.reshape((nl, ndir) + v.shape[1:])
    return unflat(a), unflat(mb.astype(BF16)), unflat(m1.astype(BF16)), unflat(m2.astype(BF16))


def _prep_stacked(prm, sizes):
    q_lora, kv_lora, conv_w, ssm_w, pool_w_ = sizes
    w_in_t = jnp.swapaxes(prm["w_in"], 1, 2)
    tail0 = q_lora + kv_lora + ROPE_DIM
    w_tail = w_in_t[:, tail0:, :].astype(BF16)
    head = w_in_t[:, :tail0, :]
    cols = {}
    o = 0
    for name, width in (("conv_u", conv_w), ("conv_bg", conv_w), ("conv_cg", conv_w), ("ssm_u", ssm_w),
                        ("pool_u", pool_w_), ("k_rope", HEAD_PAD), ("k_rope_sw", HEAD_PAD), ("q_a", q_lora),
                        ("kv_a", kv_lora)):
        cols[name] = o
        o += width
    k_rope = head[:, q_lora + kv_lora:tail0, :]
    rope_lanes = ((0, 0), (QK_NOPE, HEAD_PAD - QK_HEAD), (0, 0))
    w_att = jnp.concatenate([jnp.pad(k_rope, rope_lanes),
                             jnp.pad(k_rope[:, jnp.array(_ROPE_PARTNER), :], rope_lanes),
                             head[:, :q_lora + kv_lora, :]], axis=1).astype(BF16)
    big = dict(w_tail=w_tail, w_att=w_att)
    for name in ("w_mla_o", "w_conv_o", "w_glu", "w_pool_o", "w_o"):
        big[name] = prm[name].astype(BF16)
    return big, cols


def _prep_layer(l, prm, sizes, big, ssm):
    kv_lora = sizes[1]
    w_ukv = prm["w_ukv"][l].reshape(kv_lora, MLA_HEADS, QK_NOPE + V_HEAD)
    w_uk = _pad_heads(w_ukv[:, :, :QK_NOPE].reshape(kv_lora, -1), QK_NOPE, 0)
    w_uv = w_ukv[:, :, QK_NOPE:].reshape(kv_lora, -1)
    ssm_a, ssm_mb, ssm_m1, ssm_m2 = ssm

    row = lambda v: v.reshape(1, -1)
    head_gain = lambda g: row(jnp.pad(g, (0, HEAD_PAD - QK_HEAD)))
    partner_gain = lambda g: row(jnp.pad(g[QK_NOPE:][jnp.array(_ROPE_PARTNER)], (QK_NOPE, HEAD_PAD - QK_HEAD)))
    lw = dict(
        big, l=l,
        norm1_g=row(prm["norm1_g"][l]), norm2_g=row(prm["norm2_g"][l]),
        q_a_norm_g=row(prm["q_a_norm_g"][l]), kv_a_norm_g=row(prm["kv_a_norm_g"][l]),
        q_norm_g=head_gain(prm["q_norm_g"][l]), k_norm_g=head_gain(prm["k_norm_g"][l]),
        q_norm_g_sw=partner_gain(prm["q_norm_g"][l]), k_norm_g_sw=partner_gain(prm["k_norm_g"][l]),
        w_uq=_pad_heads(prm["w_uq"][l], QK_HEAD, 0).astype(BF16),
        w_uq_sw=_swap_heads(prm["w_uq"][l]).astype(BF16),
        w_uk=w_uk.astype(BF16), w_uv=w_uv.astype(BF16),
        conv_w=prm["conv_w"][l], conv_b=row(prm["conv_b"][l]),
        ssm_a=ssm_a, ssm_mb=ssm_mb, ssm_m1=ssm_m1, ssm_m2=ssm_m2,
        ssm_d=row(prm["ssm_d"][l]),
        pool_w=prm["pool_w"][l].astype(BF16), pool_scale=row(prm["pool_scale"][l]),
    )
    return lw


def _rope_tables(seq_len):
    half = ROPE_DIM // 2
    inv_freq = ROPE_THETA ** (-jnp.arange(0, half, 2, dtype=F32) / half)
    t = jnp.arange(seq_len)
    ang_r = (t // GRID_W).astype(F32)[:, None] * inv_freq
    ang_c = (t % GRID_W).astype(F32)[:, None] * inv_freq
    ones_lo = jnp.ones((seq_len, QK_NOPE), F32)
    zeros_lo = jnp.zeros((seq_len, QK_NOPE), F32)
    tail = jnp.zeros((seq_len, HEAD_PAD - QK_HEAD), F32)
    c = jnp.concatenate([ones_lo, jnp.cos(ang_r), jnp.cos(ang_r), jnp.cos(ang_c), jnp.cos(ang_c), tail], axis=1)
    s = jnp.concatenate([zeros_lo, -jnp.sin(ang_r), jnp.sin(ang_r), -jnp.sin(ang_c), jnp.sin(ang_c), tail], axis=1)
    return c, s


def _states_to_cols(st):
    b, ndir, _, g, n = st.shape
    st = st.reshape(b, ndir, 2, g // SSM_GB, SSM_GB * n)
    return jnp.transpose(st, (1, 0, 3, 2, 4)).reshape(ndir, b, 2 * g * n)


def _cols_to_states(hf, g, n):
    ndir, b, _ = hf.shape
    hf = hf.reshape(ndir, b, g // SSM_GB, 2, SSM_GB * n)
    return jnp.transpose(hf, (1, 0, 3, 2, 4)).reshape(b, ndir, 2, g, n)


def _mixers(x, mod, lw, cols, batch, seq_len, per_batch, rope, ctx, cast=()):
    p, h1, q, k, v, ckv = _inproj_call(x, mod, lw, rope, seq_len, per_batch, cols)
    if ctx is not None:
        ctx_ckv, ctx_kpe, h0 = ctx
        kc, vc = _ctxprep_call(ctx_ckv, ctx_kpe, lw)
        ctx_len = ctx_ckv.shape[0] // batch
    else:
        kc = vc = None
        ctx_len = 0
        h0 = jnp.zeros((2, batch, lw["ssm_a"].shape[-1]), F32)
    za = _attn_call(q, k, v, kc, vc, batch, seq_len, ctx_len)
    zb, zd = _convpool_call(p, lw, batch, seq_len, cols)

    ys, hf = _ssm_call(p.reshape(batch, seq_len, -1), h0, lw, cols["ssm_u"])
    ys = [y.reshape(2, batch * seq_len, LANES) for y in ys]
    merged, cast_out = _merge_call(h1, p, za, zb, ys, zd, lw, cols, cast)
    kr = cols["k_rope"] + QK_NOPE
    return merged, cast_out, ckv, p[:, kr:kr + ROPE_DIM], hf


def _channel_mixer(x, merged, mod, lw, seq_len, per_batch):
    x1, h2 = _outproj_call(x, merged, mod, lw, seq_len, per_batch)
    return _mlp_call(x1, h2, mod, lw, seq_len, per_batch)


def kernel(x_prompt, x_sample, c, cache_ckv, cache_krope, state_ssm, c_ctx, w_ada, b_ada, norm1_g, norm2_g, w_in, q_a_norm_g, kv_a_norm_g, w_uq, w_ukv, q_norm_g, k_norm_g, w_mla_o, conv_w, conv_b, w_conv_o, ssm_lam_re, ssm_lam_im, ssm_log_step, ssm_b_re, ssm_b_im, ssm_c_re, ssm_c_im, ssm_d, w_glu, pool_w, pool_scale, w_pool_o, w_o, w_mlp1, w_mlp2):
    prm = dict(norm1_g=norm1_g, norm2_g=norm2_g, w_in=w_in, q_a_norm_g=q_a_norm_g, kv_a_norm_g=kv_a_norm_g,
               w_uq=w_uq, w_ukv=w_ukv, q_norm_g=q_norm_g, k_norm_g=k_norm_g, w_mla_o=w_mla_o,
               conv_w=conv_w, conv_b=conv_b, w_conv_o=w_conv_o, ssm_lam_re=ssm_lam_re, ssm_lam_im=ssm_lam_im,
               ssm_log_step=ssm_log_step, ssm_b_re=ssm_b_re, ssm_b_im=ssm_b_im, ssm_c_re=ssm_c_re,
               ssm_c_im=ssm_c_im, ssm_d=ssm_d, w_glu=w_glu, pool_w=pool_w, pool_scale=pool_scale,
               w_pool_o=w_pool_o, w_o=w_o, w_mlp1=w_mlp1, w_mlp2=w_mlp2)
    depth = w_in.shape[0]
    bp, lp, d = x_prompt.shape
    bs, ls, _ = x_sample.shape
    past = cache_ckv.shape[2]
    g, n = state_ssm.shape[-2:]
    sizes = (w_uq.shape[1], w_ukv.shape[1], conv_w.shape[-1], ssm_d.shape[-1], pool_scale.shape[-1])

    rows = -(-(1 + bs) // SUBLANES) * SUBLANES
    cvec = jnp.zeros((rows, d), F32).at[0].set(c_ctx).at[1:1 + bs].set(c)
    mods = _ada_call(cvec, w_ada, b_ada)
    rope = _rope_tables(ls)
    ssm = _prep_ssm(ssm_lam_re, ssm_lam_im, ssm_log_step, ssm_b_re, ssm_b_im, ssm_c_re, ssm_c_im)
    big, cols = _prep_stacked(prm, sizes)

    yp = x_prompt.reshape(bp * lp, d)
    ys = x_sample.reshape(bs * ls, d)
    ckv_list, krope_list, ssm_list = [], [], []
    for l in range(depth):
        lw = _prep_layer(l, prm, sizes, big, ssm)
        mod_ctx = mods[l, 0:1].reshape(6, 1, d)
        mod_lat = mods[l, 1:1 + bs].reshape(bs * 6, 1, d)
        ctx_kpe = jnp.pad(cache_krope[:, l].reshape(bs * past, ROPE_DIM),
                          ((0, 0), (QK_NOPE, HEAD_PAD - QK_HEAD)))
        ctx = (cache_ckv[:, l].reshape(bs * past, -1), ctx_kpe, _states_to_cols(state_ssm[:, l]))
        merged_s, (w1b, w2b), _, _, _ = _mixers(ys, mod_lat, lw, cols, bs, ls, True, rope, ctx,
                                                cast=(w_mlp1, w_mlp2))
        merged_p, _, ckv_l, krope_l, hf_l = _mixers(yp, mod_ctx, lw, cols, bp, lp, False, None, None)
        ckv_list.append(ckv_l.reshape(bp, lp, -1))
        krope_list.append(krope_l.reshape(bp, lp, -1))
        ssm_list.append(_cols_to_states(hf_l, g, n))
        lw_mlp = dict(lw, w_mlp1=w1b[None], w_mlp2=w2b[None], l_mlp=0)
        yp = _channel_mixer(yp, merged_p, mod_ctx, lw_mlp, lp, False)
        ys = _channel_mixer(ys, merged_s, mod_lat, lw_mlp, ls, True)
    return (yp.reshape(bp, lp, d), ys.reshape(bs, ls, d), jnp.stack(ckv_list, axis=1),
            jnp.stack(krope_list, axis=1), jnp.stack(ssm_list, axis=1))
```

```python
import functools

import jax
import jax.numpy as jnp
from jax import lax
from jax.experimental import pallas as pl
from jax.experimental.pallas import tpu as pltpu

F32 = jnp.float32
BF16 = jnp.bfloat16

GRID_W = 64
MLA_HEADS = 8
QK_NOPE = 64
ROPE_DIM = 32
QK_HEAD = QK_NOPE + ROPE_DIM
V_HEAD = 64
ROPE_THETA = 10000.0
SSM_GROUP_CH = 16
POOL_WINDOWS = (2, 4, 8, 16)
EPS = 1e-6
LOG2E = 1.4426950408889634

LANES = 128
SUBLANES = 8
HEAD_PAD = LANES
MIB = 1024 * 1024


def _cparams(sem, vmem_mib):
    return pltpu.CompilerParams(dimension_semantics=sem, vmem_limit_bytes=vmem_mib * MIB)


def _dot(a, b):
    return jnp.dot(a, b, preferred_element_type=F32)


def _dot_t(a, bt):
    return lax.dot_general(a, bt, (((1,), (1,)), ((), ())), preferred_element_type=F32)


def _mod_spec(chunk, d, tm, seq_len, per_batch):
    def imap(i, *_):
        row = (i * tm) // seq_len if per_batch else 0
        return (row * 6 + chunk, 0, 0)
    return pl.BlockSpec((1, 1, d), imap)


def _ada_kernel(c_ref, w_ref, b_ref, o_ref):
    cv = c_ref[...]
    s = cv * jax.nn.sigmoid(cv)
    o_ref[0] = _dot(s.astype(BF16), w_ref[0].astype(BF16)) + b_ref[0]


def _ada_call(cvec, w_ada, b_ada):
    depth, d, n = w_ada.shape
    rows = cvec.shape[0]
    tn = 1024
    return pl.pallas_call(
        _ada_kernel,
        out_shape=jax.ShapeDtypeStruct((depth, rows, n), F32),
        grid=(depth, n // tn),
        in_specs=[
            pl.BlockSpec((rows, d), lambda l, j: (0, 0)),
            pl.BlockSpec((1, d, tn), lambda l, j: (l, 0, j)),
            pl.BlockSpec((1, 1, tn), lambda l, j: (l, 0, j)),
        ],
        out_specs=pl.BlockSpec((1, rows, tn), lambda l, j: (l, 0, j)),
        compiler_params=_cparams(("parallel", "parallel"), 40),
        name="ada_mod",
    )(cvec, w_ada, b_ada.reshape(depth, 1, n))


def _modnorm(x, g, scale, shift):
    ms = jnp.mean(x * x, axis=-1, keepdims=True)
    return (x * lax.rsqrt(ms + EPS) * g) * (1.0 + scale) + shift


def _rms(x, g, width):
    ms = jnp.sum(x * x, axis=-1, keepdims=True) * (1.0 / width)
    return x * lax.rsqrt(ms + EPS) * g


def _heads_norm(x, g, mult, shared=None, rope=None):
    t_main = g * mult
    if rope is not None:
        xs, shared_s, gs, cos, sin = rope
        t_main = cos * t_main
        t_part = sin * (gs * mult)
    outs = []
    for h in range(MLA_HEADS):
        sl = slice(h * HEAD_PAD, (h + 1) * HEAD_PAD)
        xh = x[:, sl] if shared is None else x[:, sl] + shared
        ms = jnp.sum(xh * xh, axis=-1, keepdims=True) * (1.0 / QK_HEAD)
        y = xh * t_main
        if rope is not None:
            y = y + (shared_s if xs is None else xs[:, sl]) * t_part
        outs.append(y * lax.rsqrt(ms + EPS))
    return jnp.concatenate(outs, axis=1)


def _inproj_kernel(*refs, use_rope, q_mult, cols, n_keep):
    (x_ref, sh_ref, sc_ref, g_ref, wmix_ref, watt_ref, gqa_ref, gkva_ref, gq_ref, gk_ref,
     wuq_ref, wuk_ref, wuv_ref) = refs[:13]
    if use_rope:
        gqs_ref, gks_ref, wuqs_ref, c_ref, s_ref = refs[13:18]
    p_out, h_out, q_out, k_out, v_out, ckv_out = refs[-6:]
    att0 = cols["k_rope"]
    tm = x_ref.shape[0]
    nsub = 2

    for sub in range(nsub):
        rs = slice(sub * tm // nsub, (sub + 1) * tm // nsub)
        h = _modnorm(x_ref[rs, :], g_ref[...], sc_ref[0], sh_ref[0]).astype(BF16)
        h_out[rs, :] = h
        res = _dot_t(h, watt_ref[...])
        p_out[rs, :att0] = _dot_t(h, wmix_ref[...])
        p_out[rs, att0:] = res[:, :n_keep - att0]

        def col(name, width, res=res):
            return res[:, cols[name] - att0:cols[name] - att0 + width]

        rope_tabs = (c_ref[rs, :], s_ref[rs, :]) if use_rope else None
        qa = col("q_a", wuq_ref.shape[0])
        qa_n = _rms(qa, gqa_ref[...], qa.shape[-1]).astype(BF16)
        q = _dot(qa_n, wuq_ref[...])
        q_rope = (_dot(qa_n, wuqs_ref[...]), None, gqs_ref[...]) + rope_tabs if use_rope else None
        q_out[rs, :] = _heads_norm(q, gq_ref[...], q_mult, rope=q_rope).astype(BF16)

        kva = col("kv_a", wuk_ref.shape[0])
        ckv = _rms(kva, gkva_ref[...], kva.shape[-1])
        ckv_out[rs, :] = ckv
        ckv_b = ckv.astype(BF16)
        k = _dot(ckv_b, wuk_ref[...])
        k_rope = (None, col("k_rope_sw", HEAD_PAD), gks_ref[...]) + rope_tabs if use_rope else None
        k_out[rs, :] = _heads_norm(k, gk_ref[...], 1.0, shared=col("k_rope", HEAD_PAD), rope=k_rope).astype(BF16)
        v_out[rs, :] = _dot(ckv_b, wuv_ref[...]).astype(BF16)


def _inproj_call(x, mod, lw, rope, seq_len, per_batch, cols):
    t, d = x.shape
    att0 = cols["k_rope"]
    n_att = lw["w_att"].shape[1]
    n_keep = att0 + HEAD_PAD
    tm = 512
    hq = MLA_HEADS * HEAD_PAD
    hv = MLA_HEADS * V_HEAD
    qlora = lw["w_uq"].shape[0]
    kvlora = lw["w_uk"].shape[0]
    l = lw["l"]
    mspec = functools.partial(_mod_spec, d=d, tm=tm, seq_len=seq_len, per_batch=per_batch)
    const = lambda i: (0, 0)
    row = lambda i: (i, 0)
    in_specs = [
        pl.BlockSpec((tm, d), row), mspec(0), mspec(1),
        pl.BlockSpec((1, d), const),
        pl.BlockSpec((None, att0, d), lambda i: (l, 0, 0), pipeline_mode=pl.Buffered(1)),
        pl.BlockSpec((None, n_att, d), lambda i: (l, 0, 0), pipeline_mode=pl.Buffered(1)),
        pl.BlockSpec((1, qlora), const), pl.BlockSpec((1, kvlora), const),
        pl.BlockSpec((1, HEAD_PAD), const), pl.BlockSpec((1, HEAD_PAD), const),
        pl.BlockSpec((qlora, hq), const), pl.BlockSpec((kvlora, hq), const), pl.BlockSpec((kvlora, hv), const),
    ]
    args = [x, mod, mod, lw["norm1_g"], lw["w_tail"], lw["w_att"], lw["q_a_norm_g"], lw["kv_a_norm_g"], lw["q_norm_g"],
            lw["k_norm_g"], lw["w_uq"], lw["w_uk"], lw["w_uv"]]
    if rope is not None:
        nblk = seq_len // tm
        in_specs += [
            pl.BlockSpec((1, HEAD_PAD), const), pl.BlockSpec((1, HEAD_PAD), const),
            pl.BlockSpec((qlora, hq), const),
            pl.BlockSpec((tm, HEAD_PAD), lambda i: (i % nblk, 0)),
            pl.BlockSpec((tm, HEAD_PAD), lambda i: (i % nblk, 0)),
        ]
        args += [lw["q_norm_g_sw"], lw["k_norm_g_sw"], lw["w_uq_sw"]] + list(rope)
    q_mult = LOG2E * QK_HEAD ** -0.5
    return pl.pallas_call(
        functools.partial(_inproj_kernel, use_rope=rope is not None, q_mult=q_mult, cols=cols, n_keep=n_keep),
        out_shape=(jax.ShapeDtypeStruct((t, n_keep), F32), jax.ShapeDtypeStruct((t, d), BF16),
                   jax.ShapeDtypeStruct((t, hq), BF16), jax.ShapeDtypeStruct((t, hq), BF16),
                   jax.ShapeDtypeStruct((t, hv), BF16), jax.ShapeDtypeStruct((t, kvlora), F32)),
        grid=(t // tm,),
        in_specs=in_specs,
        out_specs=(pl.BlockSpec((tm, n_keep), row), pl.BlockSpec((tm, d), row),
                   pl.BlockSpec((tm, hq), row), pl.BlockSpec((tm, hq), row),
                   pl.BlockSpec((tm, hv), row), pl.BlockSpec((tm, kvlora), row)),
        compiler_params=_cparams(("parallel",), 56),
        name="in_proj",
    )(*args)


def _ctxprep_kernel(ckv_ref, kpe_ref, gk_ref, wuk_ref, wuv_ref, k_out, v_out):
    ckv_b = ckv_ref[...].astype(BF16)
    k = _dot(ckv_b, wuk_ref[...])
    k_out[...] = _heads_norm(k, gk_ref[...], 1.0, shared=kpe_ref[...]).astype(BF16)
    v_out[...] = _dot(ckv_b, wuv_ref[...]).astype(BF16)


def _ctxprep_call(ckv, kpe, lw):
    t, kvlora = ckv.shape
    tm = 512
    hq = MLA_HEADS * HEAD_PAD
    hv = MLA_HEADS * V_HEAD
    const = lambda i: (0, 0)
    return pl.pallas_call(
        _ctxprep_kernel,
        out_shape=(jax.ShapeDtypeStruct((t, hq), BF16), jax.ShapeDtypeStruct((t, hv), BF16)),
        grid=(t // tm,),
        in_specs=[
            pl.BlockSpec((tm, kvlora), lambda i: (i, 0)),
            pl.BlockSpec((tm, HEAD_PAD), lambda i: (i, 0)),
            pl.BlockSpec((1, HEAD_PAD), const),
            pl.BlockSpec((kvlora, hq), const),
            pl.BlockSpec((kvlora, hv), const),
        ],
        out_specs=(pl.BlockSpec((tm, hq), lambda i: (i, 0)), pl.BlockSpec((tm, hv), lambda i: (i, 0))),
        compiler_params=_cparams(("parallel",), 40),
        name="ctx_prep",
    )(ckv, kpe, lw["k_norm_g"], lw["w_uk"], lw["w_uv"])


def _attn_kernel(*refs, has_ctx, heads):
    if has_ctx:
        q_ref, ko_ref, vo_ref, kc_ref, vc_ref, o_ref = refs
    else:
        q_ref, ko_ref, vo_ref, o_ref = refs
    contract_last = (((1,), (1,)), ((), ()))
    lane = lax.broadcasted_iota(jnp.int32, (q_ref.shape[0], 2 * V_HEAD), 1)
    for pair in range(heads // 2):
        vsl = slice(pair * 2 * V_HEAD, (pair + 1) * 2 * V_HEAD)
        outs = []
        for hh in range(2):
            h = 2 * pair + hh
            sl = slice(h * HEAD_PAD, (h + 1) * HEAD_PAD)
            q = q_ref[:, sl]
            s_o = lax.dot_general(q, ko_ref[:, sl], contract_last, preferred_element_type=F32)
            m = jnp.max(s_o, axis=-1, keepdims=True)
            if has_ctx:
                s_c = lax.dot_general(q, kc_ref[:, sl], contract_last, preferred_element_type=F32)
                m = jnp.maximum(m, jnp.max(s_c, axis=-1, keepdims=True))
            p_o = jnp.exp2(s_o - m)
            l = jnp.sum(p_o, axis=-1, keepdims=True)
            acc = _dot(p_o.astype(BF16), vo_ref[:, vsl])
            if has_ctx:
                p_c = jnp.exp2(s_c - m)
                l = l + jnp.sum(p_c, axis=-1, keepdims=True)
                acc = acc + _dot(p_c.astype(BF16), vc_ref[:, vsl])
            outs.append(acc / l)
        o_ref[:, vsl] = jnp.where(lane < V_HEAD, outs[0], outs[1]).astype(BF16)


def _attn_call(q, k, v, kc, vc, batch, seq_len, ctx_len):
    t = q.shape[0]
    tq = min(seq_len, 512)
    heads = 8
    nq = seq_len // tq
    npair = MLA_HEADS // heads
    qw = heads * HEAD_PAD
    vw = heads * V_HEAD
    in_specs = [
        pl.BlockSpec((tq, qw), lambda b, h, i: (b * nq + i, h)),
        pl.BlockSpec((seq_len, qw), lambda b, h, i: (b, h)),
        pl.BlockSpec((seq_len, vw), lambda b, h, i: (b, h)),
    ]
    args = [q, k, v]
    if kc is not None:
        in_specs += [pl.BlockSpec((ctx_len, qw), lambda b, h, i: (b, h)),
                     pl.BlockSpec((ctx_len, vw), lambda b, h, i: (b, h))]
        args += [kc, vc]
    return pl.pallas_call(
        functools.partial(_attn_kernel, has_ctx=kc is not None, heads=heads),
        out_shape=jax.ShapeDtypeStruct((t, MLA_HEADS * V_HEAD), BF16),
        grid=(batch, npair, nq),
        in_specs=in_specs,
        out_specs=pl.BlockSpec((tq, vw), lambda b, h, i: (b * nq + i, h)),
        compiler_params=_cparams(("parallel", "parallel", "arbitrary"), 56),
        name="attention",
    )(*args)


def _convpool_kernel(cu_ref, bg_ref, cg_ref, pu_ref, cw_ref, cb_ref, pw_ref, ps_ref, zb_ref, zd_ref):
    n = cu_ref.shape[0]
    gps = pw_ref.shape[0]
    edge = max(POOL_WINDOWS) // 2
    assert edge % SUBLANES == 0 and n >= 4 * edge

    def shifts(rows, zero_fill):
        def down(x, k):
            y = pltpu.roll(x, k, axis=0)
            if zero_fill:
                y = jnp.where(lax.broadcasted_iota(jnp.int32, x.shape, 0) >= k, y, 0.0)
            return y

        def up(x, k):
            y = pltpu.roll(x, rows - k, axis=0)
            if zero_fill:
                y = jnp.where(lax.broadcasted_iota(jnp.int32, x.shape, 0) < rows - k, y, 0.0)
            return y
        return down, up

    def conv_rows(r0, rows, zero_fill):
        down, up = shifts(rows, zero_fill)
        v = cg_ref[r0:r0 + rows, :] * cu_ref[r0:r0 + rows, :]
        conv = (down(v, 1) * cw_ref[0:1, :] + v * cw_ref[1:2, :] + up(v, 1) * cw_ref[2:3, :]
                + cb_ref[...])
        return (bg_ref[r0:r0 + rows, :] * conv).astype(BF16)

    def pool_rows(r0, rows, zero_fill, gg, half):
        down, up = shifts(rows, zero_fill)
        sl = slice(gg * LANES, (gg + 1) * LANES)
        u = pu_ref[r0:r0 + rows, sl]
        fwd = u
        bwd = u
        k = 1
        while k < half:
            fwd = fwd + up(fwd, k)
            bwd = bwd + down(bwd, k)
            k *= 2
        total = fwd + down(bwd, 1)
        rowf = (lax.broadcasted_iota(jnp.int32, u.shape, 0) + r0).astype(F32)
        cnt = jnp.minimum(rowf + half, float(n)) - jnp.maximum(rowf - half, 0.0)
        mean = total / cnt - u
        return (_dot(mean.astype(BF16), pw_ref[gg]) * ps_ref[:, sl]).astype(BF16)

    zb_ref[...] = conv_rows(0, n, False)
    zb_ref[0:edge, :] = conv_rows(0, 2 * edge, True)[0:edge]
    zb_ref[n - edge:n, :] = conv_rows(n - 2 * edge, 2 * edge, True)[edge:]

    step = pl.program_id(1)
    for first in range(0, len(POOL_WINDOWS), gps):
        @pl.when(step * gps == first)
        def _(first=first):
            for gg in range(gps):
                sl = slice(gg * LANES, (gg + 1) * LANES)
                half = POOL_WINDOWS[first + gg] // 2
                zd_ref[:, sl] = pool_rows(0, n, False, gg, half)
                zd_ref[0:edge, sl] = pool_rows(0, 2 * edge, True, gg, half)[0:edge]
                zd_ref[n - edge:n, sl] = pool_rows(n - 2 * edge, 2 * edge, True, gg, half)[edge:]


def _convpool_call(p, lw, batch, seq_len, cols):
    t = p.shape[0]
    gps = 2
    cw = gps * LANES
    nblk = lw["conv_w"].shape[1] // cw

    def pspec(name):
        base = cols[name] // cw
        return pl.BlockSpec((seq_len, cw), lambda b, g: (b, base + g))

    vec = lambda b, g: (0, g)
    return pl.pallas_call(
        _convpool_kernel,
        out_shape=(jax.ShapeDtypeStruct((t, nblk * cw), BF16), jax.ShapeDtypeStruct((t, nblk * cw), BF16)),
        grid=(batch, nblk),
        in_specs=[pspec("conv_u"), pspec("conv_bg"), pspec("conv_cg"), pspec("pool_u"),
                  pl.BlockSpec((3, cw), vec), pl.BlockSpec((1, cw), vec),
                  pl.BlockSpec((gps, LANES, LANES), lambda b, g: (g, 0, 0)), pl.BlockSpec((1, cw), vec)],
        out_specs=(pl.BlockSpec((seq_len, cw), lambda b, g: (b, g)),
                   pl.BlockSpec((seq_len, cw), lambda b, g: (b, g))),
        compiler_params=_cparams(("parallel", "parallel"), 40),
        name="conv_pool",
    )(p, p, p, p, lw["conv_w"], lw["conv_b"], lw["pool_w"], lw["pool_scale"])


SSM_R = 4
SSM_GB = LANES // SSM_GROUP_CH


def _ssm_kernel(*refs, steps, nchunks, nblk):
    u_refs = refs[:nblk]
    h0_ref, a_ref, mb_ref, m1_ref, m2_ref = refs[nblk:nblk + 5]
    y_refs = refs[nblk + 5:2 * nblk + 5]
    hf_ref, slab_scr, xs_scr, st_scr = refs[2 * nblk + 5:]
    d = pl.program_id(0)
    c = pl.program_id(2)
    width = st_scr.shape[1]
    bw = width // nblk

    @pl.when(c == 0)
    def _():
        st_scr[...] = h0_ref[0]

    for blk in range(nblk):
        for j in range(SSM_R):
            for b in range(SUBLANES):
                slab_scr[blk * SSM_R + j, pl.ds(b, steps, stride=SUBLANES), :] = (
                    u_refs[blk][b, pl.ds(j, steps, stride=SSM_R), :])

    def packed(blk):
        return jnp.concatenate([slab_scr[blk * SSM_R + j] for j in range(SSM_R)], axis=1).astype(BF16)

    for blk in range(nblk):
        xs_scr[:, blk * bw:(blk + 1) * bw] = _dot(packed(blk), mb_ref[0, blk])

    gw = width // 2
    for part in range(width // gw):
        offs = []
        for blk in range(part * nblk // 2, (part + 1) * nblk // 2):
            offs += [(blk * bw + k * LANES, blk * bw + bw // 2 + k * LANES) for k in range(bw // 2 // LANES)]
        coef = [(a_ref[0, :, r0:r0 + LANES], a_ref[0, :, i0:i0 + LANES]) for r0, i0 in offs]

        def body(t, carry, offs=offs, coef=coef):
            tt = t + d * (steps - 1 - 2 * t)
            row0 = pl.multiple_of(tt * SUBLANES, SUBLANES)
            new = []
            for (r0, i0), (a_re, a_im), x_re, x_im in zip(offs, coef, carry[0::2], carry[1::2]):
                n_re = (a_re * x_re - a_im * x_im) + xs_scr[pl.ds(row0, SUBLANES), r0:r0 + LANES]
                n_im = (a_re * x_im + a_im * x_re) + xs_scr[pl.ds(row0, SUBLANES), i0:i0 + LANES]
                xs_scr[pl.ds(row0, SUBLANES), r0:r0 + LANES] = x_re
                xs_scr[pl.ds(row0, SUBLANES), i0:i0 + LANES] = x_im
                new += [n_re, n_im]
            return tuple(new)

        init = []
        for r0, i0 in offs:
            init += [st_scr[:, r0:r0 + LANES], st_scr[:, i0:i0 + LANES]]
        final = lax.fori_loop(0, steps, body, tuple(init), unroll=2)
        for (r0, i0), x_re, x_im in zip(offs, final[0::2], final[1::2]):
            st_scr[:, r0:r0 + LANES] = x_re
            st_scr[:, i0:i0 + LANES] = x_im

    for blk in range(nblk):
        y = (_dot(xs_scr[:, blk * bw:(blk + 1) * bw].astype(BF16), m1_ref[0, blk])
             + _dot(packed(blk), m2_ref[0, blk]))
        for j in range(SSM_R):
            slab_scr[blk * SSM_R + j] = y[:, j * LANES:(j + 1) * LANES]
        for j in range(SSM_R):
            for b in range(SUBLANES):
                y_refs[blk][0, b, pl.ds(j, steps, stride=SSM_R), :] = (
                    slab_scr[blk * SSM_R + j, pl.ds(b, steps, stride=SUBLANES), :])

    @pl.when(c == nchunks - 1)
    def _():
        hf_ref[0] = st_scr[...]


def _ssm_call(p3, h0, lw, col0):
    batch, seq_len, _ = p3.shape
    width = h0.shape[-1]
    nblk = lw["ssm_mb"].shape[2]
    l = lw["l"]
    steps = min(seq_len // SSM_R, 64)
    tokens = steps * SSM_R
    nchunks = seq_len // tokens
    nbg = batch // SUBLANES
    rows = steps * SUBLANES

    def cidx(d, c):
        return c + d * (nchunks - 1 - 2 * c)

    def uspec(blk):
        return pl.BlockSpec((SUBLANES, tokens, LANES), lambda d, b, c: (b, cidx(d, c), col0 // LANES + blk))

    def wspec(arr):
        return pl.BlockSpec((None, 1) + arr.shape[2:], lambda d, b, c: (l, d, 0, 0, 0),
                            pipeline_mode=pl.Buffered(1))

    yshape = jax.ShapeDtypeStruct((2, batch, seq_len, LANES), F32)
    yspec = pl.BlockSpec((1, SUBLANES, tokens, LANES), lambda d, b, c: (d, b, cidx(d, c), 0))
    outs = pl.pallas_call(
        functools.partial(_ssm_kernel, steps=steps, nchunks=nchunks, nblk=nblk),
        out_shape=(yshape,) * nblk + (jax.ShapeDtypeStruct((2, batch, width), F32),),
        grid=(2, nbg, nchunks),
        in_specs=[uspec(blk) for blk in range(nblk)] + [
            pl.BlockSpec((1, SUBLANES, width), lambda d, b, c: (d, b, 0)),
            pl.BlockSpec((None, 1, SUBLANES, width), lambda d, b, c: (l, d, 0, 0)),
            wspec(lw["ssm_mb"]), wspec(lw["ssm_m1"]), wspec(lw["ssm_m2"]),
        ],
        out_specs=(yspec,) * nblk + (pl.BlockSpec((1, SUBLANES, width), lambda d, b, c: (d, b, 0)),),
        scratch_shapes=[pltpu.VMEM((nblk * SSM_R, rows, LANES), F32), pltpu.VMEM((rows, width), F32),
                        pltpu.VMEM((SUBLANES, width), F32)],
        compiler_params=_cparams(("arbitrary", "arbitrary", "arbitrary"), 52),
        name="s5_scan",
    )(*([p3] * nblk), h0, lw["ssm_a"], lw["ssm_mb"], lw["ssm_m1"], lw["ssm_m2"])
    return outs[:nblk], outs[nblk]


def _merge_kernel(*refs, nblk, ncast):
    (h_ref, za_ref, zb_ref, su_ref, sd_ref, zd_ref, wg0, wg1, wg2, wg3, wa, wb, wca, wcg, wd) = refs[:15]
    yf_refs = refs[15:15 + nblk]
    yb_refs = refs[15 + nblk:15 + 2 * nblk]
    cast_in = refs[15 + 2 * nblk:15 + 2 * nblk + ncast]
    o_ref = refs[15 + 2 * nblk + ncast]
    cast_out = refs[16 + 2 * nblk + ncast:16 + 2 * nblk + 2 * ncast]
    y_scr = refs[-1]

    for src, dst in zip(cast_in, cast_out):
        dst[...] = src[...].astype(BF16)

    @pl.when(pl.program_id(1) == 0)
    def _():
        y_both = jnp.concatenate([f[0] + b[0] for f, b in zip(yf_refs, yb_refs)], axis=1)
        y_scr[...] = (y_both + sd_ref[...] * su_ref[...]).astype(BF16)

    tm = h_ref.shape[0]
    nsub = 2
    for sub in range(nsub):
        rs = slice(sub * tm // nsub, (sub + 1) * tm // nsub)
        h = h_ref[rs, :]
        y = y_scr[rs, :]

        def gate(w_ref, h=h):
            return jax.nn.sigmoid(_dot_t(h, w_ref[...]))

        merged = gate(wg0) * _dot(za_ref[rs, :], wa[...])
        merged = merged + gate(wg1) * _dot(zb_ref[rs, :], wb[...])
        merged = merged + gate(wg2) * (_dot(y, wca[...]) * jax.nn.sigmoid(_dot(y, wcg[...])))
        merged = merged + gate(wg3) * _dot(zd_ref[rs, :], wd[...])
        o_ref[rs, :] = merged.astype(BF16)


def _merge_call(h, p, za, zb, ys, zd, lw, cols, cast=()):
    t, d = h.shape
    bw = za.shape[1]
    tm, tn = 512, 512
    nj = d // tn
    nblk = len(ys)
    row = lambda i, j: (i, 0)
    su_blk = cols["ssm_u"] // bw

    l = lw["l"]

    gate_blk0 = cols["k_rope"] // tn

    def gspec(k):
        return pl.BlockSpec((None, tn, d), lambda i, j: (l, gate_blk0 + k * nj + j, 0))

    def yspec(direction):
        return pl.BlockSpec((1, tm, LANES), lambda i, j: (direction, i, 0))

    wcol = pl.BlockSpec((None, bw, tn), lambda i, j: (l, 0, j))
    in_specs = [
        pl.BlockSpec((tm, d), row),
        pl.BlockSpec((tm, bw), row), pl.BlockSpec((tm, bw), row),
        pl.BlockSpec((tm, bw), lambda i, j: (i, su_blk)),
        pl.BlockSpec((1, bw), lambda i, j: (0, 0)),
        pl.BlockSpec((tm, bw), row),
        gspec(0), gspec(1), gspec(2), gspec(3),
        wcol, wcol, wcol, pl.BlockSpec((None, bw, tn), lambda i, j: (l, 0, nj + j)), wcol,
    ] + [yspec(0)] * nblk + [yspec(1)] * nblk
    out_shape = [jax.ShapeDtypeStruct((t, d), BF16)]
    out_specs = [pl.BlockSpec((tm, tn), lambda i, j: (i, j))]
    nsteps = (t // tm) * nj
    for w in cast:
        rows, n = w.shape[1] // nsteps, w.shape[2]
        assert rows * nsteps == w.shape[1] and rows % (2 * SUBLANES) == 0
        in_specs.append(pl.BlockSpec((None, rows, n), lambda i, j: (l, i * nj + j, 0)))
        out_shape.append(jax.ShapeDtypeStruct(w.shape[1:], BF16))
        out_specs.append(pl.BlockSpec((rows, n), lambda i, j: (i * nj + j, 0)))
    wg = lw["w_tail"]
    outs = pl.pallas_call(
        functools.partial(_merge_kernel, nblk=nblk, ncast=len(cast)),
        out_shape=tuple(out_shape),
        grid=(t // tm, nj),
        in_specs=in_specs,
        out_specs=tuple(out_specs),
        scratch_shapes=[pltpu.VMEM((tm, bw), BF16)],
        compiler_params=_cparams(("parallel", "arbitrary"), 48),
        name="branch_merge",
    )(h, za, zb, p, lw["ssm_d"], zd,
      wg, wg, wg, wg, lw["w_mla_o"], lw["w_conv_o"], lw["w_glu"], lw["w_glu"], lw["w_pool_o"], *ys, *ys, *cast)
    return outs[0], tuple(outs[1:])


def _outproj_kernel(x_ref, m_ref, gt_ref, sh_ref, sc_ref, g_ref, wo_ref, x1_ref, h2_ref):
    tm = x_ref.shape[0]
    nsub = 2
    for sub in range(nsub):
        rs = slice(sub * tm // nsub, (sub + 1) * tm // nsub)
        x1 = x_ref[rs, :] + gt_ref[0] * _dot(m_ref[rs, :], wo_ref[...])
        x1_ref[rs, :] = x1
        h2_ref[rs, :] = _modnorm(x1, g_ref[...], sc_ref[0], sh_ref[0]).astype(BF16)


def _outproj_call(x, merged, mod, lw, seq_len, per_batch):
    t, d = x.shape
    tm = 512
    mspec = functools.partial(_mod_spec, d=d, tm=tm, seq_len=seq_len, per_batch=per_batch)
    row = lambda i: (i, 0)
    return pl.pallas_call(
        _outproj_kernel,
        out_shape=(jax.ShapeDtypeStruct((t, d), F32), jax.ShapeDtypeStruct((t, d), BF16)),
        grid=(t // tm,),
        in_specs=[
            pl.BlockSpec((tm, d), row), pl.BlockSpec((tm, d), row),
            mspec(2), mspec(3), mspec(4),
            pl.BlockSpec((1, d), lambda i: (0, 0)),
            pl.BlockSpec((None, d, d), lambda i: (lw["l"], 0, 0), pipeline_mode=pl.Buffered(1)),
        ],
        out_specs=(pl.BlockSpec((tm, d), row), pl.BlockSpec((tm, d), row)),
        compiler_params=_cparams(("parallel",), 48),
        name="out_proj",
    )(x, merged, mod, mod, mod, lw["norm2_g"], lw["w_o"])


def _mlp_kernel(x_ref, h_ref, gt_ref, w1_ref, w2_ref, o_ref):
    @pl.when(pl.program_id(1) == 0)
    def _():
        o_ref[...] = x_ref[...]

    hid = jnp.square(jnp.maximum(_dot(h_ref[...], w1_ref[...]), 0.0))
    o_ref[...] += gt_ref[0] * _dot(hid.astype(BF16), w2_ref[...])


def _mlp_call(x, h, mod, lw, seq_len, per_batch):
    t, d = x.shape
    hidden = lw["w_mlp1"].shape[2]
    tm, tk = 512, 2048
    l = lw["l_mlp"]
    mspec = functools.partial(_mod_spec, d=d, tm=tm, seq_len=seq_len, per_batch=per_batch)
    row = lambda i, k: (i, 0)
    once = pl.Buffered(1)
    return pl.pallas_call(
        _mlp_kernel,
        out_shape=jax.ShapeDtypeStruct((t, d), F32),
        grid=(t // tm, hidden // tk),
        in_specs=[
            pl.BlockSpec((tm, d), row), pl.BlockSpec((tm, d), row, pipeline_mode=once),
            mspec(5),
            pl.BlockSpec((None, d, tk), lambda i, k: (l, 0, k)),
            pl.BlockSpec((None, tk, d), lambda i, k: (l, k, 0)),
        ],
        out_specs=pl.BlockSpec((tm, d), row),
        compiler_params=_cparams(("parallel", "arbitrary"), 56),
        name="mlp",
    )(x, h, mod, lw["w_mlp1"], lw["w_mlp2"])


def _pad_heads(w, per_head, lo):
    lead = w.shape[:-1]
    w = w.reshape(lead + (MLA_HEADS, per_head))
    w = jnp.pad(w, [(0, 0)] * len(lead) + [(0, 0), (lo, HEAD_PAD - lo - per_head)])
    return w.reshape(lead + (MLA_HEADS * HEAD_PAD,))


_ROPE_PARTNER = tuple((i // 16) * 16 + (i % 16 + 8) % 16 for i in range(ROPE_DIM))


def _swap_heads(w):
    lead = w.shape[:-1]
    w = w.reshape(lead + (MLA_HEADS, QK_HEAD))[..., QK_NOPE:][..., jnp.array(_ROPE_PARTNER)]
    w = jnp.pad(w, [(0, 0)] * len(lead) + [(0, 0), (QK_NOPE, HEAD_PAD - QK_HEAD)])
    return w.reshape(lead + (MLA_HEADS * HEAD_PAD,))


def _prep_ssm(lam_re, lam_im, log_step, b_re, b_im, c_re, c_im):
    r, gb = SSM_R, SSM_GB
    nl, ndir, g, n, cg = b_re.shape
    ns = nl * ndir
    nblk = g // gb
    flat = lambda v: v.reshape((ns,) + v.shape[2:])
    lam_re, lam_im, log_step, b_re, b_im, c_re, c_im = map(
        flat, (lam_re, lam_im, log_step, b_re, b_im, c_re, c_im))
    step = jnp.exp(log_step)[..., None]
    pw = []
    for p in range(r + 1):
        mag = jnp.exp(p * (lam_re * step))
        pw.append((mag * jnp.cos(p * (lam_im * step)), mag * jnp.sin(p * (lam_im * step))))
    ar, ai = pw[1]
    den = lam_re * lam_re + lam_im * lam_im
    qr = ((ar - 1.0) * lam_re + ai * lam_im) / den
    qi = (ai * lam_re - (ar - 1.0) * lam_im) / den
    bb_re = qr[..., None] * b_re - qi[..., None] * b_im
    bb_im = qr[..., None] * b_im + qi[..., None] * b_re
    fwd = (jnp.arange(ns) % ndir == 0)[:, None, None]

    def power(p_fwd, p_bwd):
        return (jnp.where(fwd, pw[p_fwd][0], pw[p_bwd][0]), jnp.where(fwd, pw[p_fwd][1], pw[p_bwd][1]))

    rows_g = (jnp.arange(r * gb * cg) // cg) % gb
    cols_g = jnp.arange(gb * n) // n

    t_re, t_im = [], []
    for j in range(r):
        pr, pi = power(r - 1 - j, j)
        t_re.append(pr[..., None] * bb_re - pi[..., None] * bb_im)
        t_im.append(pr[..., None] * bb_im + pi[..., None] * bb_re)
    mask_b = (rows_g[:, None] == cols_g[None, :]).astype(F32)

    def inc_block(ts):
        t = jnp.stack(ts, axis=1).reshape(ns, r, nblk, gb, n, cg)
        t = jnp.transpose(t, (0, 2, 1, 3, 5, 4)).reshape(ns, nblk, r * gb * cg, n)
        return jnp.tile(t, (1, 1, 1, gb)) * mask_b

    mb = jnp.concatenate([inc_block(t_re), inc_block(t_im)], axis=-1)

    cl_re, cl_im = [], []
    for j in range(r):
        pr, pi = power(j + 1, r - j)
        cl_re.append(c_re * pr[:, :, None, :] - c_im * pi[:, :, None, :])
        cl_im.append(c_re * pi[:, :, None, :] + c_im * pr[:, :, None, :])

    def out_block(ts):
        t = jnp.stack(ts, axis=1).reshape(ns, r, nblk, gb, cg, n)
        t = jnp.transpose(t, (0, 2, 5, 1, 3, 4)).reshape(ns, nblk, n, r * gb * cg)
        return jnp.tile(t, (1, 1, gb, 1)) * mask_b.T

    m1 = jnp.concatenate([out_block(cl_re), -out_block(cl_im)], axis=2)

    hi = lax.Precision.HIGHEST
    lane_g = jnp.arange(gb * cg) // cg
    mask_k = (lane_g[:, None] == lane_g[None, :]).astype(F32)
    kd = []
    for p in range(r):
        pr, pi = pw[p]
        k2 = (jnp.einsum("sgcn,sgnk->sgck", c_re * pr[:, :, None, :] - c_im * pi[:, :, None, :], bb_re, precision=hi)
              - jnp.einsum("sgcn,sgnk->sgck", c_re * pi[:, :, None, :] + c_im * pr[:, :, None, :], bb_im, precision=hi))
        k2 = jnp.swapaxes(k2, -1, -2).reshape(ns, nblk, gb * cg, cg)
        kd.append(jnp.tile(k2, (1, 1, 1, gb)) * mask_k)
    fwd4 = fwd[..., None]
    zero = jnp.zeros_like(kd[0])
    m2_rows = []
    for i in range(r):
        blocks = []
        for j in range(r):
            if i == j:
                blocks.append(kd[0])
            elif j > i:
                blocks.append(jnp.where(fwd4, kd[j - i], zero))
            else:
                blocks.append(jnp.where(fwd4, zero, kd[i - j]))
        m2_rows.append(jnp.concatenate(blocks, axis=-1))
    m2 = jnp.concatenate(m2_rows, axis=2)

    a = jnp.stack([pw[r][0].reshape(ns, nblk, gb * n), pw[r][1].reshape(ns, nblk, gb * n)], axis=2)
    a = jnp.broadcast_to(a.reshape(ns, 1, 2 * g * n), (ns, SUBLANES, 2 * g * n))
    unflat = lambda v: v.reshape((nl, ndir) + v.shape[1:])
    return unflat(a), unflat(mb.astype(BF16)), unflat(m1.astype(BF16)), unflat(m2.astype(BF16))


def _prep_stacked(prm, sizes):
    q_lora, kv_lora, conv_w, ssm_w, pool_w_ = sizes
    w_in_t = jnp.swapaxes(prm["w_in"], 1, 2)
    tail0 = q_lora + kv_lora + ROPE_DIM
    w_tail = w_in_t[:, tail0:, :].astype(BF16)
    head = w_in_t[:, :tail0, :]
    cols = {}
    o = 0
    for name, width in (("conv_u", conv_w), ("conv_bg", conv_w), ("conv_cg", conv_w), ("ssm_u", ssm_w),
                        ("pool_u", pool_w_), ("k_rope", HEAD_PAD), ("k_rope_sw", HEAD_PAD), ("q_a", q_lora),
                        ("kv_a", kv_lora)):
        cols[name] = o
        o += width
    k_rope = head[:, q_lora + kv_lora:tail0, :]
    rope_lanes = ((0, 0), (QK_NOPE, HEAD_PAD - QK_HEAD), (0, 0))
    w_att = jnp.concatenate([jnp.pad(k_rope, rope_lanes),
                             jnp.pad(k_rope[:, jnp.array(_ROPE_PARTNER), :], rope_lanes),
                             head[:, :q_lora + kv_lora, :]], axis=1).astype(BF16)
    big = dict(w_tail=w_tail, w_att=w_att)
    for name in ("w_mla_o", "w_conv_o", "w_glu", "w_pool_o", "w_o"):
        big[name] = prm[name].astype(BF16)
    return big, cols


def _prep_layer(l, prm, sizes, big, ssm):
    kv_lora = sizes[1]
    w_ukv = prm["w_ukv"][l].reshape(kv_lora, MLA_HEADS, QK_NOPE + V_HEAD)
    w_uk = _pad_heads(w_ukv[:, :, :QK_NOPE].reshape(kv_lora, -1), QK_NOPE, 0)
    w_uv = w_ukv[:, :, QK_NOPE:].reshape(kv_lora, -1)
    ssm_a, ssm_mb, ssm_m1, ssm_m2 = ssm

    row = lambda v: v.reshape(1, -1)
    head_gain = lambda g: row(jnp.pad(g, (0, HEAD_PAD - QK_HEAD)))
    partner_gain = lambda g: row(jnp.pad(g[QK_NOPE:][jnp.array(_ROPE_PARTNER)], (QK_NOPE, HEAD_PAD - QK_HEAD)))
    lw = dict(
        big, l=l,
        norm1_g=row(prm["norm1_g"][l]), norm2_g=row(prm["norm2_g"][l]),
        q_a_norm_g=row(prm["q_a_norm_g"][l]), kv_a_norm_g=row(prm["kv_a_norm_g"][l]),
        q_norm_g=head_gain(prm["q_norm_g"][l]), k_norm_g=head_gain(prm["k_norm_g"][l]),
        q_norm_g_sw=partner_gain(prm["q_norm_g"][l]), k_norm_g_sw=partner_gain(prm["k_norm_g"][l]),
        w_uq=_pad_heads(prm["w_uq"][l], QK_HEAD, 0).astype(BF16),
        w_uq_sw=_swap_heads(prm["w_uq"][l]).astype(BF16),
        w_uk=w_uk.astype(BF16), w_uv=w_uv.astype(BF16),
        conv_w=prm["conv_w"][l], conv_b=row(prm["conv_b"][l]),
        ssm_a=ssm_a, ssm_mb=ssm_mb, ssm_m1=ssm_m1, ssm_m2=ssm_m2,
        ssm_d=row(prm["ssm_d"][l]),
        pool_w=prm["pool_w"][l].astype(BF16), pool_scale=row(prm["pool_scale"][l]),
    )
    return lw


def _rope_tables(seq_len):
    half = ROPE_DIM // 2
    inv_freq = ROPE_THETA ** (-jnp.arange(0, half, 2, dtype=F32) / half)
    t = jnp.arange(seq_len)
    ang_r = (t // GRID_W).astype(F32)[:, None] * inv_freq
    ang_c = (t % GRID_W).astype(F32)[:, None] * inv_freq
    ones_lo = jnp.ones((seq_len, QK_NOPE), F32)
    zeros_lo = jnp.zeros((seq_len, QK_NOPE), F32)
    tail = jnp.zeros((seq_len, HEAD_PAD - QK_HEAD), F32)
    c = jnp.concatenate([ones_lo, jnp.cos(ang_r), jnp.cos(ang_r), jnp.cos(ang_c), jnp.cos(ang_c), tail], axis=1)
    s = jnp.concatenate([zeros_lo, -jnp.sin(ang_r), jnp.sin(ang_r), -jnp.sin(ang_c), jnp.sin(ang_c), tail], axis=1)
    return c, s


def _states_to_cols(st):
    b, ndir, _, g, n = st.shape
    st = st.reshape(b, ndir, 2, g // SSM_GB, SSM_GB * n)
    return jnp.transpose(st, (1, 0, 3, 2, 4)).reshape(ndir, b, 2 * g * n)


def _cols_to_states(hf, g, n):
    ndir, b, _ = hf.shape
    hf = hf.reshape(ndir, b, g // SSM_GB, 2, SSM_GB * n)
    return jnp.transpose(hf, (1, 0, 3, 2, 4)).reshape(b, ndir, 2, g, n)


def _mixers(x, mod, lw, cols, batch, seq_len, per_batch, rope, ctx, cast=()):
    p, h1, q, k, v, ckv = _inproj_call(x, mod, lw, rope, seq_len, per_batch, cols)
    if ctx is not None:
        ctx_ckv, ctx_kpe, h0 = ctx
        kc, vc = _ctxprep_call(ctx_ckv, ctx_kpe, lw)
        ctx_len = ctx_ckv.shape[0] // batch
    else:
        kc = vc = None
        ctx_len = 0
        h0 = jnp.zeros((2, batch, lw["ssm_a"].shape[-1]), F32)
    za = _attn_call(q, k, v, kc, vc, batch, seq_len, ctx_len)
    zb, zd = _convpool_call(p, lw, batch, seq_len, cols)

    ys, hf = _ssm_call(p.reshape(batch, seq_len, -1), h0, lw, cols["ssm_u"])
    ys = [y.reshape(2, batch * seq_len, LANES) for y in ys]
    merged, cast_out = _merge_call(h1, p, za, zb, ys, zd, lw, cols, cast)
    kr = cols["k_rope"] + QK_NOPE
    return merged, cast_out, ckv, p[:, kr:kr + ROPE_DIM], hf


def _channel_mixer(x, merged, mod, lw, seq_len, per_batch):
    x1, h2 = _outproj_call(x, merged, mod, lw, seq_len, per_batch)
    return _mlp_call(x1, h2, mod, lw, seq_len, per_batch)


def kernel(x_prompt, x_sample, c, cache_ckv, cache_krope, state_ssm, c_ctx, w_ada, b_ada, norm1_g, norm2_g, w_in, q_a_norm_g, kv_a_norm_g, w_uq, w_ukv, q_norm_g, k_norm_g, w_mla_o, conv_w, conv_b, w_conv_o, ssm_lam_re, ssm_lam_im, ssm_log_step, ssm_b_re, ssm_b_im, ssm_c_re, ssm_c_im, ssm_d, w_glu, pool_w, pool_scale, w_pool_o, w_o, w_mlp1, w_mlp2):
    prm = dict(norm1_g=norm1_g, norm2_g=norm2_g, w_in=w_in, q_a_norm_g=q_a_norm_g, kv_a_norm_g=kv_a_norm_g,
               w_uq=w_uq, w_ukv=w_ukv, q_norm_g=q_norm_g, k_norm_g=k_norm_g, w_mla_o=w_mla_o,
               conv_w=conv_w, conv_b=conv_b, w_conv_o=w_conv_o, ssm_lam_re=ssm_lam_re, ssm_lam_im=ssm_lam_im,
               ssm_log_step=ssm_log_step, ssm_b_re=ssm_b_re, ssm_b_im=ssm_b_im, ssm_c_re=ssm_c_re,
               ssm_c_im=ssm_c_im, ssm_d=ssm_d, w_glu=w_glu, pool_w=pool_w, pool_scale=pool_scale,
               w_pool_o=w_pool_o, w_o=w_o, w_mlp1=w_mlp1, w_mlp2=w_mlp2)
    depth = w_in.shape[0]
    bp, lp, d = x_prompt.shape
    bs, ls, _ = x_sample.shape
    past = cache_ckv.shape[2]
    g, n = state_ssm.shape[-2:]
    sizes = (w_uq.shape[1], w_ukv.shape[1], conv_w.shape[-1], ssm_d.shape[-1], pool_scale.shape[-1])

    rows = -(-(1 + bs) // SUBLANES) * SUBLANES
    cvec = jnp.zeros((rows, d), F32).at[0].set(c_ctx).at[1:1 + bs].set(c)
    mods = _ada_call(cvec, w_ada, b_ada)
    rope = _rope_tables(ls)
    ssm = _prep_ssm(ssm_lam_re, ssm_lam_im, ssm_log_step, ssm_b_re, ssm_b_im, ssm_c_re, ssm_c_im)
    big, cols = _prep_stacked(prm, sizes)

    yp = x_prompt.reshape(bp * lp, d)
    ys = x_sample.reshape(bs * ls, d)
    ckv_list, krope_list, ssm_list = [], [], []
    for l in range(depth):
        lw = _prep_layer(l, prm, sizes, big, ssm)
        mod_ctx = mods[l, 0:1].reshape(6, 1, d)
        mod_lat = mods[l, 1:1 + bs].reshape(bs * 6, 1, d)
        ctx_kpe = jnp.pad(cache_krope[:, l].reshape(bs * past, ROPE_DIM),
                          ((0, 0), (QK_NOPE, HEAD_PAD - QK_HEAD)))
        ctx = (cache_ckv[:, l].reshape(bs * past, -1), ctx_kpe, _states_to_cols(state_ssm[:, l]))
        merged_s, (w1b, w2b), _, _, _ = _mixers(ys, mod_lat, lw, cols, bs, ls, True, rope, ctx,
                                                cast=(w_mlp1, w_mlp2))
        merged_p, _, ckv_l, krope_l, hf_l = _mixers(yp, mod_ctx, lw, cols, bp, lp, False, None, None)
        ckv_list.append(ckv_l.reshape(bp, lp, -1))
        krope_list.append(krope_l.reshape(bp, lp, -1))
        ssm_list.append(_cols_to_states(hf_l, g, n))
        lw_mlp = dict(lw, w_mlp1=w1b[None], w_mlp2=w2b[None], l_mlp=0)
        yp = _channel_mixer(yp, merged_p, mod_ctx, lw_mlp, lp, False)
        ys = _channel_mixer(ys, merged_s, mod_lat, lw_mlp, ls, True)
    return (yp.reshape(bp, lp, d), ys.reshape(bs, ls, d), jnp.stack(ckv_list, axis=1),
            jnp.stack(krope_list, axis=1), jnp.stack(ssm_list, axis=1))
```

```python
import functools

import jax
import jax.numpy as jnp
from jax import lax
from jax.experimental import pallas as pl
from jax.experimental.pallas import tpu as pltpu

F32 = jnp.float32
BF16 = jnp.bfloat16

GRID_W = 64
MLA_HEADS = 8
QK_NOPE = 64
ROPE_DIM = 32
QK_HEAD = QK_NOPE + ROPE_DIM
V_HEAD = 64
ROPE_THETA = 10000.0
SSM_GROUP_CH = 16
POOL_WINDOWS = (2, 4, 8, 16)
EPS = 1e-6
LOG2E = 1.4426950408889634

LANES = 128
SUBLANES = 8
HEAD_PAD = LANES
MIB = 1024 * 1024


def _cparams(sem, vmem_mib):
    return pltpu.CompilerParams(dimension_semantics=sem, vmem_limit_bytes=vmem_mib * MIB)


def _dot(a, b):
    return jnp.dot(a, b, preferred_element_type=F32)


def _dot_t(a, bt):
    return lax.dot_general(a, bt, (((1,), (1,)), ((), ())), preferred_element_type=F32)


def _mod_spec(chunk, d, tm, seq_len, per_batch):
    def imap(i, *_):
        row = (i * tm) // seq_len if per_batch else 0
        return (row * 6 + chunk, 0, 0)
    return pl.BlockSpec((1, 1, d), imap)


def _ada_kernel(c_ref, w_ref, b_ref, o_ref):
    cv = c_ref[...]
    s = cv * jax.nn.sigmoid(cv)
    o_ref[0] = _dot(s.astype(BF16), w_ref[0].astype(BF16)) + b_ref[0]


def _ada_call(cvec, w_ada, b_ada):
    depth, d, n = w_ada.shape
    rows = cvec.shape[0]
    tn = 1024
    return pl.pallas_call(
        _ada_kernel,
        out_shape=jax.ShapeDtypeStruct((depth, rows, n), F32),
        grid=(depth, n // tn),
        in_specs=[
            pl.BlockSpec((rows, d), lambda l, j: (0, 0)),
            pl.BlockSpec((1, d, tn), lambda l, j: (l, 0, j)),
            pl.BlockSpec((1, 1, tn), lambda l, j: (l, 0, j)),
        ],
        out_specs=pl.BlockSpec((1, rows, tn), lambda l, j: (l, 0, j)),
        compiler_params=_cparams(("parallel", "parallel"), 40),
        name="ada_mod",
    )(cvec, w_ada, b_ada.reshape(depth, 1, n))


def _modnorm(x, g, scale, shift):
    ms = jnp.mean(x * x, axis=-1, keepdims=True)
    return (x * lax.rsqrt(ms + EPS) * g) * (1.0 + scale) + shift


def _rms(x, g, width):
    ms = jnp.sum(x * x, axis=-1, keepdims=True) * (1.0 / width)
    return x * lax.rsqrt(ms + EPS) * g


def _heads_norm(x, g, mult, shared=None, rope=None):
    t_main = g * mult
    if rope is not None:
        xs, shared_s, gs, cos, sin = rope
        t_main = cos * t_main
        t_part = sin * (gs * mult)
    outs = []
    for h in range(MLA_HEADS):
        sl = slice(h * HEAD_PAD, (h + 1) * HEAD_PAD)
        xh = x[:, sl] if shared is None else x[:, sl] + shared
        ms = jnp.sum(xh * xh, axis=-1, keepdims=True) * (1.0 / QK_HEAD)
        y = xh * t_main
        if rope is not None:
            y = y + (shared_s if xs is None else xs[:, sl]) * t_part
        outs.append(y * lax.rsqrt(ms + EPS))
    return jnp.concatenate(outs, axis=1)


def _inproj_kernel(*refs, use_rope, q_mult, cols, n_keep):
    (x_ref, sh_ref, sc_ref, g_ref, wmix_ref, watt_ref, gqa_ref, gkva_ref, gq_ref, gk_ref,
     wuq_ref, wuk_ref, wuv_ref) = refs[:13]
    if use_rope:
        gqs_ref, gks_ref, wuqs_ref, c_ref, s_ref = refs[13:18]
    nu = (cols["pool_u"] - cols["ssm_u"]) // LANES
    p_out, h_out, q_out, k_out, v_out, ckv_out = refs[-6 - nu:-nu]
    u_outs = refs[-nu:]
    att0 = cols["k_rope"]
    tm = x_ref.shape[0]
    nsub = 2

    for sub in range(nsub):
        rs = slice(sub * tm // nsub, (sub + 1) * tm // nsub)
        h = _modnorm(x_ref[rs, :], g_ref[...], sc_ref[0], sh_ref[0]).astype(BF16)
        h_out[rs, :] = h
        res = _dot_t(h, watt_ref[...])
        mix = _dot_t(h, wmix_ref[...])
        p_out[rs, :att0] = mix
        p_out[rs, att0:] = res[:, :n_keep - att0]
        for blk, u_out in enumerate(u_outs):
            u_out[rs, :] = mix[:, cols["ssm_u"] + blk * LANES:cols["ssm_u"] + (blk + 1) * LANES]

        def col(name, width, res=res):
            return res[:, cols[name] - att0:cols[name] - att0 + width]

        rope_tabs = (c_ref[rs, :], s_ref[rs, :]) if use_rope else None
        qa = col("q_a", wuq_ref.shape[0])
        qa_n = _rms(qa, gqa_ref[...], qa.shape[-1]).astype(BF16)
        q = _dot(qa_n, wuq_ref[...])
        q_rope = (_dot(qa_n, wuqs_ref[...]), None, gqs_ref[...]) + rope_tabs if use_rope else None
        q_out[rs, :] = _heads_norm(q, gq_ref[...], q_mult, rope=q_rope).astype(BF16)

        kva = col("kv_a", wuk_ref.shape[0])
        ckv = _rms(kva, gkva_ref[...], kva.shape[-1])
        ckv_out[rs, :] = ckv
        ckv_b = ckv.astype(BF16)
        k = _dot(ckv_b, wuk_ref[...])
        k_rope = (None, col("k_rope_sw", HEAD_PAD), gks_ref[...]) + rope_tabs if use_rope else None
        k_out[rs, :] = _heads_norm(k, gk_ref[...], 1.0, shared=col("k_rope", HEAD_PAD), rope=k_rope).astype(BF16)
        v_out[rs, :] = _dot(ckv_b, wuv_ref[...]).astype(BF16)


def _inproj_call(x, mod, lw, rope, seq_len, per_batch, cols):
    t, d = x.shape
    att0 = cols["k_rope"]
    n_att = lw["w_att"].shape[1]
    n_keep = att0 + HEAD_PAD
    tm = 512
    hq = MLA_HEADS * HEAD_PAD
    hv = MLA_HEADS * V_HEAD
    qlora = lw["w_uq"].shape[0]
    kvlora = lw["w_uk"].shape[0]
    l = lw["l"]
    mspec = functools.partial(_mod_spec, d=d, tm=tm, seq_len=seq_len, per_batch=per_batch)
    const = lambda i: (0, 0)
    row = lambda i: (i, 0)
    in_specs = [
        pl.BlockSpec((tm, d), row), mspec(0), mspec(1),
        pl.BlockSpec((1, d), const),
        pl.BlockSpec((None, att0, d), lambda i: (l, 0, 0), pipeline_mode=pl.Buffered(1)),
        pl.BlockSpec((None, n_att, d), lambda i: (l, 0, 0), pipeline_mode=pl.Buffered(1)),
        pl.BlockSpec((1, qlora), const), pl.BlockSpec((1, kvlora), const),
        pl.BlockSpec((1, HEAD_PAD), const), pl.BlockSpec((1, HEAD_PAD), const),
        pl.BlockSpec((qlora, hq), const), pl.BlockSpec((kvlora, hq), const), pl.BlockSpec((kvlora, hv), const),
    ]
    args = [x, mod, mod, lw["norm1_g"], lw["w_tail"], lw["w_att"], lw["q_a_norm_g"], lw["kv_a_norm_g"], lw["q_norm_g"],
            lw["k_norm_g"], lw["w_uq"], lw["w_uk"], lw["w_uv"]]
    if rope is not None:
        nblk = seq_len // tm
        in_specs += [
            pl.BlockSpec((1, HEAD_PAD), const), pl.BlockSpec((1, HEAD_PAD), const),
            pl.BlockSpec((qlora, hq), const),
            pl.BlockSpec((tm, HEAD_PAD), lambda i: (i % nblk, 0)),
            pl.BlockSpec((tm, HEAD_PAD), lambda i: (i % nblk, 0)),
        ]
        args += [lw["q_norm_g_sw"], lw["k_norm_g_sw"], lw["w_uq_sw"]] + list(rope)
    q_mult = LOG2E * QK_HEAD ** -0.5
    nu = (cols["pool_u"] - cols["ssm_u"]) // LANES
    return pl.pallas_call(
        functools.partial(_inproj_kernel, use_rope=rope is not None, q_mult=q_mult, cols=cols, n_keep=n_keep),
        out_shape=(jax.ShapeDtypeStruct((t, n_keep), F32), jax.ShapeDtypeStruct((t, d), BF16),
                   jax.ShapeDtypeStruct((t, hq), BF16), jax.ShapeDtypeStruct((t, hq), BF16),
                   jax.ShapeDtypeStruct((t, hv), BF16), jax.ShapeDtypeStruct((t, kvlora), F32))
        + (jax.ShapeDtypeStruct((t, LANES), F32),) * nu,
        grid=(t // tm,),
        in_specs=in_specs,
        out_specs=(pl.BlockSpec((tm, n_keep), row), pl.BlockSpec((tm, d), row),
                   pl.BlockSpec((tm, hq), row), pl.BlockSpec((tm, hq), row),
                   pl.BlockSpec((tm, hv), row), pl.BlockSpec((tm, kvlora), row))
        + (pl.BlockSpec((tm, LANES), row),) * nu,
        compiler_params=_cparams(("parallel",), 56),
        name="in_proj",
    )(*args)


def _ctxprep_kernel(ckv_ref, kpe_ref, gk_ref, wuk_ref, wuv_ref, k_out, v_out):
    ckv_b = ckv_ref[...].astype(BF16)
    k = _dot(ckv_b, wuk_ref[...])
    k_out[...] = _heads_norm(k, gk_ref[...], 1.0, shared=kpe_ref[...]).astype(BF16)
    v_out[...] = _dot(ckv_b, wuv_ref[...]).astype(BF16)


def _ctxprep_call(ckv, kpe, lw):
    t, kvlora = ckv.shape
    tm = 512
    hq = MLA_HEADS * HEAD_PAD
    hv = MLA_HEADS * V_HEAD
    const = lambda i: (0, 0)
    return pl.pallas_call(
        _ctxprep_kernel,
        out_shape=(jax.ShapeDtypeStruct((t, hq), BF16), jax.ShapeDtypeStruct((t, hv), BF16)),
        grid=(t // tm,),
        in_specs=[
            pl.BlockSpec((tm, kvlora), lambda i: (i, 0)),
            pl.BlockSpec((tm, HEAD_PAD), lambda i: (i, 0)),
            pl.BlockSpec((1, HEAD_PAD), const),
            pl.BlockSpec((kvlora, hq), const),
            pl.BlockSpec((kvlora, hv), const),
        ],
        out_specs=(pl.BlockSpec((tm, hq), lambda i: (i, 0)), pl.BlockSpec((tm, hv), lambda i: (i, 0))),
        compiler_params=_cparams(("parallel",), 40),
        name="ctx_prep",
    )(ckv, kpe, lw["k_norm_g"], lw["w_uk"], lw["w_uv"])


def _attn_kernel(*refs, has_ctx, heads):
    if has_ctx:
        q_ref, ko_ref, vo_ref, kc_ref, vc_ref, o_ref = refs
    else:
        q_ref, ko_ref, vo_ref, o_ref = refs
    contract_last = (((1,), (1,)), ((), ()))
    lane = lax.broadcasted_iota(jnp.int32, (q_ref.shape[0], 2 * V_HEAD), 1)
    for pair in range(heads // 2):
        vsl = slice(pair * 2 * V_HEAD, (pair + 1) * 2 * V_HEAD)
        outs = []
        for hh in range(2):
            h = 2 * pair + hh
            sl = slice(h * HEAD_PAD, (h + 1) * HEAD_PAD)
            q = q_ref[:, sl]
            s_o = lax.dot_general(q, ko_ref[:, sl], contract_last, preferred_element_type=F32)
            m = jnp.max(s_o, axis=-1, keepdims=True)
            if has_ctx:
                s_c = lax.dot_general(q, kc_ref[:, sl], contract_last, preferred_element_type=F32)
                m = jnp.maximum(m, jnp.max(s_c, axis=-1, keepdims=True))
            p_o = jnp.exp2(s_o - m)
            l = jnp.sum(p_o, axis=-1, keepdims=True)
            acc = _dot(p_o.astype(BF16), vo_ref[:, vsl])
            if has_ctx:
                p_c = jnp.exp2(s_c - m)
                l = l + jnp.sum(p_c, axis=-1, keepdims=True)
                acc = acc + _dot(p_c.astype(BF16), vc_ref[:, vsl])
            outs.append(acc / l)
        o_ref[:, vsl] = jnp.where(lane < V_HEAD, outs[0], outs[1]).astype(BF16)


def _attn_call(q, k, v, kc, vc, batch, seq_len, ctx_len):
    t = q.shape[0]
    tq = min(seq_len, 512)
    heads = 8
    nq = seq_len // tq
    npair = MLA_HEADS // heads
    qw = heads * HEAD_PAD
    vw = heads * V_HEAD
    in_specs = [
        pl.BlockSpec((tq, qw), lambda b, h, i: (b * nq + i, h)),
        pl.BlockSpec((seq_len, qw), lambda b, h, i: (b, h)),
        pl.BlockSpec((seq_len, vw), lambda b, h, i: (b, h)),
    ]
    args = [q, k, v]
    if kc is not None:
        in_specs += [pl.BlockSpec((ctx_len, qw), lambda b, h, i: (b, h)),
                     pl.BlockSpec((ctx_len, vw), lambda b, h, i: (b, h))]
        args += [kc, vc]
    return pl.pallas_call(
        functools.partial(_attn_kernel, has_ctx=kc is not None, heads=heads),
        out_shape=jax.ShapeDtypeStruct((t, MLA_HEADS * V_HEAD), BF16),
        grid=(batch, npair, nq),
        in_specs=in_specs,
        out_specs=pl.BlockSpec((tq, vw), lambda b, h, i: (b * nq + i, h)),
        compiler_params=_cparams(("parallel", "parallel", "arbitrary"), 56),
        name="attention",
    )(*args)


def _convpool_kernel(cu_ref, bg_ref, cg_ref, pu_ref, cw_ref, cb_ref, pw_ref, ps_ref, zb_ref, zd_ref):
    n = cu_ref.shape[0]
    gps = pw_ref.shape[0]
    edge = max(POOL_WINDOWS) // 2
    assert edge % SUBLANES == 0 and n >= 4 * edge

    def shifts(rows, zero_fill):
        def down(x, k):
            y = pltpu.roll(x, k, axis=0)
            if zero_fill:
                y = jnp.where(lax.broadcasted_iota(jnp.int32, x.shape, 0) >= k, y, 0.0)
            return y

        def up(x, k):
            y = pltpu.roll(x, rows - k, axis=0)
            if zero_fill:
                y = jnp.where(lax.broadcasted_iota(jnp.int32, x.shape, 0) < rows - k, y, 0.0)
            return y
        return down, up

    def conv_rows(r0, rows, zero_fill):
        down, up = shifts(rows, zero_fill)
        v = cg_ref[r0:r0 + rows, :] * cu_ref[r0:r0 + rows, :]
        conv = (down(v, 1) * cw_ref[0:1, :] + v * cw_ref[1:2, :] + up(v, 1) * cw_ref[2:3, :]
                + cb_ref[...])
        return (bg_ref[r0:r0 + rows, :] * conv).astype(BF16)

    def pool_rows(r0, rows, zero_fill, gg, half):
        down, up = shifts(rows, zero_fill)
        sl = slice(gg * LANES, (gg + 1) * LANES)
        u = pu_ref[r0:r0 + rows, sl]
        fwd = u
        bwd = u
        k = 1
        while k < half:
            fwd = fwd + up(fwd, k)
            bwd = bwd + down(bwd, k)
            k *= 2
        total = fwd + down(bwd, 1)
        rowf = (lax.broadcasted_iota(jnp.int32, u.shape, 0) + r0).astype(F32)
        cnt = jnp.minimum(rowf + half, float(n)) - jnp.maximum(rowf - half, 0.0)
        mean = total / cnt - u
        return (_dot(mean.astype(BF16), pw_ref[gg]) * ps_ref[:, sl]).astype(BF16)

    zb_ref[...] = conv_rows(0, n, False)
    zb_ref[0:edge, :] = conv_rows(0, 2 * edge, True)[0:edge]
    zb_ref[n - edge:n, :] = conv_rows(n - 2 * edge, 2 * edge, True)[edge:]

    step = pl.program_id(1)
    for first in range(0, len(POOL_WINDOWS), gps):
        @pl.when(step * gps == first)
        def _(first=first):
            for gg in range(gps):
                sl = slice(gg * LANES, (gg + 1) * LANES)
                half = POOL_WINDOWS[first + gg] // 2
                zd_ref[:, sl] = pool_rows(0, n, False, gg, half)
                zd_ref[0:edge, sl] = pool_rows(0, 2 * edge, True, gg, half)[0:edge]
                zd_ref[n - edge:n, sl] = pool_rows(n - 2 * edge, 2 * edge, True, gg, half)[edge:]


def _convpool_call(p, lw, batch, seq_len, cols):
    t = p.shape[0]
    gps = 2 if seq_len > 512 else len(POOL_WINDOWS)
    cw = gps * LANES
    nblk = lw["conv_w"].shape[1] // cw

    def pspec(name):
        base = cols[name] // cw
        return pl.BlockSpec((seq_len, cw), lambda b, g: (b, base + g))

    vec = lambda b, g: (0, g)
    return pl.pallas_call(
        _convpool_kernel,
        out_shape=(jax.ShapeDtypeStruct((t, nblk * cw), BF16), jax.ShapeDtypeStruct((t, nblk * cw), BF16)),
        grid=(batch, nblk),
        in_specs=[pspec("conv_u"), pspec("conv_bg"), pspec("conv_cg"), pspec("pool_u"),
                  pl.BlockSpec((3, cw), vec), pl.BlockSpec((1, cw), vec),
                  pl.BlockSpec((gps, LANES, LANES), lambda b, g: (g, 0, 0)), pl.BlockSpec((1, cw), vec)],
        out_specs=(pl.BlockSpec((seq_len, cw), lambda b, g: (b, g)),
                   pl.BlockSpec((seq_len, cw), lambda b, g: (b, g))),
        compiler_params=_cparams(("parallel", "parallel"), 40),
        name="conv_pool",
    )(p, p, p, p, lw["conv_w"], lw["conv_b"], lw["pool_w"], lw["pool_scale"])


SSM_R = 4
SSM_GB = LANES // SSM_GROUP_CH


def _ssm_kernel(*refs, steps, nchunks, nblk):
    u_refs = refs[:nblk]
    h0_ref, a_ref, mb_ref, m1_ref, m2_ref = refs[nblk:nblk + 5]
    y_refs = refs[nblk + 5:2 * nblk + 5]
    hf_ref, slab_scr, xs_scr, st_scr = refs[2 * nblk + 5:]
    d = pl.program_id(0)
    c = pl.program_id(2)
    width = st_scr.shape[1]
    bw = width // nblk

    @pl.when(c == 0)
    def _():
        st_scr[...] = h0_ref[0]

    for blk in range(nblk):
        for j in range(SSM_R):
            for b in range(SUBLANES):
                slab_scr[blk * SSM_R + j, pl.ds(b, steps, stride=SUBLANES), :] = (
                    u_refs[blk][b, pl.ds(j, steps, stride=SSM_R), :])

    def packed(blk):
        return jnp.concatenate([slab_scr[blk * SSM_R + j] for j in range(SSM_R)], axis=1).astype(BF16)

    for blk in range(nblk):
        xs_scr[:, blk * bw:(blk + 1) * bw] = _dot(packed(blk), mb_ref[0, blk])

    gw = width // 2
    for part in range(width // gw):
        offs = []
        for blk in range(part * nblk // 2, (part + 1) * nblk // 2):
            offs += [(blk * bw + k * LANES, blk * bw + bw // 2 + k * LANES) for k in range(bw // 2 // LANES)]
        coef = [(a_ref[0, :, r0:r0 + LANES], a_ref[0, :, i0:i0 + LANES]) for r0, i0 in offs]

        def body(t, carry, offs=offs, coef=coef):
            tt = t + d * (steps - 1 - 2 * t)
            row0 = pl.multiple_of(tt * SUBLANES, SUBLANES)
            new = []
            for (r0, i0), (a_re, a_im), x_re, x_im in zip(offs, coef, carry[0::2], carry[1::2]):
                n_re = (a_re * x_re - a_im * x_im) + xs_scr[pl.ds(row0, SUBLANES), r0:r0 + LANES]
                n_im = (a_re * x_im + a_im * x_re) + xs_scr[pl.ds(row0, SUBLANES), i0:i0 + LANES]
                xs_scr[pl.ds(row0, SUBLANES), r0:r0 + LANES] = x_re
                xs_scr[pl.ds(row0, SUBLANES), i0:i0 + LANES] = x_im
                new += [n_re, n_im]
            return tuple(new)

        init = []
        for r0, i0 in offs:
            init += [st_scr[:, r0:r0 + LANES], st_scr[:, i0:i0 + LANES]]
        final = lax.fori_loop(0, steps, body, tuple(init), unroll=2)
        for (r0, i0), x_re, x_im in zip(offs, final[0::2], final[1::2]):
            st_scr[:, r0:r0 + LANES] = x_re
            st_scr[:, i0:i0 + LANES] = x_im

    for blk in range(nblk):
        y = (_dot(xs_scr[:, blk * bw:(blk + 1) * bw].astype(BF16), m1_ref[0, blk])
             + _dot(packed(blk), m2_ref[0, blk]))
        for j in range(SSM_R):
            slab_scr[blk * SSM_R + j] = y[:, j * LANES:(j + 1) * LANES]
        for j in range(SSM_R):
            for b in range(SUBLANES):
                y_refs[blk][0, b, pl.ds(j, steps, stride=SSM_R), :] = (
                    slab_scr[blk * SSM_R + j, pl.ds(b, steps, stride=SUBLANES), :])

    @pl.when(c == nchunks - 1)
    def _():
        hf_ref[0] = st_scr[...]


def _ssm_call(us, h0, lw):
    batch, seq_len, _ = us[0].shape
    width = h0.shape[-1]
    nblk = lw["ssm_mb"].shape[2]
    assert len(us) == nblk
    l = lw["l"]
    steps = min(seq_len // SSM_R, 64)
    tokens = steps * SSM_R
    nchunks = seq_len // tokens
    nbg = batch // SUBLANES
    rows = steps * SUBLANES

    def cidx(d, c):
        return c + d * (nchunks - 1 - 2 * c)

    uspec = pl.BlockSpec((SUBLANES, tokens, LANES), lambda d, b, c: (b, cidx(d, c), 0))

    def wspec(arr):
        return pl.BlockSpec((None, 1) + arr.shape[2:], lambda d, b, c: (l, d, 0, 0, 0),
                            pipeline_mode=pl.Buffered(1))

    yshape = jax.ShapeDtypeStruct((2, batch, seq_len, LANES), F32)
    yspec = pl.BlockSpec((1, SUBLANES, tokens, LANES), lambda d, b, c: (d, b, cidx(d, c), 0))
    outs = pl.pallas_call(
        functools.partial(_ssm_kernel, steps=steps, nchunks=nchunks, nblk=nblk),
        out_shape=(yshape,) * nblk + (jax.ShapeDtypeStruct((2, batch, width), F32),),
        grid=(2, nbg, nchunks),
        in_specs=[uspec] * nblk + [
            pl.BlockSpec((1, SUBLANES, width), lambda d, b, c: (d, b, 0)),
            pl.BlockSpec((None, 1, SUBLANES, width), lambda d, b, c: (l, d, 0, 0)),
            wspec(lw["ssm_mb"]), wspec(lw["ssm_m1"]), wspec(lw["ssm_m2"]),
        ],
        out_specs=(yspec,) * nblk + (pl.BlockSpec((1, SUBLANES, width), lambda d, b, c: (d, b, 0)),),
        scratch_shapes=[pltpu.VMEM((nblk * SSM_R, rows, LANES), F32), pltpu.VMEM((rows, width), F32),
                        pltpu.VMEM((SUBLANES, width), F32)],
        compiler_params=_cparams(("arbitrary", "arbitrary", "arbitrary"), 52),
        name="s5_scan",
    )(*us, h0, lw["ssm_a"], lw["ssm_mb"], lw["ssm_m1"], lw["ssm_m2"])
    return outs[:nblk], outs[nblk]


def _merge_kernel(*refs, nblk, ncast):
    (h_ref, za_ref, zb_ref, su_ref, sd_ref, zd_ref, wg0, wg1, wg2, wg3, wa, wb, wca, wcg, wd) = refs[:15]
    yf_refs = refs[15:15 + nblk]
    yb_refs = refs[15 + nblk:15 + 2 * nblk]
    cast_in = refs[15 + 2 * nblk:15 + 2 * nblk + ncast]
    o_ref = refs[15 + 2 * nblk + ncast]
    cast_out = refs[16 + 2 * nblk + ncast:16 + 2 * nblk + 2 * ncast]
    y_scr = refs[-1]

    for src, dst in zip(cast_in, cast_out):
        dst[...] = src[...].astype(BF16)

    @pl.when(pl.program_id(1) == 0)
    def _():
        y_both = jnp.concatenate([f[0] + b[0] for f, b in zip(yf_refs, yb_refs)], axis=1)
        y_scr[...] = (y_both + sd_ref[...] * su_ref[...]).astype(BF16)

    tm = h_ref.shape[0]
    nsub = 2
    for sub in range(nsub):
        rs = slice(sub * tm // nsub, (sub + 1) * tm // nsub)
        h = h_ref[rs, :]
        y = y_scr[rs, :]

        def gate(w_ref, h=h):
            return jax.nn.sigmoid(_dot_t(h, w_ref[...]))

        merged = gate(wg0) * _dot(za_ref[rs, :], wa[...])
        merged = merged + gate(wg1) * _dot(zb_ref[rs, :], wb[...])
        merged = merged + gate(wg2) * (_dot(y, wca[...]) * jax.nn.sigmoid(_dot(y, wcg[...])))
        merged = merged + gate(wg3) * _dot(zd_ref[rs, :], wd[...])
        o_ref[rs, :] = merged.astype(BF16)


def _merge_call(h, p, za, zb, ys, zd, lw, cols, cast=()):
    t, d = h.shape
    bw = za.shape[1]
    tm, tn = 512, 512
    nj = d // tn
    nblk = len(ys)
    row = lambda i, j: (i, 0)
    su_blk = cols["ssm_u"] // bw

    l = lw["l"]

    gate_blk0 = cols["k_rope"] // tn

    def gspec(k):
        return pl.BlockSpec((None, tn, d), lambda i, j: (l, gate_blk0 + k * nj + j, 0))

    def yspec(direction):
        return pl.BlockSpec((1, tm, LANES), lambda i, j: (direction, i, 0))

    wcol = pl.BlockSpec((None, bw, tn), lambda i, j: (l, 0, j))
    in_specs = [
        pl.BlockSpec((tm, d), row),
        pl.BlockSpec((tm, bw), row), pl.BlockSpec((tm, bw), row),
        pl.BlockSpec((tm, bw), lambda i, j: (i, su_blk)),
        pl.BlockSpec((1, bw), lambda i, j: (0, 0)),
        pl.BlockSpec((tm, bw), row),
        gspec(0), gspec(1), gspec(2), gspec(3),
        wcol, wcol, wcol, pl.BlockSpec((None, bw, tn), lambda i, j: (l, 0, nj + j)), wcol,
    ] + [yspec(0)] * nblk + [yspec(1)] * nblk
    out_shape = [jax.ShapeDtypeStruct((t, d), BF16)]
    out_specs = [pl.BlockSpec((tm, tn), lambda i, j: (i, j))]
    nsteps = (t // tm) * nj
    for w in cast:
        rows, n = w.shape[1] // nsteps, w.shape[2]
        assert rows * nsteps == w.shape[1] and rows % (2 * SUBLANES) == 0
        in_specs.append(pl.BlockSpec((None, rows, n), lambda i, j: (l, i * nj + j, 0)))
        out_shape.append(jax.ShapeDtypeStruct(w.shape[1:], BF16))
        out_specs.append(pl.BlockSpec((rows, n), lambda i, j: (i * nj + j, 0)))
    wg = lw["w_tail"]
    outs = pl.pallas_call(
        functools.partial(_merge_kernel, nblk=nblk, ncast=len(cast)),
        out_shape=tuple(out_shape),
        grid=(t // tm, nj),
        in_specs=in_specs,
        out_specs=tuple(out_specs),
        scratch_shapes=[pltpu.VMEM((tm, bw), BF16)],
        compiler_params=_cparams(("parallel", "arbitrary"), 48),
        name="branch_merge",
    )(h, za, zb, p, lw["ssm_d"], zd,
      wg, wg, wg, wg, lw["w_mla_o"], lw["w_conv_o"], lw["w_glu"], lw["w_glu"], lw["w_pool_o"], *ys, *ys, *cast)
    return outs[0], tuple(outs[1:])


def _outproj_kernel(x_ref, m_ref, gt_ref, sh_ref, sc_ref, g_ref, wo_ref, x1_ref, h2_ref):
    tm = x_ref.shape[0]
    nsub = 2
    for sub in range(nsub):
        rs = slice(sub * tm // nsub, (sub + 1) * tm // nsub)
        x1 = x_ref[rs, :] + gt_ref[0] * _dot(m_ref[rs, :], wo_ref[...])
        x1_ref[rs, :] = x1
        h2_ref[rs, :] = _modnorm(x1, g_ref[...], sc_ref[0], sh_ref[0]).astype(BF16)


def _outproj_call(x, merged, mod, lw, seq_len, per_batch):
    t, d = x.shape
    tm = 512
    mspec = functools.partial(_mod_spec, d=d, tm=tm, seq_len=seq_len, per_batch=per_batch)
    row = lambda i: (i, 0)
    return pl.pallas_call(
        _outproj_kernel,
        out_shape=(jax.ShapeDtypeStruct((t, d), F32), jax.ShapeDtypeStruct((t, d), BF16)),
        grid=(t // tm,),
        in_specs=[
            pl.BlockSpec((tm, d), row), pl.BlockSpec((tm, d), row),
            mspec(2), mspec(3), mspec(4),
            pl.BlockSpec((1, d), lambda i: (0, 0)),
            pl.BlockSpec((None, d, d), lambda i: (lw["l"], 0, 0), pipeline_mode=pl.Buffered(1)),
        ],
        out_specs=(pl.BlockSpec((tm, d), row), pl.BlockSpec((tm, d), row)),
        compiler_params=_cparams(("parallel",), 48),
        name="out_proj",
    )(x, merged, mod, mod, mod, lw["norm2_g"], lw["w_o"])


def _mlp_kernel(x_ref, h_ref, gt_ref, w1_ref, w2_ref, o_ref):
    @pl.when(pl.program_id(1) == 0)
    def _():
        o_ref[...] = x_ref[...]

    hid = jnp.square(jnp.maximum(_dot(h_ref[...], w1_ref[...]), 0.0))
    o_ref[...] += gt_ref[0] * _dot(hid.astype(BF16), w2_ref[...])


def _mlp_call(x, h, mod, lw, seq_len, per_batch):
    t, d = x.shape
    hidden = lw["w_mlp1"].shape[2]
    tm, tk = 512, 1024
    l = lw["l_mlp"]
    mspec = functools.partial(_mod_spec, d=d, tm=tm, seq_len=seq_len, per_batch=per_batch)
    row = lambda i, k: (i, 0)
    return pl.pallas_call(
        _mlp_kernel,
        out_shape=jax.ShapeDtypeStruct((t, d), F32),
        grid=(t // tm, hidden // tk),
        in_specs=[
            pl.BlockSpec((tm, d), row), pl.BlockSpec((tm, d), row), mspec(5),
            pl.BlockSpec((None, d, tk), lambda i, k: (l, 0, k)),
            pl.BlockSpec((None, tk, d), lambda i, k: (l, k, 0)),
        ],
        out_specs=pl.BlockSpec((tm, d), row),
        compiler_params=_cparams(("parallel", "arbitrary"), 48),
        name="mlp",
    )(x, h, mod, lw["w_mlp1"], lw["w_mlp2"])


def _pad_heads(w, per_head, lo):
    lead = w.shape[:-1]
    w = w.reshape(lead + (MLA_HEADS, per_head))
    w = jnp.pad(w, [(0, 0)] * len(lead) + [(0, 0), (lo, HEAD_PAD - lo - per_head)])
    return w.reshape(lead + (MLA_HEADS * HEAD_PAD,))


_ROPE_PARTNER = tuple((i // 16) * 16 + (i % 16 + 8) % 16 for i in range(ROPE_DIM))


def _swap_heads(w):
    lead = w.shape[:-1]
    w = w.reshape(lead + (MLA_HEADS, QK_HEAD))[..., QK_NOPE:][..., jnp.array(_ROPE_PARTNER)]
    w = jnp.pad(w, [(0, 0)] * len(lead) + [(0, 0), (QK_NOPE, HEAD_PAD - QK_HEAD)])
    return w.reshape(lead + (MLA_HEADS * HEAD_PAD,))


def _prep_ssm(lam_re, lam_im, log_step, b_re, b_im, c_re, c_im):
    r, gb = SSM_R, SSM_GB
    nl, ndir, g, n, cg = b_re.shape
    ns = nl * ndir
    nblk = g // gb
    flat = lambda v: v.reshape((ns,) + v.shape[2:])
    lam_re, lam_im, log_step, b_re, b_im, c_re, c_im = map(
        flat, (lam_re, lam_im, log_step, b_re, b_im, c_re, c_im))
    step = jnp.exp(log_step)[..., None]
    pw = []
    for p in range(r + 1):
        mag = jnp.exp(p * (lam_re * step))
        pw.append((mag * jnp.cos(p * (lam_im * step)), mag * jnp.sin(p * (lam_im * step))))
    ar, ai = pw[1]
    den = lam_re * lam_re + lam_im * lam_im
    qr = ((ar - 1.0) * lam_re + ai * lam_im) / den
    qi = (ai * lam_re - (ar - 1.0) * lam_im) / den
    bb_re = qr[..., None] * b_re - qi[..., None] * b_im
    bb_im = qr[..., None] * b_im + qi[..., None] * b_re
    fwd = (jnp.arange(ns) % ndir == 0)[:, None, None]

    def power(p_fwd, p_bwd):
        return (jnp.where(fwd, pw[p_fwd][0], pw[p_bwd][0]), jnp.where(fwd, pw[p_fwd][1], pw[p_bwd][1]))

    rows_g = (jnp.arange(r * gb * cg) // cg) % gb
    cols_g = jnp.arange(gb * n) // n

    t_re, t_im = [], []
    for j in range(r):
        pr, pi = power(r - 1 - j, j)
        t_re.append(pr[..., None] * bb_re - pi[..., None] * bb_im)
        t_im.append(pr[..., None] * bb_im + pi[..., None] * bb_re)
    mask_b = (rows_g[:, None] == cols_g[None, :]).astype(F32)

    def inc_block(ts):
        t = jnp.stack(ts, axis=1).reshape(ns, r, nblk, gb, n, cg)
        t = jnp.transpose(t, (0, 2, 1, 3, 5, 4)).reshape(ns, nblk, r * gb * cg, n)
        return jnp.tile(t, (1, 1, 1, gb)) * mask_b

    mb = jnp.concatenate([inc_block(t_re), inc_block(t_im)], axis=-1)

    cl_re, cl_im = [], []
    for j in range(r):
        pr, pi = power(j + 1, r - j)
        cl_re.append(c_re * pr[:, :, None, :] - c_im * pi[:, :, None, :])
        cl_im.append(c_re * pi[:, :, None, :] + c_im * pr[:, :, None, :])

    def out_block(ts):
        t = jnp.stack(ts, axis=1).reshape(ns, r, nblk, gb, cg, n)
        t = jnp.transpose(t, (0, 2, 5, 1, 3, 4)).reshape(ns, nblk, n, r * gb * cg)
        return jnp.tile(t, (1, 1, gb, 1)) * mask_b.T

    m1 = jnp.concatenate([out_block(cl_re), -out_block(cl_im)], axis=2)

    hi = lax.Precision.HIGHEST
    lane_g = jnp.arange(gb * cg) // cg
    mask_k = (lane_g[:, None] == lane_g[None, :]).astype(F32)
    kd = []
    for p in range(r):
        pr, pi = pw[p]
        k2 = (jnp.einsum("sgcn,sgnk->sgck", c_re * pr[:, :, None, :] - c_im * pi[:, :, None, :], bb_re, precision=hi)
              - jnp.einsum("sgcn,sgnk->sgck", c_re * pi[:, :, None, :] + c_im * pr[:, :, None, :], bb_im, precision=hi))
        k2 = jnp.swapaxes(k2, -1, -2).reshape(ns, nblk, gb * cg, cg)
        kd.append(jnp.tile(k2, (1, 1, 1, gb)) * mask_k)
    fwd4 = fwd[..., None]
    zero = jnp.zeros_like(kd[0])
    m2_rows = []
    for i in range(r):
        blocks = []
        for j in range(r):
            if i == j:
                blocks.append(kd[0])
            elif j > i:
                blocks.append(jnp.where(fwd4, kd[j - i], zero))
            else:
                blocks.append(jnp.where(fwd4, zero, kd[i - j]))
        m2_rows.append(jnp.concatenate(blocks, axis=-1))
    m2 = jnp.concatenate(m2_rows, axis=2)

    a = jnp.stack([pw[r][0].reshape(ns, nblk, gb * n), pw[r][1].reshape(ns, nblk, gb * n)], axis=2)
    a = jnp.broadcast_to(a.reshape(ns, 1, 2 * g * n), (ns, SUBLANES, 2 * g * n))
    unflat = lambda v: v.reshape((nl, ndir) + v.shape[1:])
    return unflat(a), unflat(mb.astype(BF16)), unflat(m1.astype(BF16)), unflat(m2.astype(BF16))


def _prep_stacked(prm, sizes):
    q_lora, kv_lora, conv_w, ssm_w, pool_w_ = sizes
    w_in_t = jnp.swapaxes(prm["w_in"], 1, 2)
    tail0 = q_lora + kv_lora + ROPE_DIM
    w_tail = w_in_t[:, tail0:, :].astype(BF16)
    head = w_in_t[:, :tail0, :]
    cols = {}
    o = 0
    for name, width in (("conv_u", conv_w), ("conv_bg", conv_w), ("conv_cg", conv_w), ("ssm_u", ssm_w),
                        ("pool_u", pool_w_), ("k_rope", HEAD_PAD), ("k_rope_sw", HEAD_PAD), ("q_a", q_lora),
                        ("kv_a", kv_lora)):
        cols[name] = o
        o += width
    k_rope = head[:, q_lora + kv_lora:tail0, :]
    rope_lanes = ((0, 0), (QK_NOPE, HEAD_PAD - QK_HEAD), (0, 0))
    w_att = jnp.concatenate([jnp.pad(k_rope, rope_lanes),
                             jnp.pad(k_rope[:, jnp.array(_ROPE_PARTNER), :], rope_lanes),
                             head[:, :q_lora + kv_lora, :]], axis=1).astype(BF16)
    big = dict(w_tail=w_tail, w_att=w_att)
    for name in ("w_mla_o", "w_conv_o", "w_glu", "w_pool_o", "w_o"):
        big[name] = prm[name].astype(BF16)
    return big, cols


def _prep_layer(l, prm, sizes, big, ssm):
    kv_lora = sizes[1]
    w_ukv = prm["w_ukv"][l].reshape(kv_lora, MLA_HEADS, QK_NOPE + V_HEAD)
    w_uk = _pad_heads(w_ukv[:, :, :QK_NOPE].reshape(kv_lora, -1), QK_NOPE, 0)
    w_uv = w_ukv[:, :, QK_NOPE:].reshape(kv_lora, -1)
    ssm_a, ssm_mb, ssm_m1, ssm_m2 = ssm

    row = lambda v: v.reshape(1, -1)
    head_gain = lambda g: row(jnp.pad(g, (0, HEAD_PAD - QK_HEAD)))
    partner_gain = lambda g: row(jnp.pad(g[QK_NOPE:][jnp.array(_ROPE_PARTNER)], (QK_NOPE, HEAD_PAD - QK_HEAD)))
    lw = dict(
        big, l=l,
        norm1_g=row(prm["norm1_g"][l]), norm2_g=row(prm["norm2_g"][l]),
        q_a_norm_g=row(prm["q_a_norm_g"][l]), kv_a_norm_g=row(prm["kv_a_norm_g"][l]),
        q_norm_g=head_gain(prm["q_norm_g"][l]), k_norm_g=head_gain(prm["k_norm_g"][l]),
        q_norm_g_sw=partner_gain(prm["q_norm_g"][l]), k_norm_g_sw=partner_gain(prm["k_norm_g"][l]),
        w_uq=_pad_heads(prm["w_uq"][l], QK_HEAD, 0).astype(BF16),
        w_uq_sw=_swap_heads(prm["w_uq"][l]).astype(BF16),
        w_uk=w_uk.astype(BF16), w_uv=w_uv.astype(BF16),
        conv_w=prm["conv_w"][l], conv_b=row(prm["conv_b"][l]),
        ssm_a=ssm_a, ssm_mb=ssm_mb, ssm_m1=ssm_m1, ssm_m2=ssm_m2,
        ssm_d=row(prm["ssm_d"][l]),
        pool_w=prm["pool_w"][l].astype(BF16), pool_scale=row(prm["pool_scale"][l]),
    )
    return lw


def _rope_tables(seq_len):
    half = ROPE_DIM // 2
    inv_freq = ROPE_THETA ** (-jnp.arange(0, half, 2, dtype=F32) / half)
    t = jnp.arange(seq_len)
    ang_r = (t // GRID_W).astype(F32)[:, None] * inv_freq
    ang_c = (t % GRID_W).astype(F32)[:, None] * inv_freq
    ones_lo = jnp.ones((seq_len, QK_NOPE), F32)
    zeros_lo = jnp.zeros((seq_len, QK_NOPE), F32)
    tail = jnp.zeros((seq_len, HEAD_PAD - QK_HEAD), F32)
    c = jnp.concatenate([ones_lo, jnp.cos(ang_r), jnp.cos(ang_r), jnp.cos(ang_c), jnp.cos(ang_c), tail], axis=1)
    s = jnp.concatenate([zeros_lo, -jnp.sin(ang_r), jnp.sin(ang_r), -jnp.sin(ang_c), jnp.sin(ang_c), tail], axis=1)
    return c, s


def _states_to_cols(st):
    b, ndir, _, g, n = st.shape
    st = st.reshape(b, ndir, 2, g // SSM_GB, SSM_GB * n)
    return jnp.transpose(st, (1, 0, 3, 2, 4)).reshape(ndir, b, 2 * g * n)


def _cols_to_states(hf, g, n):
    ndir, b, _ = hf.shape
    hf = hf.reshape(ndir, b, g // SSM_GB, 2, SSM_GB * n)
    return jnp.transpose(hf, (1, 0, 3, 2, 4)).reshape(b, ndir, 2, g, n)


def _mixers(x, mod, lw, cols, batch, seq_len, per_batch, rope, ctx, cast=()):
    p, h1, q, k, v, ckv, *us = _inproj_call(x, mod, lw, rope, seq_len, per_batch, cols)
    if ctx is not None:
        ctx_ckv, ctx_kpe, h0 = ctx
        kc, vc = _ctxprep_call(ctx_ckv, ctx_kpe, lw)
        ctx_len = ctx_ckv.shape[0] // batch
    else:
        kc = vc = None
        ctx_len = 0
        h0 = jnp.zeros((2, batch, lw["ssm_a"].shape[-1]), F32)
    za = _attn_call(q, k, v, kc, vc, batch, seq_len, ctx_len)
    zb, zd = _convpool_call(p, lw, batch, seq_len, cols)

    ys, hf = _ssm_call([u.reshape(batch, seq_len, LANES) for u in us], h0, lw)
    ys = [y.reshape(2, batch * seq_len, LANES) for y in ys]
    merged, cast_out = _merge_call(h1, p, za, zb, ys, zd, lw, cols, cast)
    kr = cols["k_rope"] + QK_NOPE
    return merged, cast_out, ckv, p[:, kr:kr + ROPE_DIM], hf


def _channel_mixer(x, merged, mod, lw, seq_len, per_batch):
    x1, h2 = _outproj_call(x, merged, mod, lw, seq_len, per_batch)
    return _mlp_call(x1, h2, mod, lw, seq_len, per_batch)


def kernel(x_prompt, x_sample, c, cache_ckv, cache_krope, state_ssm, c_ctx, w_ada, b_ada, norm1_g, norm2_g, w_in, q_a_norm_g, kv_a_norm_g, w_uq, w_ukv, q_norm_g, k_norm_g, w_mla_o, conv_w, conv_b, w_conv_o, ssm_lam_re, ssm_lam_im, ssm_log_step, ssm_b_re, ssm_b_im, ssm_c_re, ssm_c_im, ssm_d, w_glu, pool_w, pool_scale, w_pool_o, w_o, w_mlp1, w_mlp2):
    prm = dict(norm1_g=norm1_g, norm2_g=norm2_g, w_in=w_in, q_a_norm_g=q_a_norm_g, kv_a_norm_g=kv_a_norm_g,
               w_uq=w_uq, w_ukv=w_ukv, q_norm_g=q_norm_g, k_norm_g=k_norm_g, w_mla_o=w_mla_o,
               conv_w=conv_w, conv_b=conv_b, w_conv_o=w_conv_o, ssm_lam_re=ssm_lam_re, ssm_lam_im=ssm_lam_im,
               ssm_log_step=ssm_log_step, ssm_b_re=ssm_b_re, ssm_b_im=ssm_b_im, ssm_c_re=ssm_c_re,
               ssm_c_im=ssm_c_im, ssm_d=ssm_d, w_glu=w_glu, pool_w=pool_w, pool_scale=pool_scale,
               w_pool_o=w_pool_o, w_o=w_o, w_mlp1=w_mlp1, w_mlp2=w_mlp2)
    depth = w_in.shape[0]
    bp, lp, d = x_prompt.shape
    bs, ls, _ = x_sample.shape
    past = cache_ckv.shape[2]
    g, n = state_ssm.shape[-2:]
    sizes = (w_uq.shape[1], w_ukv.shape[1], conv_w.shape[-1], ssm_d.shape[-1], pool_scale.shape[-1])

    rows = -(-(1 + bs) // SUBLANES) * SUBLANES
    cvec = jnp.zeros((rows, d), F32).at[0].set(c_ctx).at[1:1 + bs].set(c)
    mods = _ada_call(cvec, w_ada, b_ada)
    rope = _rope_tables(ls)
    ssm = _prep_ssm(ssm_lam_re, ssm_lam_im, ssm_log_step, ssm_b_re, ssm_b_im, ssm_c_re, ssm_c_im)
    big, cols = _prep_stacked(prm, sizes)

    yp = x_prompt.reshape(bp * lp, d)
    ys = x_sample.reshape(bs * ls, d)
    ckv_list, krope_list, ssm_list = [], [], []
    for l in range(depth):
        lw = _prep_layer(l, prm, sizes, big, ssm)
        mod_ctx = mods[l, 0:1].reshape(6, 1, d)
        mod_lat = mods[l, 1:1 + bs].reshape(bs * 6, 1, d)
        ctx_kpe = jnp.pad(cache_krope[:, l].reshape(bs * past, ROPE_DIM),
                          ((0, 0), (QK_NOPE, HEAD_PAD - QK_HEAD)))
        ctx = (cache_ckv[:, l].reshape(bs * past, -1), ctx_kpe, _states_to_cols(state_ssm[:, l]))
        merged_s, (w1b, w2b), _, _, _ = _mixers(ys, mod_lat, lw, cols, bs, ls, True, rope, ctx,
                                                cast=(w_mlp1, w_mlp2))
        merged_p, _, ckv_l, krope_l, hf_l = _mixers(yp, mod_ctx, lw, cols, bp, lp, False, None, None)
        ckv_list.append(ckv_l.reshape(bp, lp, -1))
        krope_list.append(krope_l.reshape(bp, lp, -1))
        ssm_list.append(_cols_to_states(hf_l, g, n))
        lw_mlp = dict(lw, w_mlp1=w1b[None], w_mlp2=w2b[None], l_mlp=0)
        yp = _channel_mixer(yp, merged_p, mod_ctx, lw_mlp, lp, False)
        ys = _channel_mixer(ys, merged_s, mod_lat, lw_mlp, ls, True)
    return (yp.reshape(bp, lp, d), ys.reshape(bs, ls, d), jnp.stack(ckv_list, axis=1),
            jnp.stack(krope_list, axis=1), jnp.stack(ssm_list, axis=1))
```

```python
import functools

import jax
import jax.numpy as jnp
from jax import lax
from jax.experimental import pallas as pl
from jax.experimental.pallas import tpu as pltpu

F32 = jnp.float32
BF16 = jnp.bfloat16

GRID_W = 64
MLA_HEADS = 8
QK_NOPE = 64
ROPE_DIM = 32
QK_HEAD = QK_NOPE + ROPE_DIM
V_HEAD = 64
ROPE_THETA = 10000.0
SSM_GROUP_CH = 16
POOL_WINDOWS = (2, 4, 8, 16)
EPS = 1e-6
LOG2E = 1.4426950408889634

LANES = 128
SUBLANES = 8
HEAD_PAD = LANES
MIB = 1024 * 1024


def _cparams(sem, vmem_mib):
    return pltpu.CompilerParams(dimension_semantics=sem, vmem_limit_bytes=vmem_mib * MIB)


def _dot(a, b):
    return jnp.dot(a, b, preferred_element_type=F32)


def _dot_t(a, bt):
    return lax.dot_general(a, bt, (((1,), (1,)), ((), ())), preferred_element_type=F32)


def _mod_spec(chunk, d, tm, seq_len, per_batch):
    def imap(i, *_):
        row = (i * tm) // seq_len if per_batch else 0
        return (row * 6 + chunk, 0, 0)
    return pl.BlockSpec((1, 1, d), imap)


def _ada_kernel(c_ref, w_ref, b_ref, o_ref):
    cv = c_ref[...]
    s = cv * jax.nn.sigmoid(cv)
    o_ref[0] = _dot(s.astype(BF16), w_ref[0].astype(BF16)) + b_ref[0]


def _ada_call(cvec, w_ada, b_ada):
    depth, d, n = w_ada.shape
    rows = cvec.shape[0]
    tn = 1024
    return pl.pallas_call(
        _ada_kernel,
        out_shape=jax.ShapeDtypeStruct((depth, rows, n), F32),
        grid=(depth, n // tn),
        in_specs=[
            pl.BlockSpec((rows, d), lambda l, j: (0, 0)),
            pl.BlockSpec((1, d, tn), lambda l, j: (l, 0, j)),
            pl.BlockSpec((1, 1, tn), lambda l, j: (l, 0, j)),
        ],
        out_specs=pl.BlockSpec((1, rows, tn), lambda l, j: (l, 0, j)),
        compiler_params=_cparams(("parallel", "parallel"), 40),
        name="ada_mod",
    )(cvec, w_ada, b_ada.reshape(depth, 1, n))


def _modnorm(x, g, scale, shift):
    ms = jnp.mean(x * x, axis=-1, keepdims=True)
    return (x * lax.rsqrt(ms + EPS) * g) * (1.0 + scale) + shift


def _rms(x, g, width):
    ms = jnp.sum(x * x, axis=-1, keepdims=True) * (1.0 / width)
    return x * lax.rsqrt(ms + EPS) * g


def _heads_norm(x, g, mult, shared=None, rope=None):
    t_main = g * mult
    if rope is not None:
        xs, shared_s, gs, cos, sin = rope
        t_main = cos * t_main
        t_part = sin * (gs * mult)
    outs = []
    for h in range(MLA_HEADS):
        sl = slice(h * HEAD_PAD, (h + 1) * HEAD_PAD)
        xh = x[:, sl] if shared is None else x[:, sl] + shared
        ms = jnp.sum(xh * xh, axis=-1, keepdims=True) * (1.0 / QK_HEAD)
        y = xh * t_main
        if rope is not None:
            y = y + (shared_s if xs is None else xs[:, sl]) * t_part
        outs.append(y * lax.rsqrt(ms + EPS))
    return jnp.concatenate(outs, axis=1)


def _inproj_kernel(*refs, use_rope, q_mult, cols, n_keep):
    (x_ref, sh_ref, sc_ref, g_ref, wmix_ref, watt_ref, gqa_ref, gkva_ref, gq_ref, gk_ref,
     wuq_ref, wuk_ref, wuv_ref) = refs[:13]
    if use_rope:
        gqs_ref, gks_ref, wuqs_ref, c_ref, s_ref = refs[13:18]
    nu = (cols["pool_u"] - cols["ssm_u"]) // LANES
    p_out, h_out, q_out, k_out, v_out, ckv_out = refs[-6 - nu:-nu]
    u_outs = refs[-nu:]
    att0 = cols["k_rope"]
    tm = x_ref.shape[0]
    nsub = 2

    for sub in range(nsub):
        rs = slice(sub * tm // nsub, (sub + 1) * tm // nsub)
        h = _modnorm(x_ref[rs, :], g_ref[...], sc_ref[0], sh_ref[0]).astype(BF16)
        h_out[rs, :] = h
        res = _dot_t(h, watt_ref[...])
        mix = _dot_t(h, wmix_ref[...])
        p_out[rs, :att0] = mix
        p_out[rs, att0:] = res[:, :n_keep - att0]
        for blk, u_out in enumerate(u_outs):
            u_out[rs, :] = mix[:, cols["ssm_u"] + blk * LANES:cols["ssm_u"] + (blk + 1) * LANES]

        def col(name, width, res=res):
            return res[:, cols[name] - att0:cols[name] - att0 + width]

        rope_tabs = (c_ref[rs, :], s_ref[rs, :]) if use_rope else None
        qa = col("q_a", wuq_ref.shape[0])
        qa_n = _rms(qa, gqa_ref[...], qa.shape[-1]).astype(BF16)
        q = _dot(qa_n, wuq_ref[...])
        q_rope = (_dot(qa_n, wuqs_ref[...]), None, gqs_ref[...]) + rope_tabs if use_rope else None
        q_out[rs, :] = _heads_norm(q, gq_ref[...], q_mult, rope=q_rope).astype(BF16)

        kva = col("kv_a", wuk_ref.shape[0])
        ckv = _rms(kva, gkva_ref[...], kva.shape[-1])
        ckv_out[rs, :] = ckv
        ckv_b = ckv.astype(BF16)
        k = _dot(ckv_b, wuk_ref[...])
        k_rope = (None, col("k_rope_sw", HEAD_PAD), gks_ref[...]) + rope_tabs if use_rope else None
        k_out[rs, :] = _heads_norm(k, gk_ref[...], 1.0, shared=col("k_rope", HEAD_PAD), rope=k_rope).astype(BF16)
        v_out[rs, :] = _dot(ckv_b, wuv_ref[...]).astype(BF16)


def _inproj_call(x, mod, lw, rope, seq_len, per_batch, cols):
    t, d = x.shape
    att0 = cols["k_rope"]
    n_att = lw["w_att"].shape[1]
    n_keep = att0 + HEAD_PAD
    tm = 512
    hq = MLA_HEADS * HEAD_PAD
    hv = MLA_HEADS * V_HEAD
    qlora = lw["w_uq"].shape[0]
    kvlora = lw["w_uk"].shape[0]
    l = lw["l"]
    mspec = functools.partial(_mod_spec, d=d, tm=tm, seq_len=seq_len, per_batch=per_batch)
    const = lambda i: (0, 0)
    row = lambda i: (i, 0)
    in_specs = [
        pl.BlockSpec((tm, d), row), mspec(0), mspec(1),
        pl.BlockSpec((1, d), const),
        pl.BlockSpec((pl.Element(att0), pl.Element(d)), lambda i: (l * lw["w_rows"] + lw["tail_row0"], 0),
                     pipeline_mode=pl.Buffered(1)),
        pl.BlockSpec((None, n_att, d), lambda i: (l, 0, 0), pipeline_mode=pl.Buffered(1)),
        pl.BlockSpec((1, qlora), const), pl.BlockSpec((1, kvlora), const),
        pl.BlockSpec((1, HEAD_PAD), const), pl.BlockSpec((1, HEAD_PAD), const),
        pl.BlockSpec((qlora, hq), const), pl.BlockSpec((kvlora, hq), const), pl.BlockSpec((kvlora, hv), const),
    ]
    args = [x, mod, mod, lw["norm1_g"], lw["w_tail"], lw["w_att"], lw["q_a_norm_g"], lw["kv_a_norm_g"], lw["q_norm_g"],
            lw["k_norm_g"], lw["w_uq"], lw["w_uk"], lw["w_uv"]]
    if rope is not None:
        nblk = seq_len // tm
        in_specs += [
            pl.BlockSpec((1, HEAD_PAD), const), pl.BlockSpec((1, HEAD_PAD), const),
            pl.BlockSpec((qlora, hq), const),
            pl.BlockSpec((tm, HEAD_PAD), lambda i: (i % nblk, 0)),
            pl.BlockSpec((tm, HEAD_PAD), lambda i: (i % nblk, 0)),
        ]
        args += [lw["q_norm_g_sw"], lw["k_norm_g_sw"], lw["w_uq_sw"]] + list(rope)
    q_mult = LOG2E * QK_HEAD ** -0.5
    nu = (cols["pool_u"] - cols["ssm_u"]) // LANES
    return pl.pallas_call(
        functools.partial(_inproj_kernel, use_rope=rope is not None, q_mult=q_mult, cols=cols, n_keep=n_keep),
        out_shape=(jax.ShapeDtypeStruct((t, n_keep), F32), jax.ShapeDtypeStruct((t, d), BF16),
                   jax.ShapeDtypeStruct((t, hq), BF16), jax.ShapeDtypeStruct((t, hq), BF16),
                   jax.ShapeDtypeStruct((t, hv), BF16), jax.ShapeDtypeStruct((t, kvlora), F32))
        + (jax.ShapeDtypeStruct((t, LANES), F32),) * nu,
        grid=(t // tm,),
        in_specs=in_specs,
        out_specs=(pl.BlockSpec((tm, n_keep), row), pl.BlockSpec((tm, d), row),
                   pl.BlockSpec((tm, hq), row), pl.BlockSpec((tm, hq), row),
                   pl.BlockSpec((tm, hv), row), pl.BlockSpec((tm, kvlora), row))
        + (pl.BlockSpec((tm, LANES), row),) * nu,
        compiler_params=_cparams(("parallel",), 56),
        name="in_proj",
    )(*args)


def _ctxprep_kernel(ckv_ref, kpe_ref, gk_ref, wuk_ref, wuv_ref, k_out, v_out):
    ckv_b = ckv_ref[...].astype(BF16)
    k = _dot(ckv_b, wuk_ref[...])
    k_out[...] = _heads_norm(k, gk_ref[...], 1.0, shared=kpe_ref[...]).astype(BF16)
    v_out[...] = _dot(ckv_b, wuv_ref[...]).astype(BF16)


def _ctxprep_call(ckv, kpe, lw):
    t, kvlora = ckv.shape
    tm = 512
    hq = MLA_HEADS * HEAD_PAD
    hv = MLA_HEADS * V_HEAD
    const = lambda i: (0, 0)
    return pl.pallas_call(
        _ctxprep_kernel,
        out_shape=(jax.ShapeDtypeStruct((t, hq), BF16), jax.ShapeDtypeStruct((t, hv), BF16)),
        grid=(t // tm,),
        in_specs=[
            pl.BlockSpec((tm, kvlora), lambda i: (i, 0)),
            pl.BlockSpec((tm, HEAD_PAD), lambda i: (i, 0)),
            pl.BlockSpec((1, HEAD_PAD), const),
            pl.BlockSpec((kvlora, hq), const),
            pl.BlockSpec((kvlora, hv), const),
        ],
        out_specs=(pl.BlockSpec((tm, hq), lambda i: (i, 0)), pl.BlockSpec((tm, hv), lambda i: (i, 0))),
        compiler_params=_cparams(("parallel",), 40),
        name="ctx_prep",
    )(ckv, kpe, lw["k_norm_g"], lw["w_uk"], lw["w_uv"])


def _attn_kernel(*refs, has_ctx, heads):
    if has_ctx:
        q_ref, ko_ref, vo_ref, kc_ref, vc_ref, o_ref = refs
    else:
        q_ref, ko_ref, vo_ref, o_ref = refs
    contract_last = (((1,), (1,)), ((), ()))
    lane = lax.broadcasted_iota(jnp.int32, (q_ref.shape[0], 2 * V_HEAD), 1)
    for pair in range(heads // 2):
        vsl = slice(pair * 2 * V_HEAD, (pair + 1) * 2 * V_HEAD)
        outs = []
        for hh in range(2):
            h = 2 * pair + hh
            sl = slice(h * HEAD_PAD, (h + 1) * HEAD_PAD)
            q = q_ref[:, sl]
            s_o = lax.dot_general(q, ko_ref[:, sl], contract_last, preferred_element_type=F32)
            m = jnp.max(s_o, axis=-1, keepdims=True)
            if has_ctx:
                s_c = lax.dot_general(q, kc_ref[:, sl], contract_last, preferred_element_type=F32)
                m = jnp.maximum(m, jnp.max(s_c, axis=-1, keepdims=True))
            p_o = jnp.exp2(s_o - m)
            l = jnp.sum(p_o, axis=-1, keepdims=True)
            acc = _dot(p_o.astype(BF16), vo_ref[:, vsl])
            if has_ctx:
                p_c = jnp.exp2(s_c - m)
                l = l + jnp.sum(p_c, axis=-1, keepdims=True)
                acc = acc + _dot(p_c.astype(BF16), vc_ref[:, vsl])
            outs.append(acc / l)
        o_ref[:, vsl] = jnp.where(lane < V_HEAD, outs[0], outs[1]).astype(BF16)


def _attn_call(q, k, v, kc, vc, batch, seq_len, ctx_len):
    t = q.shape[0]
    tq = min(seq_len, 512)
    heads = 8
    nq = seq_len // tq
    npair = MLA_HEADS // heads
    qw = heads * HEAD_PAD
    vw = heads * V_HEAD
    in_specs = [
        pl.BlockSpec((tq, qw), lambda b, h, i: (b * nq + i, h)),
        pl.BlockSpec((seq_len, qw), lambda b, h, i: (b, h)),
        pl.BlockSpec((seq_len, vw), lambda b, h, i: (b, h)),
    ]
    args = [q, k, v]
    if kc is not None:
        in_specs += [pl.BlockSpec((ctx_len, qw), lambda b, h, i: (b, h)),
                     pl.BlockSpec((ctx_len, vw), lambda b, h, i: (b, h))]
        args += [kc, vc]
    return pl.pallas_call(
        functools.partial(_attn_kernel, has_ctx=kc is not None, heads=heads),
        out_shape=jax.ShapeDtypeStruct((t, MLA_HEADS * V_HEAD), BF16),
        grid=(batch, npair, nq),
        in_specs=in_specs,
        out_specs=pl.BlockSpec((tq, vw), lambda b, h, i: (b * nq + i, h)),
        compiler_params=_cparams(("parallel", "parallel", "arbitrary"), 56),
        name="attention",
    )(*args)


def _convpool_kernel(cu_ref, bg_ref, cg_ref, pu_ref, cw_ref, cb_ref, pw_ref, ps_ref, zb_ref, zd_ref):
    n = cu_ref.shape[0]
    gps = pw_ref.shape[0]
    edge = max(POOL_WINDOWS) // 2
    assert edge % SUBLANES == 0 and n >= 4 * edge

    def shifts(rows, zero_fill):
        def down(x, k):
            y = pltpu.roll(x, k, axis=0)
            if zero_fill:
                y = jnp.where(lax.broadcasted_iota(jnp.int32, x.shape, 0) >= k, y, 0.0)
            return y

        def up(x, k):
            y = pltpu.roll(x, rows - k, axis=0)
            if zero_fill:
                y = jnp.where(lax.broadcasted_iota(jnp.int32, x.shape, 0) < rows - k, y, 0.0)
            return y
        return down, up

    def conv_rows(r0, rows, zero_fill):
        down, up = shifts(rows, zero_fill)
        v = cg_ref[r0:r0 + rows, :] * cu_ref[r0:r0 + rows, :]
        conv = (down(v, 1) * cw_ref[0:1, :] + v * cw_ref[1:2, :] + up(v, 1) * cw_ref[2:3, :]
                + cb_ref[...])
        return (bg_ref[r0:r0 + rows, :] * conv).astype(BF16)

    def pool_rows(r0, rows, zero_fill, gg, half):
        down, up = shifts(rows, zero_fill)
        sl = slice(gg * LANES, (gg + 1) * LANES)
        u = pu_ref[r0:r0 + rows, sl]
        fwd = u
        bwd = u
        k = 1
        while k < half:
            fwd = fwd + up(fwd, k)
            bwd = bwd + down(bwd, k)
            k *= 2
        total = fwd + down(bwd, 1)
        rowf = (lax.broadcasted_iota(jnp.int32, u.shape, 0) + r0).astype(F32)
        cnt = jnp.minimum(rowf + half, float(n)) - jnp.maximum(rowf - half, 0.0)
        mean = total / cnt - u
        return (_dot(mean.astype(BF16), pw_ref[gg]) * ps_ref[:, sl]).astype(BF16)

    zb_ref[...] = conv_rows(0, n, False)
    zb_ref[0:edge, :] = conv_rows(0, 2 * edge, True)[0:edge]
    zb_ref[n - edge:n, :] = conv_rows(n - 2 * edge, 2 * edge, True)[edge:]

    step = pl.program_id(1)
    for first in range(0, len(POOL_WINDOWS), gps):
        @pl.when(step * gps == first)
        def _(first=first):
            for gg in range(gps):
                sl = slice(gg * LANES, (gg + 1) * LANES)
                half = POOL_WINDOWS[first + gg] // 2
                zd_ref[:, sl] = pool_rows(0, n, False, gg, half)
                zd_ref[0:edge, sl] = pool_rows(0, 2 * edge, True, gg, half)[0:edge]
                zd_ref[n - edge:n, sl] = pool_rows(n - 2 * edge, 2 * edge, True, gg, half)[edge:]


def _convpool_call(p, lw, batch, seq_len, cols):
    t = p.shape[0]
    gps = 2 if seq_len > 512 else len(POOL_WINDOWS)
    cw = gps * LANES
    nblk = lw["conv_w"].shape[1] // cw

    def pspec(name):
        base = cols[name] // cw
        return pl.BlockSpec((seq_len, cw), lambda b, g: (b, base + g))

    vec = lambda b, g: (0, g)
    return pl.pallas_call(
        _convpool_kernel,
        out_shape=(jax.ShapeDtypeStruct((t, nblk * cw), BF16), jax.ShapeDtypeStruct((t, nblk * cw), BF16)),
        grid=(batch, nblk),
        in_specs=[pspec("conv_u"), pspec("conv_bg"), pspec("conv_cg"), pspec("pool_u"),
                  pl.BlockSpec((3, cw), vec), pl.BlockSpec((1, cw), vec),
                  pl.BlockSpec((gps, LANES, LANES), lambda b, g: (g, 0, 0)), pl.BlockSpec((1, cw), vec)],
        out_specs=(pl.BlockSpec((seq_len, cw), lambda b, g: (b, g)),
                   pl.BlockSpec((seq_len, cw), lambda b, g: (b, g))),
        compiler_params=_cparams(("parallel", "parallel"), 40),
        name="conv_pool",
    )(p, p, p, p, lw["conv_w"], lw["conv_b"], lw["pool_w"], lw["pool_scale"])


SSM_R = 4
SSM_GB = LANES // SSM_GROUP_CH


def _ssm_kernel(*refs, steps, nchunks, nblk):
    u_refs = refs[:nblk]
    h0_ref, a_ref, mb_ref, m1_ref, m2_ref = refs[nblk:nblk + 5]
    y_refs = refs[nblk + 5:2 * nblk + 5]
    hf_ref, slab_scr, xs_scr, st_scr = refs[2 * nblk + 5:]
    d = pl.program_id(0)
    c = pl.program_id(2)
    width = st_scr.shape[1]
    bw = width // nblk

    @pl.when(c == 0)
    def _():
        st_scr[...] = h0_ref[0]

    for blk in range(nblk):
        for j in range(SSM_R):
            for b in range(SUBLANES):
                slab_scr[blk * SSM_R + j, pl.ds(b, steps, stride=SUBLANES), :] = (
                    u_refs[blk][b, pl.ds(j, steps, stride=SSM_R), :])

    def packed(blk):
        return jnp.concatenate([slab_scr[blk * SSM_R + j] for j in range(SSM_R)], axis=1).astype(BF16)

    for blk in range(nblk):
        xs_scr[:, blk * bw:(blk + 1) * bw] = _dot(packed(blk), mb_ref[0, blk])

    gw = width // 2
    for part in range(width // gw):
        offs = []
        for blk in range(part * nblk // 2, (part + 1) * nblk // 2):
            offs += [(blk * bw + k * LANES, blk * bw + bw // 2 + k * LANES) for k in range(bw // 2 // LANES)]
        coef = [(a_ref[0, :, r0:r0 + LANES], a_ref[0, :, i0:i0 + LANES]) for r0, i0 in offs]

        def body(t, carry, offs=offs, coef=coef):
            tt = t + d * (steps - 1 - 2 * t)
            row0 = pl.multiple_of(tt * SUBLANES, SUBLANES)
            new = []
            for (r0, i0), (a_re, a_im), x_re, x_im in zip(offs, coef, carry[0::2], carry[1::2]):
                n_re = (a_re * x_re - a_im * x_im) + xs_scr[pl.ds(row0, SUBLANES), r0:r0 + LANES]
                n_im = (a_re * x_im + a_im * x_re) + xs_scr[pl.ds(row0, SUBLANES), i0:i0 + LANES]
                xs_scr[pl.ds(row0, SUBLANES), r0:r0 + LANES] = x_re
                xs_scr[pl.ds(row0, SUBLANES), i0:i0 + LANES] = x_im
                new += [n_re, n_im]
            return tuple(new)

        init = []
        for r0, i0 in offs:
            init += [st_scr[:, r0:r0 + LANES], st_scr[:, i0:i0 + LANES]]
        final = lax.fori_loop(0, steps, body, tuple(init), unroll=2)
        for (r0, i0), x_re, x_im in zip(offs, final[0::2], final[1::2]):
            st_scr[:, r0:r0 + LANES] = x_re
            st_scr[:, i0:i0 + LANES] = x_im

    for blk in range(nblk):
        y = (_dot(xs_scr[:, blk * bw:(blk + 1) * bw].astype(BF16), m1_ref[0, blk])
             + _dot(packed(blk), m2_ref[0, blk]))
        for j in range(SSM_R):
            slab_scr[blk * SSM_R + j] = y[:, j * LANES:(j + 1) * LANES]
        for j in range(SSM_R):
            for b in range(SUBLANES):
                y_refs[blk][0, b, pl.ds(j, steps, stride=SSM_R), :] = (
                    slab_scr[blk * SSM_R + j, pl.ds(b, steps, stride=SUBLANES), :])

    @pl.when(c == nchunks - 1)
    def _():
        hf_ref[0] = st_scr[...]


def _ssm_call(us, h0, lw):
    batch, seq_len, _ = us[0].shape
    width = h0.shape[-1]
    nblk = lw["ssm_mb"].shape[2]
    assert len(us) == nblk
    l = lw["l"]
    steps = min(seq_len // SSM_R, 64)
    tokens = steps * SSM_R
    nchunks = seq_len // tokens
    nbg = batch // SUBLANES
    rows = steps * SUBLANES

    def cidx(d, c):
        return c + d * (nchunks - 1 - 2 * c)

    uspec = pl.BlockSpec((SUBLANES, tokens, LANES), lambda d, b, c: (b, cidx(d, c), 0))

    def wspec(arr):
        return pl.BlockSpec((None, 1) + arr.shape[2:], lambda d, b, c: (l, d, 0, 0, 0),
                            pipeline_mode=pl.Buffered(1))

    yshape = jax.ShapeDtypeStruct((2, batch, seq_len, LANES), F32)
    yspec = pl.BlockSpec((1, SUBLANES, tokens, LANES), lambda d, b, c: (d, b, cidx(d, c), 0))
    outs = pl.pallas_call(
        functools.partial(_ssm_kernel, steps=steps, nchunks=nchunks, nblk=nblk),
        out_shape=(yshape,) * nblk + (jax.ShapeDtypeStruct((2, batch, width), F32),),
        grid=(2, nbg, nchunks),
        in_specs=[uspec] * nblk + [
            pl.BlockSpec((1, SUBLANES, width), lambda d, b, c: (d, b, 0)),
            pl.BlockSpec((None, 1, SUBLANES, width), lambda d, b, c: (l, d, 0, 0)),
            wspec(lw["ssm_mb"]), wspec(lw["ssm_m1"]), wspec(lw["ssm_m2"]),
        ],
        out_specs=(yspec,) * nblk + (pl.BlockSpec((1, SUBLANES, width), lambda d, b, c: (d, b, 0)),),
        scratch_shapes=[pltpu.VMEM((nblk * SSM_R, rows, LANES), F32), pltpu.VMEM((rows, width), F32),
                        pltpu.VMEM((SUBLANES, width), F32)],
        compiler_params=_cparams(("arbitrary", "arbitrary", "arbitrary"), 52),
        name="s5_scan",
    )(*us, h0, lw["ssm_a"], lw["ssm_mb"], lw["ssm_m1"], lw["ssm_m2"])
    return outs[:nblk], outs[nblk]


def _merge_kernel(*refs, nblk, ncast):
    (h_ref, za_ref, zb_ref, su_ref, sd_ref, zd_ref, wg0, wg1, wg2, wg3, wa, wb, wca, wcg, wd) = refs[:15]
    yf_refs = refs[15:15 + nblk]
    yb_refs = refs[15 + nblk:15 + 2 * nblk]
    cast_in = refs[15 + 2 * nblk:15 + 2 * nblk + ncast]
    o_ref = refs[15 + 2 * nblk + ncast]
    cast_out = refs[16 + 2 * nblk + ncast:16 + 2 * nblk + 2 * ncast]
    y_scr = refs[-1]

    for src, dst in zip(cast_in, cast_out):
        dst[...] = src[...].astype(BF16)

    @pl.when(pl.program_id(1) == 0)
    def _():
        y_both = jnp.concatenate([f[0] + b[0] for f, b in zip(yf_refs, yb_refs)], axis=1)
        y_scr[...] = (y_both + sd_ref[...] * su_ref[...]).astype(BF16)

    tm = h_ref.shape[0]
    nsub = 2
    for sub in range(nsub):
        rs = slice(sub * tm // nsub, (sub + 1) * tm // nsub)
        h = h_ref[rs, :]
        y = y_scr[rs, :]

        def gate(w_ref, h=h):
            return jax.nn.sigmoid(_dot_t(h, w_ref[...]))

        merged = gate(wg0) * _dot(za_ref[rs, :], wa[...])
        merged = merged + gate(wg1) * _dot(zb_ref[rs, :], wb[...])
        merged = merged + gate(wg2) * (_dot(y, wca[...]) * jax.nn.sigmoid(_dot(y, wcg[...])))
        merged = merged + gate(wg3) * _dot(zd_ref[rs, :], wd[...])
        o_ref[rs, :] = merged.astype(BF16)


def _merge_call(h, p, za, zb, ys, zd, lw, cols, cast=()):
    t, d = h.shape
    bw = za.shape[1]
    tm, tn = 512, 512
    nj = d // tn
    nblk = len(ys)
    row = lambda i, j: (i, 0)
    su_blk = cols["ssm_u"] // bw

    l = lw["l"]

    gate_blk0 = cols["k_rope"] // tn

    row0 = l * lw["w_rows"] + lw["tail_row0"]

    def gspec(k):
        assert (row0 + gate_blk0 * tn) % 16 == 0
        return pl.BlockSpec((pl.Element(tn), pl.Element(d)),
                            lambda i, j: (pl.multiple_of(row0 + (gate_blk0 + k * nj + j) * tn, 16), 0))

    def yspec(direction):
        return pl.BlockSpec((1, tm, LANES), lambda i, j: (direction, i, 0))

    wcol = pl.BlockSpec((None, bw, tn), lambda i, j: (l, 0, j))
    in_specs = [
        pl.BlockSpec((tm, d), row),
        pl.BlockSpec((tm, bw), row), pl.BlockSpec((tm, bw), row),
        pl.BlockSpec((tm, bw), lambda i, j: (i, su_blk)),
        pl.BlockSpec((1, bw), lambda i, j: (0, 0)),
        pl.BlockSpec((tm, bw), row),
        gspec(0), gspec(1), gspec(2), gspec(3),
        wcol, wcol, wcol, pl.BlockSpec((None, bw, tn), lambda i, j: (l, 0, nj + j)), wcol,
    ] + [yspec(0)] * nblk + [yspec(1)] * nblk
    out_shape = [jax.ShapeDtypeStruct((t, d), BF16)]
    out_specs = [pl.BlockSpec((tm, tn), lambda i, j: (i, j))]
    nsteps = (t // tm) * nj
    for w in cast:
        rows, n = w.shape[1] // nsteps, w.shape[2]
        assert rows * nsteps == w.shape[1] and rows % (2 * SUBLANES) == 0
        in_specs.append(pl.BlockSpec((None, rows, n), lambda i, j: (l, i * nj + j, 0)))
        out_shape.append(jax.ShapeDtypeStruct(w.shape[1:], BF16))
        out_specs.append(pl.BlockSpec((rows, n), lambda i, j: (i * nj + j, 0)))
    wg = lw["w_tail"]
    outs = pl.pallas_call(
        functools.partial(_merge_kernel, nblk=nblk, ncast=len(cast)),
        out_shape=tuple(out_shape),
        grid=(t // tm, nj),
        in_specs=in_specs,
        out_specs=tuple(out_specs),
        scratch_shapes=[pltpu.VMEM((tm, bw), BF16)],
        compiler_params=_cparams(("parallel", "arbitrary"), 48),
        name="branch_merge",
    )(h, za, zb, p, lw["ssm_d"], zd,
      wg, wg, wg, wg, lw["w_mla_o"], lw["w_conv_o"], lw["w_glu"], lw["w_glu"], lw["w_pool_o"], *ys, *ys, *cast)
    return outs[0], tuple(outs[1:])


def _outproj_kernel(x_ref, m_ref, gt_ref, sh_ref, sc_ref, g_ref, wo_ref, x1_ref, h2_ref):
    tm = x_ref.shape[0]
    nsub = 2
    for sub in range(nsub):
        rs = slice(sub * tm // nsub, (sub + 1) * tm // nsub)
        x1 = x_ref[rs, :] + gt_ref[0] * _dot(m_ref[rs, :], wo_ref[...])
        x1_ref[rs, :] = x1
        h2_ref[rs, :] = _modnorm(x1, g_ref[...], sc_ref[0], sh_ref[0]).astype(BF16)


def _outproj_call(x, merged, mod, lw, seq_len, per_batch):
    t, d = x.shape
    tm = 512
    mspec = functools.partial(_mod_spec, d=d, tm=tm, seq_len=seq_len, per_batch=per_batch)
    row = lambda i: (i, 0)
    return pl.pallas_call(
        _outproj_kernel,
        out_shape=(jax.ShapeDtypeStruct((t, d), F32), jax.ShapeDtypeStruct((t, d), BF16)),
        grid=(t // tm,),
        in_specs=[
            pl.BlockSpec((tm, d), row), pl.BlockSpec((tm, d), row),
            mspec(2), mspec(3), mspec(4),
            pl.BlockSpec((1, d), lambda i: (0, 0)),
            pl.BlockSpec((None, d, d), lambda i: (lw["l"], 0, 0), pipeline_mode=pl.Buffered(1)),
        ],
        out_specs=(pl.BlockSpec((tm, d), row), pl.BlockSpec((tm, d), row)),
        compiler_params=_cparams(("parallel",), 48),
        name="out_proj",
    )(x, merged, mod, mod, mod, lw["norm2_g"], lw["w_o"])


def _mlp_kernel(x_ref, h_ref, gt_ref, w1_ref, w2_ref, o_ref):
    @pl.when(pl.program_id(1) == 0)
    def _():
        o_ref[...] = x_ref[...]

    hid = jnp.square(jnp.maximum(_dot(h_ref[...], w1_ref[...]), 0.0))
    o_ref[...] += gt_ref[0] * _dot(hid.astype(BF16), w2_ref[...])


def _mlp_call(x, h, mod, lw, seq_len, per_batch):
    t, d = x.shape
    hidden = lw["w_mlp1"].shape[2]
    tm, tk = 512, 1024
    l = lw["l_mlp"]
    mspec = functools.partial(_mod_spec, d=d, tm=tm, seq_len=seq_len, per_batch=per_batch)
    row = lambda i, k: (i, 0)
    return pl.pallas_call(
        _mlp_kernel,
        out_shape=jax.ShapeDtypeStruct((t, d), F32),
        grid=(t // tm, hidden // tk),
        in_specs=[
            pl.BlockSpec((tm, d), row), pl.BlockSpec((tm, d), row), mspec(5),
            pl.BlockSpec((None, d, tk), lambda i, k: (l, 0, k)),
            pl.BlockSpec((None, tk, d), lambda i, k: (l, k, 0)),
        ],
        out_specs=pl.BlockSpec((tm, d), row),
        compiler_params=_cparams(("parallel", "arbitrary"), 48),
        name="mlp",
    )(x, h, mod, lw["w_mlp1"], lw["w_mlp2"])


def _pad_heads(w, per_head, lo):
    lead = w.shape[:-1]
    w = w.reshape(lead + (MLA_HEADS, per_head))
    w = jnp.pad(w, [(0, 0)] * len(lead) + [(0, 0), (lo, HEAD_PAD - lo - per_head)])
    return w.reshape(lead + (MLA_HEADS * HEAD_PAD,))


_ROPE_PARTNER = tuple((i // 16) * 16 + (i % 16 + 8) % 16 for i in range(ROPE_DIM))


def _swap_heads(w):
    lead = w.shape[:-1]
    w = w.reshape(lead + (MLA_HEADS, QK_HEAD))[..., QK_NOPE:][..., jnp.array(_ROPE_PARTNER)]
    w = jnp.pad(w, [(0, 0)] * len(lead) + [(0, 0), (QK_NOPE, HEAD_PAD - QK_HEAD)])
    return w.reshape(lead + (MLA_HEADS * HEAD_PAD,))


def _prep_ssm(lam_re, lam_im, log_step, b_re, b_im, c_re, c_im):
    r, gb = SSM_R, SSM_GB
    nl, ndir, g, n, cg = b_re.shape
    ns = nl * ndir
    nblk = g // gb
    flat = lambda v: v.reshape((ns,) + v.shape[2:])
    lam_re, lam_im, log_step, b_re, b_im, c_re, c_im = map(
        flat, (lam_re, lam_im, log_step, b_re, b_im, c_re, c_im))
    step = jnp.exp(log_step)[..., None]
    pw = []
    for p in range(r + 1):
        mag = jnp.exp(p * (lam_re * step))
        pw.append((mag * jnp.cos(p * (lam_im * step)), mag * jnp.sin(p * (lam_im * step))))
    ar, ai = pw[1]
    den = lam_re * lam_re + lam_im * lam_im
    qr = ((ar - 1.0) * lam_re + ai * lam_im) / den
    qi = (ai * lam_re - (ar - 1.0) * lam_im) / den
    bb_re = qr[..., None] * b_re - qi[..., None] * b_im
    bb_im = qr[..., None] * b_im + qi[..., None] * b_re
    fwd = (jnp.arange(ns) % ndir == 0)[:, None, None]

    def power(p_fwd, p_bwd):
        return (jnp.where(fwd, pw[p_fwd][0], pw[p_bwd][0]), jnp.where(fwd, pw[p_fwd][1], pw[p_bwd][1]))

    rows_g = (jnp.arange(r * gb * cg) // cg) % gb
    cols_g = jnp.arange(gb * n) // n

    t_re, t_im = [], []
    for j in range(r):
        pr, pi = power(r - 1 - j, j)
        t_re.append(pr[..., None] * bb_re - pi[..., None] * bb_im)
        t_im.append(pr[..., None] * bb_im + pi[..., None] * bb_re)
    mask_b = (rows_g[:, None] == cols_g[None, :]).astype(F32)

    def inc_block(ts):
        t = jnp.stack(ts, axis=1).reshape(ns, r, nblk, gb, n, cg)
        t = jnp.transpose(t, (0, 2, 1, 3, 5, 4)).reshape(ns, nblk, r * gb * cg, n)
        return jnp.tile(t, (1, 1, 1, gb)) * mask_b

    mb = jnp.concatenate([inc_block(t_re), inc_block(t_im)], axis=-1)

    cl_re, cl_im = [], []
    for j in range(r):
        pr, pi = power(j + 1, r - j)
        cl_re.append(c_re * pr[:, :, None, :] - c_im * pi[:, :, None, :])
        cl_im.append(c_re * pi[:, :, None, :] + c_im * pr[:, :, None, :])

    def out_block(ts):
        t = jnp.stack(ts, axis=1).reshape(ns, r, nblk, gb, cg, n)
        t = jnp.transpose(t, (0, 2, 5, 1, 3, 4)).reshape(ns, nblk, n, r * gb * cg)
        return jnp.tile(t, (1, 1, gb, 1)) * mask_b.T

    m1 = jnp.concatenate([out_block(cl_re), -out_block(cl_im)], axis=2)

    hi = lax.Precision.HIGHEST
    lane_g = jnp.arange(gb * cg) // cg
    mask_k = (lane_g[:, None] == lane_g[None, :]).astype(F32)
    kd = []
    for p in range(r):
        pr, pi = pw[p]
        k2 = (jnp.einsum("sgcn,sgnk->sgck", c_re * pr[:, :, None, :] - c_im * pi[:, :, None, :], bb_re, precision=hi)
              - jnp.einsum("sgcn,sgnk->sgck", c_re * pi[:, :, None, :] + c_im * pr[:, :, None, :], bb_im, precision=hi))
        k2 = jnp.swapaxes(k2, -1, -2).reshape(ns, nblk, gb * cg, cg)
        kd.append(jnp.tile(k2, (1, 1, 1, gb)) * mask_k)
    fwd4 = fwd[..., None]
    zero = jnp.zeros_like(kd[0])
    m2_rows = []
    for i in range(r):
        blocks = []
        for j in range(r):
            if i == j:
                blocks.append(kd[0])
            elif j > i:
                blocks.append(jnp.where(fwd4, kd[j - i], zero))
            else:
                blocks.append(jnp.where(fwd4, zero, kd[i - j]))
        m2_rows.append(jnp.concatenate(blocks, axis=-1))
    m2 = jnp.concatenate(m2_rows, axis=2)

    a = jnp.stack([pw[r][0].reshape(ns, nblk, gb * n), pw[r][1].reshape(ns, nblk, gb * n)], axis=2)
    a = jnp.broadcast_to(a.reshape(ns, 1, 2 * g * n), (ns, SUBLANES, 2 * g * n))
    unflat = lambda v: v.reshape((nl, ndir) + v.shape[1:])
    return unflat(a), unflat(mb.astype(BF16)), unflat(m1.astype(BF16)), unflat(m2.astype(BF16))


def _prep_stacked(prm, sizes):
    q_lora, kv_lora, conv_w, ssm_w, pool_w_ = sizes
    w_in_t = jnp.swapaxes(prm["w_in"], 1, 2)
    tail0 = q_lora + kv_lora + ROPE_DIM
    w_tail = w_in_t.astype(BF16).reshape(-1, w_in_t.shape[2])
    head = w_in_t[:, :tail0, :]
    cols = {}
    o = 0
    for name, width in (("conv_u", conv_w), ("conv_bg", conv_w), ("conv_cg", conv_w), ("ssm_u", ssm_w),
                        ("pool_u", pool_w_), ("k_rope", HEAD_PAD), ("k_rope_sw", HEAD_PAD), ("q_a", q_lora),
                        ("kv_a", kv_lora)):
        cols[name] = o
        o += width
    k_rope = head[:, q_lora + kv_lora:tail0, :]
    rope_lanes = ((0, 0), (QK_NOPE, HEAD_PAD - QK_HEAD), (0, 0))
    w_att = jnp.concatenate([jnp.pad(k_rope, rope_lanes),
                             jnp.pad(k_rope[:, jnp.array(_ROPE_PARTNER), :], rope_lanes),
                             head[:, :q_lora + kv_lora, :]], axis=1).astype(BF16)
    big = dict(w_tail=w_tail, w_att=w_att, tail_row0=tail0, w_rows=w_in_t.shape[1])
    for name in ("w_mla_o", "w_conv_o", "w_glu", "w_pool_o", "w_o"):
        big[name] = prm[name].astype(BF16)
    return big, cols


def _prep_layer(l, prm, sizes, big, ssm):
    kv_lora = sizes[1]
    w_ukv = prm["w_ukv"][l].reshape(kv_lora, MLA_HEADS, QK_NOPE + V_HEAD)
    w_uk = _pad_heads(w_ukv[:, :, :QK_NOPE].reshape(kv_lora, -1), QK_NOPE, 0)
    w_uv = w_ukv[:, :, QK_NOPE:].reshape(kv_lora, -1)
    ssm_a, ssm_mb, ssm_m1, ssm_m2 = ssm

    row = lambda v: v.reshape(1, -1)
    head_gain = lambda g: row(jnp.pad(g, (0, HEAD_PAD - QK_HEAD)))
    partner_gain = lambda g: row(jnp.pad(g[QK_NOPE:][jnp.array(_ROPE_PARTNER)], (QK_NOPE, HEAD_PAD - QK_HEAD)))
    lw = dict(
        big, l=l,
        norm1_g=row(prm["norm1_g"][l]), norm2_g=row(prm["norm2_g"][l]),
        q_a_norm_g=row(prm["q_a_norm_g"][l]), kv_a_norm_g=row(prm["kv_a_norm_g"][l]),
        q_norm_g=head_gain(prm["q_norm_g"][l]), k_norm_g=head_gain(prm["k_norm_g"][l]),
        q_norm_g_sw=partner_gain(prm["q_norm_g"][l]), k_norm_g_sw=partner_gain(prm["k_norm_g"][l]),
        w_uq=_pad_heads(prm["w_uq"][l], QK_HEAD, 0).astype(BF16),
        w_uq_sw=_swap_heads(prm["w_uq"][l]).astype(BF16),
        w_uk=w_uk.astype(BF16), w_uv=w_uv.astype(BF16),
        conv_w=prm["conv_w"][l], conv_b=row(prm["conv_b"][l]),
        ssm_a=ssm_a, ssm_mb=ssm_mb, ssm_m1=ssm_m1, ssm_m2=ssm_m2,
        ssm_d=row(prm["ssm_d"][l]),
        pool_w=prm["pool_w"][l].astype(BF16), pool_scale=row(prm["pool_scale"][l]),
    )
    return lw


def _rope_tables(seq_len):
    half = ROPE_DIM // 2
    inv_freq = ROPE_THETA ** (-jnp.arange(0, half, 2, dtype=F32) / half)
    t = jnp.arange(seq_len)
    ang_r = (t // GRID_W).astype(F32)[:, None] * inv_freq
    ang_c = (t % GRID_W).astype(F32)[:, None] * inv_freq
    ones_lo = jnp.ones((seq_len, QK_NOPE), F32)
    zeros_lo = jnp.zeros((seq_len, QK_NOPE), F32)
    tail = jnp.zeros((seq_len, HEAD_PAD - QK_HEAD), F32)
    c = jnp.concatenate([ones_lo, jnp.cos(ang_r), jnp.cos(ang_r), jnp.cos(ang_c), jnp.cos(ang_c), tail], axis=1)
    s = jnp.concatenate([zeros_lo, -jnp.sin(ang_r), jnp.sin(ang_r), -jnp.sin(ang_c), jnp.sin(ang_c), tail], axis=1)
    return c, s


def _states_to_cols(st):
    b, ndir, _, g, n = st.shape
    st = st.reshape(b, ndir, 2, g // SSM_GB, SSM_GB * n)
    return jnp.transpose(st, (1, 0, 3, 2, 4)).reshape(ndir, b, 2 * g * n)


def _cols_to_states(hf, g, n):
    ndir, b, _ = hf.shape
    hf = hf.reshape(ndir, b, g // SSM_GB, 2, SSM_GB * n)
    return jnp.transpose(hf, (1, 0, 3, 2, 4)).reshape(b, ndir, 2, g, n)


def _mixers(x, mod, lw, cols, batch, seq_len, per_batch, rope, ctx, cast=()):
    p, h1, q, k, v, ckv, *us = _inproj_call(x, mod, lw, rope, seq_len, per_batch, cols)
    if ctx is not None:
        ctx_ckv, ctx_kpe, h0 = ctx
        kc, vc = _ctxprep_call(ctx_ckv, ctx_kpe, lw)
        ctx_len = ctx_ckv.shape[0] // batch
    else:
        kc = vc = None
        ctx_len = 0
        h0 = jnp.zeros((2, batch, lw["ssm_a"].shape[-1]), F32)
    za = _attn_call(q, k, v, kc, vc, batch, seq_len, ctx_len)
    zb, zd = _convpool_call(p, lw, batch, seq_len, cols)

    ys, hf = _ssm_call([u.reshape(batch, seq_len, LANES) for u in us], h0, lw)
    ys = [y.reshape(2, batch * seq_len, LANES) for y in ys]
    merged, cast_out = _merge_call(h1, p, za, zb, ys, zd, lw, cols, cast)
    kr = cols["k_rope"] + QK_NOPE
    return merged, cast_out, ckv, p[:, kr:kr + ROPE_DIM], hf


def _channel_mixer(x, merged, mod, lw, seq_len, per_batch):
    x1, h2 = _outproj_call(x, merged, mod, lw, seq_len, per_batch)
    return _mlp_call(x1, h2, mod, lw, seq_len, per_batch)


def kernel(x_prompt, x_sample, c, cache_ckv, cache_krope, state_ssm, c_ctx, w_ada, b_ada, norm1_g, norm2_g, w_in, q_a_norm_g, kv_a_norm_g, w_uq, w_ukv, q_norm_g, k_norm_g, w_mla_o, conv_w, conv_b, w_conv_o, ssm_lam_re, ssm_lam_im, ssm_log_step, ssm_b_re, ssm_b_im, ssm_c_re, ssm_c_im, ssm_d, w_glu, pool_w, pool_scale, w_pool_o, w_o, w_mlp1, w_mlp2):
    prm = dict(norm1_g=norm1_g, norm2_g=norm2_g, w_in=w_in, q_a_norm_g=q_a_norm_g, kv_a_norm_g=kv_a_norm_g,
               w_uq=w_uq, w_ukv=w_ukv, q_norm_g=q_norm_g, k_norm_g=k_norm_g, w_mla_o=w_mla_o,
               conv_w=conv_w, conv_b=conv_b, w_conv_o=w_conv_o, ssm_lam_re=ssm_lam_re, ssm_lam_im=ssm_lam_im,
               ssm_log_step=ssm_log_step, ssm_b_re=ssm_b_re, ssm_b_im=ssm_b_im, ssm_c_re=ssm_c_re,
               ssm_c_im=ssm_c_im, ssm_d=ssm_d, w_glu=w_glu, pool_w=pool_w, pool_scale=pool_scale,
               w_pool_o=w_pool_o, w_o=w_o, w_mlp1=w_mlp1, w_mlp2=w_mlp2)
    depth = w_in.shape[0]
    bp, lp, d = x_prompt.shape
    bs, ls, _ = x_sample.shape
    past = cache_ckv.shape[2]
    g, n = state_ssm.shape[-2:]
    sizes = (w_uq.shape[1], w_ukv.shape[1], conv_w.shape[-1], ssm_d.shape[-1], pool_scale.shape[-1])

    rows = -(-(1 + bs) // SUBLANES) * SUBLANES
    cvec = jnp.zeros((rows, d), F32).at[0].set(c_ctx).at[1:1 + bs].set(c)
    mods = _ada_call(cvec, w_ada, b_ada)
    rope = _rope_tables(ls)
    ssm = _prep_ssm(ssm_lam_re, ssm_lam_im, ssm_log_step, ssm_b_re, ssm_b_im, ssm_c_re, ssm_c_im)
    big, cols = _prep_stacked(prm, sizes)

    yp = x_prompt.reshape(bp * lp, d)
    ys = x_sample.reshape(bs * ls, d)
    ckv_list, krope_list, ssm_list = [], [], []
    for l in range(depth):
        lw = _prep_layer(l, prm, sizes, big, ssm)
        mod_ctx = mods[l, 0:1].reshape(6, 1, d)
        mod_lat = mods[l, 1:1 + bs].reshape(bs * 6, 1, d)
        ctx_kpe = jnp.pad(cache_krope[:, l].reshape(bs * past, ROPE_DIM),
                          ((0, 0), (QK_NOPE, HEAD_PAD - QK_HEAD)))
        ctx = (cache_ckv[:, l].reshape(bs * past, -1), ctx_kpe, _states_to_cols(state_ssm[:, l]))
        merged_s, (w1b, w2b), _, _, _ = _mixers(ys, mod_lat, lw, cols, bs, ls, True, rope, ctx,
                                                cast=(w_mlp1, w_mlp2))
        merged_p, _, ckv_l, krope_l, hf_l = _mixers(yp, mod_ctx, lw, cols, bp, lp, False, None, None)
        ckv_list.append(ckv_l.reshape(bp, lp, -1))
        krope_list.append(krope_l.reshape(bp, lp, -1))
        ssm_list.append(_cols_to_states(hf_l, g, n))
        lw_mlp = dict(lw, w_mlp1=w1b[None], w_mlp2=w2b[None], l_mlp=0)
        yp = _channel_mixer(yp, merged_p, mod_ctx, lw_mlp, lp, False)
        ys = _channel_mixer(ys, merged_s, mod_lat, lw_mlp, ls, True)
    return (yp.reshape(bp, lp, d), ys.reshape(bs, ls, d), jnp.stack(ckv_list, axis=1),
            jnp.stack(krope_list, axis=1), jnp.stack(ssm_list, axis=1))
```

```python
import functools

import jax
import jax.numpy as jnp
from jax import lax
from jax.experimental import pallas as pl
from jax.experimental.pallas import tpu as pltpu

F32 = jnp.float32
BF16 = jnp.bfloat16

GRID_W = 64
MLA_HEADS = 8
QK_NOPE = 64
ROPE_DIM = 32
QK_HEAD = QK_NOPE + ROPE_DIM
V_HEAD = 64
ROPE_THETA = 10000.0
SSM_GROUP_CH = 16
POOL_WINDOWS = (2, 4, 8, 16)
EPS = 1e-6
LOG2E = 1.4426950408889634

LANES = 128
SUBLANES = 8
HEAD_PAD = LANES
MIB = 1024 * 1024


def _cparams(sem, vmem_mib):
    return pltpu.CompilerParams(dimension_semantics=sem, vmem_limit_bytes=vmem_mib * MIB)


def _dot(a, b):
    return jnp.dot(a, b, preferred_element_type=F32)


def _dot_t(a, bt):
    return lax.dot_general(a, bt, (((1,), (1,)), ((), ())), preferred_element_type=F32)


def _mod_spec(chunk, d, tm, seq_len, per_batch):
    def imap(i, *_):
        row = (i * tm) // seq_len if per_batch else 0
        return (row * 6 + chunk, 0, 0)
    return pl.BlockSpec((1, 1, d), imap)


def _ada_kernel(c_ref, w_ref, b_ref, o_ref):
    cv = c_ref[...]
    s = cv * jax.nn.sigmoid(cv)
    o_ref[0] = _dot(s.astype(BF16), w_ref[0].astype(BF16)) + b_ref[0]


def _ada_call(cvec, w_ada, b_ada):
    depth, d, n = w_ada.shape
    rows = cvec.shape[0]
    tn = 1024
    return pl.pallas_call(
        _ada_kernel,
        out_shape=jax.ShapeDtypeStruct((depth, rows, n), F32),
        grid=(depth, n // tn),
        in_specs=[
            pl.BlockSpec((rows, d), lambda l, j: (0, 0)),
            pl.BlockSpec((1, d, tn), lambda l, j: (l, 0, j)),
            pl.BlockSpec((1, 1, tn), lambda l, j: (l, 0, j)),
        ],
        out_specs=pl.BlockSpec((1, rows, tn), lambda l, j: (l, 0, j)),
        compiler_params=_cparams(("parallel", "parallel"), 40),
        name="ada_mod",
    )(cvec, w_ada, b_ada.reshape(depth, 1, n))


def _modnorm(x, g, scale, shift):
    ms = jnp.mean(x * x, axis=-1, keepdims=True)
    return (x * lax.rsqrt(ms + EPS) * g) * (1.0 + scale) + shift


def _rms(x, g, width):
    ms = jnp.sum(x * x, axis=-1, keepdims=True) * (1.0 / width)
    return x * lax.rsqrt(ms + EPS) * g


def _heads_norm(x, g, mult, shared=None, rope=None):
    t_main = g * mult
    if rope is not None:
        xs, shared_s, gs, cos, sin = rope
        t_main = cos * t_main
        t_part = sin * (gs * mult)
    outs = []
    for h in range(MLA_HEADS):
        sl = slice(h * HEAD_PAD, (h + 1) * HEAD_PAD)
        xh = x[:, sl] if shared is None else x[:, sl] + shared
        ms = jnp.sum(xh * xh, axis=-1, keepdims=True) * (1.0 / QK_HEAD)
        y = xh * t_main
        if rope is not None:
            y = y + (shared_s if xs is None else xs[:, sl]) * t_part
        outs.append(y * lax.rsqrt(ms + EPS))
    return jnp.concatenate(outs, axis=1)


def _inproj_kernel(*refs, use_rope, q_mult, cols, n_keep):
    (x_ref, sh_ref, sc_ref, g_ref, wmix_ref, watt_ref, gqa_ref, gkva_ref, gq_ref, gk_ref,
     wuq_ref, wuk_ref, wuv_ref) = refs[:13]
    if use_rope:
        gqs_ref, gks_ref, wuqs_ref, c_ref, s_ref = refs[13:18]
    nu = (cols["pool_u"] - cols["ssm_u"]) // LANES
    p_out, h_out, q_out, k_out, v_out, ckv_out = refs[-6 - nu:-nu]
    u_outs = refs[-nu:]
    att0 = cols["k_rope"]
    tm = x_ref.shape[0]
    nsub = 2

    for sub in range(nsub):
        rs = slice(sub * tm // nsub, (sub + 1) * tm // nsub)
        h = _modnorm(x_ref[rs, :], g_ref[...], sc_ref[0], sh_ref[0]).astype(BF16)
        h_out[rs, :] = h
        res = _dot_t(h, watt_ref[...])
        mix = _dot_t(h, wmix_ref[...])
        p_out[rs, :att0] = mix
        p_out[rs, att0:] = res[:, :n_keep - att0]
        for blk, u_out in enumerate(u_outs):
            u_out[rs, :] = mix[:, cols["ssm_u"] + blk * LANES:cols["ssm_u"] + (blk + 1) * LANES]

        def col(name, width, res=res):
            return res[:, cols[name] - att0:cols[name] - att0 + width]

        rope_tabs = (c_ref[rs, :], s_ref[rs, :]) if use_rope else None
        qa = col("q_a", wuq_ref.shape[0])
        qa_n = _rms(qa, gqa_ref[...], qa.shape[-1]).astype(BF16)
        q = _dot(qa_n, wuq_ref[...])
        q_rope = (_dot(qa_n, wuqs_ref[...]), None, gqs_ref[...]) + rope_tabs if use_rope else None
        q_out[rs, :] = _heads_norm(q, gq_ref[...], q_mult, rope=q_rope).astype(BF16)

        kva = col("kv_a", wuk_ref.shape[0])
        ckv = _rms(kva, gkva_ref[...], kva.shape[-1])
        ckv_out[rs, :] = ckv
        ckv_b = ckv.astype(BF16)
        k = _dot(ckv_b, wuk_ref[...])
        k_rope = (None, col("k_rope_sw", HEAD_PAD), gks_ref[...]) + rope_tabs if use_rope else None
        k_out[rs, :] = _heads_norm(k, gk_ref[...], 1.0, shared=col("k_rope", HEAD_PAD), rope=k_rope).astype(BF16)
        v_out[rs, :] = _dot(ckv_b, wuv_ref[...]).astype(BF16)


def _inproj_call(x, mod, lw, rope, seq_len, per_batch, cols):
    t, d = x.shape
    att0 = cols["k_rope"]
    n_att = lw["w_att"].shape[1]
    n_keep = att0 + HEAD_PAD
    tm = 512
    hq = MLA_HEADS * HEAD_PAD
    hv = MLA_HEADS * V_HEAD
    qlora = lw["w_uq"].shape[0]
    kvlora = lw["w_uk"].shape[0]
    l = lw["l"]
    mspec = functools.partial(_mod_spec, d=d, tm=tm, seq_len=seq_len, per_batch=per_batch)
    const = lambda i: (0, 0)
    row = lambda i: (i, 0)
    in_specs = [
        pl.BlockSpec((tm, d), row), mspec(0), mspec(1),
        pl.BlockSpec((1, d), const),
        pl.BlockSpec((pl.Element(att0), pl.Element(d)), lambda i: (l * lw["w_rows"] + lw["tail_row0"], 0),
                     pipeline_mode=pl.Buffered(1)),
        pl.BlockSpec((None, n_att, d), lambda i: (l, 0, 0), pipeline_mode=pl.Buffered(1)),
        pl.BlockSpec((1, qlora), const), pl.BlockSpec((1, kvlora), const),
        pl.BlockSpec((1, HEAD_PAD), const), pl.BlockSpec((1, HEAD_PAD), const),
        pl.BlockSpec((qlora, hq), const), pl.BlockSpec((kvlora, hq), const), pl.BlockSpec((kvlora, hv), const),
    ]
    args = [x, mod, mod, lw["norm1_g"], lw["w_tail"], lw["w_att"], lw["q_a_norm_g"], lw["kv_a_norm_g"], lw["q_norm_g"],
            lw["k_norm_g"], lw["w_uq"], lw["w_uk"], lw["w_uv"]]
    if rope is not None:
        nblk = seq_len // tm
        in_specs += [
            pl.BlockSpec((1, HEAD_PAD), const), pl.BlockSpec((1, HEAD_PAD), const),
            pl.BlockSpec((qlora, hq), const),
            pl.BlockSpec((tm, HEAD_PAD), lambda i: (i % nblk, 0)),
            pl.BlockSpec((tm, HEAD_PAD), lambda i: (i % nblk, 0)),
        ]
        args += [lw["q_norm_g_sw"], lw["k_norm_g_sw"], lw["w_uq_sw"]] + list(rope)
    q_mult = LOG2E * QK_HEAD ** -0.5
    nu = (cols["pool_u"] - cols["ssm_u"]) // LANES
    return pl.pallas_call(
        functools.partial(_inproj_kernel, use_rope=rope is not None, q_mult=q_mult, cols=cols, n_keep=n_keep),
        out_shape=(jax.ShapeDtypeStruct((t, n_keep), F32), jax.ShapeDtypeStruct((t, d), BF16),
                   jax.ShapeDtypeStruct((t, hq), BF16), jax.ShapeDtypeStruct((t, hq), BF16),
                   jax.ShapeDtypeStruct((t, hv), BF16), jax.ShapeDtypeStruct((t, kvlora), F32))
        + (jax.ShapeDtypeStruct((t, LANES), F32),) * nu,
        grid=(t // tm,),
        in_specs=in_specs,
        out_specs=(pl.BlockSpec((tm, n_keep), row), pl.BlockSpec((tm, d), row),
                   pl.BlockSpec((tm, hq), row), pl.BlockSpec((tm, hq), row),
                   pl.BlockSpec((tm, hv), row), pl.BlockSpec((tm, kvlora), row))
        + (pl.BlockSpec((tm, LANES), row),) * nu,
        compiler_params=_cparams(("parallel",), 56),
        name="in_proj",
    )(*args)


def _ctxprep_kernel(ckv_ref, kpe_ref, gk_ref, wuk_ref, wuv_ref, k_out, v_out):
    ckv_b = ckv_ref[...].astype(BF16)
    k = _dot(ckv_b, wuk_ref[...])
    k_out[...] = _heads_norm(k, gk_ref[...], 1.0, shared=kpe_ref[...]).astype(BF16)
    v_out[...] = _dot(ckv_b, wuv_ref[...]).astype(BF16)


def _ctxprep_call(ckv, kpe, lw):
    t, kvlora = ckv.shape
    tm = 512
    hq = MLA_HEADS * HEAD_PAD
    hv = MLA_HEADS * V_HEAD
    const = lambda i: (0, 0)
    return pl.pallas_call(
        _ctxprep_kernel,
        out_shape=(jax.ShapeDtypeStruct((t, hq), BF16), jax.ShapeDtypeStruct((t, hv), BF16)),
        grid=(t // tm,),
        in_specs=[
            pl.BlockSpec((tm, kvlora), lambda i: (i, 0)),
            pl.BlockSpec((tm, HEAD_PAD), lambda i: (i, 0)),
            pl.BlockSpec((1, HEAD_PAD), const),
            pl.BlockSpec((kvlora, hq), const),
            pl.BlockSpec((kvlora, hv), const),
        ],
        out_specs=(pl.BlockSpec((tm, hq), lambda i: (i, 0)), pl.BlockSpec((tm, hv), lambda i: (i, 0))),
        compiler_params=_cparams(("parallel",), 40),
        name="ctx_prep",
    )(ckv, kpe, lw["k_norm_g"], lw["w_uk"], lw["w_uv"])


def _attn_kernel(*refs, has_ctx, heads):
    if has_ctx:
        q_ref, ko_ref, vo_ref, kc_ref, vc_ref, o_ref = refs
    else:
        q_ref, ko_ref, vo_ref, o_ref = refs
    contract_last = (((1,), (1,)), ((), ()))
    lane = lax.broadcasted_iota(jnp.int32, (q_ref.shape[0], 2 * V_HEAD), 1)
    for pair in range(heads // 2):
        vsl = slice(pair * 2 * V_HEAD, (pair + 1) * 2 * V_HEAD)
        outs = []
        for hh in range(2):
            h = 2 * pair + hh
            sl = slice(h * HEAD_PAD, (h + 1) * HEAD_PAD)
            q = q_ref[:, sl]
            s_o = lax.dot_general(q, ko_ref[:, sl], contract_last, preferred_element_type=F32)
            m = jnp.max(s_o, axis=-1, keepdims=True)
            if has_ctx:
                s_c = lax.dot_general(q, kc_ref[:, sl], contract_last, preferred_element_type=F32)
                m = jnp.maximum(m, jnp.max(s_c, axis=-1, keepdims=True))
            p_o = jnp.exp2(s_o - m)
            l = jnp.sum(p_o, axis=-1, keepdims=True)
            acc = _dot(p_o.astype(BF16), vo_ref[:, vsl])
            if has_ctx:
                p_c = jnp.exp2(s_c - m)
                l = l + jnp.sum(p_c, axis=-1, keepdims=True)
                acc = acc + _dot(p_c.astype(BF16), vc_ref[:, vsl])
            outs.append(acc / l)
        o_ref[:, vsl] = jnp.where(lane < V_HEAD, outs[0], outs[1]).astype(BF16)


def _attn_call(q, k, v, kc, vc, batch, seq_len, ctx_len):
    t = q.shape[0]
    tq = min(seq_len, 512)
    heads = 8
    nq = seq_len // tq
    npair = MLA_HEADS // heads
    qw = heads * HEAD_PAD
    vw = heads * V_HEAD
    in_specs = [
        pl.BlockSpec((tq, qw), lambda b, h, i: (b * nq + i, h)),
        pl.BlockSpec((seq_len, qw), lambda b, h, i: (b, h)),
        pl.BlockSpec((seq_len, vw), lambda b, h, i: (b, h)),
    ]
    args = [q, k, v]
    if kc is not None:
        in_specs += [pl.BlockSpec((ctx_len, qw), lambda b, h, i: (b, h)),
                     pl.BlockSpec((ctx_len, vw), lambda b, h, i: (b, h))]
        args += [kc, vc]
    return pl.pallas_call(
        functools.partial(_attn_kernel, has_ctx=kc is not None, heads=heads),
        out_shape=jax.ShapeDtypeStruct((t, MLA_HEADS * V_HEAD), BF16),
        grid=(batch, npair, nq),
        in_specs=in_specs,
        out_specs=pl.BlockSpec((tq, vw), lambda b, h, i: (b * nq + i, h)),
        compiler_params=_cparams(("parallel", "parallel", "arbitrary"), 56),
        name="attention",
    )(*args)


def _convpool_kernel(cu_ref, bg_ref, cg_ref, pu_ref, cw_ref, cb_ref, pw_ref, ps_ref, zb_ref, zd_ref):
    n = cu_ref.shape[0]
    gps = pw_ref.shape[0]
    edge = max(POOL_WINDOWS) // 2
    assert edge % SUBLANES == 0 and n >= 4 * edge

    def shifts(rows, zero_fill):
        def down(x, k):
            y = pltpu.roll(x, k, axis=0)
            if zero_fill:
                y = jnp.where(lax.broadcasted_iota(jnp.int32, x.shape, 0) >= k, y, 0.0)
            return y

        def up(x, k):
            y = pltpu.roll(x, rows - k, axis=0)
            if zero_fill:
                y = jnp.where(lax.broadcasted_iota(jnp.int32, x.shape, 0) < rows - k, y, 0.0)
            return y
        return down, up

    def conv_rows(r0, rows, zero_fill):
        down, up = shifts(rows, zero_fill)
        v = cg_ref[r0:r0 + rows, :] * cu_ref[r0:r0 + rows, :]
        conv = (down(v, 1) * cw_ref[0:1, :] + v * cw_ref[1:2, :] + up(v, 1) * cw_ref[2:3, :]
                + cb_ref[...])
        return (bg_ref[r0:r0 + rows, :] * conv).astype(BF16)

    def pool_rows(r0, rows, zero_fill, gg, half):
        down, up = shifts(rows, zero_fill)
        sl = slice(gg * LANES, (gg + 1) * LANES)
        u = pu_ref[r0:r0 + rows, sl]
        fwd = u
        bwd = u
        k = 1
        while k < half:
            fwd = fwd + up(fwd, k)
            bwd = bwd + down(bwd, k)
            k *= 2
        total = fwd + down(bwd, 1)
        rowf = (lax.broadcasted_iota(jnp.int32, u.shape, 0) + r0).astype(F32)
        cnt = jnp.minimum(rowf + half, float(n)) - jnp.maximum(rowf - half, 0.0)
        mean = total / cnt - u
        return (_dot(mean.astype(BF16), pw_ref[gg]) * ps_ref[:, sl]).astype(BF16)

    zb_ref[...] = conv_rows(0, n, False)
    zb_ref[0:edge, :] = conv_rows(0, 2 * edge, True)[0:edge]
    zb_ref[n - edge:n, :] = conv_rows(n - 2 * edge, 2 * edge, True)[edge:]

    step = pl.program_id(1)
    for first in range(0, len(POOL_WINDOWS), gps):
        @pl.when(step * gps == first)
        def _(first=first):
            for gg in range(gps):
                sl = slice(gg * LANES, (gg + 1) * LANES)
                half = POOL_WINDOWS[first + gg] // 2
                zd_ref[:, sl] = pool_rows(0, n, False, gg, half)
                zd_ref[0:edge, sl] = pool_rows(0, 2 * edge, True, gg, half)[0:edge]
                zd_ref[n - edge:n, sl] = pool_rows(n - 2 * edge, 2 * edge, True, gg, half)[edge:]


def _convpool_call(p, lw, batch, seq_len, cols):
    t = p.shape[0]
    gps = 2 if seq_len > 512 else len(POOL_WINDOWS)
    cw = gps * LANES
    nblk = lw["conv_w"].shape[1] // cw

    def pspec(name):
        base = cols[name] // cw
        return pl.BlockSpec((seq_len, cw), lambda b, g: (b, base + g))

    vec = lambda b, g: (0, g)
    return pl.pallas_call(
        _convpool_kernel,
        out_shape=(jax.ShapeDtypeStruct((t, nblk * cw), BF16), jax.ShapeDtypeStruct((t, nblk * cw), BF16)),
        grid=(batch, nblk),
        in_specs=[pspec("conv_u"), pspec("conv_bg"), pspec("conv_cg"), pspec("pool_u"),
                  pl.BlockSpec((3, cw), vec), pl.BlockSpec((1, cw), vec),
                  pl.BlockSpec((gps, LANES, LANES), lambda b, g: (g, 0, 0)), pl.BlockSpec((1, cw), vec)],
        out_specs=(pl.BlockSpec((seq_len, cw), lambda b, g: (b, g)),
                   pl.BlockSpec((seq_len, cw), lambda b, g: (b, g))),
        compiler_params=_cparams(("parallel", "parallel"), 40),
        name="conv_pool",
    )(p, p, p, p, lw["conv_w"], lw["conv_b"], lw["pool_w"], lw["pool_scale"])


SSM_R = 4
SSM_GB = LANES // SSM_GROUP_CH


def _ssm_kernel(*refs, steps, nchunks, nblk):
    u_refs = refs[:nblk]
    h0_ref, a_ref, mb_ref, m1_ref, m2_ref = refs[nblk:nblk + 5]
    y_refs = refs[nblk + 5:2 * nblk + 5]
    hf_ref, slab_scr, xs_scr, st_scr = refs[2 * nblk + 5:]
    d = pl.program_id(0)
    c = pl.program_id(2)
    width = st_scr.shape[1]
    bw = width // nblk

    @pl.when(c == 0)
    def _():
        st_scr[...] = h0_ref[0]

    for blk in range(nblk):
        for j in range(SSM_R):
            for b in range(SUBLANES):
                slab_scr[blk * SSM_R + j, pl.ds(b, steps, stride=SUBLANES), :] = (
                    u_refs[blk][b, pl.ds(j, steps, stride=SSM_R), :])

    def packed(blk):
        return jnp.concatenate([slab_scr[blk * SSM_R + j] for j in range(SSM_R)], axis=1).astype(BF16)

    for blk in range(nblk):
        xs_scr[:, blk * bw:(blk + 1) * bw] = _dot(packed(blk), mb_ref[0, blk])

    gw = width // 2
    for part in range(width // gw):
        offs = []
        for blk in range(part * nblk // 2, (part + 1) * nblk // 2):
            offs += [(blk * bw + k * LANES, blk * bw + bw // 2 + k * LANES) for k in range(bw // 2 // LANES)]
        coef = [(a_ref[0, :, r0:r0 + LANES], a_ref[0, :, i0:i0 + LANES]) for r0, i0 in offs]

        def body(t, carry, offs=offs, coef=coef):
            tt = t + d * (steps - 1 - 2 * t)
            row0 = pl.multiple_of(tt * SUBLANES, SUBLANES)
            new = []
            for (r0, i0), (a_re, a_im), x_re, x_im in zip(offs, coef, carry[0::2], carry[1::2]):
                n_re = (a_re * x_re - a_im * x_im) + xs_scr[pl.ds(row0, SUBLANES), r0:r0 + LANES]
                n_im = (a_re * x_im + a_im * x_re) + xs_scr[pl.ds(row0, SUBLANES), i0:i0 + LANES]
                xs_scr[pl.ds(row0, SUBLANES), r0:r0 + LANES] = x_re
                xs_scr[pl.ds(row0, SUBLANES), i0:i0 + LANES] = x_im
                new += [n_re, n_im]
            return tuple(new)

        init = []
        for r0, i0 in offs:
            init += [st_scr[:, r0:r0 + LANES], st_scr[:, i0:i0 + LANES]]
        final = lax.fori_loop(0, steps, body, tuple(init), unroll=2)
        for (r0, i0), x_re, x_im in zip(offs, final[0::2], final[1::2]):
            st_scr[:, r0:r0 + LANES] = x_re
            st_scr[:, i0:i0 + LANES] = x_im

    for blk in range(nblk):
        y = (_dot(xs_scr[:, blk * bw:(blk + 1) * bw].astype(BF16), m1_ref[0, blk])
             + _dot(packed(blk), m2_ref[0, blk]))
        for j in range(SSM_R):
            slab_scr[blk * SSM_R + j] = y[:, j * LANES:(j + 1) * LANES]
        for j in range(SSM_R):
            for b in range(SUBLANES):
                y_refs[blk][0, b, pl.ds(j, steps, stride=SSM_R), :] = (
                    slab_scr[blk * SSM_R + j, pl.ds(b, steps, stride=SUBLANES), :])

    @pl.when(c == nchunks - 1)
    def _():
        hf_ref[0] = st_scr[...]


def _ssm_call(us, h0, lw):
    batch, seq_len, _ = us[0].shape
    width = h0.shape[-1]
    nblk = lw["ssm_mb"].shape[2]
    assert len(us) == nblk
    l = lw["l"]
    steps = min(seq_len // SSM_R, 64)
    tokens = steps * SSM_R
    nchunks = seq_len // tokens
    nbg = batch // SUBLANES
    rows = steps * SUBLANES

    def cidx(d, c):
        return c + d * (nchunks - 1 - 2 * c)

    uspec = pl.BlockSpec((SUBLANES, tokens, LANES), lambda d, b, c: (b, cidx(d, c), 0))

    def wspec(arr):
        return pl.BlockSpec((None, 1) + arr.shape[2:], lambda d, b, c: (l, d, 0, 0, 0),
                            pipeline_mode=pl.Buffered(1))

    yshape = jax.ShapeDtypeStruct((2, batch, seq_len, LANES), F32)
    yspec = pl.BlockSpec((1, SUBLANES, tokens, LANES), lambda d, b, c: (d, b, cidx(d, c), 0))
    outs = pl.pallas_call(
        functools.partial(_ssm_kernel, steps=steps, nchunks=nchunks, nblk=nblk),
        out_shape=(yshape,) * nblk + (jax.ShapeDtypeStruct((2, batch, width), F32),),
        grid=(2, nbg, nchunks),
        in_specs=[uspec] * nblk + [
            pl.BlockSpec((1, SUBLANES, width), lambda d, b, c: (d, b, 0)),
            pl.BlockSpec((None, 1, SUBLANES, width), lambda d, b, c: (l, d, 0, 0)),
            wspec(lw["ssm_mb"]), wspec(lw["ssm_m1"]), wspec(lw["ssm_m2"]),
        ],
        out_specs=(yspec,) * nblk + (pl.BlockSpec((1, SUBLANES, width), lambda d, b, c: (d, b, 0)),),
        scratch_shapes=[pltpu.VMEM((nblk * SSM_R, rows, LANES), F32), pltpu.VMEM((rows, width), F32),
                        pltpu.VMEM((SUBLANES, width), F32)],
        compiler_params=_cparams(("arbitrary", "arbitrary", "arbitrary"), 52),
        name="s5_scan",
    )(*us, h0, lw["ssm_a"], lw["ssm_mb"], lw["ssm_m1"], lw["ssm_m2"])
    return outs[:nblk], outs[nblk]


def _merge_kernel(*refs, nblk, ncast):
    (h_ref, za_ref, zb_ref, su_ref, sd_ref, zd_ref, wg0, wg1, wg2, wg3, wa, wb, wca, wcg, wd) = refs[:15]
    yf_refs = refs[15:15 + nblk]
    yb_refs = refs[15 + nblk:15 + 2 * nblk]
    cast_in = refs[15 + 2 * nblk:15 + 2 * nblk + ncast]
    o_ref = refs[15 + 2 * nblk + ncast]
    cast_out = refs[16 + 2 * nblk + ncast:16 + 2 * nblk + 2 * ncast]
    y_scr = refs[-1]

    for src, dst in zip(cast_in, cast_out):
        dst[...] = src[...].astype(BF16)

    @pl.when(pl.program_id(1) == 0)
    def _():
        y_both = jnp.concatenate([f[0] + b[0] for f, b in zip(yf_refs, yb_refs)], axis=1)
        y_scr[...] = (y_both + sd_ref[...] * su_ref[...]).astype(BF16)

    tm = h_ref.shape[0]
    nsub = 2
    for sub in range(nsub):
        rs = slice(sub * tm // nsub, (sub + 1) * tm // nsub)
        h = h_ref[rs, :]
        y = y_scr[rs, :]

        def gate(w_ref, h=h):
            return jax.nn.sigmoid(_dot_t(h, w_ref[...]))

        merged = gate(wg0) * _dot(za_ref[rs, :], wa[...])
        merged = merged + gate(wg1) * _dot(zb_ref[rs, :], wb[...])
        merged = merged + gate(wg2) * (_dot(y, wca[...]) * jax.nn.sigmoid(_dot(y, wcg[...])))
        merged = merged + gate(wg3) * _dot(zd_ref[rs, :], wd[...])
        o_ref[rs, :] = merged.astype(BF16)


def _merge_call(h, p, za, zb, ys, zd, lw, cols, cast=()):
    t, d = h.shape
    bw = za.shape[1]
    tm, tn = 512, 512
    nj = d // tn
    nblk = len(ys)
    row = lambda i, j: (i, 0)
    su_blk = cols["ssm_u"] // bw

    l = lw["l"]

    gate_blk0 = cols["k_rope"] // tn

    row0 = l * lw["w_rows"] + lw["tail_row0"]

    def gspec(k):
        assert (row0 + gate_blk0 * tn) % 16 == 0
        return pl.BlockSpec((pl.Element(tn), pl.Element(d)),
                            lambda i, j: (pl.multiple_of(row0 + (gate_blk0 + k * nj + j) * tn, 16), 0))

    def yspec(direction):
        return pl.BlockSpec((1, tm, LANES), lambda i, j: (direction, i, 0))

    wcol = pl.BlockSpec((None, bw, tn), lambda i, j: (l, 0, j))
    in_specs = [
        pl.BlockSpec((tm, d), row),
        pl.BlockSpec((tm, bw), row), pl.BlockSpec((tm, bw), row),
        pl.BlockSpec((tm, bw), lambda i, j: (i, su_blk)),
        pl.BlockSpec((1, bw), lambda i, j: (0, 0)),
        pl.BlockSpec((tm, bw), row),
        gspec(0), gspec(1), gspec(2), gspec(3),
        wcol, wcol, wcol, pl.BlockSpec((None, bw, tn), lambda i, j: (l, 0, nj + j)), wcol,
    ] + [yspec(0)] * nblk + [yspec(1)] * nblk
    out_shape = [jax.ShapeDtypeStruct((t, d), BF16)]
    out_specs = [pl.BlockSpec((tm, tn), lambda i, j: (i, j))]
    nsteps = (t // tm) * nj
    for w in cast:
        rows, n = w.shape[1] // nsteps, w.shape[2]
        assert rows * nsteps == w.shape[1] and rows % (2 * SUBLANES) == 0
        in_specs.append(pl.BlockSpec((None, rows, n), lambda i, j: (l, i * nj + j, 0)))
        out_shape.append(jax.ShapeDtypeStruct(w.shape[1:], BF16))
        out_specs.append(pl.BlockSpec((rows, n), lambda i, j: (i * nj + j, 0)))
    wg = lw["w_tail"]
    outs = pl.pallas_call(
        functools.partial(_merge_kernel, nblk=nblk, ncast=len(cast)),
        out_shape=tuple(out_shape),
        grid=(t // tm, nj),
        in_specs=in_specs,
        out_specs=tuple(out_specs),
        scratch_shapes=[pltpu.VMEM((tm, bw), BF16)],
        compiler_params=_cparams(("parallel", "arbitrary"), 48),
        name="branch_merge",
    )(h, za, zb, p, lw["ssm_d"], zd,
      wg, wg, wg, wg, lw["w_mla_o"], lw["w_conv_o"], lw["w_glu"], lw["w_glu"], lw["w_pool_o"], *ys, *ys, *cast)
    return outs[0], tuple(outs[1:])


def _outproj_kernel(x_ref, m_ref, gt_ref, sh_ref, sc_ref, g_ref, wo_ref, x1_ref, h2_ref):
    tm = x_ref.shape[0]
    nsub = 2
    for sub in range(nsub):
        rs = slice(sub * tm // nsub, (sub + 1) * tm // nsub)
        x1 = x_ref[rs, :] + gt_ref[0] * _dot(m_ref[rs, :], wo_ref[...])
        x1_ref[rs, :] = x1
        h2_ref[rs, :] = _modnorm(x1, g_ref[...], sc_ref[0], sh_ref[0]).astype(BF16)


def _outproj_call(x, merged, mod, lw, seq_len, per_batch):
    t, d = x.shape
    tm = 512
    mspec = functools.partial(_mod_spec, d=d, tm=tm, seq_len=seq_len, per_batch=per_batch)
    row = lambda i: (i, 0)
    return pl.pallas_call(
        _outproj_kernel,
        out_shape=(jax.ShapeDtypeStruct((t, d), F32), jax.ShapeDtypeStruct((t, d), BF16)),
        grid=(t // tm,),
        in_specs=[
            pl.BlockSpec((tm, d), row), pl.BlockSpec((tm, d), row),
            mspec(2), mspec(3), mspec(4),
            pl.BlockSpec((1, d), lambda i: (0, 0)),
            pl.BlockSpec((None, d, d), lambda i: (lw["l"], 0, 0), pipeline_mode=pl.Buffered(1)),
        ],
        out_specs=(pl.BlockSpec((tm, d), row), pl.BlockSpec((tm, d), row)),
        compiler_params=_cparams(("parallel",), 48),
        name="out_proj",
    )(x, merged, mod, mod, mod, lw["norm2_g"], lw["w_o"])


def _mlp_kernel(x_ref, h_ref, gt_ref, w1_ref, w2_ref, o_ref):
    @pl.when(pl.program_id(1) == 0)
    def _():
        o_ref[...] = x_ref[...]

    hid = jnp.square(jnp.maximum(_dot(h_ref[...], w1_ref[...]), 0.0))
    o_ref[...] += gt_ref[0] * _dot(hid.astype(BF16), w2_ref[...])


def _mlp_call(x, h, mod, lw, seq_len, per_batch):
    t, d = x.shape
    hidden = lw["w_mlp1"].shape[2]
    tm, tk = 512, 1024
    l = lw["l_mlp"]
    mspec = functools.partial(_mod_spec, d=d, tm=tm, seq_len=seq_len, per_batch=per_batch)
    row = lambda i, k: (i, 0)
    return pl.pallas_call(
        _mlp_kernel,
        out_shape=jax.ShapeDtypeStruct((t, d), F32),
        grid=(t // tm, hidden // tk),
        in_specs=[
            pl.BlockSpec((tm, d), row), pl.BlockSpec((tm, d), row), mspec(5),
            pl.BlockSpec((None, d, tk), lambda i, k: (l, 0, k)),
            pl.BlockSpec((None, tk, d), lambda i, k: (l, k, 0)),
        ],
        out_specs=pl.BlockSpec((tm, d), row),
        compiler_params=_cparams(("parallel", "arbitrary"), 48),
        name="mlp",
    )(x, h, mod, lw["w_mlp1"], lw["w_mlp2"])


def _pad_heads(w, per_head, lo):
    lead = w.shape[:-1]
    w = w.reshape(lead + (MLA_HEADS, per_head))
    w = jnp.pad(w, [(0, 0)] * len(lead) + [(0, 0), (lo, HEAD_PAD - lo - per_head)])
    return w.reshape(lead + (MLA_HEADS * HEAD_PAD,))


_ROPE_PARTNER = tuple((i // 16) * 16 + (i % 16 + 8) % 16 for i in range(ROPE_DIM))


def _swap_heads(w):
    lead = w.shape[:-1]
    w = w.reshape(lead + (MLA_HEADS, QK_HEAD))[..., QK_NOPE:][..., jnp.array(_ROPE_PARTNER)]
    w = jnp.pad(w, [(0, 0)] * len(lead) + [(0, 0), (QK_NOPE, HEAD_PAD - QK_HEAD)])
    return w.reshape(lead + (MLA_HEADS * HEAD_PAD,))


def _expand_kernel(t_ref, o_ref, *, axis, gb, cg):
    t = t_ref[...]
    big = jnp.concatenate([t] * gb, axis=axis)
    rows = lax.broadcasted_iota(jnp.int32, big.shape, 0)
    lanes = lax.broadcasted_iota(jnp.int32, big.shape, 1)
    if axis == 1:
        keep = (rows // cg) % gb == lanes // t.shape[1]
    else:
        keep = rows // t.shape[0] == (lanes // cg) % gb
    o_ref[...] = jnp.where(keep, big, 0.0).astype(BF16)


def _expand_call(parts, axis, gb, cg):
    _, ns, nblk, r, c = parts.shape
    side = gb * (c if axis == 1 else r)
    out_rc = (r, 2 * side) if axis == 1 else (2 * side, c)
    blk_rc = (r, side) if axis == 1 else (side, c)
    omap = (lambda p, s, b: (s, b, 0, p)) if axis == 1 else (lambda p, s, b: (s, b, p, 0))
    return pl.pallas_call(
        functools.partial(_expand_kernel, axis=axis, gb=gb, cg=cg),
        out_shape=jax.ShapeDtypeStruct((ns, nblk) + out_rc, BF16),
        grid=(2, ns, nblk),
        in_specs=[pl.BlockSpec((None, None, None, r, c), lambda p, s, b: (p, s, b, 0, 0))],
        out_specs=pl.BlockSpec((None, None) + blk_rc, omap),
        compiler_params=_cparams(("parallel", "parallel", "parallel"), 32),
        name="s5_expand",
    )(parts)


def _prep_ssm(lam_re, lam_im, log_step, b_re, b_im, c_re, c_im):
    r, gb = SSM_R, SSM_GB
    nl, ndir, g, n, cg = b_re.shape
    ns = nl * ndir
    nblk = g // gb
    flat = lambda v: v.reshape((ns,) + v.shape[2:])
    lam_re, lam_im, log_step, b_re, b_im, c_re, c_im = map(
        flat, (lam_re, lam_im, log_step, b_re, b_im, c_re, c_im))
    step = jnp.exp(log_step)[..., None]
    pw = []
    for p in range(r + 1):
        mag = jnp.exp(p * (lam_re * step))
        pw.append((mag * jnp.cos(p * (lam_im * step)), mag * jnp.sin(p * (lam_im * step))))
    ar, ai = pw[1]
    den = lam_re * lam_re + lam_im * lam_im
    qr = ((ar - 1.0) * lam_re + ai * lam_im) / den
    qi = (ai * lam_re - (ar - 1.0) * lam_im) / den
    bb_re = qr[..., None] * b_re - qi[..., None] * b_im
    bb_im = qr[..., None] * b_im + qi[..., None] * b_re
    fwd = (jnp.arange(ns) % ndir == 0)[:, None, None]

    def power(p_fwd, p_bwd):
        return (jnp.where(fwd, pw[p_fwd][0], pw[p_bwd][0]), jnp.where(fwd, pw[p_fwd][1], pw[p_bwd][1]))


    t_re, t_im = [], []
    for j in range(r):
        pr, pi = power(r - 1 - j, j)
        t_re.append(pr[..., None] * bb_re - pi[..., None] * bb_im)
        t_im.append(pr[..., None] * bb_im + pi[..., None] * bb_re)
    def inc_block(ts):
        t = jnp.stack(ts, axis=1).reshape(ns, r, nblk, gb, n, cg)
        return jnp.transpose(t, (0, 2, 1, 3, 5, 4)).reshape(ns, nblk, r * gb * cg, n)

    mb = _expand_call(jnp.stack([inc_block(t_re), inc_block(t_im)]), 1, gb, cg)

    cl_re, cl_im = [], []
    for j in range(r):
        pr, pi = power(j + 1, r - j)
        cl_re.append(c_re * pr[:, :, None, :] - c_im * pi[:, :, None, :])
        cl_im.append(c_re * pi[:, :, None, :] + c_im * pr[:, :, None, :])

    def out_block(ts):
        t = jnp.stack(ts, axis=1).reshape(ns, r, nblk, gb, cg, n)
        return jnp.transpose(t, (0, 2, 5, 1, 3, 4)).reshape(ns, nblk, n, r * gb * cg)

    m1 = _expand_call(jnp.stack([out_block(cl_re), -out_block(cl_im)]), 0, gb, cg)

    hi = lax.Precision.HIGHEST
    lane_g = jnp.arange(gb * cg) // cg
    mask_k = (lane_g[:, None] == lane_g[None, :]).astype(F32)
    kd = []
    for p in range(r):
        pr, pi = pw[p]
        k2 = (jnp.einsum("sgcn,sgnk->sgck", c_re * pr[:, :, None, :] - c_im * pi[:, :, None, :], bb_re, precision=hi)
              - jnp.einsum("sgcn,sgnk->sgck", c_re * pi[:, :, None, :] + c_im * pr[:, :, None, :], bb_im, precision=hi))
        k2 = jnp.swapaxes(k2, -1, -2).reshape(ns, nblk, gb * cg, cg)
        kd.append(jnp.tile(k2, (1, 1, 1, gb)) * mask_k)
    fwd4 = fwd[..., None]
    zero = jnp.zeros_like(kd[0])
    m2_rows = []
    for i in range(r):
        blocks = []
        for j in range(r):
            if i == j:
                blocks.append(kd[0])
            elif j > i:
                blocks.append(jnp.where(fwd4, kd[j - i], zero))
            else:
                blocks.append(jnp.where(fwd4, zero, kd[i - j]))
        m2_rows.append(jnp.concatenate(blocks, axis=-1))
    m2 = jnp.concatenate(m2_rows, axis=2)

    a = jnp.stack([pw[r][0].reshape(ns, nblk, gb * n), pw[r][1].reshape(ns, nblk, gb * n)], axis=2)
    a = jnp.broadcast_to(a.reshape(ns, 1, 2 * g * n), (ns, SUBLANES, 2 * g * n))
    unflat = lambda v: v.reshape((nl, ndir) + v.shape[1:])
    return unflat(a), unflat(mb), unflat(m1), unflat(m2.astype(BF16))


def _prep_stacked(prm, sizes):
    q_lora, kv_lora, conv_w, ssm_w, pool_w_ = sizes
    w_in_t = jnp.swapaxes(prm["w_in"], 1, 2)
    tail0 = q_lora + kv_lora + ROPE_DIM
    w_tail = w_in_t.astype(BF16).reshape(-1, w_in_t.shape[2])
    head = w_in_t[:, :tail0, :]
    cols = {}
    o = 0
    for name, width in (("conv_u", conv_w), ("conv_bg", conv_w), ("conv_cg", conv_w), ("ssm_u", ssm_w),
                        ("pool_u", pool_w_), ("k_rope", HEAD_PAD), ("k_rope_sw", HEAD_PAD), ("q_a", q_lora),
                        ("kv_a", kv_lora)):
        cols[name] = o
        o += width
    k_rope = head[:, q_lora + kv_lora:tail0, :]
    rope_lanes = ((0, 0), (QK_NOPE, HEAD_PAD - QK_HEAD), (0, 0))
    w_att = jnp.concatenate([jnp.pad(k_rope, rope_lanes),
                             jnp.pad(k_rope[:, jnp.array(_ROPE_PARTNER), :], rope_lanes),
                             head[:, :q_lora + kv_lora, :]], axis=1).astype(BF16)
    big = dict(w_tail=w_tail, w_att=w_att, tail_row0=tail0, w_rows=w_in_t.shape[1])
    for name in ("w_mla_o", "w_conv_o", "w_glu", "w_pool_o", "w_o"):
        big[name] = prm[name].astype(BF16)
    return big, cols


def _prep_layer(l, prm, sizes, big, ssm):
    kv_lora = sizes[1]
    w_ukv = prm["w_ukv"][l].reshape(kv_lora, MLA_HEADS, QK_NOPE + V_HEAD)
    w_uk = _pad_heads(w_ukv[:, :, :QK_NOPE].reshape(kv_lora, -1), QK_NOPE, 0)
    w_uv = w_ukv[:, :, QK_NOPE:].reshape(kv_lora, -1)
    ssm_a, ssm_mb, ssm_m1, ssm_m2 = ssm

    row = lambda v: v.reshape(1, -1)
    head_gain = lambda g: row(jnp.pad(g, (0, HEAD_PAD - QK_HEAD)))
    partner_gain = lambda g: row(jnp.pad(g[QK_NOPE:][jnp.array(_ROPE_PARTNER)], (QK_NOPE, HEAD_PAD - QK_HEAD)))
    lw = dict(
        big, l=l,
        norm1_g=row(prm["norm1_g"][l]), norm2_g=row(prm["norm2_g"][l]),
        q_a_norm_g=row(prm["q_a_norm_g"][l]), kv_a_norm_g=row(prm["kv_a_norm_g"][l]),
        q_norm_g=head_gain(prm["q_norm_g"][l]), k_norm_g=head_gain(prm["k_norm_g"][l]),
        q_norm_g_sw=partner_gain(prm["q_norm_g"][l]), k_norm_g_sw=partner_gain(prm["k_norm_g"][l]),
        w_uq=_pad_heads(prm["w_uq"][l], QK_HEAD, 0).astype(BF16),
        w_uq_sw=_swap_heads(prm["w_uq"][l]).astype(BF16),
        w_uk=w_uk.astype(BF16), w_uv=w_uv.astype(BF16),
        conv_w=prm["conv_w"][l], conv_b=row(prm["conv_b"][l]),
        ssm_a=ssm_a, ssm_mb=ssm_mb, ssm_m1=ssm_m1, ssm_m2=ssm_m2,
        ssm_d=row(prm["ssm_d"][l]),
        pool_w=prm["pool_w"][l].astype(BF16), pool_scale=row(prm["pool_scale"][l]),
    )
    return lw


def _rope_tables(seq_len):
    half = ROPE_DIM // 2
    inv_freq = ROPE_THETA ** (-jnp.arange(0, half, 2, dtype=F32) / half)
    t = jnp.arange(seq_len)
    ang_r = (t // GRID_W).astype(F32)[:, None] * inv_freq
    ang_c = (t % GRID_W).astype(F32)[:, None] * inv_freq
    ones_lo = jnp.ones((seq_len, QK_NOPE), F32)
    zeros_lo = jnp.zeros((seq_len, QK_NOPE), F32)
    tail = jnp.zeros((seq_len, HEAD_PAD - QK_HEAD), F32)
    c = jnp.concatenate([ones_lo, jnp.cos(ang_r), jnp.cos(ang_r), jnp.cos(ang_c), jnp.cos(ang_c), tail], axis=1)
    s = jnp.concatenate([zeros_lo, -jnp.sin(ang_r), jnp.sin(ang_r), -jnp.sin(ang_c), jnp.sin(ang_c), tail], axis=1)
    return c, s


def _states_to_cols(st):
    b, ndir, _, g, n = st.shape
    st = st.reshape(b, ndir, 2, g // SSM_GB, SSM_GB * n)
    return jnp.transpose(st, (1, 0, 3, 2, 4)).reshape(ndir, b, 2 * g * n)


def _cols_to_states(hf, g, n):
    ndir, b, _ = hf.shape
    hf = hf.reshape(ndir, b, g // SSM_GB, 2, SSM_GB * n)
    return jnp.transpose(hf, (1, 0, 3, 2, 4)).reshape(b, ndir, 2, g, n)


def _mixers(x, mod, lw, cols, batch, seq_len, per_batch, rope, ctx, cast=()):
    p, h1, q, k, v, ckv, *us = _inproj_call(x, mod, lw, rope, seq_len, per_batch, cols)
    if ctx is not None:
        ctx_ckv, ctx_kpe, h0 = ctx
        kc, vc = _ctxprep_call(ctx_ckv, ctx_kpe, lw)
        ctx_len = ctx_ckv.shape[0] // batch
    else:
        kc = vc = None
        ctx_len = 0
        h0 = jnp.zeros((2, batch, lw["ssm_a"].shape[-1]), F32)
    za = _attn_call(q, k, v, kc, vc, batch, seq_len, ctx_len)
    zb, zd = _convpool_call(p, lw, batch, seq_len, cols)

    ys, hf = _ssm_call([u.reshape(batch, seq_len, LANES) for u in us], h0, lw)
    ys = [y.reshape(2, batch * seq_len, LANES) for y in ys]
    merged, cast_out = _merge_call(h1, p, za, zb, ys, zd, lw, cols, cast)
    kr = cols["k_rope"] + QK_NOPE
    return merged, cast_out, ckv, p[:, kr:kr + ROPE_DIM], hf


def _channel_mixer(x, merged, mod, lw, seq_len, per_batch):
    x1, h2 = _outproj_call(x, merged, mod, lw, seq_len, per_batch)
    return _mlp_call(x1, h2, mod, lw, seq_len, per_batch)


def kernel(x_prompt, x_sample, c, cache_ckv, cache_krope, state_ssm, c_ctx, w_ada, b_ada, norm1_g, norm2_g, w_in, q_a_norm_g, kv_a_norm_g, w_uq, w_ukv, q_norm_g, k_norm_g, w_mla_o, conv_w, conv_b, w_conv_o, ssm_lam_re, ssm_lam_im, ssm_log_step, ssm_b_re, ssm_b_im, ssm_c_re, ssm_c_im, ssm_d, w_glu, pool_w, pool_scale, w_pool_o, w_o, w_mlp1, w_mlp2):
    prm = dict(norm1_g=norm1_g, norm2_g=norm2_g, w_in=w_in, q_a_norm_g=q_a_norm_g, kv_a_norm_g=kv_a_norm_g,
               w_uq=w_uq, w_ukv=w_ukv, q_norm_g=q_norm_g, k_norm_g=k_norm_g, w_mla_o=w_mla_o,
               conv_w=conv_w, conv_b=conv_b, w_conv_o=w_conv_o, ssm_lam_re=ssm_lam_re, ssm_lam_im=ssm_lam_im,
               ssm_log_step=ssm_log_step, ssm_b_re=ssm_b_re, ssm_b_im=ssm_b_im, ssm_c_re=ssm_c_re,
               ssm_c_im=ssm_c_im, ssm_d=ssm_d, w_glu=w_glu, pool_w=pool_w, pool_scale=pool_scale,
               w_pool_o=w_pool_o, w_o=w_o, w_mlp1=w_mlp1, w_mlp2=w_mlp2)
    depth = w_in.shape[0]
    bp, lp, d = x_prompt.shape
    bs, ls, _ = x_sample.shape
    past = cache_ckv.shape[2]
    g, n = state_ssm.shape[-2:]
    sizes = (w_uq.shape[1], w_ukv.shape[1], conv_w.shape[-1], ssm_d.shape[-1], pool_scale.shape[-1])

    rows = -(-(1 + bs) // SUBLANES) * SUBLANES
    cvec = jnp.zeros((rows, d), F32).at[0].set(c_ctx).at[1:1 + bs].set(c)
    mods = _ada_call(cvec, w_ada, b_ada)
    rope = _rope_tables(ls)
    ssm = _prep_ssm(ssm_lam_re, ssm_lam_im, ssm_log_step, ssm_b_re, ssm_b_im, ssm_c_re, ssm_c_im)
    big, cols = _prep_stacked(prm, sizes)

    yp = x_prompt.reshape(bp * lp, d)
    ys = x_sample.reshape(bs * ls, d)
    ckv_list, krope_list, ssm_list = [], [], []
    for l in range(depth):
        lw = _prep_layer(l, prm, sizes, big, ssm)
        mod_ctx = mods[l, 0:1].reshape(6, 1, d)
        mod_lat = mods[l, 1:1 + bs].reshape(bs * 6, 1, d)
        ctx_kpe = jnp.pad(cache_krope[:, l].reshape(bs * past, ROPE_DIM),
                          ((0, 0), (QK_NOPE, HEAD_PAD - QK_HEAD)))
        ctx = (cache_ckv[:, l].reshape(bs * past, -1), ctx_kpe, _states_to_cols(state_ssm[:, l]))
        merged_s, (w1b, w2b), _, _, _ = _mixers(ys, mod_lat, lw, cols, bs, ls, True, rope, ctx,
                                                cast=(w_mlp1, w_mlp2))
        merged_p, _, ckv_l, krope_l, hf_l = _mixers(yp, mod_ctx, lw, cols, bp, lp, False, None, None)
        ckv_list.append(ckv_l.reshape(bp, lp, -1))
        krope_list.append(krope_l.reshape(bp, lp, -1))
        ssm_list.append(_cols_to_states(hf_l, g, n))
        lw_mlp = dict(lw, w_mlp1=w1b[None], w_mlp2=w2b[None], l_mlp=0)
        yp = _channel_mixer(yp, merged_p, mod_ctx, lw_mlp, lp, False)
        ys = _channel_mixer(ys, merged_s, mod_lat, lw_mlp, ls, True)
    return (yp.reshape(bp, lp, d), ys.reshape(bs, ls, d), jnp.stack(ckv_list, axis=1),
            jnp.stack(krope_list, axis=1), jnp.stack(ssm_list, axis=1))
```

```python
import functools

import jax
import jax.numpy as jnp
from jax import lax
from jax.experimental import pallas as pl
from jax.experimental.pallas import tpu as pltpu

F32 = jnp.float32
BF16 = jnp.bfloat16

GRID_W = 64
MLA_HEADS = 8
QK_NOPE = 64
ROPE_DIM = 32
QK_HEAD = QK_NOPE + ROPE_DIM
V_HEAD = 64
ROPE_THETA = 10000.0
SSM_GROUP_CH = 16
POOL_WINDOWS = (2, 4, 8, 16)
EPS = 1e-6
LOG2E = 1.4426950408889634

LANES = 128
SUBLANES = 8
HEAD_PAD = LANES
MIB = 1024 * 1024


def _cparams(sem, vmem_mib):
    return pltpu.CompilerParams(dimension_semantics=sem, vmem_limit_bytes=vmem_mib * MIB)


def _dot(a, b):
    return jnp.dot(a, b, preferred_element_type=F32)


def _dot_t(a, bt):
    return lax.dot_general(a, bt, (((1,), (1,)), ((), ())), preferred_element_type=F32)


def _mod_spec(chunk, d, tm, seq_len, per_batch):
    def imap(i, *_):
        row = (i * tm) // seq_len if per_batch else 0
        return (row * 6 + chunk, 0, 0)
    return pl.BlockSpec((1, 1, d), imap)


def _ada_kernel(c_ref, w_ref, b_ref, o_ref):
    cv = c_ref[...]
    s = cv * jax.nn.sigmoid(cv)
    o_ref[0] = _dot(s.astype(BF16), w_ref[0].astype(BF16)) + b_ref[0]


def _ada_call(cvec, w_ada, b_ada):
    depth, d, n = w_ada.shape
    rows = cvec.shape[0]
    tn = 1024
    return pl.pallas_call(
        _ada_kernel,
        out_shape=jax.ShapeDtypeStruct((depth, rows, n), F32),
        grid=(depth, n // tn),
        in_specs=[
            pl.BlockSpec((rows, d), lambda l, j: (0, 0)),
            pl.BlockSpec((1, d, tn), lambda l, j: (l, 0, j)),
            pl.BlockSpec((1, 1, tn), lambda l, j: (l, 0, j)),
        ],
        out_specs=pl.BlockSpec((1, rows, tn), lambda l, j: (l, 0, j)),
        compiler_params=_cparams(("parallel", "parallel"), 40),
        name="ada_mod",
    )(cvec, w_ada, b_ada.reshape(depth, 1, n))


def _modnorm(x, g, scale, shift):
    ms = jnp.mean(x * x, axis=-1, keepdims=True)
    return (x * lax.rsqrt(ms + EPS) * g) * (1.0 + scale) + shift


def _rms(x, g, width):
    ms = jnp.sum(x * x, axis=-1, keepdims=True) * (1.0 / width)
    return x * lax.rsqrt(ms + EPS) * g


def _heads_norm(x, g, mult, shared=None, rope=None):
    t_main = g * mult
    if rope is not None:
        xs, shared_s, gs, cos, sin = rope
        t_main = cos * t_main
        t_part = sin * (gs * mult)
    outs = []
    for h in range(MLA_HEADS):
        sl = slice(h * HEAD_PAD, (h + 1) * HEAD_PAD)
        xh = x[:, sl] if shared is None else x[:, sl] + shared
        ms = jnp.sum(xh * xh, axis=-1, keepdims=True) * (1.0 / QK_HEAD)
        y = xh * t_main
        if rope is not None:
            y = y + (shared_s if xs is None else xs[:, sl]) * t_part
        outs.append(y * lax.rsqrt(ms + EPS))
    return jnp.concatenate(outs, axis=1)


def _inproj_kernel(*refs, use_rope, q_mult, cols, n_keep):
    (x_ref, sh_ref, sc_ref, g_ref, wmix_ref, watt_ref, gqa_ref, gkva_ref, gq_ref, gk_ref,
     wuq_ref, wuk_ref, wuv_ref) = refs[:13]
    if use_rope:
        gqs_ref, gks_ref, wuqs_ref, c_ref, s_ref = refs[13:18]
    nu = (cols["pool_u"] - cols["ssm_u"]) // LANES
    p_out, h_out, q_out, k_out, v_out, ckv_out = refs[-6 - nu:-nu]
    u_outs = refs[-nu:]
    att0 = cols["k_rope"]
    tm = x_ref.shape[0]
    nsub = 2

    for sub in range(nsub):
        rs = slice(sub * tm // nsub, (sub + 1) * tm // nsub)
        h = _modnorm(x_ref[rs, :], g_ref[...], sc_ref[0], sh_ref[0]).astype(BF16)
        h_out[rs, :] = h
        res = _dot_t(h, watt_ref[...])
        mix = _dot_t(h, wmix_ref[...])
        p_out[rs, :att0] = mix
        p_out[rs, att0:] = res[:, :n_keep - att0]
        for blk, u_out in enumerate(u_outs):
            u_out[rs, :] = mix[:, cols["ssm_u"] + blk * LANES:cols["ssm_u"] + (blk + 1) * LANES]

        def col(name, width, res=res):
            return res[:, cols[name] - att0:cols[name] - att0 + width]

        rope_tabs = (c_ref[rs, :], s_ref[rs, :]) if use_rope else None
        qa = col("q_a", wuq_ref.shape[0])
        qa_n = _rms(qa, gqa_ref[...], qa.shape[-1]).astype(BF16)
        q = _dot(qa_n, wuq_ref[...])
        q_rope = (_dot(qa_n, wuqs_ref[...]), None, gqs_ref[...]) + rope_tabs if use_rope else None
        q_out[rs, :] = _heads_norm(q, gq_ref[...], q_mult, rope=q_rope).astype(BF16)

        kva = col("kv_a", wuk_ref.shape[0])
        ckv = _rms(kva, gkva_ref[...], kva.shape[-1])
        ckv_out[rs, :] = ckv
        ckv_b = ckv.astype(BF16)
        k = _dot(ckv_b, wuk_ref[...])
        k_rope = (None, col("k_rope_sw", HEAD_PAD), gks_ref[...]) + rope_tabs if use_rope else None
        k_out[rs, :] = _heads_norm(k, gk_ref[...], 1.0, shared=col("k_rope", HEAD_PAD), rope=k_rope).astype(BF16)
        v_out[rs, :] = _dot(ckv_b, wuv_ref[...]).astype(BF16)


def _inproj_call(x, mod, lw, rope, seq_len, per_batch, cols):
    t, d = x.shape
    att0 = cols["k_rope"]
    n_att = lw["w_att"].shape[1]
    n_keep = att0 + HEAD_PAD
    tm = 512
    hq = MLA_HEADS * HEAD_PAD
    hv = MLA_HEADS * V_HEAD
    qlora = lw["w_uq"].shape[0]
    kvlora = lw["w_uk"].shape[0]
    l = lw["l"]
    mspec = functools.partial(_mod_spec, d=d, tm=tm, seq_len=seq_len, per_batch=per_batch)
    const = lambda i: (0, 0)
    row = lambda i: (i, 0)
    in_specs = [
        pl.BlockSpec((tm, d), row), mspec(0), mspec(1),
        pl.BlockSpec((1, d), const),
        pl.BlockSpec((pl.Element(att0), pl.Element(d)), lambda i: (l * lw["w_rows"] + lw["tail_row0"], 0),
                     pipeline_mode=pl.Buffered(1)),
        pl.BlockSpec((None, n_att, d), lambda i: (l, 0, 0), pipeline_mode=pl.Buffered(1)),
        pl.BlockSpec((1, qlora), const), pl.BlockSpec((1, kvlora), const),
        pl.BlockSpec((1, HEAD_PAD), const), pl.BlockSpec((1, HEAD_PAD), const),
        pl.BlockSpec((qlora, hq), const), pl.BlockSpec((kvlora, hq), const), pl.BlockSpec((kvlora, hv), const),
    ]
    args = [x, mod, mod, lw["norm1_g"], lw["w_tail"], lw["w_att"], lw["q_a_norm_g"], lw["kv_a_norm_g"], lw["q_norm_g"],
            lw["k_norm_g"], lw["w_uq"], lw["w_uk"], lw["w_uv"]]
    if rope is not None:
        nblk = seq_len // tm
        in_specs += [
            pl.BlockSpec((1, HEAD_PAD), const), pl.BlockSpec((1, HEAD_PAD), const),
            pl.BlockSpec((qlora, hq), const),
            pl.BlockSpec((tm, HEAD_PAD), lambda i: (i % nblk, 0)),
            pl.BlockSpec((tm, HEAD_PAD), lambda i: (i % nblk, 0)),
        ]
        args += [lw["q_norm_g_sw"], lw["k_norm_g_sw"], lw["w_uq_sw"]] + list(rope)
    q_mult = LOG2E * QK_HEAD ** -0.5
    nu = (cols["pool_u"] - cols["ssm_u"]) // LANES
    return pl.pallas_call(
        functools.partial(_inproj_kernel, use_rope=rope is not None, q_mult=q_mult, cols=cols, n_keep=n_keep),
        out_shape=(jax.ShapeDtypeStruct((t, n_keep), F32), jax.ShapeDtypeStruct((t, d), BF16),
                   jax.ShapeDtypeStruct((t, hq), BF16), jax.ShapeDtypeStruct((t, hq), BF16),
                   jax.ShapeDtypeStruct((t, hv), BF16), jax.ShapeDtypeStruct((t, kvlora), F32))
        + (jax.ShapeDtypeStruct((t, LANES), F32),) * nu,
        grid=(t // tm,),
        in_specs=in_specs,
        out_specs=(pl.BlockSpec((tm, n_keep), row), pl.BlockSpec((tm, d), row),
                   pl.BlockSpec((tm, hq), row), pl.BlockSpec((tm, hq), row),
                   pl.BlockSpec((tm, hv), row), pl.BlockSpec((tm, kvlora), row))
        + (pl.BlockSpec((tm, LANES), row),) * nu,
        compiler_params=_cparams(("parallel",), 56),
        name="in_proj",
    )(*args)


def _ctxprep_kernel(ckv_ref, kpe_ref, gk_ref, wuk_ref, wuv_ref, k_out, v_out):
    ckv_b = ckv_ref[...].astype(BF16)
    k = _dot(ckv_b, wuk_ref[...])
    k_out[...] = _heads_norm(k, gk_ref[...], 1.0, shared=kpe_ref[...]).astype(BF16)
    v_out[...] = _dot(ckv_b, wuv_ref[...]).astype(BF16)


def _ctxprep_call(ckv, kpe, lw):
    t, kvlora = ckv.shape
    tm = 512
    hq = MLA_HEADS * HEAD_PAD
    hv = MLA_HEADS * V_HEAD
    const = lambda i: (0, 0)
    return pl.pallas_call(
        _ctxprep_kernel,
        out_shape=(jax.ShapeDtypeStruct((t, hq), BF16), jax.ShapeDtypeStruct((t, hv), BF16)),
        grid=(t // tm,),
        in_specs=[
            pl.BlockSpec((tm, kvlora), lambda i: (i, 0)),
            pl.BlockSpec((tm, HEAD_PAD), lambda i: (i, 0)),
            pl.BlockSpec((1, HEAD_PAD), const),
            pl.BlockSpec((kvlora, hq), const),
            pl.BlockSpec((kvlora, hv), const),
        ],
        out_specs=(pl.BlockSpec((tm, hq), lambda i: (i, 0)), pl.BlockSpec((tm, hv), lambda i: (i, 0))),
        compiler_params=_cparams(("parallel",), 40),
        name="ctx_prep",
    )(ckv, kpe, lw["k_norm_g"], lw["w_uk"], lw["w_uv"])


def _attn_kernel(*refs, has_ctx, heads):
    if has_ctx:
        q_ref, ko_ref, vo_ref, kc_ref, vc_ref, o_ref = refs
    else:
        q_ref, ko_ref, vo_ref, o_ref = refs
    contract_last = (((1,), (1,)), ((), ()))
    lane = lax.broadcasted_iota(jnp.int32, (q_ref.shape[0], 2 * V_HEAD), 1)
    for pair in range(heads // 2):
        vsl = slice(pair * 2 * V_HEAD, (pair + 1) * 2 * V_HEAD)
        outs = []
        for hh in range(2):
            h = 2 * pair + hh
            sl = slice(h * HEAD_PAD, (h + 1) * HEAD_PAD)
            q = q_ref[:, sl]
            s_o = lax.dot_general(q, ko_ref[:, sl], contract_last, preferred_element_type=F32)
            m = jnp.max(s_o, axis=-1, keepdims=True)
            if has_ctx:
                s_c = lax.dot_general(q, kc_ref[:, sl], contract_last, preferred_element_type=F32)
                m = jnp.maximum(m, jnp.max(s_c, axis=-1, keepdims=True))
            p_o = jnp.exp2(s_o - m)
            l = jnp.sum(p_o, axis=-1, keepdims=True)
            acc = _dot(p_o.astype(BF16), vo_ref[:, vsl])
            if has_ctx:
                p_c = jnp.exp2(s_c - m)
                l = l + jnp.sum(p_c, axis=-1, keepdims=True)
                acc = acc + _dot(p_c.astype(BF16), vc_ref[:, vsl])
            outs.append(acc / l)
        o_ref[:, vsl] = jnp.where(lane < V_HEAD, outs[0], outs[1]).astype(BF16)


def _attn_call(q, k, v, kc, vc, batch, seq_len, ctx_len):
    t = q.shape[0]
    tq = min(seq_len, 512)
    heads = 8
    nq = seq_len // tq
    npair = MLA_HEADS // heads
    qw = heads * HEAD_PAD
    vw = heads * V_HEAD
    in_specs = [
        pl.BlockSpec((tq, qw), lambda b, h, i: (b * nq + i, h)),
        pl.BlockSpec((seq_len, qw), lambda b, h, i: (b, h)),
        pl.BlockSpec((seq_len, vw), lambda b, h, i: (b, h)),
    ]
    args = [q, k, v]
    if kc is not None:
        in_specs += [pl.BlockSpec((ctx_len, qw), lambda b, h, i: (b, h)),
                     pl.BlockSpec((ctx_len, vw), lambda b, h, i: (b, h))]
        args += [kc, vc]
    return pl.pallas_call(
        functools.partial(_attn_kernel, has_ctx=kc is not None, heads=heads),
        out_shape=jax.ShapeDtypeStruct((t, MLA_HEADS * V_HEAD), BF16),
        grid=(batch, npair, nq),
        in_specs=in_specs,
        out_specs=pl.BlockSpec((tq, vw), lambda b, h, i: (b * nq + i, h)),
        compiler_params=_cparams(("parallel", "parallel", "arbitrary"), 56),
        name="attention",
    )(*args)


def _convpool_kernel(cu_ref, bg_ref, cg_ref, pu_ref, cw_ref, cb_ref, pw_ref, ps_ref, zb_ref, zd_ref):
    n = cu_ref.shape[0]
    gps = pw_ref.shape[0]
    edge = max(POOL_WINDOWS) // 2
    assert edge % SUBLANES == 0 and n >= 4 * edge

    def shifts(rows, zero_fill):
        def down(x, k):
            y = pltpu.roll(x, k, axis=0)
            if zero_fill:
                y = jnp.where(lax.broadcasted_iota(jnp.int32, x.shape, 0) >= k, y, 0.0)
            return y

        def up(x, k):
            y = pltpu.roll(x, rows - k, axis=0)
            if zero_fill:
                y = jnp.where(lax.broadcasted_iota(jnp.int32, x.shape, 0) < rows - k, y, 0.0)
            return y
        return down, up

    def conv_rows(r0, rows, zero_fill):
        down, up = shifts(rows, zero_fill)
        v = cg_ref[r0:r0 + rows, :] * cu_ref[r0:r0 + rows, :]
        conv = (down(v, 1) * cw_ref[0:1, :] + v * cw_ref[1:2, :] + up(v, 1) * cw_ref[2:3, :]
                + cb_ref[...])
        return (bg_ref[r0:r0 + rows, :] * conv).astype(BF16)

    def pool_rows(r0, rows, zero_fill, gg, half):
        down, up = shifts(rows, zero_fill)
        sl = slice(gg * LANES, (gg + 1) * LANES)
        u = pu_ref[r0:r0 + rows, sl]
        fwd = u
        bwd = u
        k = 1
        while k < half:
            fwd = fwd + up(fwd, k)
            bwd = bwd + down(bwd, k)
            k *= 2
        total = fwd + down(bwd, 1)
        rowf = (lax.broadcasted_iota(jnp.int32, u.shape, 0) + r0).astype(F32)
        cnt = jnp.minimum(rowf + half, float(n)) - jnp.maximum(rowf - half, 0.0)
        mean = total / cnt - u
        return (_dot(mean.astype(BF16), pw_ref[gg]) * ps_ref[:, sl]).astype(BF16)

    zb_ref[...] = conv_rows(0, n, False)
    zb_ref[0:edge, :] = conv_rows(0, 2 * edge, True)[0:edge]
    zb_ref[n - edge:n, :] = conv_rows(n - 2 * edge, 2 * edge, True)[edge:]

    step = pl.program_id(1)
    for first in range(0, len(POOL_WINDOWS), gps):
        @pl.when(step * gps == first)
        def _(first=first):
            for gg in range(gps):
                sl = slice(gg * LANES, (gg + 1) * LANES)
                half = POOL_WINDOWS[first + gg] // 2
                zd_ref[:, sl] = pool_rows(0, n, False, gg, half)
                zd_ref[0:edge, sl] = pool_rows(0, 2 * edge, True, gg, half)[0:edge]
                zd_ref[n - edge:n, sl] = pool_rows(n - 2 * edge, 2 * edge, True, gg, half)[edge:]


def _convpool_call(p, lw, batch, seq_len, cols):
    t = p.shape[0]
    gps = 2 if seq_len > 512 else len(POOL_WINDOWS)
    cw = gps * LANES
    nblk = lw["conv_w"].shape[1] // cw

    def pspec(name):
        base = cols[name] // cw
        return pl.BlockSpec((seq_len, cw), lambda b, g: (b, base + g))

    vec = lambda b, g: (0, g)
    return pl.pallas_call(
        _convpool_kernel,
        out_shape=(jax.ShapeDtypeStruct((t, nblk * cw), BF16), jax.ShapeDtypeStruct((t, nblk * cw), BF16)),
        grid=(batch, nblk),
        in_specs=[pspec("conv_u"), pspec("conv_bg"), pspec("conv_cg"), pspec("pool_u"),
                  pl.BlockSpec((3, cw), vec), pl.BlockSpec((1, cw), vec),
                  pl.BlockSpec((gps, LANES, LANES), lambda b, g: (g, 0, 0)), pl.BlockSpec((1, cw), vec)],
        out_specs=(pl.BlockSpec((seq_len, cw), lambda b, g: (b, g)),
                   pl.BlockSpec((seq_len, cw), lambda b, g: (b, g))),
        compiler_params=_cparams(("parallel", "parallel"), 40),
        name="conv_pool",
    )(p, p, p, p, lw["conv_w"], lw["conv_b"], lw["pool_w"], lw["pool_scale"])


SSM_R = 4
SSM_GB = LANES // SSM_GROUP_CH


def _ssm_kernel(*refs, steps, nchunks, nblk):
    u_refs = refs[:nblk]
    h0_ref, a_ref, mb_ref, m1_ref, m2_ref = refs[nblk:nblk + 5]
    y_refs = refs[nblk + 5:2 * nblk + 5]
    hf_ref, slab_scr, xs_scr, st_scr = refs[2 * nblk + 5:]
    d = pl.program_id(0)
    c = pl.program_id(2)
    width = st_scr.shape[1]
    bw = width // nblk

    @pl.when(c == 0)
    def _():
        st_scr[...] = h0_ref[0]

    for blk in range(nblk):
        for j in range(SSM_R):
            for b in range(SUBLANES):
                slab_scr[blk * SSM_R + j, pl.ds(b, steps, stride=SUBLANES), :] = (
                    u_refs[blk][b, pl.ds(j, steps, stride=SSM_R), :])

    def packed(blk):
        return jnp.concatenate([slab_scr[blk * SSM_R + j] for j in range(SSM_R)], axis=1).astype(BF16)

    for blk in range(nblk):
        xs_scr[:, blk * bw:(blk + 1) * bw] = _dot(packed(blk), mb_ref[0, blk])

    gw = width // 2
    for part in range(width // gw):
        offs = []
        for blk in range(part * nblk // 2, (part + 1) * nblk // 2):
            offs += [(blk * bw + k * LANES, blk * bw + bw // 2 + k * LANES) for k in range(bw // 2 // LANES)]
        coef = [(a_ref[0, :, r0:r0 + LANES], a_ref[0, :, i0:i0 + LANES]) for r0, i0 in offs]

        def body(t, carry, offs=offs, coef=coef):
            tt = t + d * (steps - 1 - 2 * t)
            row0 = pl.multiple_of(tt * SUBLANES, SUBLANES)
            new = []
            for (r0, i0), (a_re, a_im), x_re, x_im in zip(offs, coef, carry[0::2], carry[1::2]):
                n_re = (a_re * x_re - a_im * x_im) + xs_scr[pl.ds(row0, SUBLANES), r0:r0 + LANES]
                n_im = (a_re * x_im + a_im * x_re) + xs_scr[pl.ds(row0, SUBLANES), i0:i0 + LANES]
                xs_scr[pl.ds(row0, SUBLANES), r0:r0 + LANES] = x_re
                xs_scr[pl.ds(row0, SUBLANES), i0:i0 + LANES] = x_im
                new += [n_re, n_im]
            return tuple(new)

        init = []
        for r0, i0 in offs:
            init += [st_scr[:, r0:r0 + LANES], st_scr[:, i0:i0 + LANES]]
        final = lax.fori_loop(0, steps, body, tuple(init), unroll=2)
        for (r0, i0), x_re, x_im in zip(offs, final[0::2], final[1::2]):
            st_scr[:, r0:r0 + LANES] = x_re
            st_scr[:, i0:i0 + LANES] = x_im

    for blk in range(nblk):
        y = (_dot(xs_scr[:, blk * bw:(blk + 1) * bw].astype(BF16), m1_ref[0, blk])
             + _dot(packed(blk), m2_ref[0, blk]))
        for j in range(SSM_R):
            slab_scr[blk * SSM_R + j] = y[:, j * LANES:(j + 1) * LANES]
        for j in range(SSM_R):
            for b in range(SUBLANES):
                y_refs[blk][0, b, pl.ds(j, steps, stride=SSM_R), :] = (
                    slab_scr[blk * SSM_R + j, pl.ds(b, steps, stride=SUBLANES), :])

    @pl.when(c == nchunks - 1)
    def _():
        hf_ref[0] = st_scr[...]


def _ssm_call(us, h0, lw):
    batch, seq_len, _ = us[0].shape
    width = h0.shape[-1]
    nblk = lw["ssm_mb"].shape[2]
    assert len(us) == nblk
    l = lw["l"]
    steps = min(seq_len // SSM_R, 64)
    tokens = steps * SSM_R
    nchunks = seq_len // tokens
    nbg = batch // SUBLANES
    rows = steps * SUBLANES

    def cidx(d, c):
        return c + d * (nchunks - 1 - 2 * c)

    uspec = pl.BlockSpec((SUBLANES, tokens, LANES), lambda d, b, c: (b, cidx(d, c), 0))

    def wspec(arr):
        return pl.BlockSpec((None, 1) + arr.shape[2:], lambda d, b, c: (l, d, 0, 0, 0))

    yshape = jax.ShapeDtypeStruct((2, batch, seq_len, LANES), F32)
    yspec = pl.BlockSpec((1, SUBLANES, tokens, LANES), lambda d, b, c: (d, b, cidx(d, c), 0))
    outs = pl.pallas_call(
        functools.partial(_ssm_kernel, steps=steps, nchunks=nchunks, nblk=nblk),
        out_shape=(yshape,) * nblk + (jax.ShapeDtypeStruct((2, batch, width), F32),),
        grid=(2, nbg, nchunks),
        in_specs=[uspec] * nblk + [
            pl.BlockSpec((1, SUBLANES, width), lambda d, b, c: (d, b, 0)),
            pl.BlockSpec((None, 1, SUBLANES, width), lambda d, b, c: (l, d, 0, 0)),
            wspec(lw["ssm_mb"]), wspec(lw["ssm_m1"]), wspec(lw["ssm_m2"]),
        ],
        out_specs=(yspec,) * nblk + (pl.BlockSpec((1, SUBLANES, width), lambda d, b, c: (d, b, 0)),),
        scratch_shapes=[pltpu.VMEM((nblk * SSM_R, rows, LANES), F32), pltpu.VMEM((rows, width), F32),
                        pltpu.VMEM((SUBLANES, width), F32)],
        compiler_params=_cparams(("arbitrary", "arbitrary", "arbitrary"), 52),
        name="s5_scan",
    )(*us, h0, lw["ssm_a"], lw["ssm_mb"], lw["ssm_m1"], lw["ssm_m2"])
    return outs[:nblk], outs[nblk]


def _merge_kernel(*refs, nblk, ncast):
    (h_ref, za_ref, zb_ref, su_ref, sd_ref, zd_ref, wg0, wg1, wg2, wg3, wa, wb, wca, wcg, wd) = refs[:15]
    yf_refs = refs[15:15 + nblk]
    yb_refs = refs[15 + nblk:15 + 2 * nblk]
    cast_in = refs[15 + 2 * nblk:15 + 2 * nblk + ncast]
    o_ref = refs[15 + 2 * nblk + ncast]
    cast_out = refs[16 + 2 * nblk + ncast:16 + 2 * nblk + 2 * ncast]
    y_scr = refs[-1]

    for src, dst in zip(cast_in, cast_out):
        dst[...] = src[...].astype(BF16)

    @pl.when(pl.program_id(1) == 0)
    def _():
        y_both = jnp.concatenate([f[0] + b[0] for f, b in zip(yf_refs, yb_refs)], axis=1)
        y_scr[...] = (y_both + sd_ref[...] * su_ref[...]).astype(BF16)

    tm = h_ref.shape[0]
    nsub = 2
    for sub in range(nsub):
        rs = slice(sub * tm // nsub, (sub + 1) * tm // nsub)
        h = h_ref[rs, :]
        y = y_scr[rs, :]

        def gate(w_ref, h=h):
            return jax.nn.sigmoid(_dot_t(h, w_ref[...]))

        merged = gate(wg0) * _dot(za_ref[rs, :], wa[...])
        merged = merged + gate(wg1) * _dot(zb_ref[rs, :], wb[...])
        merged = merged + gate(wg2) * (_dot(y, wca[...]) * jax.nn.sigmoid(_dot(y, wcg[...])))
        merged = merged + gate(wg3) * _dot(zd_ref[rs, :], wd[...])
        o_ref[rs, :] = merged.astype(BF16)


def _merge_call(h, p, za, zb, ys, zd, lw, cols, cast=()):
    t, d = h.shape
    bw = za.shape[1]
    tm, tn = 512, 512
    nj = d // tn
    nblk = len(ys)
    row = lambda i, j: (i, 0)
    su_blk = cols["ssm_u"] // bw

    l = lw["l"]

    gate_blk0 = cols["k_rope"] // tn

    row0 = l * lw["w_rows"] + lw["tail_row0"]

    def gspec(k):
        assert (row0 + gate_blk0 * tn) % 16 == 0
        return pl.BlockSpec((pl.Element(tn), pl.Element(d)),
                            lambda i, j: (pl.multiple_of(row0 + (gate_blk0 + k * nj + j) * tn, 16), 0))

    def yspec(direction):
        return pl.BlockSpec((1, tm, LANES), lambda i, j: (direction, i, 0))

    wcol = pl.BlockSpec((None, bw, tn), lambda i, j: (l, 0, j))
    in_specs = [
        pl.BlockSpec((tm, d), row),
        pl.BlockSpec((tm, bw), row), pl.BlockSpec((tm, bw), row),
        pl.BlockSpec((tm, bw), lambda i, j: (i, su_blk)),
        pl.BlockSpec((1, bw), lambda i, j: (0, 0)),
        pl.BlockSpec((tm, bw), row),
        gspec(0), gspec(1), gspec(2), gspec(3),
        wcol, wcol, wcol, pl.BlockSpec((None, bw, tn), lambda i, j: (l, 0, nj + j)), wcol,
    ] + [yspec(0)] * nblk + [yspec(1)] * nblk
    out_shape = [jax.ShapeDtypeStruct((t, d), BF16)]
    out_specs = [pl.BlockSpec((tm, tn), lambda i, j: (i, j))]
    nsteps = (t // tm) * nj
    for w in cast:
        rows, n = w.shape[1] // nsteps, w.shape[2]
        assert rows * nsteps == w.shape[1] and rows % (2 * SUBLANES) == 0
        in_specs.append(pl.BlockSpec((None, rows, n), lambda i, j: (l, i * nj + j, 0)))
        out_shape.append(jax.ShapeDtypeStruct(w.shape[1:], BF16))
        out_specs.append(pl.BlockSpec((rows, n), lambda i, j: (i * nj + j, 0)))
    wg = lw["w_tail"]
    outs = pl.pallas_call(
        functools.partial(_merge_kernel, nblk=nblk, ncast=len(cast)),
        out_shape=tuple(out_shape),
        grid=(t // tm, nj),
        in_specs=in_specs,
        out_specs=tuple(out_specs),
        scratch_shapes=[pltpu.VMEM((tm, bw), BF16)],
        compiler_params=_cparams(("parallel", "arbitrary"), 48),
        name="branch_merge",
    )(h, za, zb, p, lw["ssm_d"], zd,
      wg, wg, wg, wg, lw["w_mla_o"], lw["w_conv_o"], lw["w_glu"], lw["w_glu"], lw["w_pool_o"], *ys, *ys, *cast)
    return outs[0], tuple(outs[1:])


def _outproj_kernel(x_ref, m_ref, gt_ref, sh_ref, sc_ref, g_ref, wo_ref, x1_ref, h2_ref):
    tm = x_ref.shape[0]
    nsub = 2
    for sub in range(nsub):
        rs = slice(sub * tm // nsub, (sub + 1) * tm // nsub)
        x1 = x_ref[rs, :] + gt_ref[0] * _dot(m_ref[rs, :], wo_ref[...])
        x1_ref[rs, :] = x1
        h2_ref[rs, :] = _modnorm(x1, g_ref[...], sc_ref[0], sh_ref[0]).astype(BF16)


def _outproj_call(x, merged, mod, lw, seq_len, per_batch):
    t, d = x.shape
    tm = 512
    mspec = functools.partial(_mod_spec, d=d, tm=tm, seq_len=seq_len, per_batch=per_batch)
    row = lambda i: (i, 0)
    return pl.pallas_call(
        _outproj_kernel,
        out_shape=(jax.ShapeDtypeStruct((t, d), F32), jax.ShapeDtypeStruct((t, d), BF16)),
        grid=(t // tm,),
        in_specs=[
            pl.BlockSpec((tm, d), row), pl.BlockSpec((tm, d), row),
            mspec(2), mspec(3), mspec(4),
            pl.BlockSpec((1, d), lambda i: (0, 0)),
            pl.BlockSpec((None, d, d), lambda i: (lw["l"], 0, 0), pipeline_mode=pl.Buffered(1)),
        ],
        out_specs=(pl.BlockSpec((tm, d), row), pl.BlockSpec((tm, d), row)),
        compiler_params=_cparams(("parallel",), 48),
        name="out_proj",
    )(x, merged, mod, mod, mod, lw["norm2_g"], lw["w_o"])


def _mlp_kernel(x_ref, h_ref, gt_ref, w1_ref, w2_ref, o_ref):
    @pl.when(pl.program_id(1) == 0)
    def _():
        o_ref[...] = x_ref[...]

    hid = jnp.square(jnp.maximum(_dot(h_ref[...], w1_ref[...]), 0.0))
    o_ref[...] += gt_ref[0] * _dot(hid.astype(BF16), w2_ref[...])


def _mlp_call(x, h, mod, lw, seq_len, per_batch):
    t, d = x.shape
    hidden = lw["w_mlp1"].shape[2]
    tm, tk = 512, 1024
    l = lw["l_mlp"]
    mspec = functools.partial(_mod_spec, d=d, tm=tm, seq_len=seq_len, per_batch=per_batch)
    row = lambda i, k: (i, 0)
    return pl.pallas_call(
        _mlp_kernel,
        out_shape=jax.ShapeDtypeStruct((t, d), F32),
        grid=(t // tm, hidden // tk),
        in_specs=[
            pl.BlockSpec((tm, d), row), pl.BlockSpec((tm, d), row), mspec(5),
            pl.BlockSpec((None, d, tk), lambda i, k: (l, 0, k)),
            pl.BlockSpec((None, tk, d), lambda i, k: (l, k, 0)),
        ],
        out_specs=pl.BlockSpec((tm, d), row),
        compiler_params=_cparams(("parallel", "arbitrary"), 48),
        name="mlp",
    )(x, h, mod, lw["w_mlp1"], lw["w_mlp2"])


def _pad_heads(w, per_head, lo):
    lead = w.shape[:-1]
    w = w.reshape(lead + (MLA_HEADS, per_head))
    w = jnp.pad(w, [(0, 0)] * len(lead) + [(0, 0), (lo, HEAD_PAD - lo - per_head)])
    return w.reshape(lead + (MLA_HEADS * HEAD_PAD,))


_ROPE_PARTNER = tuple((i // 16) * 16 + (i % 16 + 8) % 16 for i in range(ROPE_DIM))


def _swap_heads(w):
    lead = w.shape[:-1]
    w = w.reshape(lead + (MLA_HEADS, QK_HEAD))[..., QK_NOPE:][..., jnp.array(_ROPE_PARTNER)]
    w = jnp.pad(w, [(0, 0)] * len(lead) + [(0, 0), (QK_NOPE, HEAD_PAD - QK_HEAD)])
    return w.reshape(lead + (MLA_HEADS * HEAD_PAD,))


def _expand_kernel(t_ref, o_ref, *, axis, gb, cg):
    t = t_ref[...]
    big = jnp.concatenate([t] * gb, axis=axis)
    rows = lax.broadcasted_iota(jnp.int32, big.shape, 0)
    lanes = lax.broadcasted_iota(jnp.int32, big.shape, 1)
    if axis == 1:
        keep = (rows // cg) % gb == lanes // t.shape[1]
    else:
        keep = rows // t.shape[0] == (lanes // cg) % gb
    o_ref[...] = jnp.where(keep, big, 0.0).astype(BF16)


def _expand_call(parts, axis, gb, cg):
    _, ns, nblk, r, c = parts.shape
    side = gb * (c if axis == 1 else r)
    out_rc = (r, 2 * side) if axis == 1 else (2 * side, c)
    blk_rc = (r, side) if axis == 1 else (side, c)
    omap = (lambda p, s, b: (s, b, 0, p)) if axis == 1 else (lambda p, s, b: (s, b, p, 0))
    return pl.pallas_call(
        functools.partial(_expand_kernel, axis=axis, gb=gb, cg=cg),
        out_shape=jax.ShapeDtypeStruct((ns, nblk) + out_rc, BF16),
        grid=(2, ns, nblk),
        in_specs=[pl.BlockSpec((None, None, None, r, c), lambda p, s, b: (p, s, b, 0, 0))],
        out_specs=pl.BlockSpec((None, None) + blk_rc, omap),
        compiler_params=_cparams(("parallel", "parallel", "parallel"), 32),
        name="s5_expand",
    )(parts)


def _prep_ssm(lam_re, lam_im, log_step, b_re, b_im, c_re, c_im):
    r, gb = SSM_R, SSM_GB
    nl, ndir, g, n, cg = b_re.shape
    ns = nl * ndir
    nblk = g // gb
    flat = lambda v: v.reshape((ns,) + v.shape[2:])
    lam_re, lam_im, log_step, b_re, b_im, c_re, c_im = map(
        flat, (lam_re, lam_im, log_step, b_re, b_im, c_re, c_im))
    step = jnp.exp(log_step)[..., None]
    pw = []
    for p in range(r + 1):
        mag = jnp.exp(p * (lam_re * step))
        pw.append((mag * jnp.cos(p * (lam_im * step)), mag * jnp.sin(p * (lam_im * step))))
    ar, ai = pw[1]
    den = lam_re * lam_re + lam_im * lam_im
    qr = ((ar - 1.0) * lam_re + ai * lam_im) / den
    qi = (ai * lam_re - (ar - 1.0) * lam_im) / den
    bb_re = qr[..., None] * b_re - qi[..., None] * b_im
    bb_im = qr[..., None] * b_im + qi[..., None] * b_re
    fwd = (jnp.arange(ns) % ndir == 0)[:, None, None]

    def power(p_fwd, p_bwd):
        return (jnp.where(fwd, pw[p_fwd][0], pw[p_bwd][0]), jnp.where(fwd, pw[p_fwd][1], pw[p_bwd][1]))


    t_re, t_im = [], []
    for j in range(r):
        pr, pi = power(r - 1 - j, j)
        t_re.append(pr[..., None] * bb_re - pi[..., None] * bb_im)
        t_im.append(pr[..., None] * bb_im + pi[..., None] * bb_re)
    def inc_block(ts):
        t = jnp.stack(ts, axis=1).reshape(ns, r, nblk, gb, n, cg)
        return jnp.transpose(t, (0, 2, 1, 3, 5, 4)).reshape(ns, nblk, r * gb * cg, n)

    mb = _expand_call(jnp.stack([inc_block(t_re), inc_block(t_im)]), 1, gb, cg)

    cl_re, cl_im = [], []
    for j in range(r):
        pr, pi = power(j + 1, r - j)
        cl_re.append(c_re * pr[:, :, None, :] - c_im * pi[:, :, None, :])
        cl_im.append(c_re * pi[:, :, None, :] + c_im * pr[:, :, None, :])

    def out_block(ts):
        t = jnp.stack(ts, axis=1).reshape(ns, r, nblk, gb, cg, n)
        return jnp.transpose(t, (0, 2, 5, 1, 3, 4)).reshape(ns, nblk, n, r * gb * cg)

    m1 = _expand_call(jnp.stack([out_block(cl_re), -out_block(cl_im)]), 0, gb, cg)

    hi = lax.Precision.HIGHEST
    lane_g = jnp.arange(gb * cg) // cg
    mask_k = (lane_g[:, None] == lane_g[None, :]).astype(F32)
    kd = []
    for p in range(r):
        pr, pi = pw[p]
        k2 = (jnp.einsum("sgcn,sgnk->sgck", c_re * pr[:, :, None, :] - c_im * pi[:, :, None, :], bb_re, precision=hi)
              - jnp.einsum("sgcn,sgnk->sgck", c_re * pi[:, :, None, :] + c_im * pr[:, :, None, :], bb_im, precision=hi))
        k2 = jnp.swapaxes(k2, -1, -2).reshape(ns, nblk, gb * cg, cg)
        kd.append(jnp.tile(k2, (1, 1, 1, gb)) * mask_k)
    fwd4 = fwd[..., None]
    zero = jnp.zeros_like(kd[0])
    m2_rows = []
    for i in range(r):
        blocks = []
        for j in range(r):
            if i == j:
                blocks.append(kd[0])
            elif j > i:
                blocks.append(jnp.where(fwd4, kd[j - i], zero))
            else:
                blocks.append(jnp.where(fwd4, zero, kd[i - j]))
        m2_rows.append(jnp.concatenate(blocks, axis=-1))
    m2 = jnp.concatenate(m2_rows, axis=2)

    a = jnp.stack([pw[r][0].reshape(ns, nblk, gb * n), pw[r][1].reshape(ns, nblk, gb * n)], axis=2)
    a = jnp.broadcast_to(a.reshape(ns, 1, 2 * g * n), (ns, SUBLANES, 2 * g * n))
    unflat = lambda v: v.reshape((nl, ndir) + v.shape[1:])
    return unflat(a), unflat(mb), unflat(m1), unflat(m2.astype(BF16))


def _prep_stacked(prm, sizes):
    q_lora, kv_lora, conv_w, ssm_w, pool_w_ = sizes
    w_in_t = jnp.swapaxes(prm["w_in"], 1, 2)
    tail0 = q_lora + kv_lora + ROPE_DIM
    w_tail = w_in_t.astype(BF16).reshape(-1, w_in_t.shape[2])
    head = w_in_t[:, :tail0, :]
    cols = {}
    o = 0
    for name, width in (("conv_u", conv_w), ("conv_bg", conv_w), ("conv_cg", conv_w), ("ssm_u", ssm_w),
                        ("pool_u", pool_w_), ("k_rope", HEAD_PAD), ("k_rope_sw", HEAD_PAD), ("q_a", q_lora),
                        ("kv_a", kv_lora)):
        cols[name] = o
        o += width
    k_rope = head[:, q_lora + kv_lora:tail0, :]
    rope_lanes = ((0, 0), (QK_NOPE, HEAD_PAD - QK_HEAD), (0, 0))
    w_att = jnp.concatenate([jnp.pad(k_rope, rope_lanes),
                             jnp.pad(k_rope[:, jnp.array(_ROPE_PARTNER), :], rope_lanes),
                             head[:, :q_lora + kv_lora, :]], axis=1).astype(BF16)
    big = dict(w_tail=w_tail, w_att=w_att, tail_row0=tail0, w_rows=w_in_t.shape[1])
    for name in ("w_mla_o", "w_conv_o", "w_glu", "w_pool_o", "w_o"):
        big[name] = prm[name].astype(BF16)
    return big, cols


def _prep_layer(l, prm, sizes, big, ssm):
    kv_lora = sizes[1]
    w_ukv = prm["w_ukv"][l].reshape(kv_lora, MLA_HEADS, QK_NOPE + V_HEAD)
    w_uk = _pad_heads(w_ukv[:, :, :QK_NOPE].reshape(kv_lora, -1), QK_NOPE, 0)
    w_uv = w_ukv[:, :, QK_NOPE:].reshape(kv_lora, -1)
    ssm_a, ssm_mb, ssm_m1, ssm_m2 = ssm

    row = lambda v: v.reshape(1, -1)
    head_gain = lambda g: row(jnp.pad(g, (0, HEAD_PAD - QK_HEAD)))
    partner_gain = lambda g: row(jnp.pad(g[QK_NOPE:][jnp.array(_ROPE_PARTNER)], (QK_NOPE, HEAD_PAD - QK_HEAD)))
    lw = dict(
        big, l=l,
        norm1_g=row(prm["norm1_g"][l]), norm2_g=row(prm["norm2_g"][l]),
        q_a_norm_g=row(prm["q_a_norm_g"][l]), kv_a_norm_g=row(prm["kv_a_norm_g"][l]),
        q_norm_g=head_gain(prm["q_norm_g"][l]), k_norm_g=head_gain(prm["k_norm_g"][l]),
        q_norm_g_sw=partner_gain(prm["q_norm_g"][l]), k_norm_g_sw=partner_gain(prm["k_norm_g"][l]),
        w_uq=_pad_heads(prm["w_uq"][l], QK_HEAD, 0).astype(BF16),
        w_uq_sw=_swap_heads(prm["w_uq"][l]).astype(BF16),
        w_uk=w_uk.astype(BF16), w_uv=w_uv.astype(BF16),
        conv_w=prm["conv_w"][l], conv_b=row(prm["conv_b"][l]),
        ssm_a=ssm_a, ssm_mb=ssm_mb, ssm_m1=ssm_m1, ssm_m2=ssm_m2,
        ssm_d=row(prm["ssm_d"][l]),
        pool_w=prm["pool_w"][l].astype(BF16), pool_scale=row(prm["pool_scale"][l]),
    )
    return lw


def _rope_tables(seq_len):
    half = ROPE_DIM // 2
    inv_freq = ROPE_THETA ** (-jnp.arange(0, half, 2, dtype=F32) / half)
    t = jnp.arange(seq_len)
    ang_r = (t // GRID_W).astype(F32)[:, None] * inv_freq
    ang_c = (t % GRID_W).astype(F32)[:, None] * inv_freq
    ones_lo = jnp.ones((seq_len, QK_NOPE), F32)
    zeros_lo = jnp.zeros((seq_len, QK_NOPE), F32)
    tail = jnp.zeros((seq_len, HEAD_PAD - QK_HEAD), F32)
    c = jnp.concatenate([ones_lo, jnp.cos(ang_r), jnp.cos(ang_r), jnp.cos(ang_c), jnp.cos(ang_c), tail], axis=1)
    s = jnp.concatenate([zeros_lo, -jnp.sin(ang_r), jnp.sin(ang_r), -jnp.sin(ang_c), jnp.sin(ang_c), tail], axis=1)
    return c, s


def _states_to_cols(st):
    b, ndir, _, g, n = st.shape
    st = st.reshape(b, ndir, 2, g // SSM_GB, SSM_GB * n)
    return jnp.transpose(st, (1, 0, 3, 2, 4)).reshape(ndir, b, 2 * g * n)


def _cols_to_states(hf, g, n):
    ndir, b, _ = hf.shape
    hf = hf.reshape(ndir, b, g // SSM_GB, 2, SSM_GB * n)
    return jnp.transpose(hf, (1, 0, 3, 2, 4)).reshape(b, ndir, 2, g, n)


def _mixers(x, mod, lw, cols, batch, seq_len, per_batch, rope, ctx, cast=()):
    p, h1, q, k, v, ckv, *us = _inproj_call(x, mod, lw, rope, seq_len, per_batch, cols)
    if ctx is not None:
        ctx_ckv, ctx_kpe, h0 = ctx
        kc, vc = _ctxprep_call(ctx_ckv, ctx_kpe, lw)
        ctx_len = ctx_ckv.shape[0] // batch
    else:
        kc = vc = None
        ctx_len = 0
        h0 = jnp.zeros((2, batch, lw["ssm_a"].shape[-1]), F32)
    za = _attn_call(q, k, v, kc, vc, batch, seq_len, ctx_len)
    zb, zd = _convpool_call(p, lw, batch, seq_len, cols)

    ys, hf = _ssm_call([u.reshape(batch, seq_len, LANES) for u in us], h0, lw)
    ys = [y.reshape(2, batch * seq_len, LANES) for y in ys]
    merged, cast_out = _merge_call(h1, p, za, zb, ys, zd, lw, cols, cast)
    kr = cols["k_rope"] + QK_NOPE
    return merged, cast_out, ckv, p[:, kr:kr + ROPE_DIM], hf


def _channel_mixer(x, merged, mod, lw, seq_len, per_batch):
    x1, h2 = _outproj_call(x, merged, mod, lw, seq_len, per_batch)
    return _mlp_call(x1, h2, mod, lw, seq_len, per_batch)


def kernel(x_prompt, x_sample, c, cache_ckv, cache_krope, state_ssm, c_ctx, w_ada, b_ada, norm1_g, norm2_g, w_in, q_a_norm_g, kv_a_norm_g, w_uq, w_ukv, q_norm_g, k_norm_g, w_mla_o, conv_w, conv_b, w_conv_o, ssm_lam_re, ssm_lam_im, ssm_log_step, ssm_b_re, ssm_b_im, ssm_c_re, ssm_c_im, ssm_d, w_glu, pool_w, pool_scale, w_pool_o, w_o, w_mlp1, w_mlp2):
    prm = dict(norm1_g=norm1_g, norm2_g=norm2_g, w_in=w_in, q_a_norm_g=q_a_norm_g, kv_a_norm_g=kv_a_norm_g,
               w_uq=w_uq, w_ukv=w_ukv, q_norm_g=q_norm_g, k_norm_g=k_norm_g, w_mla_o=w_mla_o,
               conv_w=conv_w, conv_b=conv_b, w_conv_o=w_conv_o, ssm_lam_re=ssm_lam_re, ssm_lam_im=ssm_lam_im,
               ssm_log_step=ssm_log_step, ssm_b_re=ssm_b_re, ssm_b_im=ssm_b_im, ssm_c_re=ssm_c_re,
               ssm_c_im=ssm_c_im, ssm_d=ssm_d, w_glu=w_glu, pool_w=pool_w, pool_scale=pool_scale,
               w_pool_o=w_pool_o, w_o=w_o, w_mlp1=w_mlp1, w_mlp2=w_mlp2)
    depth = w_in.shape[0]
    bp, lp, d = x_prompt.shape
    bs, ls, _ = x_sample.shape
    past = cache_ckv.shape[2]
    g, n = state_ssm.shape[-2:]
    sizes = (w_uq.shape[1], w_ukv.shape[1], conv_w.shape[-1], ssm_d.shape[-1], pool_scale.shape[-1])

    rows = -(-(1 + bs) // SUBLANES) * SUBLANES
    cvec = jnp.zeros((rows, d), F32).at[0].set(c_ctx).at[1:1 + bs].set(c)
    mods = _ada_call(cvec, w_ada, b_ada)
    rope = _rope_tables(ls)
    ssm = _prep_ssm(ssm_lam_re, ssm_lam_im, ssm_log_step, ssm_b_re, ssm_b_im, ssm_c_re, ssm_c_im)
    big, cols = _prep_stacked(prm, sizes)

    yp = x_prompt.reshape(bp * lp, d)
    ys = x_sample.reshape(bs * ls, d)
    ckv_list, krope_list, ssm_list = [], [], []
    for l in range(depth):
        lw = _prep_layer(l, prm, sizes, big, ssm)
        mod_ctx = mods[l, 0:1].reshape(6, 1, d)
        mod_lat = mods[l, 1:1 + bs].reshape(bs * 6, 1, d)
        ctx_kpe = jnp.pad(cache_krope[:, l].reshape(bs * past, ROPE_DIM),
                          ((0, 0), (QK_NOPE, HEAD_PAD - QK_HEAD)))
        ctx = (cache_ckv[:, l].reshape(bs * past, -1), ctx_kpe, _states_to_cols(state_ssm[:, l]))
        merged_s, (w1b, w2b), _, _, _ = _mixers(ys, mod_lat, lw, cols, bs, ls, True, rope, ctx,
                                                cast=(w_mlp1, w_mlp2))
        merged_p, _, ckv_l, krope_l, hf_l = _mixers(yp, mod_ctx, lw, cols, bp, lp, False, None, None)
        ckv_list.append(ckv_l.reshape(bp, lp, -1))
        krope_list.append(krope_l.reshape(bp, lp, -1))
        ssm_list.append(_cols_to_states(hf_l, g, n))
        lw_mlp = dict(lw, w_mlp1=w1b[None], w_mlp2=w2b[None], l_mlp=0)
        yp = _channel_mixer(yp, merged_p, mod_ctx, lw_mlp, lp, False)
        ys = _channel_mixer(ys, merged_s, mod_lat, lw_mlp, ls, True)
    return (yp.reshape(bp, lp, d), ys.reshape(bs, ls, d), jnp.stack(ckv_list, axis=1),
            jnp.stack(krope_list, axis=1), jnp.stack(ssm_list, axis=1))
```

```python
import functools

import jax
import jax.numpy as jnp
from jax import lax
from jax.experimental import pallas as pl
from jax.experimental.pallas import tpu as pltpu

F32 = jnp.float32
BF16 = jnp.bfloat16

GRID_W = 64
MLA_HEADS = 8
QK_NOPE = 64
ROPE_DIM = 32
QK_HEAD = QK_NOPE + ROPE_DIM
V_HEAD = 64
ROPE_THETA = 10000.0
SSM_GROUP_CH = 16
POOL_WINDOWS = (2, 4, 8, 16)
EPS = 1e-6
LOG2E = 1.4426950408889634

LANES = 128
SUBLANES = 8
HEAD_PAD = LANES
MIB = 1024 * 1024


def _cparams(sem, vmem_mib):
    return pltpu.CompilerParams(dimension_semantics=sem, vmem_limit_bytes=vmem_mib * MIB)


def _dot(a, b):
    return jnp.dot(a, b, preferred_element_type=F32)


def _dot_t(a, bt):
    return lax.dot_general(a, bt, (((1,), (1,)), ((), ())), preferred_element_type=F32)


def _mod_spec(chunk, d, tm, seq_len, per_batch):
    def imap(i, *_):
        row = (i * tm) // seq_len if per_batch else 0
        return (row * 6 + chunk, 0, 0)
    return pl.BlockSpec((1, 1, d), imap)


def _ada_kernel(c_ref, w_ref, b_ref, o_ref):
    cv = c_ref[...]
    s = cv * jax.nn.sigmoid(cv)
    o_ref[0] = _dot(s.astype(BF16), w_ref[0].astype(BF16)) + b_ref[0]


def _ada_call(cvec, w_ada, b_ada):
    depth, d, n = w_ada.shape
    rows = cvec.shape[0]
    tn = 1024
    return pl.pallas_call(
        _ada_kernel,
        out_shape=jax.ShapeDtypeStruct((depth, rows, n), F32),
        grid=(depth, n // tn),
        in_specs=[
            pl.BlockSpec((rows, d), lambda l, j: (0, 0)),
            pl.BlockSpec((1, d, tn), lambda l, j: (l, 0, j)),
            pl.BlockSpec((1, 1, tn), lambda l, j: (l, 0, j)),
        ],
        out_specs=pl.BlockSpec((1, rows, tn), lambda l, j: (l, 0, j)),
        compiler_params=_cparams(("parallel", "parallel"), 40),
        name="ada_mod",
    )(cvec, w_ada, b_ada.reshape(depth, 1, n))


def _modnorm(x, g, scale, shift):
    ms = jnp.mean(x * x, axis=-1, keepdims=True)
    return (x * lax.rsqrt(ms + EPS) * g) * (1.0 + scale) + shift


def _rms(x, g, width):
    ms = jnp.sum(x * x, axis=-1, keepdims=True) * (1.0 / width)
    return x * lax.rsqrt(ms + EPS) * g


def _heads_norm(x, g, mult, shared=None, rope=None):
    t_main = g * mult
    if rope is not None:
        xs, shared_s, gs, cos, sin = rope
        t_main = cos * t_main
        t_part = sin * (gs * mult)
    outs = []
    for h in range(MLA_HEADS):
        sl = slice(h * HEAD_PAD, (h + 1) * HEAD_PAD)
        xh = x[:, sl] if shared is None else x[:, sl] + shared
        ms = jnp.sum(xh * xh, axis=-1, keepdims=True) * (1.0 / QK_HEAD)
        y = xh * t_main
        if rope is not None:
            y = y + (shared_s if xs is None else xs[:, sl]) * t_part
        outs.append(y * lax.rsqrt(ms + EPS))
    return jnp.concatenate(outs, axis=1)


def _inproj_kernel(*refs, use_rope, q_mult, cols, n_keep):
    (x_ref, sh_ref, sc_ref, g_ref, wmix_ref, watt_ref, gqa_ref, gkva_ref, gq_ref, gk_ref,
     wuq_ref, wuk_ref, wuv_ref) = refs[:13]
    if use_rope:
        gqs_ref, gks_ref, wuqs_ref, c_ref, s_ref = refs[13:18]
    nu = (cols["pool_u"] - cols["ssm_u"]) // LANES
    p_out, h_out, q_out, k_out, v_out, ckv_out = refs[-6 - nu:-nu]
    u_outs = refs[-nu:]
    att0 = cols["k_rope"]
    tm = x_ref.shape[0]
    nsub = 2

    for sub in range(nsub):
        rs = slice(sub * tm // nsub, (sub + 1) * tm // nsub)
        h = _modnorm(x_ref[rs, :], g_ref[...], sc_ref[0], sh_ref[0]).astype(BF16)
        h_out[rs, :] = h
        res = _dot_t(h, watt_ref[...])
        mix = _dot_t(h, wmix_ref[...])
        p_out[rs, :att0] = mix
        p_out[rs, att0:] = res[:, :n_keep - att0]
        for blk, u_out in enumerate(u_outs):
            u_out[rs, :] = mix[:, cols["ssm_u"] + blk * LANES:cols["ssm_u"] + (blk + 1) * LANES]

        def col(name, width, res=res):
            return res[:, cols[name] - att0:cols[name] - att0 + width]

        rope_tabs = (c_ref[rs, :], s_ref[rs, :]) if use_rope else None
        qa = col("q_a", wuq_ref.shape[0])
        qa_n = _rms(qa, gqa_ref[...], qa.shape[-1]).astype(BF16)
        q = _dot(qa_n, wuq_ref[...])
        q_rope = (_dot(qa_n, wuqs_ref[...]), None, gqs_ref[...]) + rope_tabs if use_rope else None
        q_out[rs, :] = _heads_norm(q, gq_ref[...], q_mult, rope=q_rope).astype(BF16)

        kva = col("kv_a", wuk_ref.shape[0])
        ckv = _rms(kva, gkva_ref[...], kva.shape[-1])
        ckv_out[rs, :] = ckv
        ckv_b = ckv.astype(BF16)
        k = _dot(ckv_b, wuk_ref[...])
        k_rope = (None, col("k_rope_sw", HEAD_PAD), gks_ref[...]) + rope_tabs if use_rope else None
        k_out[rs, :] = _heads_norm(k, gk_ref[...], 1.0, shared=col("k_rope", HEAD_PAD), rope=k_rope).astype(BF16)
        v_out[rs, :] = _dot(ckv_b, wuv_ref[...]).astype(BF16)


def _inproj_call(x, mod, lw, rope, seq_len, per_batch, cols):
    t, d = x.shape
    att0 = cols["k_rope"]
    n_att = lw["w_att"].shape[1]
    n_keep = att0 + HEAD_PAD
    tm = 512
    hq = MLA_HEADS * HEAD_PAD
    hv = MLA_HEADS * V_HEAD
    qlora = lw["w_uq"].shape[0]
    kvlora = lw["w_uk"].shape[0]
    l = lw["l"]
    mspec = functools.partial(_mod_spec, d=d, tm=tm, seq_len=seq_len, per_batch=per_batch)
    const = lambda i: (0, 0)
    row = lambda i: (i, 0)
    in_specs = [
        pl.BlockSpec((tm, d), row), mspec(0), mspec(1),
        pl.BlockSpec((1, d), const),
        pl.BlockSpec((pl.Element(att0), pl.Element(d)), lambda i: (l * lw["w_rows"] + lw["tail_row0"], 0),
                     pipeline_mode=pl.Buffered(1)),
        pl.BlockSpec((None, n_att, d), lambda i: (l, 0, 0), pipeline_mode=pl.Buffered(1)),
        pl.BlockSpec((1, qlora), const), pl.BlockSpec((1, kvlora), const),
        pl.BlockSpec((1, HEAD_PAD), const), pl.BlockSpec((1, HEAD_PAD), const),
        pl.BlockSpec((qlora, hq), const), pl.BlockSpec((kvlora, hq), const), pl.BlockSpec((kvlora, hv), const),
    ]
    args = [x, mod, mod, lw["norm1_g"], lw["w_tail"], lw["w_att"], lw["q_a_norm_g"], lw["kv_a_norm_g"], lw["q_norm_g"],
            lw["k_norm_g"], lw["w_uq"], lw["w_uk"], lw["w_uv"]]
    if rope is not None:
        nblk = seq_len // tm
        in_specs += [
            pl.BlockSpec((1, HEAD_PAD), const), pl.BlockSpec((1, HEAD_PAD), const),
            pl.BlockSpec((qlora, hq), const),
            pl.BlockSpec((tm, HEAD_PAD), lambda i: (i % nblk, 0)),
            pl.BlockSpec((tm, HEAD_PAD), lambda i: (i % nblk, 0)),
        ]
        args += [lw["q_norm_g_sw"], lw["k_norm_g_sw"], lw["w_uq_sw"]] + list(rope)
    q_mult = LOG2E * QK_HEAD ** -0.5
    nu = (cols["pool_u"] - cols["ssm_u"]) // LANES
    return pl.pallas_call(
        functools.partial(_inproj_kernel, use_rope=rope is not None, q_mult=q_mult, cols=cols, n_keep=n_keep),
        out_shape=(jax.ShapeDtypeStruct((t, n_keep), F32), jax.ShapeDtypeStruct((t, d), BF16),
                   jax.ShapeDtypeStruct((t, hq), BF16), jax.ShapeDtypeStruct((t, hq), BF16),
                   jax.ShapeDtypeStruct((t, hv), BF16), jax.ShapeDtypeStruct((t, kvlora), F32))
        + (jax.ShapeDtypeStruct((t, LANES), F32),) * nu,
        grid=(t // tm,),
        in_specs=in_specs,
        out_specs=(pl.BlockSpec((tm, n_keep), row), pl.BlockSpec((tm, d), row),
                   pl.BlockSpec((tm, hq), row), pl.BlockSpec((tm, hq), row),
                   pl.BlockSpec((tm, hv), row), pl.BlockSpec((tm, kvlora), row))
        + (pl.BlockSpec((tm, LANES), row),) * nu,
        compiler_params=_cparams(("parallel",), 56),
        name="in_proj",
    )(*args)


def _ctxprep_kernel(ckv_ref, kpe_ref, gk_ref, wuk_ref, wuv_ref, k_out, v_out):
    ckv_b = ckv_ref[...].astype(BF16)
    k = _dot(ckv_b, wuk_ref[...])
    k_out[...] = _heads_norm(k, gk_ref[...], 1.0, shared=kpe_ref[...]).astype(BF16)
    v_out[...] = _dot(ckv_b, wuv_ref[...]).astype(BF16)


def _ctxprep_call(ckv, kpe, lw):
    t, kvlora = ckv.shape
    tm = 512
    hq = MLA_HEADS * HEAD_PAD
    hv = MLA_HEADS * V_HEAD
    const = lambda i: (0, 0)
    return pl.pallas_call(
        _ctxprep_kernel,
        out_shape=(jax.ShapeDtypeStruct((t, hq), BF16), jax.ShapeDtypeStruct((t, hv), BF16)),
        grid=(t // tm,),
        in_specs=[
            pl.BlockSpec((tm, kvlora), lambda i: (i, 0)),
            pl.BlockSpec((tm, HEAD_PAD), lambda i: (i, 0)),
            pl.BlockSpec((1, HEAD_PAD), const),
            pl.BlockSpec((kvlora, hq), const),
            pl.BlockSpec((kvlora, hv), const),
        ],
        out_specs=(pl.BlockSpec((tm, hq), lambda i: (i, 0)), pl.BlockSpec((tm, hv), lambda i: (i, 0))),
        compiler_params=_cparams(("parallel",), 40),
        name="ctx_prep",
    )(ckv, kpe, lw["k_norm_g"], lw["w_uk"], lw["w_uv"])


def _attn_kernel(*refs, has_ctx, heads):
    if has_ctx:
        q_ref, ko_ref, vo_ref, kc_ref, vc_ref, o_ref = refs
    else:
        q_ref, ko_ref, vo_ref, o_ref = refs
    contract_last = (((1,), (1,)), ((), ()))
    lane = lax.broadcasted_iota(jnp.int32, (q_ref.shape[0], 2 * V_HEAD), 1)
    for pair in range(heads // 2):
        vsl = slice(pair * 2 * V_HEAD, (pair + 1) * 2 * V_HEAD)
        outs = []
        for hh in range(2):
            h = 2 * pair + hh
            sl = slice(h * HEAD_PAD, (h + 1) * HEAD_PAD)
            q = q_ref[:, sl]
            s_o = lax.dot_general(q, ko_ref[:, sl], contract_last, preferred_element_type=F32)
            m = jnp.max(s_o, axis=-1, keepdims=True)
            if has_ctx:
                s_c = lax.dot_general(q, kc_ref[:, sl], contract_last, preferred_element_type=F32)
                m = jnp.maximum(m, jnp.max(s_c, axis=-1, keepdims=True))
            p_o = jnp.exp2(s_o - m)
            l = jnp.sum(p_o, axis=-1, keepdims=True)
            acc = _dot(p_o.astype(BF16), vo_ref[:, vsl])
            if has_ctx:
                p_c = jnp.exp2(s_c - m)
                l = l + jnp.sum(p_c, axis=-1, keepdims=True)
                acc = acc + _dot(p_c.astype(BF16), vc_ref[:, vsl])
            outs.append(acc / l)
        o_ref[:, vsl] = jnp.where(lane < V_HEAD, outs[0], outs[1]).astype(BF16)


def _attn_call(q, k, v, kc, vc, batch, seq_len, ctx_len):
    t = q.shape[0]
    tq = min(seq_len, 512)
    heads = 8
    nq = seq_len // tq
    npair = MLA_HEADS // heads
    qw = heads * HEAD_PAD
    vw = heads * V_HEAD
    in_specs = [
        pl.BlockSpec((tq, qw), lambda b, h, i: (b * nq + i, h)),
        pl.BlockSpec((seq_len, qw), lambda b, h, i: (b, h)),
        pl.BlockSpec((seq_len, vw), lambda b, h, i: (b, h)),
    ]
    args = [q, k, v]
    if kc is not None:
        in_specs += [pl.BlockSpec((ctx_len, qw), lambda b, h, i: (b, h)),
                     pl.BlockSpec((ctx_len, vw), lambda b, h, i: (b, h))]
        args += [kc, vc]
    return pl.pallas_call(
        functools.partial(_attn_kernel, has_ctx=kc is not None, heads=heads),
        out_shape=jax.ShapeDtypeStruct((t, MLA_HEADS * V_HEAD), BF16),
        grid=(batch, npair, nq),
        in_specs=in_specs,
        out_specs=pl.BlockSpec((tq, vw), lambda b, h, i: (b * nq + i, h)),
        compiler_params=_cparams(("parallel", "parallel", "arbitrary"), 56),
        name="attention",
    )(*args)


def _convpool_kernel(cu_ref, bg_ref, cg_ref, pu_ref, cw_ref, cb_ref, pw_ref, ps_ref, zb_ref, zd_ref):
    n = cu_ref.shape[0]
    gps = pw_ref.shape[0]
    edge = max(POOL_WINDOWS) // 2
    assert edge % SUBLANES == 0 and n >= 4 * edge

    def shifts(rows, zero_fill):
        def down(x, k):
            y = pltpu.roll(x, k, axis=0)
            if zero_fill:
                y = jnp.where(lax.broadcasted_iota(jnp.int32, x.shape, 0) >= k, y, 0.0)
            return y

        def up(x, k):
            y = pltpu.roll(x, rows - k, axis=0)
            if zero_fill:
                y = jnp.where(lax.broadcasted_iota(jnp.int32, x.shape, 0) < rows - k, y, 0.0)
            return y
        return down, up

    def conv_rows(r0, rows, zero_fill):
        down, up = shifts(rows, zero_fill)
        v = cg_ref[r0:r0 + rows, :] * cu_ref[r0:r0 + rows, :]
        conv = (down(v, 1) * cw_ref[0:1, :] + v * cw_ref[1:2, :] + up(v, 1) * cw_ref[2:3, :]
                + cb_ref[...])
        return (bg_ref[r0:r0 + rows, :] * conv).astype(BF16)

    def pool_rows(r0, rows, zero_fill, gg, half):
        down, up = shifts(rows, zero_fill)
        sl = slice(gg * LANES, (gg + 1) * LANES)
        u = pu_ref[r0:r0 + rows, sl]
        fwd = u
        bwd = u
        k = 1
        while k < half:
            fwd = fwd + up(fwd, k)
            bwd = bwd + down(bwd, k)
            k *= 2
        total = fwd + down(bwd, 1)
        rowf = (lax.broadcasted_iota(jnp.int32, u.shape, 0) + r0).astype(F32)
        cnt = jnp.minimum(rowf + half, float(n)) - jnp.maximum(rowf - half, 0.0)
        mean = total / cnt - u
        return (_dot(mean.astype(BF16), pw_ref[gg]) * ps_ref[:, sl]).astype(BF16)

    zb_ref[...] = conv_rows(0, n, False)
    zb_ref[0:edge, :] = conv_rows(0, 2 * edge, True)[0:edge]
    zb_ref[n - edge:n, :] = conv_rows(n - 2 * edge, 2 * edge, True)[edge:]

    step = pl.program_id(1)
    for first in range(0, len(POOL_WINDOWS), gps):
        @pl.when(step * gps == first)
        def _(first=first):
            for gg in range(gps):
                sl = slice(gg * LANES, (gg + 1) * LANES)
                half = POOL_WINDOWS[first + gg] // 2
                zd_ref[:, sl] = pool_rows(0, n, False, gg, half)
                zd_ref[0:edge, sl] = pool_rows(0, 2 * edge, True, gg, half)[0:edge]
                zd_ref[n - edge:n, sl] = pool_rows(n - 2 * edge, 2 * edge, True, gg, half)[edge:]


def _convpool_call(p, lw, batch, seq_len, cols):
    t = p.shape[0]
    gps = 2 if seq_len > 512 else len(POOL_WINDOWS)
    cw = gps * LANES
    nblk = lw["conv_w"].shape[1] // cw

    def pspec(name):
        base = cols[name] // cw
        return pl.BlockSpec((seq_len, cw), lambda b, g: (b, base + g))

    vec = lambda b, g: (0, g)
    return pl.pallas_call(
        _convpool_kernel,
        out_shape=(jax.ShapeDtypeStruct((t, nblk * cw), BF16), jax.ShapeDtypeStruct((t, nblk * cw), BF16)),
        grid=(batch, nblk),
        in_specs=[pspec("conv_u"), pspec("conv_bg"), pspec("conv_cg"), pspec("pool_u"),
                  pl.BlockSpec((3, cw), vec), pl.BlockSpec((1, cw), vec),
                  pl.BlockSpec((gps, LANES, LANES), lambda b, g: (g, 0, 0)), pl.BlockSpec((1, cw), vec)],
        out_specs=(pl.BlockSpec((seq_len, cw), lambda b, g: (b, g)),
                   pl.BlockSpec((seq_len, cw), lambda b, g: (b, g))),
        compiler_params=_cparams(("parallel", "parallel"), 40),
        name="conv_pool",
    )(p, p, p, p, lw["conv_w"], lw["conv_b"], lw["pool_w"], lw["pool_scale"])


SSM_R = 4
SSM_GB = LANES // SSM_GROUP_CH


def _ssm_kernel(*refs, steps, nchunks, nblk):
    u_refs = refs[:nblk]
    h0_ref, a_ref, mb_ref, m1_ref, m2_ref = refs[nblk:nblk + 5]
    y_refs = refs[nblk + 5:2 * nblk + 5]
    hf_ref, slab_scr, xs_scr, st_scr = refs[2 * nblk + 5:]
    d = pl.program_id(0)
    c = pl.program_id(2)
    width = st_scr.shape[1]
    bw = width // nblk

    @pl.when(c == 0)
    def _():
        st_scr[...] = h0_ref[0]

    for blk in range(nblk):
        for j in range(SSM_R):
            for b in range(SUBLANES):
                slab_scr[blk * SSM_R + j, pl.ds(b, steps, stride=SUBLANES), :] = (
                    u_refs[blk][b, pl.ds(j, steps, stride=SSM_R), :])

    def packed(blk):
        return jnp.concatenate([slab_scr[blk * SSM_R + j] for j in range(SSM_R)], axis=1).astype(BF16)

    for blk in range(nblk):
        xs_scr[:, blk * bw:(blk + 1) * bw] = _dot(packed(blk), mb_ref[0, blk])

    gw = width // 2
    for part in range(width // gw):
        offs = []
        for blk in range(part * nblk // 2, (part + 1) * nblk // 2):
            offs += [(blk * bw + k * LANES, blk * bw + bw // 2 + k * LANES) for k in range(bw // 2 // LANES)]
        coef = [(a_ref[0, :, r0:r0 + LANES], a_ref[0, :, i0:i0 + LANES]) for r0, i0 in offs]

        def body(t, carry, offs=offs, coef=coef):
            tt = t + d * (steps - 1 - 2 * t)
            row0 = pl.multiple_of(tt * SUBLANES, SUBLANES)
            new = []
            for (r0, i0), (a_re, a_im), x_re, x_im in zip(offs, coef, carry[0::2], carry[1::2]):
                n_re = (a_re * x_re - a_im * x_im) + xs_scr[pl.ds(row0, SUBLANES), r0:r0 + LANES]
                n_im = (a_re * x_im + a_im * x_re) + xs_scr[pl.ds(row0, SUBLANES), i0:i0 + LANES]
                xs_scr[pl.ds(row0, SUBLANES), r0:r0 + LANES] = x_re
                xs_scr[pl.ds(row0, SUBLANES), i0:i0 + LANES] = x_im
                new += [n_re, n_im]
            return tuple(new)

        init = []
        for r0, i0 in offs:
            init += [st_scr[:, r0:r0 + LANES], st_scr[:, i0:i0 + LANES]]
        final = lax.fori_loop(0, steps, body, tuple(init), unroll=2)
        for (r0, i0), x_re, x_im in zip(offs, final[0::2], final[1::2]):
            st_scr[:, r0:r0 + LANES] = x_re
            st_scr[:, i0:i0 + LANES] = x_im

    for blk in range(nblk):
        y = (_dot(xs_scr[:, blk * bw:(blk + 1) * bw].astype(BF16), m1_ref[0, blk])
             + _dot(packed(blk), m2_ref[0, blk]))
        for j in range(SSM_R):
            slab_scr[blk * SSM_R + j] = y[:, j * LANES:(j + 1) * LANES]
        for j in range(SSM_R):
            for b in range(SUBLANES):
                y_refs[blk][0, b, pl.ds(j, steps, stride=SSM_R), :] = (
                    slab_scr[blk * SSM_R + j, pl.ds(b, steps, stride=SUBLANES), :])

    @pl.when(c == nchunks - 1)
    def _():
        hf_ref[0] = st_scr[...]


def _ssm_call(us, h0, lw):
    batch, seq_len, _ = us[0].shape
    width = h0.shape[-1]
    nblk = lw["ssm_mb"].shape[2]
    assert len(us) == nblk
    l = lw["l"]
    steps = min(seq_len // SSM_R, 64)
    tokens = steps * SSM_R
    nchunks = seq_len // tokens
    nbg = batch // SUBLANES
    rows = steps * SUBLANES

    def cidx(d, c):
        return c + d * (nchunks - 1 - 2 * c)

    uspec = pl.BlockSpec((SUBLANES, tokens, LANES), lambda d, b, c: (b, cidx(d, c), 0))

    def wspec(arr):
        return pl.BlockSpec((None, 1) + arr.shape[2:], lambda d, b, c: (l, d, 0, 0, 0))

    yshape = jax.ShapeDtypeStruct((2, batch, seq_len, LANES), F32)
    yspec = pl.BlockSpec((1, SUBLANES, tokens, LANES), lambda d, b, c: (d, b, cidx(d, c), 0))
    outs = pl.pallas_call(
        functools.partial(_ssm_kernel, steps=steps, nchunks=nchunks, nblk=nblk),
        out_shape=(yshape,) * nblk + (jax.ShapeDtypeStruct((2, batch, width), F32),),
        grid=(2, nbg, nchunks),
        in_specs=[uspec] * nblk + [
            pl.BlockSpec((1, SUBLANES, width), lambda d, b, c: (d, b, 0)),
            pl.BlockSpec((None, 1, SUBLANES, width), lambda d, b, c: (l, d, 0, 0)),
            wspec(lw["ssm_mb"]), wspec(lw["ssm_m1"]), wspec(lw["ssm_m2"]),
        ],
        out_specs=(yspec,) * nblk + (pl.BlockSpec((1, SUBLANES, width), lambda d, b, c: (d, b, 0)),),
        scratch_shapes=[pltpu.VMEM((nblk * SSM_R, rows, LANES), F32), pltpu.VMEM((rows, width), F32),
                        pltpu.VMEM((SUBLANES, width), F32)],
        compiler_params=_cparams(("arbitrary", "arbitrary", "arbitrary"), 52),
        name="s5_scan",
    )(*us, h0, lw["ssm_a"], lw["ssm_mb"], lw["ssm_m1"], lw["ssm_m2"])
    return outs[:nblk], outs[nblk]


def _merge_kernel(*refs, nblk, ncast):
    (h_ref, za_ref, zb_ref, su_ref, sd_ref, zd_ref, wg0, wg1, wg2, wg3, wa, wb, wca, wcg, wd) = refs[:15]
    yf_refs = refs[15:15 + nblk]
    yb_refs = refs[15 + nblk:15 + 2 * nblk]
    cast_in = refs[15 + 2 * nblk:15 + 2 * nblk + ncast]
    o_ref = refs[15 + 2 * nblk + ncast]
    cast_out = refs[16 + 2 * nblk + ncast:16 + 2 * nblk + 2 * ncast]
    y_scr = refs[-1]

    for src, dst in zip(cast_in, cast_out):
        dst[...] = src[...].astype(BF16)

    @pl.when(pl.program_id(1) == 0)
    def _():
        y_both = jnp.concatenate([f[0] + b[0] for f, b in zip(yf_refs, yb_refs)], axis=1)
        y_scr[...] = (y_both + sd_ref[...] * su_ref[...]).astype(BF16)

    tm = h_ref.shape[0]
    nsub = 2
    for sub in range(nsub):
        rs = slice(sub * tm // nsub, (sub + 1) * tm // nsub)
        h = h_ref[rs, :]
        y = y_scr[rs, :]

        def gate(w_ref, h=h):
            return jax.nn.sigmoid(_dot_t(h, w_ref[...]))

        merged = gate(wg0) * _dot(za_ref[rs, :], wa[...])
        merged = merged + gate(wg1) * _dot(zb_ref[rs, :], wb[...])
        merged = merged + gate(wg2) * (_dot(y, wca[...]) * jax.nn.sigmoid(_dot(y, wcg[...])))
        merged = merged + gate(wg3) * _dot(zd_ref[rs, :], wd[...])
        o_ref[rs, :] = merged.astype(BF16)


def _merge_call(h, p, za, zb, ys, zd, lw, cols, cast=()):
    t, d = h.shape
    bw = za.shape[1]
    tm, tn = 512, 512
    nj = d // tn
    nblk = len(ys)
    row = lambda i, j: (i, 0)
    su_blk = cols["ssm_u"] // bw

    l = lw["l"]

    gate_blk0 = cols["k_rope"] // tn

    row0 = l * lw["w_rows"] + lw["tail_row0"]

    def gspec(k):
        assert (row0 + gate_blk0 * tn) % 16 == 0
        return pl.BlockSpec((pl.Element(tn), pl.Element(d)),
                            lambda i, j: (pl.multiple_of(row0 + (gate_blk0 + k * nj + j) * tn, 16), 0))

    def yspec(direction):
        return pl.BlockSpec((1, tm, LANES), lambda i, j: (direction, i, 0))

    wcol = pl.BlockSpec((None, bw, tn), lambda i, j: (l, 0, j))
    in_specs = [
        pl.BlockSpec((tm, d), row),
        pl.BlockSpec((tm, bw), row), pl.BlockSpec((tm, bw), row),
        pl.BlockSpec((tm, bw), lambda i, j: (i, su_blk)),
        pl.BlockSpec((1, bw), lambda i, j: (0, 0)),
        pl.BlockSpec((tm, bw), row),
        gspec(0), gspec(1), gspec(2), gspec(3),
        wcol, wcol, wcol, pl.BlockSpec((None, bw, tn), lambda i, j: (l, 0, nj + j)), wcol,
    ] + [yspec(0)] * nblk + [yspec(1)] * nblk
    out_shape = [jax.ShapeDtypeStruct((t, d), BF16)]
    out_specs = [pl.BlockSpec((tm, tn), lambda i, j: (i, j))]
    nsteps = (t // tm) * nj
    for w in cast:
        rows, n = w.shape[1] // nsteps, w.shape[2]
        assert rows * nsteps == w.shape[1] and rows % (2 * SUBLANES) == 0
        in_specs.append(pl.BlockSpec((None, rows, n), lambda i, j: (l, i * nj + j, 0)))
        out_shape.append(jax.ShapeDtypeStruct(w.shape[1:], BF16))
        out_specs.append(pl.BlockSpec((rows, n), lambda i, j: (i * nj + j, 0)))
    wg = lw["w_tail"]
    outs = pl.pallas_call(
        functools.partial(_merge_kernel, nblk=nblk, ncast=len(cast)),
        out_shape=tuple(out_shape),
        grid=(t // tm, nj),
        in_specs=in_specs,
        out_specs=tuple(out_specs),
        scratch_shapes=[pltpu.VMEM((tm, bw), BF16)],
        compiler_params=_cparams(("parallel", "arbitrary"), 48),
        name="branch_merge",
    )(h, za, zb, p, lw["ssm_d"], zd,
      wg, wg, wg, wg, lw["w_mla_o"], lw["w_conv_o"], lw["w_glu"], lw["w_glu"], lw["w_pool_o"], *ys, *ys, *cast)
    return outs[0], tuple(outs[1:])


def _outproj_kernel(x_ref, m_ref, gt_ref, sh_ref, sc_ref, g_ref, wo_ref, x1_ref, h2_ref):
    tm = x_ref.shape[0]
    nsub = 2
    for sub in range(nsub):
        rs = slice(sub * tm // nsub, (sub + 1) * tm // nsub)
        x1 = x_ref[rs, :] + gt_ref[0] * _dot(m_ref[rs, :], wo_ref[...])
        x1_ref[rs, :] = x1
        h2_ref[rs, :] = _modnorm(x1, g_ref[...], sc_ref[0], sh_ref[0]).astype(BF16)


def _outproj_call(x, merged, mod, lw, seq_len, per_batch):
    t, d = x.shape
    tm = 512
    mspec = functools.partial(_mod_spec, d=d, tm=tm, seq_len=seq_len, per_batch=per_batch)
    row = lambda i: (i, 0)
    return pl.pallas_call(
        _outproj_kernel,
        out_shape=(jax.ShapeDtypeStruct((t, d), F32), jax.ShapeDtypeStruct((t, d), BF16)),
        grid=(t // tm,),
        in_specs=[
            pl.BlockSpec((tm, d), row), pl.BlockSpec((tm, d), row),
            mspec(2), mspec(3), mspec(4),
            pl.BlockSpec((1, d), lambda i: (0, 0)),
            pl.BlockSpec((None, d, d), lambda i: (lw["l_o"], 0, 0), pipeline_mode=pl.Buffered(1)),
        ],
        out_specs=(pl.BlockSpec((tm, d), row), pl.BlockSpec((tm, d), row)),
        compiler_params=_cparams(("parallel",), 48),
        name="out_proj",
    )(x, merged, mod, mod, mod, lw["norm2_g"], lw["w_o"])


def _mlp_kernel(x_ref, h_ref, gt_ref, w1_ref, w2_ref, o_ref):
    @pl.when(pl.program_id(1) == 0)
    def _():
        o_ref[...] = x_ref[...]

    hid = jnp.square(jnp.maximum(_dot(h_ref[...], w1_ref[...]), 0.0))
    o_ref[...] += gt_ref[0] * _dot(hid.astype(BF16), w2_ref[...])


def _mlp_call(x, h, mod, lw, seq_len, per_batch):
    t, d = x.shape
    hidden = lw["w_mlp1"].shape[2]
    tm, tk = 512, 1024
    l = lw["l_mlp"]
    mspec = functools.partial(_mod_spec, d=d, tm=tm, seq_len=seq_len, per_batch=per_batch)
    row = lambda i, k: (i, 0)
    return pl.pallas_call(
        _mlp_kernel,
        out_shape=jax.ShapeDtypeStruct((t, d), F32),
        grid=(t // tm, hidden // tk),
        in_specs=[
            pl.BlockSpec((tm, d), row), pl.BlockSpec((tm, d), row), mspec(5),
            pl.BlockSpec((None, d, tk), lambda i, k: (l, 0, k)),
            pl.BlockSpec((None, tk, d), lambda i, k: (l, k, 0)),
        ],
        out_specs=pl.BlockSpec((tm, d), row),
        compiler_params=_cparams(("parallel", "arbitrary"), 48),
        name="mlp",
    )(x, h, mod, lw["w_mlp1"], lw["w_mlp2"])


def _pad_heads(w, per_head, lo):
    lead = w.shape[:-1]
    w = w.reshape(lead + (MLA_HEADS, per_head))
    w = jnp.pad(w, [(0, 0)] * len(lead) + [(0, 0), (lo, HEAD_PAD - lo - per_head)])
    return w.reshape(lead + (MLA_HEADS * HEAD_PAD,))


_ROPE_PARTNER = tuple((i // 16) * 16 + (i % 16 + 8) % 16 for i in range(ROPE_DIM))


def _swap_heads(w):
    lead = w.shape[:-1]
    w = w.reshape(lead + (MLA_HEADS, QK_HEAD))[..., QK_NOPE:][..., jnp.array(_ROPE_PARTNER)]
    w = jnp.pad(w, [(0, 0)] * len(lead) + [(0, 0), (QK_NOPE, HEAD_PAD - QK_HEAD)])
    return w.reshape(lead + (MLA_HEADS * HEAD_PAD,))


def _expand_kernel(t_ref, o_ref, *, axis, gb, cg):
    t = t_ref[...]
    big = jnp.concatenate([t] * gb, axis=axis)
    rows = lax.broadcasted_iota(jnp.int32, big.shape, 0)
    lanes = lax.broadcasted_iota(jnp.int32, big.shape, 1)
    if axis == 1:
        keep = (rows // cg) % gb == lanes // t.shape[1]
    else:
        keep = rows // t.shape[0] == (lanes // cg) % gb
    o_ref[...] = jnp.where(keep, big, 0.0).astype(BF16)


def _expand_call(parts, axis, gb, cg):
    _, ns, nblk, r, c = parts.shape
    side = gb * (c if axis == 1 else r)
    out_rc = (r, 2 * side) if axis == 1 else (2 * side, c)
    blk_rc = (r, side) if axis == 1 else (side, c)
    omap = (lambda p, s, b: (s, b, 0, p)) if axis == 1 else (lambda p, s, b: (s, b, p, 0))
    return pl.pallas_call(
        functools.partial(_expand_kernel, axis=axis, gb=gb, cg=cg),
        out_shape=jax.ShapeDtypeStruct((ns, nblk) + out_rc, BF16),
        grid=(2, ns, nblk),
        in_specs=[pl.BlockSpec((None, None, None, r, c), lambda p, s, b: (p, s, b, 0, 0))],
        out_specs=pl.BlockSpec((None, None) + blk_rc, omap),
        compiler_params=_cparams(("parallel", "parallel", "parallel"), 32),
        name="s5_expand",
    )(parts)


def _prep_ssm(lam_re, lam_im, log_step, b_re, b_im, c_re, c_im):
    r, gb = SSM_R, SSM_GB
    nl, ndir, g, n, cg = b_re.shape
    ns = nl * ndir
    nblk = g // gb
    flat = lambda v: v.reshape((ns,) + v.shape[2:])
    lam_re, lam_im, log_step, b_re, b_im, c_re, c_im = map(
        flat, (lam_re, lam_im, log_step, b_re, b_im, c_re, c_im))
    step = jnp.exp(log_step)[..., None]
    pw = []
    for p in range(r + 1):
        mag = jnp.exp(p * (lam_re * step))
        pw.append((mag * jnp.cos(p * (lam_im * step)), mag * jnp.sin(p * (lam_im * step))))
    ar, ai = pw[1]
    den = lam_re * lam_re + lam_im * lam_im
    qr = ((ar - 1.0) * lam_re + ai * lam_im) / den
    qi = (ai * lam_re - (ar - 1.0) * lam_im) / den
    bb_re = qr[..., None] * b_re - qi[..., None] * b_im
    bb_im = qr[..., None] * b_im + qi[..., None] * b_re
    fwd = (jnp.arange(ns) % ndir == 0)[:, None, None]

    def power(p_fwd, p_bwd):
        return (jnp.where(fwd, pw[p_fwd][0], pw[p_bwd][0]), jnp.where(fwd, pw[p_fwd][1], pw[p_bwd][1]))


    t_re, t_im = [], []
    for j in range(r):
        pr, pi = power(r - 1 - j, j)
        t_re.append(pr[..., None] * bb_re - pi[..., None] * bb_im)
        t_im.append(pr[..., None] * bb_im + pi[..., None] * bb_re)
    def inc_block(ts):
        t = jnp.stack(ts, axis=1).reshape(ns, r, nblk, gb, n, cg)
        return jnp.transpose(t, (0, 2, 1, 3, 5, 4)).reshape(ns, nblk, r * gb * cg, n)

    mb = _expand_call(jnp.stack([inc_block(t_re), inc_block(t_im)]), 1, gb, cg)

    cl_re, cl_im = [], []
    for j in range(r):
        pr, pi = power(j + 1, r - j)
        cl_re.append(c_re * pr[:, :, None, :] - c_im * pi[:, :, None, :])
        cl_im.append(c_re * pi[:, :, None, :] + c_im * pr[:, :, None, :])

    def out_block(ts):
        t = jnp.stack(ts, axis=1).reshape(ns, r, nblk, gb, cg, n)
        return jnp.transpose(t, (0, 2, 5, 1, 3, 4)).reshape(ns, nblk, n, r * gb * cg)

    m1 = _expand_call(jnp.stack([out_block(cl_re), -out_block(cl_im)]), 0, gb, cg)

    hi = lax.Precision.HIGHEST
    lane_g = jnp.arange(gb * cg) // cg
    mask_k = (lane_g[:, None] == lane_g[None, :]).astype(F32)
    kd = []
    for p in range(r):
        pr, pi = pw[p]
        k2 = (jnp.einsum("sgcn,sgnk->sgck", c_re * pr[:, :, None, :] - c_im * pi[:, :, None, :], bb_re, precision=hi)
              - jnp.einsum("sgcn,sgnk->sgck", c_re * pi[:, :, None, :] + c_im * pr[:, :, None, :], bb_im, precision=hi))
        k2 = jnp.swapaxes(k2, -1, -2).reshape(ns, nblk, gb * cg, cg)
        kd.append(jnp.tile(k2, (1, 1, 1, gb)) * mask_k)
    fwd4 = fwd[..., None]
    zero = jnp.zeros_like(kd[0])
    m2_rows = []
    for i in range(r):
        blocks = []
        for j in range(r):
            if i == j:
                blocks.append(kd[0])
            elif j > i:
                blocks.append(jnp.where(fwd4, kd[j - i], zero))
            else:
                blocks.append(jnp.where(fwd4, zero, kd[i - j]))
        m2_rows.append(jnp.concatenate(blocks, axis=-1))
    m2 = jnp.concatenate(m2_rows, axis=2)

    a = jnp.stack([pw[r][0].reshape(ns, nblk, gb * n), pw[r][1].reshape(ns, nblk, gb * n)], axis=2)
    a = jnp.broadcast_to(a.reshape(ns, 1, 2 * g * n), (ns, SUBLANES, 2 * g * n))
    unflat = lambda v: v.reshape((nl, ndir) + v.shape[1:])
    return unflat(a), unflat(mb), unflat(m1), unflat(m2.astype(BF16))


def _prep_stacked(prm, sizes):
    q_lora, kv_lora, conv_w, ssm_w, pool_w_ = sizes
    w_in_t = jnp.swapaxes(prm["w_in"], 1, 2)
    tail0 = q_lora + kv_lora + ROPE_DIM
    w_tail = w_in_t.astype(BF16).reshape(-1, w_in_t.shape[2])
    head = w_in_t[:, :tail0, :]
    cols = {}
    o = 0
    for name, width in (("conv_u", conv_w), ("conv_bg", conv_w), ("conv_cg", conv_w), ("ssm_u", ssm_w),
                        ("pool_u", pool_w_), ("k_rope", HEAD_PAD), ("k_rope_sw", HEAD_PAD), ("q_a", q_lora),
                        ("kv_a", kv_lora)):
        cols[name] = o
        o += width
    k_rope = head[:, q_lora + kv_lora:tail0, :]
    rope_lanes = ((0, 0), (QK_NOPE, HEAD_PAD - QK_HEAD), (0, 0))
    w_att = jnp.concatenate([jnp.pad(k_rope, rope_lanes),
                             jnp.pad(k_rope[:, jnp.array(_ROPE_PARTNER), :], rope_lanes),
                             head[:, :q_lora + kv_lora, :]], axis=1).astype(BF16)
    big = dict(w_tail=w_tail, w_att=w_att, tail_row0=tail0, w_rows=w_in_t.shape[1])
    for name in ("w_mla_o", "w_conv_o", "w_glu", "w_pool_o"):
        big[name] = prm[name].astype(BF16)
    return big, cols


def _prep_layer(l, prm, sizes, big, ssm):
    kv_lora = sizes[1]
    w_ukv = prm["w_ukv"][l].reshape(kv_lora, MLA_HEADS, QK_NOPE + V_HEAD)
    w_uk = _pad_heads(w_ukv[:, :, :QK_NOPE].reshape(kv_lora, -1), QK_NOPE, 0)
    w_uv = w_ukv[:, :, QK_NOPE:].reshape(kv_lora, -1)
    ssm_a, ssm_mb, ssm_m1, ssm_m2 = ssm

    row = lambda v: v.reshape(1, -1)
    head_gain = lambda g: row(jnp.pad(g, (0, HEAD_PAD - QK_HEAD)))
    partner_gain = lambda g: row(jnp.pad(g[QK_NOPE:][jnp.array(_ROPE_PARTNER)], (QK_NOPE, HEAD_PAD - QK_HEAD)))
    lw = dict(
        big, l=l,
        norm1_g=row(prm["norm1_g"][l]), norm2_g=row(prm["norm2_g"][l]),
        q_a_norm_g=row(prm["q_a_norm_g"][l]), kv_a_norm_g=row(prm["kv_a_norm_g"][l]),
        q_norm_g=head_gain(prm["q_norm_g"][l]), k_norm_g=head_gain(prm["k_norm_g"][l]),
        q_norm_g_sw=partner_gain(prm["q_norm_g"][l]), k_norm_g_sw=partner_gain(prm["k_norm_g"][l]),
        w_uq=_pad_heads(prm["w_uq"][l], QK_HEAD, 0).astype(BF16),
        w_uq_sw=_swap_heads(prm["w_uq"][l]).astype(BF16),
        w_uk=w_uk.astype(BF16), w_uv=w_uv.astype(BF16),
        conv_w=prm["conv_w"][l], conv_b=row(prm["conv_b"][l]),
        ssm_a=ssm_a, ssm_mb=ssm_mb, ssm_m1=ssm_m1, ssm_m2=ssm_m2,
        ssm_d=row(prm["ssm_d"][l]),
        pool_w=prm["pool_w"][l].astype(BF16), pool_scale=row(prm["pool_scale"][l]),
    )
    return lw


def _rope_tables(seq_len):
    half = ROPE_DIM // 2
    inv_freq = ROPE_THETA ** (-jnp.arange(0, half, 2, dtype=F32) / half)
    t = jnp.arange(seq_len)
    ang_r = (t // GRID_W).astype(F32)[:, None] * inv_freq
    ang_c = (t % GRID_W).astype(F32)[:, None] * inv_freq
    ones_lo = jnp.ones((seq_len, QK_NOPE), F32)
    zeros_lo = jnp.zeros((seq_len, QK_NOPE), F32)
    tail = jnp.zeros((seq_len, HEAD_PAD - QK_HEAD), F32)
    c = jnp.concatenate([ones_lo, jnp.cos(ang_r), jnp.cos(ang_r), jnp.cos(ang_c), jnp.cos(ang_c), tail], axis=1)
    s = jnp.concatenate([zeros_lo, -jnp.sin(ang_r), jnp.sin(ang_r), -jnp.sin(ang_c), jnp.sin(ang_c), tail], axis=1)
    return c, s


def _states_to_cols(st):
    b, ndir, _, g, n = st.shape
    st = st.reshape(b, ndir, 2, g // SSM_GB, SSM_GB * n)
    return jnp.transpose(st, (1, 0, 3, 2, 4)).reshape(ndir, b, 2 * g * n)


def _cols_to_states(hf, g, n):
    ndir, b, _ = hf.shape
    hf = hf.reshape(ndir, b, g // SSM_GB, 2, SSM_GB * n)
    return jnp.transpose(hf, (1, 0, 3, 2, 4)).reshape(b, ndir, 2, g, n)


def _mixers(x, mod, lw, cols, batch, seq_len, per_batch, rope, ctx, cast=()):
    p, h1, q, k, v, ckv, *us = _inproj_call(x, mod, lw, rope, seq_len, per_batch, cols)
    if ctx is not None:
        ctx_ckv, ctx_kpe, h0 = ctx
        kc, vc = _ctxprep_call(ctx_ckv, ctx_kpe, lw)
        ctx_len = ctx_ckv.shape[0] // batch
    else:
        kc = vc = None
        ctx_len = 0
        h0 = jnp.zeros((2, batch, lw["ssm_a"].shape[-1]), F32)
    za = _attn_call(q, k, v, kc, vc, batch, seq_len, ctx_len)
    zb, zd = _convpool_call(p, lw, batch, seq_len, cols)

    ys, hf = _ssm_call([u.reshape(batch, seq_len, LANES) for u in us], h0, lw)
    ys = [y.reshape(2, batch * seq_len, LANES) for y in ys]
    merged, cast_out = _merge_call(h1, p, za, zb, ys, zd, lw, cols, cast)
    kr = cols["k_rope"] + QK_NOPE
    return merged, cast_out, ckv, p[:, kr:kr + ROPE_DIM], hf


def _channel_mixer(x, merged, mod, lw, seq_len, per_batch):
    x1, h2 = _outproj_call(x, merged, mod, lw, seq_len, per_batch)
    return _mlp_call(x1, h2, mod, lw, seq_len, per_batch)


def kernel(x_prompt, x_sample, c, cache_ckv, cache_krope, state_ssm, c_ctx, w_ada, b_ada, norm1_g, norm2_g, w_in, q_a_norm_g, kv_a_norm_g, w_uq, w_ukv, q_norm_g, k_norm_g, w_mla_o, conv_w, conv_b, w_conv_o, ssm_lam_re, ssm_lam_im, ssm_log_step, ssm_b_re, ssm_b_im, ssm_c_re, ssm_c_im, ssm_d, w_glu, pool_w, pool_scale, w_pool_o, w_o, w_mlp1, w_mlp2):
    prm = dict(norm1_g=norm1_g, norm2_g=norm2_g, w_in=w_in, q_a_norm_g=q_a_norm_g, kv_a_norm_g=kv_a_norm_g,
               w_uq=w_uq, w_ukv=w_ukv, q_norm_g=q_norm_g, k_norm_g=k_norm_g, w_mla_o=w_mla_o,
               conv_w=conv_w, conv_b=conv_b, w_conv_o=w_conv_o, ssm_lam_re=ssm_lam_re, ssm_lam_im=ssm_lam_im,
               ssm_log_step=ssm_log_step, ssm_b_re=ssm_b_re, ssm_b_im=ssm_b_im, ssm_c_re=ssm_c_re,
               ssm_c_im=ssm_c_im, ssm_d=ssm_d, w_glu=w_glu, pool_w=pool_w, pool_scale=pool_scale,
               w_pool_o=w_pool_o, w_o=w_o, w_mlp1=w_mlp1, w_mlp2=w_mlp2)
    depth = w_in.shape[0]
    bp, lp, d = x_prompt.shape
    bs, ls, _ = x_sample.shape
    past = cache_ckv.shape[2]
    g, n = state_ssm.shape[-2:]
    sizes = (w_uq.shape[1], w_ukv.shape[1], conv_w.shape[-1], ssm_d.shape[-1], pool_scale.shape[-1])

    rows = -(-(1 + bs) // SUBLANES) * SUBLANES
    cvec = jnp.zeros((rows, d), F32).at[0].set(c_ctx).at[1:1 + bs].set(c)
    mods = _ada_call(cvec, w_ada, b_ada)
    rope = _rope_tables(ls)
    ssm = _prep_ssm(ssm_lam_re, ssm_lam_im, ssm_log_step, ssm_b_re, ssm_b_im, ssm_c_re, ssm_c_im)
    big, cols = _prep_stacked(prm, sizes)

    yp = x_prompt.reshape(bp * lp, d)
    ys = x_sample.reshape(bs * ls, d)
    ckv_list, krope_list, ssm_list = [], [], []
    for l in range(depth):
        lw = _prep_layer(l, prm, sizes, big, ssm)
        mod_ctx = mods[l, 0:1].reshape(6, 1, d)
        mod_lat = mods[l, 1:1 + bs].reshape(bs * 6, 1, d)
        ctx_kpe = jnp.pad(cache_krope[:, l].reshape(bs * past, ROPE_DIM),
                          ((0, 0), (QK_NOPE, HEAD_PAD - QK_HEAD)))
        ctx = (cache_ckv[:, l].reshape(bs * past, -1), ctx_kpe, _states_to_cols(state_ssm[:, l]))
        merged_s, (w1b, w2b, wob), _, _, _ = _mixers(ys, mod_lat, lw, cols, bs, ls, True, rope, ctx,
                                                     cast=(w_mlp1, w_mlp2, w_o))
        merged_p, _, ckv_l, krope_l, hf_l = _mixers(yp, mod_ctx, lw, cols, bp, lp, False, None, None)
        ckv_list.append(ckv_l.reshape(bp, lp, -1))
        krope_list.append(krope_l.reshape(bp, lp, -1))
        ssm_list.append(_cols_to_states(hf_l, g, n))
        lw_mlp = dict(lw, w_mlp1=w1b[None], w_mlp2=w2b[None], l_mlp=0, w_o=wob[None], l_o=0)
        yp = _channel_mixer(yp, merged_p, mod_ctx, lw_mlp, lp, False)
        ys = _channel_mixer(ys, merged_s, mod_lat, lw_mlp, ls, True)
    return (yp.reshape(bp, lp, d), ys.reshape(bs, ls, d), jnp.stack(ckv_list, axis=1),
            jnp.stack(krope_list, axis=1), jnp.stack(ssm_list, axis=1))
```
